```python
import math
import numpy as np
import jax, jax.numpy as jnp
from jax import lax

D_MODEL = 1024
BATCH = 16
SEQ = 2048
DEPTH = 2
DEC_BATCH = 128
DEC_SEQ = 1
PAST_LEN = 16384
PAGE_SIZE = 128

HEAD_DIM = 64
ROT_DIM = HEAD_DIM // 4
ROPE_THETA = 500000.0
A_HEADS = 8
A_KV_HEADS = 2
A_GROUP = A_HEADS // A_KV_HEADS
WINDOW = 128
A_QBLOCK = 128
MB_HEADDIM = 64
MB_HEADS = 8
MB_INNER = MB_HEADS * MB_HEADDIM
MB_GROUPS = 2
MB_DSTATE = 128
MB_CONV = 4
MB_GN = MB_GROUPS * MB_DSTATE
MB_CONV_DIM = MB_INNER + 2 * MB_GN
SSD_CHUNK = 128
C_HEADS = 16
C_KV_HEADS = 4
C_GROUP = C_HEADS // C_KV_HEADS
MOBA_BLOCK = 256
MOBA_TOPK = 3
MOBA_QBLOCK = 128
A_QW = A_HEADS * HEAD_DIM
A_KVW = A_KV_HEADS * HEAD_DIM
IN_A_SPLITS = (A_QW, A_KVW, A_KVW, MB_INNER, MB_CONV_DIM, MB_HEADS)
IN_A_DIM = sum(IN_A_SPLITS)
MIX_A_DIM = A_QW + MB_INNER
C_QW = C_HEADS * HEAD_DIM
C_KVW = C_KV_HEADS * HEAD_DIM
IN_C_DIM = C_QW + 2 * C_KVW
N_EXPERTS = 16
N_EXPERT_GROUPS = 4
EXPERTS_PER_GROUP = N_EXPERTS // N_EXPERT_GROUPS
TOP_K = 2
D_FF = 384
N_EVEN = (DEPTH + 1) // 2
N_ODD = DEPTH // 2
RMS_EPS = 1e-6
NEG_INF = -1e30

kernel_name = 'hybrid_swa_ssd_moba_moe_step'


def rmsnorm(x, g):
    xf = x.astype(jnp.float32)
    y = xf * lax.rsqrt(jnp.mean(xf * xf, axis=-1, keepdims=True) + RMS_EPS)
    return (y * g.astype(jnp.float32)).astype(x.dtype)


def split_cols(u, sizes):
    offs = np.cumsum(sizes)[:-1].tolist()
    return jnp.split(u, offs, axis=-1)


def rope_partial(x, pos):
    half = ROT_DIM // 2
    inv_freq = jnp.power(ROPE_THETA, -jnp.arange(half, dtype=jnp.float32) / half)
    ang = pos.astype(jnp.float32)[:, None] * inv_freq
    cos = jnp.cos(ang)[:, None, :]
    sin = jnp.sin(ang)[:, None, :]
    xr = x[..., :ROT_DIM].astype(jnp.float32)
    x1, x2 = xr[..., :half], xr[..., half:]
    rot = jnp.concatenate([x1 * cos - x2 * sin, x2 * cos + x1 * sin], axis=-1).astype(x.dtype)
    return jnp.concatenate([rot, x[..., ROT_DIM:]], axis=-1)


def sink_attend(q, k, v, mask, sinks):
    s = jnp.einsum('...qkgd,...skd->...kgqs', q, k).astype(jnp.float32) * (HEAD_DIM ** -0.5)
    s = jnp.where(mask[..., None, None, :, :], s, NEG_INF)
    sink = jnp.broadcast_to(sinks.astype(jnp.float32).reshape(A_KV_HEADS, A_GROUP)[:, :, None, None], s.shape[:-1] + (1,))
    p = jax.nn.softmax(jnp.concatenate([s, sink], axis=-1), axis=-1)[..., :-1]
    return jnp.einsum('...kgqs,...skd->...qkgd', p.astype(v.dtype), v)


def window_attn_prompt(q, k, v, sinks):
    b, l = q.shape[:2]
    nb = l // A_QBLOCK
    qb = q.reshape(b, nb, A_QBLOCK, A_KV_HEADS, A_GROUP, HEAD_DIM)
    pad = ((0, 0), (A_QBLOCK, 0), (0, 0), (0, 0))
    kp = jnp.pad(k, pad).reshape(b, nb + 1, A_QBLOCK, A_KV_HEADS, HEAD_DIM)
    vp = jnp.pad(v, pad).reshape(b, nb + 1, A_QBLOCK, A_KV_HEADS, HEAD_DIM)
    kb = jnp.concatenate([kp[:, :-1], kp[:, 1:]], axis=2)
    vb = jnp.concatenate([vp[:, :-1], vp[:, 1:]], axis=2)
    qrel = jnp.arange(A_QBLOCK)[:, None] + A_QBLOCK
    krel = jnp.arange(2 * A_QBLOCK)[None, :]
    diff = qrel - krel
    rel_ok = (diff >= 0) & (diff <= WINDOW)
    abs_k = (jnp.arange(nb)[:, None, None] - 1) * A_QBLOCK + krel[None]
    mask = rel_ok[None] & (abs_k >= 0)
    o = sink_attend(qb, kb, vb, mask, sinks)
    return o.reshape(b, l, A_QW)


def window_attn_sample(q, k_new, v_new, buf_k, buf_v, sinks, pos):
    db, lq = q.shape[:2]
    kk = jnp.concatenate([buf_k, k_new], axis=1)
    vv = jnp.concatenate([buf_v, v_new], axis=1)
    kpos = jnp.concatenate([pos[0] - WINDOW + jnp.arange(WINDOW), pos])
    diff = pos[:, None] - kpos[None, :]
    mask = (diff >= 0) & (diff <= WINDOW)
    o = sink_attend(q.reshape(db, lq, A_KV_HEADS, A_GROUP, HEAD_DIM), kk, vv, mask, sinks)
    return o.reshape(db, lq, A_QW), kk[:, -WINDOW:], vv[:, -WINDOW:]


def ssd_scan(x, dt, a, bm, cm, h0):
    f32 = jnp.float32
    bsz, l, nh, hp = x.shape
    g, n = bm.shape[-2:]
    r = nh // g
    q = math.gcd(l, SSD_CHUNK)
    nc = l // q
    xd = (x.astype(f32) * dt[..., None]).reshape(bsz, nc, q, g, r, hp)
    acum = jnp.cumsum((dt * a).reshape(bsz, nc, q, g, r), axis=2)
    bq = bm.astype(f32).reshape(bsz, nc, q, g, n)
    cq = cm.astype(f32).reshape(bsz, nc, q, g, n)
    causal = jnp.tril(jnp.ones((q, q), dtype=bool))[:, :, None, None]
    seg = acum[:, :, :, None] - acum[:, :, None, :]
    decay = jnp.where(causal, jnp.exp(jnp.where(causal, seg, 0.0)), 0.0)
    cb = jnp.einsum('bctgn,bcsgn->bctsg', cq, bq)
    y_diag = jnp.einsum('bctsg,bctsgr,bcsgrp->bctgrp', cb, decay, xd)
    w_end = jnp.exp(acum[:, :, -1:] - acum)
    states = jnp.einsum('bcsgn,bcsgr,bcsgrp->bcgrpn', bq, w_end, xd)
    chunk_decay = jnp.exp(acum[:, :, -1])

    def step(h, inp):
        st, dec = inp
        return h * dec[..., None, None] + st, h

    h_init = h0.astype(f32).reshape(bsz, g, r, hp, n)
    h_last, h_prev = lax.scan(step, h_init, (jnp.moveaxis(states, 1, 0), jnp.moveaxis(chunk_decay, 1, 0)))
    h_prev = jnp.moveaxis(h_prev, 0, 1)
    y_off = jnp.einsum('bctgn,bctgr,bcgrpn->bctgrp', cq, jnp.exp(acum), h_prev)
    y = (y_diag + y_off).reshape(bsz, l, nh, hp)
    return y.astype(x.dtype), h_last.reshape(bsz, nh, hp, n).astype(h0.dtype)


def mamba_mixer(z, xbc, dt_raw, conv_prev, h0, conv_w, conv_b, dt_bias, a_log, d_skip, norm_g):
    bsz, l, _ = z.shape
    xp = jnp.concatenate([conv_prev, xbc], axis=1)
    acc = conv_b
    for tap in range(MB_CONV):
        acc = acc + xp[:, tap:tap + l] * conv_w[tap]
    xbc_c = jax.nn.silu(acc)
    conv_new = xp[:, -(MB_CONV - 1):]
    xs, bm, cm = split_cols(xbc_c, (MB_INNER, MB_GN, MB_GN))
    xs = xs.reshape(bsz, l, MB_HEADS, MB_HEADDIM)
    bm = bm.reshape(bsz, l, MB_GROUPS, MB_DSTATE)
    cm = cm.reshape(bsz, l, MB_GROUPS, MB_DSTATE)
    dt = jax.nn.softplus(dt_raw.astype(jnp.float32) + dt_bias.astype(jnp.float32))
    a = -jnp.exp(a_log.astype(jnp.float32))
    y, h_new = ssd_scan(xs, dt, a, bm, cm, h0)
    y = y + xs * d_skip[:, None]
    y = (y.reshape(bsz, l, MB_INNER) * jax.nn.silu(z)).reshape(bsz, l, MB_GROUPS, MB_INNER // MB_GROUPS)
    y = rmsnorm(y, norm_g.reshape(MB_GROUPS, MB_INNER // MB_GROUPS)).reshape(bsz, l, MB_INNER)
    return y, conv_new, h_new


def moba_gate(q, kmean, t_pos):
    nblk = kmean.shape[-3]
    if nblk < MOBA_TOPK:
        pad = [(0, 0)] * kmean.ndim
        pad[-3] = (0, MOBA_TOPK - nblk)
        kmean = jnp.pad(kmean, pad)
        nblk = MOBA_TOPK
    qg = q.reshape(q.shape[:-2] + (C_KV_HEADS, C_GROUP, HEAD_DIM))
    s = jnp.einsum('...qkgd,...jkd->...qkgj', qg, kmean).astype(jnp.float32)
    s = s.reshape(s.shape[:-3] + (C_HEADS, nblk))
    own = t_pos // MOBA_BLOCK
    past = jnp.arange(nblk)[None, :] < own[:, None]
    s = jnp.where(past[:, None, :], s, NEG_INF)
    _, idx = lax.top_k(s, MOBA_TOPK)
    slot_ok = jnp.arange(MOBA_TOPK)[None, :] < own[:, None]
    return idx, slot_ok


def moba_positions(idx, slot_ok, t_pos):
    off = jnp.arange(MOBA_BLOCK)
    sel_pos = (idx[..., None] * MOBA_BLOCK + off).reshape(idx.shape[:-1] + (MOBA_TOPK * MOBA_BLOCK,))
    sel_ok = jnp.repeat(slot_ok, MOBA_BLOCK, axis=-1)[:, None, :]
    own_pos = ((t_pos // MOBA_BLOCK) * MOBA_BLOCK)[:, None] + off
    own_ok = own_pos <= t_pos[:, None]
    return sel_pos, sel_ok, own_pos, own_ok


def moba_attend(q, sel_k, sel_v, sel_ok, own_k, own_v, own_ok):
    scale = HEAD_DIM ** -0.5
    lead = q.shape[:-2]
    qg = q.reshape(lead + (C_KV_HEADS, C_GROUP, HEAD_DIM))
    s_sel = jnp.einsum('...hd,...hsd->...hs', q, sel_k).astype(jnp.float32) * scale
    s_own = jnp.einsum('...kgd,...skd->...kgs', qg, own_k).astype(jnp.float32) * scale
    s_own = s_own.reshape(lead + (C_HEADS, MOBA_BLOCK))
    s = jnp.concatenate([jnp.where(sel_ok, s_sel, NEG_INF), jnp.where(own_ok[:, None, :], s_own, NEG_INF)], axis=-1)
    pr = jax.nn.softmax(s, axis=-1).astype(q.dtype)
    nsel = sel_k.shape[-2]
    o_sel = jnp.einsum('...hs,...hsd->...hd', pr[..., :nsel], sel_v)
    p_own = pr[..., nsel:].reshape(lead + (C_KV_HEADS, C_GROUP, MOBA_BLOCK))
    o_own = jnp.einsum('...kgs,...skd->...kgd', p_own, own_v)
    return o_sel + o_own.reshape(q.shape)


def moba_prompt(q, k, v):
    b, l = q.shape[:2]
    nblk = -(-l // MOBA_BLOCK)
    kp = jnp.pad(k, ((0, 0), (0, nblk * MOBA_BLOCK - l), (0, 0), (0, 0)))
    kmean = jnp.mean(kp.reshape(b, nblk, MOBA_BLOCK, C_KV_HEADS, HEAD_DIM).astype(jnp.float32), axis=2).astype(k.dtype)
    nq = l // MOBA_QBLOCK
    qb = q.reshape(b * nq, MOBA_QBLOCK, C_HEADS, HEAD_DIM)
    bidx = jnp.repeat(jnp.arange(b), nq)
    qstart = jnp.tile(jnp.arange(nq) * MOBA_QBLOCK, b)
    kvh = (jnp.arange(C_HEADS) // C_GROUP)[:, None]

    def step(args):
        qblk, bi, q0 = args
        kb, vb = k[bi], v[bi]
        t_pos = q0 + jnp.arange(MOBA_QBLOCK)
        idx, slot_ok = moba_gate(qblk, kmean[bi], t_pos)
        sel_pos, sel_ok, own_pos, own_ok = moba_positions(idx, slot_ok, t_pos)
        sel_pos = jnp.clip(sel_pos, 0, l - 1)
        own_pos = jnp.clip(own_pos, 0, l - 1)
        return moba_attend(qblk, kb[sel_pos, kvh], vb[sel_pos, kvh], sel_ok, kb[own_pos], vb[own_pos], own_ok)

    o = lax.map(step, (qb, bidx, qstart))
    return o.reshape(b, l, C_QW)


def paged_rows(pool, layer, page_table, new_rows, pos, head):
    db, n_pages = page_table.shape
    past = n_pages * PAGE_SIZE
    bi = jnp.arange(db).reshape((db,) + (1,) * (pos.ndim - 1))
    phys = page_table[bi, jnp.clip(pos // PAGE_SIZE, 0, n_pages - 1)]
    slot = pos % PAGE_SIZE
    ni = jnp.clip(pos - past, 0, new_rows.shape[1] - 1)
    in_past = pos < past
    if head is None:
        old = pool[layer, phys, slot]
        fresh = new_rows[bi, ni]
        in_past = in_past[..., None, None]
    else:
        old = pool[layer, phys, slot, head]
        fresh = new_rows[bi, ni, head]
        in_past = in_past[..., None]
    return jnp.where(in_past, old, fresh)


def moba_sample(q, k_new, v_new, pool_k, pool_v, page_ksum, layer, page_table, t_pos):
    db, lq = q.shape[:2]
    n_pages = page_table.shape[1]
    nblk = -(-(n_pages * PAGE_SIZE + lq) // MOBA_BLOCK)
    contrib = jnp.concatenate([page_ksum[layer][page_table], k_new.astype(jnp.float32)], axis=1)
    seg = jnp.concatenate([jnp.arange(n_pages) * PAGE_SIZE // MOBA_BLOCK, t_pos // MOBA_BLOCK])
    bsum = jax.ops.segment_sum(jnp.swapaxes(contrib, 0, 1), seg, num_segments=nblk)
    kmean = (jnp.swapaxes(bsum, 0, 1) / MOBA_BLOCK).astype(q.dtype)
    idx, slot_ok = moba_gate(q, kmean, t_pos)
    sel_pos, sel_ok, own_pos, own_ok = moba_positions(idx, slot_ok, t_pos)
    kvh = (jnp.arange(C_HEADS) // C_GROUP)[:, None]
    sel_k = paged_rows(pool_k, layer, page_table, k_new, sel_pos, kvh)
    sel_v = paged_rows(pool_v, layer, page_table, v_new, sel_pos, kvh)
    own_b = jnp.broadcast_to(own_pos, (db,) + own_pos.shape)
    own_k = paged_rows(pool_k, layer, page_table, k_new, own_b, None)
    own_v = paged_rows(pool_v, layer, page_table, v_new, own_b, None)
    o = moba_attend(q, sel_k, sel_v, sel_ok, own_k, own_v, own_ok)
    return o.reshape(db, lq, C_QW)


def moe(h, layer, p):
    shp = h.shape
    t = h.reshape(-1, D_MODEL)
    scores = jax.nn.sigmoid((t @ p['w_router']).astype(jnp.float32))
    biased = scores + p['router_bias'].astype(jnp.float32)
    grp_score = lax.top_k(biased.reshape(-1, N_EXPERT_GROUPS, EXPERTS_PER_GROUP), TOP_K)[0].sum(-1)
    gsel = jnp.argmax(grp_score, axis=-1)
    in_group = (jnp.arange(N_EXPERTS) // EXPERTS_PER_GROUP)[None, :] == gsel[:, None]
    _, eidx = lax.top_k(jnp.where(in_group, biased, NEG_INF), TOP_K)
    wsel = jnp.take_along_axis(scores, eidx, axis=-1)
    wsel = wsel / jnp.sum(wsel, axis=-1, keepdims=True)
    combine = jnp.sum(jax.nn.one_hot(eidx, N_EXPERTS, dtype=jnp.float32) * wsel[..., None], axis=-2).astype(h.dtype)
    out = jnp.zeros_like(t)
    for e in range(N_EXPERTS):
        he = jax.nn.silu(t @ p['w_gate'][layer, e]) * (t @ p['w_up'][layer, e])
        out = out + combine[:, e:e + 1] * (he @ p['w_down'][layer, e])
    return out.reshape(shp)


def decoder(x, c, pos, p, st):
    is_prompt = st is None
    bsz, l, _ = x.shape
    new = {'win_k': [], 'win_v': [], 'conv': [], 'ssm': [], 'k': [], 'v': []}
    if not is_prompt:
        page_ksum = jnp.sum(st['cache_k'], axis=2, dtype=jnp.float32)
    cs = jax.nn.silu(c)
    for layer in range(DEPTH):
        mod = (cs @ p['w_ada'][layer] + p['b_ada'][layer])[:, None, :]
        sh1, sc1, g1, sh2, sc2, g2 = jnp.split(mod, 6, axis=-1)
        h = rmsnorm(x, p['norm_g'][layer, 0]) * (1 + sc1) + sh1
        if layer % 2 == 0:
            i = layer // 2
            u = h @ p['w_in_a'][i]
            qa, ka, va, z, xbc, dt_raw = split_cols(u, IN_A_SPLITS)
            qa = rope_partial(qa.reshape(bsz, l, A_HEADS, HEAD_DIM), pos)
            ka = rope_partial(ka.reshape(bsz, l, A_KV_HEADS, HEAD_DIM), pos)
            va = va.reshape(bsz, l, A_KV_HEADS, HEAD_DIM)
            if is_prompt:
                o_a = window_attn_prompt(qa, ka, va, p['sinks'][i])
                wk, wv = ka[:, -WINDOW:], va[:, -WINDOW:]
                conv_prev = jnp.zeros((bsz, MB_CONV - 1, MB_CONV_DIM), x.dtype)
                h0 = jnp.zeros((bsz, MB_HEADS, MB_HEADDIM, MB_DSTATE), x.dtype)
            else:
                o_a, wk, wv = window_attn_sample(qa, ka, va, st['win_k'][i], st['win_v'][i], p['sinks'][i], pos)
                conv_prev = st['conv'][i]
                h0 = st['ssm'][i]
            o_b, conv_new, h_new = mamba_mixer(z, xbc, dt_raw, conv_prev, h0, p['conv_w'][i], p['conv_b'][i],
                                               p['dt_bias'][i], p['a_log'][i], p['d_skip'][i], p['ssm_norm_g'][i])
            mix = jnp.concatenate([o_a, o_b], axis=-1) @ p['w_out_a'][i]
            new['win_k'].append(wk)
            new['win_v'].append(wv)
            new['conv'].append(conv_new)
            new['ssm'].append(h_new)
        else:
            j = layer // 2
            u = h @ p['w_in_c'][j]
            qc, kc, vc = split_cols(u, (C_QW, C_KVW, C_KVW))
            qc = rope_partial(qc.reshape(bsz, l, C_HEADS, HEAD_DIM), pos)
            kc = rope_partial(kc.reshape(bsz, l, C_KV_HEADS, HEAD_DIM), pos)
            vc = vc.reshape(bsz, l, C_KV_HEADS, HEAD_DIM)
            if is_prompt:
                o_c = moba_prompt(qc, kc, vc)
            else:
                o_c = moba_sample(qc, kc, vc, st['cache_k'], st['cache_v'], page_ksum, j, st['page_table'], pos)
            mix = o_c @ p['w_out_c'][j]
            new['k'].append(kc)
            new['v'].append(vc)
        x = x + g1 * mix
        h = rmsnorm(x, p['norm_g'][layer, 1]) * (1 + sc2) + sh2
        x = x + g2 * moe(h, layer, p)
    y = rmsnorm(x, p['final_norm_g'])
    return y, {name: jnp.stack(rows) for name, rows in new.items()}


def setup_inputs(seed: int = 0) -> dict:
    key = jax.random.key(seed)
    ks = iter(jax.random.split(key, 40))
    f32 = jnp.float32

    def nrm(shape, scale):
        return jax.random.normal(next(ks), shape, f32) * scale

    n_pages = PAST_LEN // PAGE_SIZE
    n_pool = (DEC_BATCH * n_pages * 5) // 4
    x_prompt = nrm((BATCH, SEQ, D_MODEL), 1.0)
    x_sample = nrm((DEC_BATCH, DEC_SEQ, D_MODEL), 1.0)
    c_prompt = nrm((BATCH, D_MODEL), 1.0)
    c_sample = nrm((DEC_BATCH, D_MODEL), 1.0)
    state_win_k = nrm((N_EVEN, DEC_BATCH, WINDOW, A_KV_HEADS, HEAD_DIM), 1.0)
    state_win_v = nrm((N_EVEN, DEC_BATCH, WINDOW, A_KV_HEADS, HEAD_DIM), 1.0)
    state_conv = nrm((N_EVEN, DEC_BATCH, MB_CONV - 1, MB_CONV_DIM), 1.0)
    state_ssm = nrm((N_EVEN, DEC_BATCH, MB_HEADS, MB_HEADDIM, MB_DSTATE), 0.1)
    cache_k = nrm((N_ODD, n_pool, PAGE_SIZE, C_KV_HEADS, HEAD_DIM), 1.0)
    cache_v = nrm((N_ODD, n_pool, PAGE_SIZE, C_KV_HEADS, HEAD_DIM), 1.0)
    page_table = jax.random.permutation(next(ks), n_pool)[:DEC_BATCH * n_pages].reshape(DEC_BATCH, n_pages).astype(jnp.int32)
    w_ada = nrm((DEPTH, D_MODEL, 6 * D_MODEL), 0.5 * D_MODEL ** -0.5)
    b_ada = nrm((DEPTH, 6 * D_MODEL), 0.02)
    norm_g = 1.0 + nrm((DEPTH, 2, D_MODEL), 0.02)
    w_in_a = nrm((N_EVEN, D_MODEL, IN_A_DIM), D_MODEL ** -0.5)
    sinks = nrm((N_EVEN, A_HEADS), 0.5)
    conv_w = nrm((N_EVEN, MB_CONV, MB_CONV_DIM), MB_CONV ** -0.5)
    conv_b = nrm((N_EVEN, MB_CONV_DIM), 0.02)
    dt0 = jnp.exp(jax.random.uniform(next(ks), (N_EVEN, MB_HEADS), f32, math.log(1e-3), math.log(1e-1)))
    dt_bias = dt0 + jnp.log(-jnp.expm1(-dt0))
    a_log = jnp.log(jax.random.uniform(next(ks), (N_EVEN, MB_HEADS), f32, 1.0, 16.0))
    d_skip = 1.0 + nrm((N_EVEN, MB_HEADS), 0.02)
    ssm_norm_g = 1.0 + nrm((N_EVEN, MB_INNER), 0.02)
    w_out_a = nrm((N_EVEN, MIX_A_DIM, D_MODEL), MIX_A_DIM ** -0.5)
    w_in_c = nrm((N_ODD, D_MODEL, IN_C_DIM), D_MODEL ** -0.5)
    w_out_c = nrm((N_ODD, C_QW, D_MODEL), C_QW ** -0.5)
    w_router = nrm((D_MODEL, N_EXPERTS), D_MODEL ** -0.5)
    router_bias = nrm((N_EXPERTS,), 0.01)
    w_gate = nrm((DEPTH, N_EXPERTS, D_MODEL, D_FF), D_MODEL ** -0.5)
    w_up = nrm((DEPTH, N_EXPERTS, D_MODEL, D_FF), D_MODEL ** -0.5)
    w_down = nrm((DEPTH, N_EXPERTS, D_FF, D_MODEL), D_FF ** -0.5)
    final_norm_g = 1.0 + nrm((D_MODEL,), 0.02)
    return {'x_prompt': x_prompt, 'x_sample': x_sample, 'c_prompt': c_prompt, 'c_sample': c_sample,
            'state_win_k': state_win_k, 'state_win_v': state_win_v, 'state_conv': state_conv, 'state_ssm': state_ssm,
            'cache_k': cache_k, 'cache_v': cache_v, 'page_table': page_table,
            'w_ada': w_ada, 'b_ada': b_ada, 'norm_g': norm_g, 'w_in_a': w_in_a, 'sinks': sinks,
            'conv_w': conv_w, 'conv_b': conv_b, 'dt_bias': dt_bias, 'a_log': a_log, 'd_skip': d_skip,
            'ssm_norm_g': ssm_norm_g, 'w_out_a': w_out_a, 'w_in_c': w_in_c, 'w_out_c': w_out_c,
            'w_router': w_router, 'router_bias': router_bias, 'w_gate': w_gate, 'w_up': w_up, 'w_down': w_down,
            'final_norm_g': final_norm_g}


def reference(x_prompt, x_sample, c_prompt, c_sample, state_win_k, state_win_v, state_conv, state_ssm,
              cache_k, cache_v, page_table, w_ada, b_ada, norm_g, w_in_a, sinks, conv_w, conv_b, dt_bias,
              a_log, d_skip, ssm_norm_g, w_out_a, w_in_c, w_out_c, w_router, router_bias, w_gate, w_up,
              w_down, final_norm_g):
    p = {'w_ada': w_ada, 'b_ada': b_ada, 'norm_g': norm_g, 'w_in_a': w_in_a, 'sinks': sinks,
         'conv_w': conv_w, 'conv_b': conv_b, 'dt_bias': dt_bias, 'a_log': a_log, 'd_skip': d_skip,
         'ssm_norm_g': ssm_norm_g, 'w_out_a': w_out_a, 'w_in_c': w_in_c, 'w_out_c': w_out_c,
         'w_router': w_router, 'router_bias': router_bias, 'w_gate': w_gate, 'w_up': w_up, 'w_down': w_down,
         'final_norm_g': final_norm_g}
    pos_p = jnp.arange(x_prompt.shape[1], dtype=jnp.int32)
    y_prompt, pn = decoder(x_prompt, c_prompt, pos_p, p, None)
    st = {'win_k': state_win_k, 'win_v': state_win_v, 'conv': state_conv, 'ssm': state_ssm,
          'cache_k': cache_k, 'cache_v': cache_v, 'page_table': page_table}
    pos_s = page_table.shape[1] * PAGE_SIZE + jnp.arange(x_sample.shape[1], dtype=jnp.int32)
    y_sample, sn = decoder(x_sample, c_sample, pos_s, p, st)
    return (y_prompt, y_sample, pn['win_k'], pn['win_v'], pn['conv'], pn['ssm'], pn['k'], pn['v'],
            sn['win_k'], sn['win_v'], sn['conv'], sn['ssm'], sn['k'], sn['v'])
```

```python
import functools
import math

import jax
import jax.numpy as jnp
import numpy as np
from jax import lax
from jax.experimental import pallas as pl
from jax.experimental.pallas import tpu as pltpu

F32 = jnp.float32
BF16 = jnp.bfloat16

HEAD_DIM = 64
ROT_DIM = HEAD_DIM // 4
ROPE_THETA = 500000.0
A_HEADS = 8
A_KV_HEADS = 2
A_GROUP = A_HEADS // A_KV_HEADS
WINDOW = 128
MB_HEADDIM = 64
MB_HEADS = 8
MB_INNER = MB_HEADS * MB_HEADDIM
MB_GROUPS = 2
MB_DSTATE = 128
MB_CONV = 4
MB_GN = MB_GROUPS * MB_DSTATE
MB_CONV_DIM = MB_INNER + 2 * MB_GN
SSD_CHUNK = 128
C_HEADS = 16
C_KV_HEADS = 4
C_GROUP = C_HEADS // C_KV_HEADS
MOBA_BLOCK = 256
MOBA_TOPK = 3
MOBA_QBLOCK = 128
PAGE_SIZE = 128
A_QW = A_HEADS * HEAD_DIM
A_KVW = A_KV_HEADS * HEAD_DIM
C_QW = C_HEADS * HEAD_DIM
C_KVW = C_KV_HEADS * HEAD_DIM
N_EXPERTS = 16
N_EXPERT_GROUPS = 4
EXPERTS_PER_GROUP = N_EXPERTS // N_EXPERT_GROUPS
RMS_EPS = 1e-6
NEG_INF = -1e30
ATT_SCALE = HEAD_DIM ** -0.5

LANES = 128
VMEM_LIMIT_BYTES = 56 * 1024 * 1024


def _cparams(*sem):
    return pltpu.CompilerParams(dimension_semantics=sem, vmem_limit_bytes=VMEM_LIMIT_BYTES)


def _dot(a, b):
    return jnp.dot(a, b, preferred_element_type=F32)


def _dot_nt(a, b):
    return lax.dot_general(a, b, (((1,), (1,)), ((), ())), preferred_element_type=F32)


def _split2(x):
    hi = x.astype(BF16)
    lo = (x - hi.astype(F32)).astype(BF16)
    return hi, lo


def _dot_hi(a, b):
    ah, al = _split2(a)
    bh, bl = _split2(b)
    return _dot(ah, bh) + (_dot(al, bh) + _dot(ah, bl))


def _dot_hi_nt(a, b):
    ah, al = _split2(a)
    bh, bl = _split2(b)
    return _dot_nt(ah, bh) + (_dot_nt(al, bh) + _dot_nt(ah, bl))


def _dot_exact_lhs(a_bf16, b):
    b0 = b.astype(BF16)
    r = b - b0.astype(F32)
    b1 = r.astype(BF16)
    b2 = (r - b1.astype(F32)).astype(BF16)
    return _dot(a_bf16, b0) + (_dot(a_bf16, b1) + _dot(a_bf16, b2))


def _sigmoid(x):
    return 1.0 / (1.0 + jnp.exp(-x))


def _silu(x):
    return x * _sigmoid(x)


def _softplus(x):
    return jnp.maximum(x, 0.0) + jnp.log(1.0 + jnp.exp(-jnp.abs(x)))


def _rms(x, g):
    return x * lax.rsqrt(jnp.mean(x * x, axis=-1, keepdims=True) + RMS_EPS) * g


def _rope_tables(pos):
    half = ROT_DIM // 2
    inv_freq = jnp.power(ROPE_THETA, -jnp.arange(half, dtype=F32) / half)
    ang = pos.astype(F32)[:, None] * inv_freq
    cos, sin = jnp.cos(ang), jnp.sin(ang)
    n = pos.shape[0]
    rest = HEAD_DIM - ROT_DIM
    ct = jnp.concatenate([cos, cos, jnp.ones((n, rest), F32)], axis=1)
    sa = jnp.concatenate([jnp.zeros((n, half), F32), sin, jnp.zeros((n, rest), F32)], axis=1)
    sb = jnp.concatenate([-sin, jnp.zeros((n, half), F32), jnp.zeros((n, rest), F32)], axis=1)
    rep = LANES // HEAD_DIM
    return jnp.tile(ct, (1, rep)), jnp.tile(sa, (1, rep)), jnp.tile(sb, (1, rep))


def _rope(x, ct, sa, sb):
    half = ROT_DIM // 2
    return x * ct + pltpu.roll(x, half, 1) * sa + pltpu.roll(x, LANES - half, 1) * sb


def _ada_kernel(c_ref, w_ref, b_ref, o_ref):
    o_ref[...] = _dot_hi(_silu(c_ref[...]), w_ref[...]) + b_ref[...]


def _ada(c_all, w_ada, b_ada):
    depth, d, n6 = w_ada.shape
    nc = c_all.shape[0]
    nk = n6 // d
    return pl.pallas_call(
        _ada_kernel,
        grid=(depth, nk),
        in_specs=[pl.BlockSpec((nc, d), lambda l, k: (0, 0)),
                  pl.BlockSpec((None, d, d), lambda l, k: (l, 0, k)),
                  pl.BlockSpec((None, 1, d), lambda l, k: (l, 0, k))],
        out_specs=pl.BlockSpec((None, nc, d), lambda l, k: (l, 0, k)),
        out_shape=jax.ShapeDtypeStruct((depth, nc, n6), F32),
        compiler_params=_cparams("parallel", "parallel"),
        name="ada",
    )(c_all, w_ada, b_ada.reshape(depth, 1, n6))


class _Stream:
    def __init__(self, nseq, seqlen, d, mod):
        self.nseq, self.seqlen, self.d = nseq, seqlen, d
        self.tokens = nseq * seqlen
        if seqlen == 1:
            self.tm = self.tokens
            self.tiles_per_seq = None
            self.mod = mod
        else:
            self.tm = math.gcd(seqlen, 512)
            self.tiles_per_seq = seqlen // self.tm
            depth = mod.shape[0]
            self.mod = mod.reshape(depth, nseq * 6, 1, d)
        self.ntiles = self.tokens // self.tm

    def mod_arg(self, layer):
        return self.mod[layer]

    def mod_spec(self, k):
        if self.tiles_per_seq is None:
            return pl.BlockSpec((self.tm, self.d), lambda i, *_: (0, k))
        tps = self.tiles_per_seq
        return pl.BlockSpec((None, 1, self.d), lambda i, *_: ((i // tps) * 6 + k, 0, 0))

    def rope_spec(self):
        if self.tiles_per_seq is None:
            return pl.BlockSpec((self.tm, LANES), lambda i, *_: (0, 0))
        tps = self.tiles_per_seq
        return pl.BlockSpec((self.tm, LANES), lambda i, *_: (i % tps, 0))


def _proj_kernel(x_ref, sh_ref, sc_ref, g_ref, w_ref, ct_ref, sa_ref, sb_ref, *out_refs, splits):
    h = _rms(x_ref[...], g_ref[...]) * (1.0 + sc_ref[...]) + sh_ref[...]
    u = _dot(h.astype(BF16), w_ref[...])
    for (start, width, rope), o_ref in zip(splits, out_refs):
        if rope:
            ct, sa, sb = ct_ref[...], sa_ref[...], sb_ref[...]
            for c0 in range(0, width, LANES):
                o_ref[:, c0:c0 + LANES] = _rope(u[:, start + c0:start + c0 + LANES], ct, sa, sb)
        else:
            o_ref[...] = u[:, start:start + width]


def _proj(st, layer, x, norm_g, w_bf16, rope_tabs, splits):
    d = st.d
    n = w_bf16.shape[1]
    tm = st.tm
    row = lambda i: (i, 0)
    const = lambda i: (0, 0)
    return pl.pallas_call(
        functools.partial(_proj_kernel, splits=splits),
        grid=(st.ntiles,),
        in_specs=[pl.BlockSpec((tm, d), row), st.mod_spec(0), st.mod_spec(1),
                  pl.BlockSpec((1, d), const), pl.BlockSpec((d, n), const),
                  st.rope_spec(), st.rope_spec(), st.rope_spec()],
        out_specs=[pl.BlockSpec((tm, w), row) for _, w, _ in splits],
        out_shape=[jax.ShapeDtypeStruct((st.tokens, w), F32) for _, w, _ in splits],
        compiler_params=_cparams("parallel"),
        name="proj",
    )(x, st.mod_arg(layer), st.mod_arg(layer), norm_g.reshape(1, d), w_bf16, *rope_tabs)


def _swa_prompt_kernel(sink_ref, q_ref, kp_ref, kc_ref, vp_ref, vc_ref, o_ref):
    i = pl.program_id(1)
    q = q_ref[...]
    kk = jnp.concatenate([kp_ref[...], kc_ref[...]], axis=0).astype(BF16)
    vv = jnp.concatenate([vp_ref[...], vc_ref[...]], axis=0).astype(BF16)
    qb = q_ref.shape[0]
    row = lax.broadcasted_iota(jnp.int32, (qb, 2 * qb), 0)
    col = lax.broadcasted_iota(jnp.int32, (qb, 2 * qb), 1)
    diff = row + qb - col
    ok = (diff >= 0) & (diff <= WINDOW) & ((col >= qb) | (i > 0))
    outs = []
    for h in range(A_HEADS):
        kv = h // A_GROUP
        qh = q[:, h * HEAD_DIM:(h + 1) * HEAD_DIM].astype(BF16)
        s = _dot_nt(qh, kk[:, kv * HEAD_DIM:(kv + 1) * HEAD_DIM]) * ATT_SCALE
        s = jnp.where(ok, s, NEG_INF)
        sink = sink_ref[h]
        m = jnp.maximum(jnp.max(s, axis=-1, keepdims=True), sink)
        p = jnp.exp(s - m)
        denom = jnp.sum(p, axis=-1, keepdims=True) + jnp.exp(sink - m)
        outs.append(_dot(p.astype(BF16), vv[:, kv * HEAD_DIM:(kv + 1) * HEAD_DIM]) / denom)
    o_ref[...] = jnp.concatenate(outs, axis=-1)


def _swa_prompt(q, k, v, sinks, nseq, seqlen):
    qb = WINDOW
    nb = seqlen // qb
    cur = lambda b, i: (b * nb + i, 0)
    prev = lambda b, i: (b * nb + jnp.maximum(i - 1, 0), 0)
    return pl.pallas_call(
        _swa_prompt_kernel,
        grid=(nseq, nb),
        in_specs=[pl.BlockSpec(memory_space=pltpu.SMEM),
                  pl.BlockSpec((qb, A_QW), cur),
                  pl.BlockSpec((qb, A_KVW), prev), pl.BlockSpec((qb, A_KVW), cur),
                  pl.BlockSpec((qb, A_KVW), prev), pl.BlockSpec((qb, A_KVW), cur)],
        out_specs=pl.BlockSpec((qb, A_QW), cur),
        out_shape=jax.ShapeDtypeStruct(q.shape, F32),
        compiler_params=_cparams("parallel", "parallel"),
        name="swa_prompt",
    )(sinks, q, k, k, v, v)


def _ssd_prompt_kernel(xbc_ref, z_ref, dt_ref, cprev_ref, h0_ref, cw_ref, cb_ref, dtb_ref, alog_ref, dskip_ref,
                       ng_ref, o_ref, cnew_ref, hlast_ref, xp_ref, h_ref):
    c = pl.program_id(1)
    nc = pl.num_programs(1)
    q = SSD_CHUNK
    halo = MB_CONV - 1
    base = 8 - halo

    @pl.when(c == 0)
    def _():
        xp_ref[base:8, :] = cprev_ref[...]
        h_ref[...] = h0_ref[...]

    xp_ref[8:8 + q, :] = xbc_ref[...]
    acc = cb_ref[...] + xp_ref[base:base + q, :] * cw_ref[0:1, :]
    for tap in range(1, MB_CONV):
        acc = acc + xp_ref[base + tap:base + tap + q, :] * cw_ref[tap:tap + 1, :]
    tail = xp_ref[8 + q - halo:8 + q, :]
    xp_ref[base:8, :] = tail

    @pl.when(c == nc - 1)
    def _():
        cnew_ref[...] = tail

    xbc = _silu(acc)
    xs = xbc[:, :MB_INNER]
    dt = _softplus(dt_ref[...] + dtb_ref[...])
    da = dt * (-jnp.exp(alog_ref[...]))
    r_i = lax.broadcasted_iota(jnp.int32, (q, q), 0)
    c_i = lax.broadcasted_iota(jnp.int32, (q, q), 1)
    causal = r_i >= c_i
    acum = _dot_exact_lhs(causal.astype(BF16), da)
    acum_t = acum.T
    rpg = MB_HEADS // MB_GROUPS
    ys = []
    for g in range(MB_GROUPS):
        bq = xbc[:, MB_INNER + g * MB_DSTATE:MB_INNER + (g + 1) * MB_DSTATE].astype(BF16)
        cq = xbc[:, MB_INNER + MB_GN + g * MB_DSTATE:MB_INNER + MB_GN + (g + 1) * MB_DSTATE].astype(BF16)
        cbm = _dot_nt(cq, bq)
        for r in range(rpg):
            h = g * rpg + r
            a_col = acum[:, h:h + 1]
            a_row = acum_t[h:h + 1, :]
            a_last = acum[q - 1:q, h:h + 1]
            decay = jnp.where(causal, jnp.exp(jnp.where(causal, a_col - a_row, 0.0)), 0.0)
            xd = xs[:, h * MB_HEADDIM:(h + 1) * MB_HEADDIM] * dt[:, h:h + 1]
            hprev = h_ref[h]
            y = _dot((cbm * decay).astype(BF16), xd.astype(BF16))
            y = y + jnp.exp(a_col) * _dot_nt(cq, hprev.astype(BF16))
            xw = (xd * jnp.exp(a_last - a_col)).astype(BF16)
            states = lax.dot_general(xw, bq, (((0,), (0,)), ((), ())), preferred_element_type=F32)
            h_ref[h] = hprev * jnp.exp(a_last) + states
            ys.append(y)
    y = jnp.concatenate(ys, axis=-1) + xs * dskip_ref[...]
    y = y * _silu(z_ref[...])
    gw = MB_INNER // MB_GROUPS
    ng = ng_ref[...]
    o_ref[...] = jnp.concatenate(
        [_rms(y[:, g * gw:(g + 1) * gw], ng[:, g * gw:(g + 1) * gw]) for g in range(MB_GROUPS)], axis=-1)

    @pl.when(c == nc - 1)
    def _():
        hlast_ref[...] = h_ref[...]


def _pad_lanes(v, n=LANES):
    return jnp.pad(v, (0, n - v.shape[0])).reshape(1, n)


def _ssd_prompt(xbc, z, dt, conv_prev, h0, conv_w, conv_b, dt_bias, a_log, d_skip, norm_g, nseq, seqlen):
    q = SSD_CHUNK
    nc = seqlen // q
    row = lambda b, c: (b * nc + c, 0)
    per_seq3 = lambda b, c: (b, 0, 0)
    per_seq4 = lambda b, c: (b, 0, 0, 0)
    const = lambda b, c: (0, 0)
    halo = MB_CONV - 1
    return pl.pallas_call(
        _ssd_prompt_kernel,
        grid=(nseq, nc),
        in_specs=[pl.BlockSpec((q, MB_CONV_DIM), row), pl.BlockSpec((q, MB_INNER), row),
                  pl.BlockSpec((q, LANES), row),
                  pl.BlockSpec((None, halo, MB_CONV_DIM), per_seq3),
                  pl.BlockSpec((None, MB_HEADS, MB_HEADDIM, MB_DSTATE), per_seq4),
                  pl.BlockSpec((MB_CONV, MB_CONV_DIM), const), pl.BlockSpec((1, MB_CONV_DIM), const),
                  pl.BlockSpec((1, LANES), const), pl.BlockSpec((1, LANES), const),
                  pl.BlockSpec((1, MB_INNER), const), pl.BlockSpec((1, MB_INNER), const)],
        out_specs=[pl.BlockSpec((q, MB_INNER), row),
                   pl.BlockSpec((None, halo, MB_CONV_DIM), per_seq3),
                   pl.BlockSpec((None, MB_HEADS, MB_HEADDIM, MB_DSTATE), per_seq4)],
        out_shape=[jax.ShapeDtypeStruct((nseq * seqlen, MB_INNER), F32),
                   jax.ShapeDtypeStruct((nseq, halo, MB_CONV_DIM), F32),
                   jax.ShapeDtypeStruct((nseq, MB_HEADS, MB_HEADDIM, MB_DSTATE), F32)],
        scratch_shapes=[pltpu.VMEM((8 + q, MB_CONV_DIM), F32),
                        pltpu.VMEM((MB_HEADS, MB_HEADDIM, MB_DSTATE), F32)],
        compiler_params=_cparams("parallel", "arbitrary"),
        name="ssd_prompt",
    )(xbc, z, dt, conv_prev, h0, conv_w, conv_b.reshape(1, -1), _pad_lanes(dt_bias), _pad_lanes(a_log),
      jnp.repeat(d_skip, MB_HEADDIM).reshape(1, -1), norm_g.reshape(1, -1))


def _route(logits_t, rbias_col):
    tm = logits_t.shape[1]
    scores = _sigmoid(logits_t[0:N_EXPERTS, :])
    biased = scores + rbias_col[0:N_EXPERTS, :]
    s = [scores[e:e + 1, :] for e in range(N_EXPERTS)]
    b = [biased[e:e + 1, :] for e in range(N_EXPERTS)]
    epg = EXPERTS_PER_GROUP
    gscore = []
    for g in range(N_EXPERT_GROUPS):
        v = b[g * epg:(g + 1) * epg]
        best = None
        for i in range(epg):
            for j in range(i + 1, epg):
                pair = v[i] + v[j]
                best = pair if best is None else jnp.maximum(best, pair)
        gscore.append(best)
    gsel = jnp.zeros((1, tm), jnp.int32)
    gbest = gscore[0]
    for g in range(1, N_EXPERT_GROUPS):
        better = gscore[g] > gbest
        gsel = jnp.where(better, g, gsel)
        gbest = jnp.where(better, gscore[g], gbest)
    bs, ss = [], []
    for k in range(epg):
        bk, sk = b[k], s[k]
        for g in range(1, N_EXPERT_GROUPS):
            bk = jnp.where(gsel == g, b[g * epg + k], bk)
            sk = jnp.where(gsel == g, s[g * epg + k], sk)
        bs.append(bk)
        ss.append(sk)
    i1 = jnp.zeros((1, tm), jnp.int32)
    m1 = bs[0]
    for k in range(1, epg):
        better = bs[k] > m1
        i1 = jnp.where(better, k, i1)
        m1 = jnp.where(better, bs[k], m1)
    i2 = jnp.full((1, tm), -1, jnp.int32)
    m2 = jnp.full((1, tm), -jnp.inf, F32)
    for k in range(epg):
        better = (i1 != k) & ((bs[k] > m2) | (i2 < 0))
        i2 = jnp.where(better, k, i2)
        m2 = jnp.where(better, bs[k], m2)
    s1 = jnp.zeros((1, tm), F32)
    s2 = jnp.zeros((1, tm), F32)
    for k in range(epg):
        s1 = jnp.where(i1 == k, ss[k], s1)
        s2 = jnp.where(i2 == k, ss[k], s2)
    denom = s1 + s2
    rows = lax.broadcasted_iota(jnp.int32, (N_EXPERTS, tm), 0)
    comb = jnp.zeros((N_EXPERTS, tm), F32)
    for e in range(N_EXPERTS):
        g, k = divmod(e, epg)
        chosen = (gsel == g) & ((i1 == k) | (i2 == k))
        comb = jnp.where(rows == e, jnp.where(chosen, s[e] / denom, 0.0), comb)
    return comb


def _mix_kernel(*refs, n_in):
    x_ref, g1_ref = refs[0], refs[1]
    a_refs = refs[2:2 + n_in]
    w_refs = refs[2 + n_in:2 + 2 * n_in]
    ng_ref, sc_ref, sh_ref, wr_ref, rb_ref, x1_ref, h2_ref, comb_ref = refs[2 + 2 * n_in:]
    mix = _dot(a_refs[0][...].astype(BF16), w_refs[0][...])
    for a_ref, w_ref in zip(a_refs[1:], w_refs[1:]):
        mix = mix + _dot(a_ref[...].astype(BF16), w_ref[...])
    x1 = x_ref[...] + g1_ref[...] * mix
    x1_ref[...] = x1
    h2 = _rms(x1, ng_ref[...]) * (1.0 + sc_ref[...]) + sh_ref[...]
    h2_ref[...] = h2.astype(BF16)
    logits = _dot_hi(h2, wr_ref[...])
    comb_t = _route(logits.T, rb_ref[...])
    tm = x1.shape[0]
    comb_ref[...] = jnp.concatenate([comb_t, jnp.zeros((LANES - N_EXPERTS, tm), F32)], axis=0).T


def _mix(st, layer, x, a_list, w_list, norm_g, wr_pad, rb_col):
    d, tm = st.d, st.tm
    row = lambda i: (i, 0)
    const = lambda i: (0, 0)
    n_in = len(a_list)
    in_specs = [pl.BlockSpec((tm, d), row), st.mod_spec(2)]
    in_specs += [pl.BlockSpec((tm, a.shape[1]), row) for a in a_list]
    in_specs += [pl.BlockSpec(w.shape, const) for w in w_list]
    in_specs += [pl.BlockSpec((1, d), const), st.mod_spec(4), st.mod_spec(3),
                 pl.BlockSpec((d, LANES), const), pl.BlockSpec((LANES, 1), const)]
    m = st.mod_arg(layer)
    return pl.pallas_call(
        functools.partial(_mix_kernel, n_in=n_in),
        grid=(st.ntiles,),
        in_specs=in_specs,
        out_specs=[pl.BlockSpec((tm, d), row), pl.BlockSpec((tm, d), row), pl.BlockSpec((tm, LANES), row)],
        out_shape=[jax.ShapeDtypeStruct((st.tokens, d), F32), jax.ShapeDtypeStruct((st.tokens, d), BF16),
                   jax.ShapeDtypeStruct((st.tokens, LANES), F32)],
        compiler_params=_cparams("parallel"),
        name="mix",
    )(x, m, *a_list, *w_list, norm_g.reshape(1, d), m, m, wr_pad, rb_col)


def _moe_kernel(h_ref, comb_ref, wg_ref, wu_ref, wd_ref, x1_ref, g2_ref, fg_ref, o_ref, acc_ref, *, final):
    e = pl.program_id(1)

    @pl.when(e == 0)
    def _():
        acc_ref[...] = jnp.zeros_like(acc_ref)

    h = h_ref[...]
    he = _silu(_dot(h, wg_ref[...])) * _dot(h, wu_ref[...])
    comb = comb_ref[...]
    lane = lax.broadcasted_iota(jnp.int32, comb.shape, 1)
    ce = jnp.sum(jnp.where(lane == e, comb, 0.0), axis=-1, keepdims=True)
    acc_ref[...] += ce * _dot(he.astype(BF16), wd_ref[...])

    @pl.when(e == pl.num_programs(1) - 1)
    def _():
        x2 = x1_ref[...] + g2_ref[...] * acc_ref[...]
        o_ref[...] = _rms(x2, fg_ref[...]) if final else x2


def _moe(st, layer, h2, comb, wg, wu, wd, x1, final_g, final):
    d, tm = st.d, st.tm
    ne, _, dff = wg.shape[1:]
    row = lambda i, e: (i, 0)
    return pl.pallas_call(
        functools.partial(_moe_kernel, final=final),
        grid=(st.ntiles, ne),
        in_specs=[pl.BlockSpec((tm, d), row), pl.BlockSpec((tm, LANES), row),
                  pl.BlockSpec((None, None, d, dff), lambda i, e: (layer, e, 0, 0)),
                  pl.BlockSpec((None, None, d, dff), lambda i, e: (layer, e, 0, 0)),
                  pl.BlockSpec((None, None, dff, d), lambda i, e: (layer, e, 0, 0)),
                  pl.BlockSpec((tm, d), row), st.mod_spec(5),
                  pl.BlockSpec((1, d), lambda i, e: (0, 0))],
        out_specs=pl.BlockSpec((tm, d), row),
        out_shape=jax.ShapeDtypeStruct((st.tokens, d), F32),
        scratch_shapes=[pltpu.VMEM((tm, d), F32)],
        compiler_params=_cparams("parallel", "arbitrary"),
        name="moe",
    )(h2, comb, wg, wu, wd, x1, st.mod_arg(layer), final_g.reshape(1, d))


def _kmean_kernel(k_ref, o_ref):
    nblk = o_ref.shape[0]
    k = k_ref[...]
    o_ref[...] = jnp.concatenate(
        [jnp.sum(k[j * MOBA_BLOCK:(j + 1) * MOBA_BLOCK, :], axis=0, keepdims=True) for j in range(nblk)],
        axis=0) * (1.0 / MOBA_BLOCK)


def _kmean(k, nseq, seqlen):
    nblk = seqlen // MOBA_BLOCK
    return pl.pallas_call(
        _kmean_kernel,
        grid=(nseq,),
        in_specs=[pl.BlockSpec((seqlen, C_KVW), lambda b: (b, 0))],
        out_specs=pl.BlockSpec((None, nblk, C_KVW), lambda b: (b, 0, 0)),
        out_shape=jax.ShapeDtypeStruct((nseq, nblk, C_KVW), F32),
        compiler_params=_cparams("parallel"),
        name="kmean",
    )(k)


def _moba_prompt_kernel(q_ref, k_ref, v_ref, km_ref, o_ref):
    i = pl.program_id(1)
    qb = MOBA_QBLOCK
    blk = MOBA_BLOCK
    nblk = km_ref.shape[0]
    own = (i * qb) // blk
    q_off = i * qb - own * blk
    q = q_ref[...]
    km = km_ref[...]
    rows = C_GROUP * qb
    lane = lax.broadcasted_iota(jnp.int32, (rows, LANES), 1)
    past = lane < own
    r_i = lax.broadcasted_iota(jnp.int32, (rows, blk), 0)
    c_i = lax.broadcasted_iota(jnp.int32, (rows, blk), 1)
    own_ok = c_i <= (r_i % qb) + q_off
    own_start = pl.multiple_of(own * blk, blk)
    outs = []
    for kv in range(C_KV_HEADS):
        lo, hi = kv * HEAD_DIM, (kv + 1) * HEAD_DIM
        qs = jnp.concatenate(
            [q[:, (kv * C_GROUP + g) * HEAD_DIM:(kv * C_GROUP + g + 1) * HEAD_DIM] for g in range(C_GROUP)], axis=0)
        kmp = jnp.concatenate([km[:, lo:hi], jnp.zeros((LANES - nblk, HEAD_DIM), F32)], axis=0)
        sg = jnp.where(past, _dot_hi_nt(qs, kmp), NEG_INF)
        rank = jnp.zeros((rows, LANES), jnp.int32)
        for j in range(nblk):
            cj = sg[:, j:j + 1]
            rank = rank + ((cj > sg) | ((cj == sg) & (lane > j))).astype(jnp.int32)
        sel = (past & (rank < MOBA_TOPK)).astype(F32)
        qsb = qs.astype(BF16)

        s = _dot_nt(qsb, k_ref[pl.ds(own_start, blk), lo:hi].astype(BF16)) * ATT_SCALE
        s = jnp.where(own_ok, s, NEG_INF)
        m0 = jnp.max(s, axis=-1, keepdims=True)
        p = jnp.exp(s - m0)
        l0 = jnp.sum(p, axis=-1, keepdims=True)
        acc0 = _dot(p.astype(BF16), v_ref[pl.ds(own_start, blk), lo:hi].astype(BF16))

        def body(j, carry, lo=lo, hi=hi, qsb=qsb, sel=sel):
            m, l, acc = carry
            start = pl.multiple_of(j * blk, blk)
            chosen = jnp.sum(jnp.where(lane == j, sel, 0.0), axis=-1, keepdims=True) > 0.0
            s = _dot_nt(qsb, k_ref[pl.ds(start, blk), lo:hi].astype(BF16)) * ATT_SCALE
            s = jnp.where(chosen, s, NEG_INF)
            m_new = jnp.maximum(m, jnp.max(s, axis=-1, keepdims=True))
            alpha = jnp.exp(m - m_new)
            p = jnp.exp(s - m_new)
            l = alpha * l + jnp.sum(p, axis=-1, keepdims=True)
            acc = alpha * acc + _dot(p.astype(BF16), v_ref[pl.ds(start, blk), lo:hi].astype(BF16))
            return m_new, l, acc

        _, l, acc = lax.fori_loop(0, own, body, (m0, l0, acc0))
        o = acc / l
        outs += [o[g * qb:(g + 1) * qb, :] for g in range(C_GROUP)]
    o_ref[...] = jnp.concatenate(outs, axis=-1)


def _moba_prompt(q, k, v, kmean, nseq, seqlen):
    qb = MOBA_QBLOCK
    nq = seqlen // qb
    nblk = kmean.shape[1]
    seq = lambda b, i: (b, 0)
    return pl.pallas_call(
        _moba_prompt_kernel,
        grid=(nseq, nq),
        in_specs=[pl.BlockSpec((qb, C_QW), lambda b, i: (b * nq + i, 0)),
                  pl.BlockSpec((seqlen, C_KVW), seq), pl.BlockSpec((seqlen, C_KVW), seq),
                  pl.BlockSpec((None, nblk, C_KVW), lambda b, i: (b, 0, 0))],
        out_specs=pl.BlockSpec((qb, C_QW), lambda b, i: (b * nq + i, 0)),
        out_shape=jax.ShapeDtypeStruct(q.shape, F32),
        compiler_params=_cparams("parallel", "parallel"),
        name="moba_prompt",
    )(q, k, v, kmean)


def _swa_sample_kernel(sink_ref, q_ref, kn_ref, vn_ref, wk_ref, wv_ref, o_ref, nk_ref, nv_ref):
    q = q_ref[...]
    kn, vn = kn_ref[...], vn_ref[...]
    wk, wv = wk_ref[...], wv_ref[...]
    tb, w = wk.shape[0], wk.shape[1]
    nk_ref[:, 0:w - 1, :] = wk[:, 1:w, :]
    nk_ref[:, w - 1:w, :] = kn[:, None, :]
    nv_ref[:, 0:w - 1, :] = wv[:, 1:w, :]
    nv_ref[:, w - 1:w, :] = vn[:, None, :]
    lane = lax.broadcasted_iota(jnp.int32, (tb, LANES), 1)
    lane3 = lax.broadcasted_iota(jnp.int32, (tb, w, LANES), 2)
    heads = [None] * A_HEADS
    for g in range(A_GROUP):
        qg = jnp.concatenate(
            [q[:, (kv * A_GROUP + g) * HEAD_DIM:(kv * A_GROUP + g + 1) * HEAD_DIM] for kv in range(A_KV_HEADS)], axis=-1)
        prod = wk * qg[:, None, :]
        prod_n = kn * qg
        p_full = None
        pn_full = None
        for kv in range(A_KV_HEADS):
            lo, hi = kv * HEAD_DIM, (kv + 1) * HEAD_DIM
            sink = sink_ref[kv * A_GROUP + g]
            s = jnp.sum(prod[:, :, lo:hi], axis=-1, keepdims=True) * ATT_SCALE
            sn = jnp.sum(prod_n[:, lo:hi], axis=-1, keepdims=True) * ATT_SCALE
            m = jnp.maximum(jnp.maximum(jnp.max(s, axis=1), sn), sink)
            p = jnp.exp(s - m[:, None, :])
            pn = jnp.exp(sn - m)
            inv = 1.0 / (jnp.sum(p, axis=1) + pn + jnp.exp(sink - m))
            p = p * inv[:, None, :]
            pn = pn * inv
            pb = jnp.broadcast_to(p, (tb, w, LANES))
            pnb = jnp.broadcast_to(pn, (tb, LANES))
            if kv == 0:
                p_full, pn_full = pb, pnb
            else:
                p_full = jnp.where(lane3 >= lo, pb, p_full)
                pn_full = jnp.where(lane >= lo, pnb, pn_full)
        og = jnp.sum(p_full * wv, axis=1) + pn_full * vn
        for kv in range(A_KV_HEADS):
            heads[kv * A_GROUP + g] = og[:, kv * HEAD_DIM:(kv + 1) * HEAD_DIM]
    o_ref[...] = jnp.concatenate(heads, axis=-1)


def _swa_sample(q, kn, vn, win_k, win_v, sinks):
    db, w = win_k.shape[0], win_k.shape[1]
    tb = math.gcd(db, 8)
    row = lambda i: (i, 0)
    row3 = lambda i: (i, 0, 0)
    return pl.pallas_call(
        _swa_sample_kernel,
        grid=(db // tb,),
        in_specs=[pl.BlockSpec(memory_space=pltpu.SMEM),
                  pl.BlockSpec((tb, A_QW), row), pl.BlockSpec((tb, A_KVW), row), pl.BlockSpec((tb, A_KVW), row),
                  pl.BlockSpec((tb, w, A_KVW), row3), pl.BlockSpec((tb, w, A_KVW), row3)],
        out_specs=[pl.BlockSpec((tb, A_QW), row), pl.BlockSpec((tb, w, A_KVW), row3),
                   pl.BlockSpec((tb, w, A_KVW), row3)],
        out_shape=[jax.ShapeDtypeStruct((db, A_QW), F32), jax.ShapeDtypeStruct(win_k.shape, F32),
                   jax.ShapeDtypeStruct(win_v.shape, F32)],
        compiler_params=_cparams("parallel"),
        name="swa_sample",
    )(sinks, q, kn, vn, win_k, win_v)


def _ssd_sample_pre_kernel(xbc_ref, cst_ref, dt_ref, cw_ref, cb_ref, dtb_ref, alog_ref, exp_ref,
                           xs_ref, bm_ref, cm_ref, xd_ref, dec_ref, cnew_ref):
    xn = xbc_ref[...]
    halo = MB_CONV - 1
    acc = cb_ref[...] + xn * cw_ref[halo:halo + 1, :]
    for tap in range(halo):
        acc = acc + cst_ref[tap] * cw_ref[tap:tap + 1, :]
    for tap in range(1, halo):
        cnew_ref[tap - 1] = cst_ref[tap]
    cnew_ref[halo - 1] = xn
    xbc = _silu(acc)
    xs = xbc[:, :MB_INNER]
    xs_ref[...] = xs
    bm_ref[...] = xbc[:, MB_INNER:MB_INNER + MB_GN]
    cm_ref[...] = xbc[:, MB_INNER + MB_GN:]
    dt = _softplus(dt_ref[...] + dtb_ref[...])
    da = dt * (-jnp.exp(alog_ref[...]))
    expand = exp_ref[...]
    xd_ref[...] = xs * _dot_exact_lhs_rhs(dt, expand)
    dec_ref[...] = jnp.exp(_dot_exact_lhs_rhs(da, expand))


def _dot_exact_lhs_rhs(a, b_bf16):
    a0 = a.astype(BF16)
    r = a - a0.astype(F32)
    a1 = r.astype(BF16)
    a2 = (r - a1.astype(F32)).astype(BF16)
    return _dot(a0, b_bf16) + (_dot(a1, b_bf16) + _dot(a2, b_bf16))


def _ssd_sample_state_kernel(h_ref, xd_ref, dec_ref, bm_ref, cm_ref, hn_ref, y_ref):
    tb = h_ref.shape[0]
    rpg = MB_HEADS // MB_GROUPS
    for t in range(tb):
        for h in range(MB_HEADS):
            g = h // rpg
            rows = slice(h * MB_HEADDIM, (h + 1) * MB_HEADDIM)
            hn = h_ref[t, h] * dec_ref[t, rows, :] + xd_ref[t, rows, :] * bm_ref[t, g:g + 1, :]
            hn_ref[t, h] = hn
            y_ref[t, rows, :] = jnp.sum(hn * cm_ref[t, g:g + 1, :], axis=-1, keepdims=True)


def _ssd_sample_post_kernel(y_ref, xs_ref, z_ref, dskip_ref, ng_ref, o_ref):
    y = (y_ref[...] + xs_ref[...] * dskip_ref[...]) * _silu(z_ref[...])
    gw = MB_INNER // MB_GROUPS
    ng = ng_ref[...]
    o_ref[...] = jnp.concatenate(
        [_rms(y[:, g * gw:(g + 1) * gw], ng[:, g * gw:(g + 1) * gw]) for g in range(MB_GROUPS)], axis=-1)


def _ssd_sample(xbc, z, dt, conv_state, h0, conv_w, conv_b, dt_bias, a_log, d_skip, norm_g):
    db = xbc.shape[0]
    halo = MB_CONV - 1
    expand = (jnp.arange(LANES)[:, None] == (jnp.arange(MB_INNER) // MB_HEADDIM)[None, :]).astype(BF16)
    full = lambda *shape: pl.BlockSpec(shape, lambda: (0,) * len(shape))
    xs, bm, cm, xd, dec, conv_new = pl.pallas_call(
        _ssd_sample_pre_kernel,
        in_specs=[full(db, MB_CONV_DIM), full(halo, db, MB_CONV_DIM), full(db, LANES),
                  full(MB_CONV, MB_CONV_DIM), full(1, MB_CONV_DIM), full(1, LANES), full(1, LANES),
                  full(LANES, MB_INNER)],
        out_specs=[full(db, MB_INNER), full(db, MB_GN), full(db, MB_GN), full(db, MB_INNER), full(db, MB_INNER),
                   full(halo, db, MB_CONV_DIM)],
        out_shape=[jax.ShapeDtypeStruct((db, MB_INNER), F32), jax.ShapeDtypeStruct((db, MB_GN), F32),
                   jax.ShapeDtypeStruct((db, MB_GN), F32), jax.ShapeDtypeStruct((db, MB_INNER), F32),
                   jax.ShapeDtypeStruct((db, MB_INNER), F32), jax.ShapeDtypeStruct((halo, db, MB_CONV_DIM), F32)],
        compiler_params=pltpu.CompilerParams(vmem_limit_bytes=VMEM_LIMIT_BYTES),
        name="ssd_sample_pre",
    )(xbc, jnp.swapaxes(conv_state, 0, 1), dt, conv_w, conv_b.reshape(1, -1), _pad_lanes(dt_bias), _pad_lanes(a_log), expand)

    tb = math.gcd(db, 8)
    r3 = lambda i: (i, 0, 0)
    r4 = lambda i: (i, 0, 0, 0)
    h_new, y_col = pl.pallas_call(
        _ssd_sample_state_kernel,
        grid=(db // tb,),
        in_specs=[pl.BlockSpec((tb, MB_HEADS, MB_HEADDIM, MB_DSTATE), r4),
                  pl.BlockSpec((tb, MB_INNER, 1), r3), pl.BlockSpec((tb, MB_INNER, 1), r3),
                  pl.BlockSpec((tb, MB_GROUPS, MB_DSTATE), r3), pl.BlockSpec((tb, MB_GROUPS, MB_DSTATE), r3)],
        out_specs=[pl.BlockSpec((tb, MB_HEADS, MB_HEADDIM, MB_DSTATE), r4), pl.BlockSpec((tb, MB_INNER, 1), r3)],
        out_shape=[jax.ShapeDtypeStruct(h0.shape, F32), jax.ShapeDtypeStruct((db, MB_INNER, 1), F32)],
        compiler_params=_cparams("parallel"),
        name="ssd_sample_state",
    )(h0, xd.reshape(db, MB_INNER, 1), dec.reshape(db, MB_INNER, 1),
      bm.reshape(db, MB_GROUPS, MB_DSTATE), cm.reshape(db, MB_GROUPS, MB_DSTATE))

    o_b = pl.pallas_call(
        _ssd_sample_post_kernel,
        in_specs=[full(db, MB_INNER), full(db, MB_INNER), full(db, MB_INNER), full(1, MB_INNER), full(1, MB_INNER)],
        out_specs=full(db, MB_INNER),
        out_shape=jax.ShapeDtypeStruct((db, MB_INNER), F32),
        name="ssd_sample_post",
    )(y_col.reshape(db, MB_INNER), xs, z, jnp.repeat(d_skip, MB_HEADDIM).reshape(1, -1), norm_g.reshape(1, -1))
    return o_b, jnp.swapaxes(conv_new, 0, 1), h_new


def _page_ksum_kernel(k_ref, o_ref):
    o_ref[...] = jnp.sum(k_ref[...], axis=1)


def _page_ksum(cache_k3, layer):
    _, n_pool, ps, w = cache_k3.shape
    tp = math.gcd(n_pool, 32)
    return pl.pallas_call(
        _page_ksum_kernel,
        grid=(n_pool // tp,),
        in_specs=[pl.BlockSpec((None, tp, ps, w), lambda i: (layer, i, 0, 0))],
        out_specs=pl.BlockSpec((tp, w), lambda i: (i, 0)),
        out_shape=jax.ShapeDtypeStruct((n_pool, w), F32),
        compiler_params=_cparams("parallel"),
        name="page_ksum",
    )(cache_k3)


def _block_kmean_kernel(pt_ref, ksum_ref, o_ref, *, pages_per_block):
    b = pl.program_id(0)
    nblk = o_ref.shape[0]

    def body(j, _):
        acc = ksum_ref[pl.ds(pt_ref[b, j * pages_per_block], 1), :]
        for e in range(1, pages_per_block):
            acc = acc + ksum_ref[pl.ds(pt_ref[b, j * pages_per_block + e], 1), :]
        o_ref[pl.ds(j, 1), :] = acc * (1.0 / MOBA_BLOCK)
        return 0

    lax.fori_loop(0, nblk, body, 0)


def _block_kmean(page_table, ksum):
    db, n_pages = page_table.shape
    ppb = MOBA_BLOCK // PAGE_SIZE
    nblk = n_pages // ppb
    w = ksum.shape[1]
    return pl.pallas_call(
        functools.partial(_block_kmean_kernel, pages_per_block=ppb),
        grid_spec=pltpu.PrefetchScalarGridSpec(
            num_scalar_prefetch=1,
            grid=(db,),
            in_specs=[pl.BlockSpec(memory_space=pltpu.VMEM)],
            out_specs=pl.BlockSpec((None, nblk, w), lambda b, pt: (b, 0, 0))),
        out_shape=jax.ShapeDtypeStruct((db, nblk, w), F32),
        compiler_params=_cparams("arbitrary"),
        name="block_kmean",
    )(page_table, ksum)


def _moba_gate_sample_kernel(q_ref, km_ref, idx_ref):
    q = q_ref[...]
    km = km_ref[...]
    tb, nblk = km.shape[0], km.shape[1]
    blk_i = lax.broadcasted_iota(jnp.int32, (tb, nblk, 1), 1)
    lane = lax.broadcasted_iota(jnp.int32, (tb, LANES), 1)
    out = jnp.zeros((tb, LANES), jnp.int32)
    for g in range(C_GROUP):
        qg = jnp.concatenate(
            [q[:, (kv * C_GROUP + g) * HEAD_DIM:(kv * C_GROUP + g + 1) * HEAD_DIM] for kv in range(C_KV_HEADS)], axis=-1)
        prod = km * qg[:, None, :]
        for kv in range(C_KV_HEADS):
            h = kv * C_GROUP + g
            s = jnp.sum(prod[:, :, kv * HEAD_DIM:(kv + 1) * HEAD_DIM], axis=-1, keepdims=True)
            for k in range(MOBA_TOPK):
                m = jnp.max(s, axis=1, keepdims=True)
                idx = jnp.min(jnp.where(s == m, blk_i, nblk), axis=1)
                out = jnp.where(lane == h * MOBA_TOPK + k, idx, out)
                s = jnp.where(blk_i == idx[:, None, :], -jnp.inf, s)
    idx_ref[...] = out


def _moba_gate_sample(q, kmean):
    db, nblk, w = kmean.shape
    tb = math.gcd(db, 8)
    return pl.pallas_call(
        _moba_gate_sample_kernel,
        grid=(db // tb,),
        in_specs=[pl.BlockSpec((tb, C_QW), lambda i: (i, 0)), pl.BlockSpec((tb, nblk, w), lambda i: (i, 0, 0))],
        out_specs=pl.BlockSpec((tb, LANES), lambda i: (i, 0)),
        out_shape=jax.ShapeDtypeStruct((db, LANES), jnp.int32),
        compiler_params=_cparams("parallel"),
        name="moba_gate_sample",
    )(q, kmean)


def _moba_sample_kernel(idx_ref, pt_ref, q_ref, kn_ref, vn_ref, ck_ref, cv_ref, o_ref, kbuf, vbuf, sem,
                        *, pages_per_block, layer):
    b = pl.program_id(0)
    ppb = pages_per_block
    npg = MOBA_TOPK * ppb
    ps = PAGE_SIZE

    def copies(h):
        kvh = h // C_GROUP
        col = (kvh // 2) * LANES
        out = []
        for k in range(MOBA_TOPK):
            blk = idx_ref[b, h * MOBA_TOPK + k]
            for e in range(ppb):
                page = pt_ref[b, blk * ppb + e]
                slot = k * ppb + e
                out.append(pltpu.make_async_copy(ck_ref.at[layer, page, :, pl.ds(col, LANES)],
                                                 kbuf.at[h, pl.ds(slot * ps, ps), :], sem.at[0, h]))
                out.append(pltpu.make_async_copy(cv_ref.at[layer, page, :, pl.ds(col, LANES)],
                                                 vbuf.at[h, pl.ds(slot * ps, ps), :], sem.at[1, h]))
        return out

    for h in range(C_HEADS):
        for cp in copies(h):
            cp.start()

    q = q_ref[0]
    kn = kn_ref[0]
    vn = vn_ref[0]
    lane = lax.broadcasted_iota(jnp.int32, (1, LANES), 1)
    outs = []
    for h in range(C_HEADS):
        for cp in copies(h):
            cp.wait()
        kvh = h // C_GROUP
        half = kvh % 2
        qh = q[:, h * HEAD_DIM:(h + 1) * HEAD_DIM]
        knh = kn[:, kvh * HEAD_DIM:(kvh + 1) * HEAD_DIM]
        vnh = vn[:, kvh * HEAD_DIM:(kvh + 1) * HEAD_DIM]
        zero = jnp.zeros_like(qh)
        q2 = jnp.concatenate([zero, qh] if half else [qh, zero], axis=-1)
        in_half = (lane >= HEAD_DIM) if half else (lane < HEAD_DIM)
        kb = jnp.where(in_half, kbuf[h], 0.0)
        s = jnp.sum(kb * q2, axis=-1, keepdims=True) * ATT_SCALE
        sn = jnp.sum(qh * knh, axis=-1, keepdims=True) * ATT_SCALE
        m = jnp.maximum(jnp.max(s, axis=0, keepdims=True), sn)
        p = jnp.exp(s - m)
        pn = jnp.exp(sn - m)
        inv = 1.0 / (jnp.sum(p, axis=0, keepdims=True) + pn)
        vb = jnp.where(in_half, vbuf[h], 0.0)
        o2 = jnp.sum((p * inv) * vb, axis=0, keepdims=True)
        oh = o2[:, HEAD_DIM:] if half else o2[:, :HEAD_DIM]
        outs.append(oh + (pn * inv) * vnh)
    o_ref[0] = jnp.concatenate(outs, axis=-1)


def _moba_sample(idx, page_table, q, kn, vn, cache_k3, cache_v3, layer):
    db = q.shape[0]
    ppb = MOBA_BLOCK // PAGE_SIZE
    rows = MOBA_TOPK * MOBA_BLOCK
    r3 = lambda b, *_: (b, 0, 0)
    o = pl.pallas_call(
        functools.partial(_moba_sample_kernel, pages_per_block=ppb, layer=layer),
        grid_spec=pltpu.PrefetchScalarGridSpec(
            num_scalar_prefetch=2,
            grid=(db,),
            in_specs=[pl.BlockSpec((1, 1, C_QW), r3), pl.BlockSpec((1, 1, C_KVW), r3), pl.BlockSpec((1, 1, C_KVW), r3),
                      pl.BlockSpec(memory_space=pl.ANY), pl.BlockSpec(memory_space=pl.ANY)],
            out_specs=pl.BlockSpec((1, 1, C_QW), r3),
            scratch_shapes=[pltpu.VMEM((C_HEADS, rows, LANES), F32), pltpu.VMEM((C_HEADS, rows, LANES), F32),
                            pltpu.SemaphoreType.DMA((2, C_HEADS))]),
        out_shape=jax.ShapeDtypeStruct((db, 1, C_QW), F32),
        compiler_params=_cparams("arbitrary"),
        name="moba_sample",
    )(idx, page_table, q.reshape(db, 1, C_QW), kn.reshape(db, 1, C_KVW), vn.reshape(db, 1, C_KVW), cache_k3, cache_v3)
    return o.reshape(db, C_QW)


def _decoder(st, x, pos, p, state):
    is_prompt = state is None
    nseq, seqlen, d = st.nseq, st.seqlen, st.d
    depth = p['w_ada'].shape[0]
    rope_tabs = _rope_tables(pos if is_prompt else jnp.broadcast_to(pos, (st.tokens,)))
    new = {'win_k': [], 'win_v': [], 'conv': [], 'ssm': [], 'k': [], 'v': []}
    for layer in range(depth):
        g_norm = p['norm_g'][layer]
        if layer % 2 == 0:
            i = layer // 2
            splits = ((0, A_QW, True), (A_QW, A_KVW, True), (A_QW + A_KVW, A_KVW, False),
                      (A_QW + 2 * A_KVW, MB_INNER, False), (A_QW + 2 * A_KVW + MB_INNER, MB_CONV_DIM, False),
                      (A_QW + 2 * A_KVW + MB_INNER + MB_CONV_DIM, LANES, False))
            qa, ka, va, z, xbc, dt = _proj(st, layer, x, g_norm[0], p['w_in_a'][i], rope_tabs, splits)
            if is_prompt:
                o_a = _swa_prompt(qa, ka, va, p['sinks'][i], nseq, seqlen)
                wk = ka.reshape(nseq, seqlen, A_KV_HEADS, HEAD_DIM)[:, -WINDOW:]
                wv = va.reshape(nseq, seqlen, A_KV_HEADS, HEAD_DIM)[:, -WINDOW:]
                conv_prev = jnp.zeros((nseq, MB_CONV - 1, MB_CONV_DIM), F32)
                h0 = jnp.zeros((nseq, MB_HEADS, MB_HEADDIM, MB_DSTATE), F32)
                o_b, conv_new, h_new = _ssd_prompt(xbc, z, dt, conv_prev, h0, p['conv_w'][i], p['conv_b'][i],
                                                   p['dt_bias'][i], p['a_log'][i], p['d_skip'][i],
                                                   p['ssm_norm_g'][i], nseq, seqlen)
            else:
                o_a, wk, wv = _swa_sample(qa, ka, va, state['win_k'][i].reshape(nseq, WINDOW, A_KVW),
                                          state['win_v'][i].reshape(nseq, WINDOW, A_KVW), p['sinks'][i])
                wk = wk.reshape(nseq, WINDOW, A_KV_HEADS, HEAD_DIM)
                wv = wv.reshape(nseq, WINDOW, A_KV_HEADS, HEAD_DIM)
                o_b, conv_new, h_new = _ssd_sample(xbc, z, dt, state['conv'][i], state['ssm'][i], p['conv_w'][i],
                                                   p['conv_b'][i], p['dt_bias'][i], p['a_log'][i], p['d_skip'][i],
                                                   p['ssm_norm_g'][i])
            a_list = [o_a, o_b]
            w_list = [p['w_out_a'][i][:A_QW], p['w_out_a'][i][A_QW:]]
            new['win_k'].append(wk)
            new['win_v'].append(wv)
            new['conv'].append(conv_new)
            new['ssm'].append(h_new)
        else:
            j = layer // 2
            splits = ((0, C_QW, True), (C_QW, C_KVW, True), (C_QW + C_KVW, C_KVW, False))
            qc, kc, vc = _proj(st, layer, x, g_norm[0], p['w_in_c'][j], rope_tabs, splits)
            if is_prompt:
                kmean = _kmean(kc, nseq, seqlen)
                o_c = _moba_prompt(qc, kc, vc, kmean, nseq, seqlen)
            else:
                kmean = _block_kmean(state['page_table'], state['page_ksum'][j])
                idx = _moba_gate_sample(qc, kmean)
                o_c = _moba_sample(idx, state['page_table'], qc, kc, vc, state['cache_k'], state['cache_v'], j)
            a_list = [o_c]
            w_list = [p['w_out_c'][j]]
            new['k'].append(kc.reshape(nseq, seqlen, C_KV_HEADS, HEAD_DIM))
            new['v'].append(vc.reshape(nseq, seqlen, C_KV_HEADS, HEAD_DIM))
        x1, h2, comb = _mix(st, layer, x, a_list, w_list, g_norm[1], p['wr_pad'], p['rb_col'])
        x = _moe(st, layer, h2, comb, p['w_gate'], p['w_up'], p['w_down'], x1, p['final_norm_g'],
                 final=(layer == depth - 1))
    return x.reshape(nseq, seqlen, d), {name: jnp.stack(rows) for name, rows in new.items()}


def kernel(x_prompt, x_sample, c_prompt, c_sample, state_win_k, state_win_v, state_conv, state_ssm, cache_k, cache_v, page_table, w_ada, b_ada, norm_g, w_in_a, sinks, conv_w, conv_b, dt_bias, a_log, d_skip, ssm_norm_g, w_out_a, w_in_c, w_out_c, w_router, router_bias, w_gate, w_up, w_down, final_norm_g):
    nb, seqlen, d = x_prompt.shape
    db, dec_seq, _ = x_sample.shape
    assert dec_seq == 1 and seqlen % MOBA_BLOCK == 0 and d % LANES == 0
    n_odd, n_pool, page_size, ckv, hd = cache_k.shape
    assert page_size == PAGE_SIZE and ckv == C_KV_HEADS and hd == HEAD_DIM
    n_pages = page_table.shape[1]

    n_in_a = w_in_a.shape[2]
    pad_a = (-n_in_a) % LANES
    p = {'w_ada': w_ada, 'norm_g': norm_g, 'sinks': sinks, 'conv_w': conv_w, 'conv_b': conv_b, 'dt_bias': dt_bias,
         'a_log': a_log, 'd_skip': d_skip, 'ssm_norm_g': ssm_norm_g, 'final_norm_g': final_norm_g,
         'w_in_a': jnp.pad(w_in_a, ((0, 0), (0, 0), (0, pad_a))).astype(BF16),
         'w_out_a': w_out_a.astype(BF16), 'w_in_c': w_in_c.astype(BF16), 'w_out_c': w_out_c.astype(BF16),
         'w_gate': w_gate.astype(BF16), 'w_up': w_up.astype(BF16), 'w_down': w_down.astype(BF16),
         'wr_pad': jnp.pad(w_router, ((0, 0), (0, LANES - N_EXPERTS))),
         'rb_col': jnp.pad(router_bias, (0, LANES - N_EXPERTS)).reshape(LANES, 1)}

    mod = _ada(jnp.concatenate([c_prompt, c_sample], axis=0), w_ada, b_ada)
    st_p = _Stream(nb, seqlen, d, mod[:, :nb])
    st_s = _Stream(db, 1, d, mod[:, nb:])

    pos_p = jnp.arange(seqlen, dtype=jnp.int32)
    y_prompt, pn = _decoder(st_p, x_prompt.reshape(nb * seqlen, d), pos_p, p, None)

    cache_k3 = cache_k.reshape(n_odd, n_pool, PAGE_SIZE, C_KVW)
    cache_v3 = cache_v.reshape(n_odd, n_pool, PAGE_SIZE, C_KVW)
    state = {'win_k': state_win_k, 'win_v': state_win_v, 'conv': state_conv, 'ssm': state_ssm,
             'cache_k': cache_k3, 'cache_v': cache_v3, 'page_table': page_table,
             'page_ksum': [_page_ksum(cache_k3, j) for j in range(n_odd)]}
    pos_s = n_pages * PAGE_SIZE + jnp.arange(1, dtype=jnp.int32)
    y_sample, sn = _decoder(st_s, x_sample.reshape(db, d), pos_s, p, state)
    return (y_prompt, y_sample, pn['win_k'], pn['win_v'], pn['conv'], pn['ssm'], pn['k'], pn['v'],
            sn['win_k'], sn['win_v'], sn['conv'], sn['ssm'], sn['k'], sn['v'])
```

```python
import functools
import math

import jax
import jax.numpy as jnp
import numpy as np
from jax import lax
from jax.experimental import pallas as pl
from jax.experimental.pallas import tpu as pltpu

F32 = jnp.float32
BF16 = jnp.bfloat16

HEAD_DIM = 64
ROT_DIM = HEAD_DIM // 4
ROPE_THETA = 500000.0
A_HEADS = 8
A_KV_HEADS = 2
A_GROUP = A_HEADS // A_KV_HEADS
WINDOW = 128
MB_HEADDIM = 64
MB_HEADS = 8
MB_INNER = MB_HEADS * MB_HEADDIM
MB_GROUPS = 2
MB_DSTATE = 128
MB_CONV = 4
MB_GN = MB_GROUPS * MB_DSTATE
MB_CONV_DIM = MB_INNER + 2 * MB_GN
SSD_CHUNK = 128
C_HEADS = 16
C_KV_HEADS = 4
C_GROUP = C_HEADS // C_KV_HEADS
MOBA_BLOCK = 256
MOBA_TOPK = 3
MOBA_QBLOCK = 128
PAGE_SIZE = 128
A_QW = A_HEADS * HEAD_DIM
A_KVW = A_KV_HEADS * HEAD_DIM
C_QW = C_HEADS * HEAD_DIM
C_KVW = C_KV_HEADS * HEAD_DIM
N_EXPERTS = 16
N_EXPERT_GROUPS = 4
EXPERTS_PER_GROUP = N_EXPERTS // N_EXPERT_GROUPS
PAIRS_PER_GROUP = EXPERTS_PER_GROUP * (EXPERTS_PER_GROUP - 1) // 2
N_BUCKETS = N_EXPERT_GROUPS * PAIRS_PER_GROUP
BUCKET_ROWS = 32
ROUTE_ROWS = 8
SORT_TILE = 256
RMS_EPS = 1e-6
NEG_INF = -1e30
ATT_SCALE = HEAD_DIM ** -0.5

LANES = 128
VMEM_LIMIT_BYTES = 56 * 1024 * 1024


def _cparams(*sem):
    return pltpu.CompilerParams(dimension_semantics=sem, vmem_limit_bytes=VMEM_LIMIT_BYTES)


def _dot(a, b):
    return jnp.dot(a, b, preferred_element_type=F32)


def _dot_nt(a, b):
    return lax.dot_general(a, b, (((1,), (1,)), ((), ())), preferred_element_type=F32)


def _dot_tn(a, b):
    return lax.dot_general(a, b, (((0,), (0,)), ((), ())), preferred_element_type=F32)


def _split2(x):
    hi = x.astype(BF16)
    lo = (x - hi.astype(F32)).astype(BF16)
    return hi, lo


def _dot_hi(a, b):
    ah, al = _split2(a)
    bh, bl = _split2(b)
    return _dot(ah, bh) + (_dot(al, bh) + _dot(ah, bl))


def _dot_hi_nt(a, b):
    ah, al = _split2(a)
    bh, bl = _split2(b)
    return _dot_nt(ah, bh) + (_dot_nt(al, bh) + _dot_nt(ah, bl))


def _dot_exact_lhs(a_bf16, b):
    b0 = b.astype(BF16)
    r = b - b0.astype(F32)
    b1 = r.astype(BF16)
    b2 = (r - b1.astype(F32)).astype(BF16)
    return _dot(a_bf16, b0) + (_dot(a_bf16, b1) + _dot(a_bf16, b2))


def _sigmoid(x):
    return 1.0 / (1.0 + jnp.exp(-x))


def _silu(x):
    return x * _sigmoid(x)


def _softplus(x):
    return jnp.maximum(x, 0.0) + jnp.log(1.0 + jnp.exp(-jnp.abs(x)))


def _rms(x, g):
    return x * lax.rsqrt(jnp.mean(x * x, axis=-1, keepdims=True) + RMS_EPS) * g


def _rope_tables(pos):
    half = ROT_DIM // 2
    inv_freq = jnp.power(ROPE_THETA, -jnp.arange(half, dtype=F32) / half)
    ang = pos.astype(F32)[:, None] * inv_freq
    cos, sin = jnp.cos(ang), jnp.sin(ang)
    n = pos.shape[0]
    rest = HEAD_DIM - ROT_DIM
    ct = jnp.concatenate([cos, cos, jnp.ones((n, rest), F32)], axis=1)
    sa = jnp.concatenate([jnp.zeros((n, half), F32), sin, jnp.zeros((n, rest), F32)], axis=1)
    sb = jnp.concatenate([-sin, jnp.zeros((n, half), F32), jnp.zeros((n, rest), F32)], axis=1)
    rep = LANES // HEAD_DIM
    return jnp.tile(ct, (1, rep)), jnp.tile(sa, (1, rep)), jnp.tile(sb, (1, rep))


def _rope(x, ct, sa, sb):
    half = ROT_DIM // 2
    return x * ct + pltpu.roll(x, half, 1) * sa + pltpu.roll(x, LANES - half, 1) * sb


def _ada_kernel(c_ref, w_ref, b_ref, o_ref):
    o_ref[...] = _dot_hi(_silu(c_ref[...]), w_ref[...]) + b_ref[...]


def _ada(c_all, w_ada, b_ada):
    depth, d, n6 = w_ada.shape
    nc = c_all.shape[0]
    nk = n6 // d
    return pl.pallas_call(
        _ada_kernel,
        grid=(depth, nk),
        in_specs=[pl.BlockSpec((nc, d), lambda l, k: (0, 0)),
                  pl.BlockSpec((None, d, d), lambda l, k: (l, 0, k)),
                  pl.BlockSpec((None, 1, d), lambda l, k: (l, 0, k))],
        out_specs=pl.BlockSpec((None, nc, d), lambda l, k: (l, 0, k)),
        out_shape=jax.ShapeDtypeStruct((depth, nc, n6), F32),
        compiler_params=_cparams("parallel", "parallel"),
        name="ada",
    )(c_all, w_ada, b_ada.reshape(depth, 1, n6))


class _Stream:
    def __init__(self, nseq, seqlen, d, mod):
        self.nseq, self.seqlen, self.d = nseq, seqlen, d
        self.tokens = nseq * seqlen
        if seqlen == 1:
            self.tm = self.tokens
            self.tiles_per_seq = None
            self.mod = mod
        else:
            self.tm = math.gcd(seqlen, 512)
            self.tiles_per_seq = seqlen // self.tm
            depth = mod.shape[0]
            self.mod = mod.reshape(depth, nseq * 6, 1, d)
        self.ntiles = self.tokens // self.tm

    def mod_arg(self, layer):
        return self.mod[layer]

    def mod_spec(self, k):
        if self.tiles_per_seq is None:
            return pl.BlockSpec((self.tm, self.d), lambda i, *_: (0, k))
        tps = self.tiles_per_seq
        return pl.BlockSpec((None, 1, self.d), lambda i, *_: ((i // tps) * 6 + k, 0, 0))

    def rope_spec(self):
        if self.tiles_per_seq is None:
            return pl.BlockSpec((self.tm, LANES), lambda i, *_: (0, 0))
        tps = self.tiles_per_seq
        return pl.BlockSpec((self.tm, LANES), lambda i, *_: (i % tps, 0))


def _proj_kernel(x_ref, sh_ref, sc_ref, g_ref, w_ref, ct_ref, sa_ref, sb_ref, *out_refs, splits):
    h = _rms(x_ref[...], g_ref[...]) * (1.0 + sc_ref[...]) + sh_ref[...]
    u = _dot(h.astype(BF16), w_ref[...])
    for (start, width, rope), o_ref in zip(splits, out_refs):
        if rope:
            ct, sa, sb = ct_ref[...], sa_ref[...], sb_ref[...]
            for c0 in range(0, width, LANES):
                o_ref[:, c0:c0 + LANES] = _rope(u[:, start + c0:start + c0 + LANES], ct, sa, sb)
        else:
            o_ref[...] = u[:, start:start + width]


def _proj(st, layer, x, norm_g, w_bf16, rope_tabs, splits):
    d = st.d
    n = w_bf16.shape[1]
    tm = st.tm
    row = lambda i: (i, 0)
    const = lambda i: (0, 0)
    return pl.pallas_call(
        functools.partial(_proj_kernel, splits=splits),
        grid=(st.ntiles,),
        in_specs=[pl.BlockSpec((tm, d), row), st.mod_spec(0), st.mod_spec(1),
                  pl.BlockSpec((1, d), const), pl.BlockSpec((d, n), const),
                  st.rope_spec(), st.rope_spec(), st.rope_spec()],
        out_specs=[pl.BlockSpec((tm, w), row) for _, w, _ in splits],
        out_shape=[jax.ShapeDtypeStruct((st.tokens, w), F32) for _, w, _ in splits],
        compiler_params=_cparams("parallel"),
        name="proj",
    )(x, st.mod_arg(layer), st.mod_arg(layer), norm_g.reshape(1, d), w_bf16, *rope_tabs)


def _swa_prompt_kernel(sink_ref, q_ref, kp_ref, kc_ref, vp_ref, vc_ref, o_ref):
    i = pl.program_id(1)
    q = q_ref[...]
    kk = jnp.concatenate([kp_ref[...], kc_ref[...]], axis=0).astype(BF16)
    vv = jnp.concatenate([vp_ref[...], vc_ref[...]], axis=0).astype(BF16)
    qb = q_ref.shape[0]
    row = lax.broadcasted_iota(jnp.int32, (qb, 2 * qb), 0)
    col = lax.broadcasted_iota(jnp.int32, (qb, 2 * qb), 1)
    diff = row + qb - col
    ok = (diff >= 0) & (diff <= WINDOW) & ((col >= qb) | (i > 0))
    outs = []
    for h in range(A_HEADS):
        kv = h // A_GROUP
        qh = q[:, h * HEAD_DIM:(h + 1) * HEAD_DIM].astype(BF16)
        s = _dot_nt(qh, kk[:, kv * HEAD_DIM:(kv + 1) * HEAD_DIM]) * ATT_SCALE
        s = jnp.where(ok, s, NEG_INF)
        sink = sink_ref[h]
        m = jnp.maximum(jnp.max(s, axis=-1, keepdims=True), sink)
        p = jnp.exp(s - m)
        denom = jnp.sum(p, axis=-1, keepdims=True) + jnp.exp(sink - m)
        outs.append(_dot(p.astype(BF16), vv[:, kv * HEAD_DIM:(kv + 1) * HEAD_DIM]) / denom)
    o_ref[...] = jnp.concatenate(outs, axis=-1)


def _swa_prompt(q, k, v, sinks, nseq, seqlen):
    qb = WINDOW
    nb = seqlen // qb
    cur = lambda b, i: (b * nb + i, 0)
    prev = lambda b, i: (b * nb + jnp.maximum(i - 1, 0), 0)
    return pl.pallas_call(
        _swa_prompt_kernel,
        grid=(nseq, nb),
        in_specs=[pl.BlockSpec(memory_space=pltpu.SMEM),
                  pl.BlockSpec((qb, A_QW), cur),
                  pl.BlockSpec((qb, A_KVW), prev), pl.BlockSpec((qb, A_KVW), cur),
                  pl.BlockSpec((qb, A_KVW), prev), pl.BlockSpec((qb, A_KVW), cur)],
        out_specs=pl.BlockSpec((qb, A_QW), cur),
        out_shape=jax.ShapeDtypeStruct(q.shape, F32),
        compiler_params=_cparams("parallel", "parallel"),
        name="swa_prompt",
    )(sinks, q, k, k, v, v)


def _ssd_prompt_kernel(xbc_ref, z_ref, dt_ref, cprev_ref, h0_ref, cw_ref, cb_ref, dtb_ref, alog_ref, dskip_ref,
                       ng_ref, o_ref, cnew_ref, hlast_ref, xp_ref, h_ref):
    c = pl.program_id(1)
    nc = pl.num_programs(1)
    q = SSD_CHUNK
    halo = MB_CONV - 1
    base = 8 - halo

    @pl.when(c == 0)
    def _():
        xp_ref[base:8, :] = cprev_ref[...]
        h_ref[...] = h0_ref[...]

    xp_ref[8:8 + q, :] = xbc_ref[...]
    acc = cb_ref[...] + xp_ref[base:base + q, :] * cw_ref[0:1, :]
    for tap in range(1, MB_CONV):
        acc = acc + xp_ref[base + tap:base + tap + q, :] * cw_ref[tap:tap + 1, :]
    tail = xp_ref[8 + q - halo:8 + q, :]
    xp_ref[base:8, :] = tail

    @pl.when(c == nc - 1)
    def _():
        cnew_ref[...] = tail

    xbc = _silu(acc)
    xs = xbc[:, :MB_INNER]
    dt = _softplus(dt_ref[...] + dtb_ref[...])
    da = dt * (-jnp.exp(alog_ref[...]))
    r_i = lax.broadcasted_iota(jnp.int32, (q, q), 0)
    c_i = lax.broadcasted_iota(jnp.int32, (q, q), 1)
    causal = r_i >= c_i
    acum = _dot_exact_lhs(causal.astype(BF16), da)
    acum_t = acum.T
    rpg = MB_HEADS // MB_GROUPS
    ys = []
    for g in range(MB_GROUPS):
        bq = xbc[:, MB_INNER + g * MB_DSTATE:MB_INNER + (g + 1) * MB_DSTATE].astype(BF16)
        cq = xbc[:, MB_INNER + MB_GN + g * MB_DSTATE:MB_INNER + MB_GN + (g + 1) * MB_DSTATE].astype(BF16)
        cbm = _dot_nt(cq, bq)
        for r in range(rpg):
            h = g * rpg + r
            a_col = acum[:, h:h + 1]
            a_row = acum_t[h:h + 1, :]
            a_last = acum[q - 1:q, h:h + 1]
            decay = jnp.where(causal, jnp.exp(jnp.where(causal, a_col - a_row, 0.0)), 0.0)
            xd = xs[:, h * MB_HEADDIM:(h + 1) * MB_HEADDIM] * dt[:, h:h + 1]
            hprev = h_ref[h]
            y = _dot((cbm * decay).astype(BF16), xd.astype(BF16))
            y = y + jnp.exp(a_col) * _dot_nt(cq, hprev.astype(BF16))
            xw = (xd * jnp.exp(a_last - a_col)).astype(BF16)
            states = lax.dot_general(xw, bq, (((0,), (0,)), ((), ())), preferred_element_type=F32)
            h_ref[h] = hprev * jnp.exp(a_last) + states
            ys.append(y)
    y = jnp.concatenate(ys, axis=-1) + xs * dskip_ref[...]
    y = y * _silu(z_ref[...])
    gw = MB_INNER // MB_GROUPS
    ng = ng_ref[...]
    o_ref[...] = jnp.concatenate(
        [_rms(y[:, g * gw:(g + 1) * gw], ng[:, g * gw:(g + 1) * gw]) for g in range(MB_GROUPS)], axis=-1)

    @pl.when(c == nc - 1)
    def _():
        hlast_ref[...] = h_ref[...]


def _pad_lanes(v, n=LANES):
    return jnp.pad(v, (0, n - v.shape[0])).reshape(1, n)


def _ssd_prompt(xbc, z, dt, conv_prev, h0, conv_w, conv_b, dt_bias, a_log, d_skip, norm_g, nseq, seqlen):
    q = SSD_CHUNK
    nc = seqlen // q
    row = lambda b, c: (b * nc + c, 0)
    per_seq3 = lambda b, c: (b, 0, 0)
    per_seq4 = lambda b, c: (b, 0, 0, 0)
    const = lambda b, c: (0, 0)
    halo = MB_CONV - 1
    return pl.pallas_call(
        _ssd_prompt_kernel,
        grid=(nseq, nc),
        in_specs=[pl.BlockSpec((q, MB_CONV_DIM), row), pl.BlockSpec((q, MB_INNER), row),
                  pl.BlockSpec((q, LANES), row),
                  pl.BlockSpec((None, halo, MB_CONV_DIM), per_seq3),
                  pl.BlockSpec((None, MB_HEADS, MB_HEADDIM, MB_DSTATE), per_seq4),
                  pl.BlockSpec((MB_CONV, MB_CONV_DIM), const), pl.BlockSpec((1, MB_CONV_DIM), const),
                  pl.BlockSpec((1, LANES), const), pl.BlockSpec((1, LANES), const),
                  pl.BlockSpec((1, MB_INNER), const), pl.BlockSpec((1, MB_INNER), const)],
        out_specs=[pl.BlockSpec((q, MB_INNER), row),
                   pl.BlockSpec((None, halo, MB_CONV_DIM), per_seq3),
                   pl.BlockSpec((None, MB_HEADS, MB_HEADDIM, MB_DSTATE), per_seq4)],
        out_shape=[jax.ShapeDtypeStruct((nseq * seqlen, MB_INNER), F32),
                   jax.ShapeDtypeStruct((nseq, halo, MB_CONV_DIM), F32),
                   jax.ShapeDtypeStruct((nseq, MB_HEADS, MB_HEADDIM, MB_DSTATE), F32)],
        scratch_shapes=[pltpu.VMEM((8 + q, MB_CONV_DIM), F32),
                        pltpu.VMEM((MB_HEADS, MB_HEADDIM, MB_DSTATE), F32)],
        compiler_params=_cparams("parallel", "arbitrary"),
        name="ssd_prompt",
    )(xbc, z, dt, conv_prev, h0, conv_w, conv_b.reshape(1, -1), _pad_lanes(dt_bias), _pad_lanes(a_log),
      jnp.repeat(d_skip, MB_HEADDIM).reshape(1, -1), norm_g.reshape(1, -1))


def _route(logits_t, rbias_col):
    tm = logits_t.shape[1]
    scores = _sigmoid(logits_t[0:N_EXPERTS, :])
    biased = scores + rbias_col[0:N_EXPERTS, :]
    s = [scores[e:e + 1, :] for e in range(N_EXPERTS)]
    b = [biased[e:e + 1, :] for e in range(N_EXPERTS)]
    epg = EXPERTS_PER_GROUP
    gscore = []
    for g in range(N_EXPERT_GROUPS):
        v = b[g * epg:(g + 1) * epg]
        best = None
        for i in range(epg):
            for j in range(i + 1, epg):
                pair = v[i] + v[j]
                best = pair if best is None else jnp.maximum(best, pair)
        gscore.append(best)
    gsel = jnp.zeros((1, tm), jnp.int32)
    gbest = gscore[0]
    for g in range(1, N_EXPERT_GROUPS):
        better = gscore[g] > gbest
        gsel = jnp.where(better, g, gsel)
        gbest = jnp.where(better, gscore[g], gbest)
    bs, ss = [], []
    for k in range(epg):
        bk, sk = b[k], s[k]
        for g in range(1, N_EXPERT_GROUPS):
            bk = jnp.where(gsel == g, b[g * epg + k], bk)
            sk = jnp.where(gsel == g, s[g * epg + k], sk)
        bs.append(bk)
        ss.append(sk)
    i1 = jnp.zeros((1, tm), jnp.int32)
    m1 = bs[0]
    for k in range(1, epg):
        better = bs[k] > m1
        i1 = jnp.where(better, k, i1)
        m1 = jnp.where(better, bs[k], m1)
    i2 = jnp.full((1, tm), -1, jnp.int32)
    m2 = jnp.full((1, tm), -jnp.inf, F32)
    for k in range(epg):
        better = (i1 != k) & ((bs[k] > m2) | (i2 < 0))
        i2 = jnp.where(better, k, i2)
        m2 = jnp.where(better, bs[k], m2)
    s1 = jnp.zeros((1, tm), F32)
    s2 = jnp.zeros((1, tm), F32)
    for k in range(epg):
        s1 = jnp.where(i1 == k, ss[k], s1)
        s2 = jnp.where(i2 == k, ss[k], s2)
    denom = s1 + s2
    rows = lax.broadcasted_iota(jnp.int32, (N_EXPERTS, tm), 0)
    comb = jnp.zeros((N_EXPERTS, tm), F32)
    for e in range(N_EXPERTS):
        g, k = divmod(e, epg)
        chosen = (gsel == g) & ((i1 == k) | (i2 == k))
        comb = jnp.where(rows == e, jnp.where(chosen, s[e] / denom, 0.0), comb)
    lo = jnp.minimum(i1, i2)
    hi = jnp.maximum(i1, i2)
    pair = jnp.where(lo == 0, 0, jnp.where(lo == 1, epg - 1, 2 * epg - 3)) + (hi - lo - 1)
    bucket = (gsel * PAIRS_PER_GROUP + pair).astype(F32)
    first_is_lo = i1 < i2
    w_lo = jnp.where(first_is_lo, s1, s2) / denom
    w_hi = jnp.where(first_is_lo, s2, s1) / denom
    return comb, bucket, w_lo, w_hi


def _mix_kernel(*refs, n_in, sorted_moe):
    x_ref, g1_ref = refs[0], refs[1]
    a_refs = refs[2:2 + n_in]
    w_refs = refs[2 + n_in:2 + 2 * n_in]
    ng_ref, sc_ref, sh_ref, wr_ref, rb_ref, x1_ref, h2_ref, aux_ref = refs[2 + 2 * n_in:]
    mix = _dot(a_refs[0][...].astype(BF16), w_refs[0][...])
    for a_ref, w_ref in zip(a_refs[1:], w_refs[1:]):
        mix = mix + _dot(a_ref[...].astype(BF16), w_ref[...])
    x1 = x_ref[...] + g1_ref[...] * mix
    x1_ref[...] = x1
    h2 = _rms(x1, ng_ref[...]) * (1.0 + sc_ref[...]) + sh_ref[...]
    logits = _dot_hi(h2, wr_ref[...])
    comb_t, bucket, w_lo, w_hi = _route(logits.T, rb_ref[...])
    tm, d = x1.shape
    if sorted_moe:
        r = lax.broadcasted_iota(jnp.int32, (LANES, tm), 0)
        rt = jnp.where(r == 0, bucket, jnp.where(r == 1, w_lo, jnp.where(r == 2, w_hi, 0.0)))
        h2_ref[:, :d] = h2
        h2_ref[:, d:] = rt.T
        aux_ref[...] = rt[0:aux_ref.shape[0], :]
    else:
        h2_ref[...] = h2.astype(BF16)
        aux_ref[...] = jnp.concatenate([comb_t, jnp.zeros((LANES - N_EXPERTS, tm), F32)], axis=0).T


def _mix(st, layer, x, a_list, w_list, norm_g, wr_pad, rb_col, sorted_moe):
    d, tm = st.d, st.tm
    row = lambda i: (i, 0)
    const = lambda i: (0, 0)
    n_in = len(a_list)
    in_specs = [pl.BlockSpec((tm, d), row), st.mod_spec(2)]
    in_specs += [pl.BlockSpec((tm, a.shape[1]), row) for a in a_list]
    in_specs += [pl.BlockSpec(w.shape, const) for w in w_list]
    in_specs += [pl.BlockSpec((1, d), const), st.mod_spec(4), st.mod_spec(3),
                 pl.BlockSpec((d, LANES), const), pl.BlockSpec((LANES, 1), const)]
    m = st.mod_arg(layer)
    if sorted_moe:
        out_specs = [pl.BlockSpec((tm, d), row), pl.BlockSpec((tm, d + LANES), row),
                     pl.BlockSpec((ROUTE_ROWS, tm), lambda i: (0, i))]
        out_shape = [jax.ShapeDtypeStruct((st.tokens, d), F32), jax.ShapeDtypeStruct((st.tokens, d + LANES), F32),
                     jax.ShapeDtypeStruct((ROUTE_ROWS, st.tokens), F32)]
    else:
        out_specs = [pl.BlockSpec((tm, d), row), pl.BlockSpec((tm, d), row), pl.BlockSpec((tm, LANES), row)]
        out_shape = [jax.ShapeDtypeStruct((st.tokens, d), F32), jax.ShapeDtypeStruct((st.tokens, d), BF16),
                     jax.ShapeDtypeStruct((st.tokens, LANES), F32)]
    return pl.pallas_call(
        functools.partial(_mix_kernel, n_in=n_in, sorted_moe=sorted_moe),
        grid=(st.ntiles,),
        in_specs=in_specs,
        out_specs=out_specs,
        out_shape=out_shape,
        compiler_params=_cparams("parallel"),
        name="mix",
    )(x, m, *a_list, *w_list, norm_g.reshape(1, d), m, m, wr_pad, rb_col)


def _moe_kernel(h_ref, comb_ref, wg_ref, wu_ref, wd_ref, x1_ref, g2_ref, fg_ref, o_ref, acc_ref, *, final):
    e = pl.program_id(1)

    @pl.when(e == 0)
    def _():
        acc_ref[...] = jnp.zeros_like(acc_ref)

    h = h_ref[...]
    he = _silu(_dot(h, wg_ref[...])) * _dot(h, wu_ref[...])
    comb = comb_ref[...]
    lane = lax.broadcasted_iota(jnp.int32, comb.shape, 1)
    ce = jnp.sum(jnp.where(lane == e, comb, 0.0), axis=-1, keepdims=True)
    acc_ref[...] += ce * _dot(he.astype(BF16), wd_ref[...])

    @pl.when(e == pl.num_programs(1) - 1)
    def _():
        x2 = x1_ref[...] + g2_ref[...] * acc_ref[...]
        o_ref[...] = _rms(x2, fg_ref[...]) if final else x2


def _moe(st, layer, h2, comb, wg, wu, wd, x1, final_g, final):
    d, tm = st.d, st.tm
    ne, _, dff = wg.shape[1:]
    row = lambda i, e: (i, 0)
    return pl.pallas_call(
        functools.partial(_moe_kernel, final=final),
        grid=(st.ntiles, ne),
        in_specs=[pl.BlockSpec((tm, d), row), pl.BlockSpec((tm, LANES), row),
                  pl.BlockSpec((None, None, d, dff), lambda i, e: (layer, e, 0, 0)),
                  pl.BlockSpec((None, None, d, dff), lambda i, e: (layer, e, 0, 0)),
                  pl.BlockSpec((None, None, dff, d), lambda i, e: (layer, e, 0, 0)),
                  pl.BlockSpec((tm, d), row), st.mod_spec(5),
                  pl.BlockSpec((1, d), lambda i, e: (0, 0))],
        out_specs=pl.BlockSpec((tm, d), row),
        out_shape=jax.ShapeDtypeStruct((st.tokens, d), F32),
        scratch_shapes=[pltpu.VMEM((tm, d), F32)],
        compiler_params=_cparams("parallel", "arbitrary"),
        name="moe",
    )(h2, comb, wg, wu, wd, x1, st.mod_arg(layer), final_g.reshape(1, d))


def _bucket_rank_kernel(rt_ref, rank_ref, cnt_ref, carry_ref):
    i = pl.program_id(0)
    tm = rt_ref.shape[1]

    @pl.when(i == 0)
    def _():
        carry_ref[...] = jnp.zeros_like(carry_ref)

    bucket = rt_ref[0:1, :]
    rows = lax.broadcasted_iota(jnp.int32, (BUCKET_ROWS, tm), 0).astype(F32)
    onehot = rows == bucket
    s_i = lax.broadcasted_iota(jnp.int32, (tm, tm), 0)
    t_i = lax.broadcasted_iota(jnp.int32, (tm, tm), 1)
    incl = jnp.where(s_i <= t_i, 1.0, 0.0).astype(BF16)
    cum = _dot(jnp.where(onehot, 1.0, 0.0).astype(BF16), incl)
    carry = carry_ref[...]
    rank = jnp.sum(jnp.where(onehot, cum - 1.0 + carry, 0.0), axis=0, keepdims=True)
    rank_ref[...] = rank.astype(jnp.int32)
    carry = carry + cum[:, tm - 1:tm]
    carry_ref[...] = carry
    cnt_ref[...] = jnp.broadcast_to(carry, cnt_ref.shape)


def _bucket_rank(st, rt):
    tm = st.tm
    return pl.pallas_call(
        _bucket_rank_kernel,
        grid=(st.ntiles,),
        in_specs=[pl.BlockSpec((ROUTE_ROWS, tm), lambda i: (0, i))],
        out_specs=[pl.BlockSpec((1, tm), lambda i: (0, i)), pl.BlockSpec((BUCKET_ROWS, LANES), lambda i: (0, 0))],
        out_shape=[jax.ShapeDtypeStruct((1, st.tokens), jnp.int32),
                   jax.ShapeDtypeStruct((BUCKET_ROWS, LANES), F32)],
        scratch_shapes=[pltpu.VMEM((BUCKET_ROWS, 1), F32)],
        compiler_params=_cparams("arbitrary"),
        name="bucket_rank",
    )(rt)


def _sort_plan(rt, rank, cnt, tokens):
    ts = SORT_TILE
    bucket = rt[0].astype(jnp.int32)
    counts = cnt[:N_BUCKETS, 0].astype(jnp.int32)
    padded = (counts + ts - 1) // ts * ts
    ends = jnp.cumsum(padded)
    dest = (ends - padded)[bucket] + rank[0]
    n_tiles = -(-tokens // ts) + N_BUCKETS
    n_used = ends[-1] // ts
    tile = jnp.arange(n_tiles, dtype=jnp.int32)
    tile_bucket = jnp.searchsorted(ends, jnp.minimum(tile, n_used - 1) * ts, side='right').astype(jnp.int32)
    tile_bucket = jnp.minimum(tile_bucket, N_BUCKETS - 1)
    pairs = [(a, b) for a in range(EXPERTS_PER_GROUP) for b in range(a + 1, EXPERTS_PER_GROUP)]
    lo_tab = jnp.array([a for a, _ in pairs], jnp.int32)
    hi_tab = jnp.array([b for _, b in pairs], jnp.int32)
    base = tile_bucket // PAIRS_PER_GROUP * EXPERTS_PER_GROUP
    pair = tile_bucket % PAIRS_PER_GROUP
    return dest, base + lo_tab[pair], base + hi_tab[pair], n_used.reshape(1).astype(jnp.int32), n_tiles


def _dispatch_kernel(dest_ref, h_ref, init_ref, out_ref, sem):
    del init_ref
    i = pl.program_id(0)
    tm = h_ref.shape[0]

    def body(r, carry):
        pltpu.make_async_copy(h_ref.at[pl.ds(r, 1)], out_ref.at[pl.ds(dest_ref[i * tm + r], 1)], sem).start()
        return carry

    lax.fori_loop(0, tm, body, 0, unroll=8)
    pltpu.make_async_copy(h_ref, out_ref.at[pl.ds(0, tm)], sem).wait()


def _dispatch(st, dest, h2w, n_rows):
    tm = st.tm
    w = h2w.shape[1]
    return pl.pallas_call(
        _dispatch_kernel,
        grid_spec=pltpu.PrefetchScalarGridSpec(
            num_scalar_prefetch=1,
            grid=(st.ntiles,),
            in_specs=[pl.BlockSpec((tm, w), lambda i, d: (i, 0)), pl.BlockSpec(memory_space=pl.ANY)],
            out_specs=pl.BlockSpec(memory_space=pl.ANY),
            scratch_shapes=[pltpu.SemaphoreType.DMA(())]),
        out_shape=jax.ShapeDtypeStruct((n_rows, w), F32),
        input_output_aliases={2: 0},
        compiler_params=_cparams("arbitrary"),
        name="moe_dispatch",
    )(dest, h2w, jnp.zeros((n_rows, w), F32))


def _experts_kernel(elo_ref, ehi_ref, nused_ref, hs_ref, wgl_ref, wul_ref, wdl_ref, wgh_ref, wuh_ref, wdh_ref, y_ref):
    del elo_ref, ehi_ref
    i = pl.program_id(0)
    d = y_ref.shape[1]

    @pl.when(i < nused_ref[0])
    def _():
        h = hs_ref[:, :d].astype(BF16)

        def expert(wg_ref, wu_ref, wd_ref):
            he = _silu(_dot(h, wg_ref[...])) * _dot(h, wu_ref[...])
            return _dot(he.astype(BF16), wd_ref[...])

        y_ref[...] = (hs_ref[:, d + 1:d + 2] * expert(wgl_ref, wul_ref, wdl_ref)
                      + hs_ref[:, d + 2:d + 3] * expert(wgh_ref, wuh_ref, wdh_ref))

    @pl.when(i >= nused_ref[0])
    def _():
        y_ref[...] = jnp.zeros_like(y_ref)


def _experts(layer, hs, e_lo, e_hi, n_used, n_tiles, wg, wu, wd):
    ts = SORT_TILE
    _, _, d, dff = wg.shape
    lo = lambda i, elo, ehi, nu: (layer, elo[i], 0, 0)
    hi = lambda i, elo, ehi, nu: (layer, ehi[i], 0, 0)
    up = lambda idx: pl.BlockSpec((None, None, d, dff), idx)
    down = lambda idx: pl.BlockSpec((None, None, dff, d), idx)
    return pl.pallas_call(
        _experts_kernel,
        grid_spec=pltpu.PrefetchScalarGridSpec(
            num_scalar_prefetch=3,
            grid=(n_tiles,),
            in_specs=[pl.BlockSpec((ts, hs.shape[1]), lambda i, *_: (i, 0)),
                      up(lo), up(lo), down(lo), up(hi), up(hi), down(hi)],
            out_specs=pl.BlockSpec((ts, d), lambda i, *_: (i, 0))),
        out_shape=jax.ShapeDtypeStruct((n_tiles * ts, d), F32),
        compiler_params=_cparams("arbitrary"),
        name="moe_experts",
    )(e_lo, e_hi, n_used, hs, wg, wu, wd, wg, wu, wd)


def _combine_kernel(dest_ref, x1_ref, g2_ref, fg_ref, ys_ref, o_ref, buf, sem, *, final):
    i = pl.program_id(0)
    n = pl.num_programs(0)
    tm = x1_ref.shape[0]
    slot = i % 2

    def start(tile, sl):
        def body(r, carry):
            pltpu.make_async_copy(ys_ref.at[pl.ds(dest_ref[tile * tm + r], 1)], buf.at[sl, pl.ds(r, 1)],
                                  sem.at[sl]).start()
            return carry

        lax.fori_loop(0, tm, body, 0, unroll=8)

    @pl.when(i == 0)
    def _():
        start(0, 0)

    @pl.when(i + 1 < n)
    def _():
        start(i + 1, 1 - slot)

    pltpu.make_async_copy(ys_ref.at[pl.ds(0, tm)], buf.at[slot], sem.at[slot]).wait()
    x2 = x1_ref[...] + g2_ref[...] * buf[slot]
    o_ref[...] = _rms(x2, fg_ref[...]) if final else x2


def _combine(st, layer, dest, ys, x1, final_g, final):
    d, tm = st.d, st.tm
    row = lambda i, dst: (i, 0)
    return pl.pallas_call(
        functools.partial(_combine_kernel, final=final),
        grid_spec=pltpu.PrefetchScalarGridSpec(
            num_scalar_prefetch=1,
            grid=(st.ntiles,),
            in_specs=[pl.BlockSpec((tm, d), row), st.mod_spec(5), pl.BlockSpec((1, d), lambda i, dst: (0, 0)),
                      pl.BlockSpec(memory_space=pl.ANY)],
            out_specs=pl.BlockSpec((tm, d), row),
            scratch_shapes=[pltpu.VMEM((2, tm, d), F32), pltpu.SemaphoreType.DMA((2,))]),
        out_shape=jax.ShapeDtypeStruct((st.tokens, d), F32),
        compiler_params=_cparams("arbitrary"),
        name="moe_combine",
    )(dest, x1, st.mod_arg(layer), final_g.reshape(1, d), ys)


def _moe_sorted(st, layer, h2w, rt, wg, wu, wd, x1, final_g, final):
    rank, cnt = _bucket_rank(st, rt)
    dest, e_lo, e_hi, n_used, n_tiles = _sort_plan(rt, rank, cnt, st.tokens)
    hs = _dispatch(st, dest, h2w, n_tiles * SORT_TILE)
    ys = _experts(layer, hs, e_lo, e_hi, n_used, n_tiles, wg, wu, wd)
    return _combine(st, layer, dest, ys, x1, final_g, final)


def _kmean_kernel(k_ref, o_ref):
    nblk = o_ref.shape[0]
    k = k_ref[...]
    o_ref[...] = jnp.concatenate(
        [jnp.sum(k[j * MOBA_BLOCK:(j + 1) * MOBA_BLOCK, :], axis=0, keepdims=True) for j in range(nblk)],
        axis=0) * (1.0 / MOBA_BLOCK)


def _kmean(k, nseq, seqlen):
    nblk = seqlen // MOBA_BLOCK
    return pl.pallas_call(
        _kmean_kernel,
        grid=(nseq,),
        in_specs=[pl.BlockSpec((seqlen, C_KVW), lambda b: (b, 0))],
        out_specs=pl.BlockSpec((None, nblk, C_KVW), lambda b: (b, 0, 0)),
        out_shape=jax.ShapeDtypeStruct((nseq, nblk, C_KVW), F32),
        compiler_params=_cparams("parallel"),
        name="kmean",
    )(k)


def _moba_prompt_kernel(q_ref, k_ref, v_ref, km_ref, o_ref, sel_ref):
    i = pl.program_id(1)
    qb = MOBA_QBLOCK
    blk = MOBA_BLOCK
    nblk = km_ref.shape[0]
    own = (i * qb) // blk
    q_off = i * qb - own * blk
    q = q_ref[...]
    km = km_ref[...]
    cols = C_GROUP * qb
    nb_pad = sel_ref.shape[1]
    blk_row = lax.broadcasted_iota(jnp.int32, (nb_pad, cols), 0)
    past = blk_row < own
    key_i = lax.broadcasted_iota(jnp.int32, (blk, cols), 0)
    qry_i = lax.broadcasted_iota(jnp.int32, (blk, cols), 1)
    own_ok = key_i <= (qry_i % qb) + q_off
    own_start = pl.multiple_of(own * blk, blk)
    accs = []
    for kv in range(C_KV_HEADS):
        lo, hi = kv * HEAD_DIM, (kv + 1) * HEAD_DIM
        qs = jnp.concatenate(
            [q[:, (kv * C_GROUP + g) * HEAD_DIM:(kv * C_GROUP + g + 1) * HEAD_DIM] for g in range(C_GROUP)], axis=0)
        kmp = jnp.concatenate([km[:, lo:hi], jnp.zeros((nb_pad - nblk, HEAD_DIM), F32)], axis=0)
        sg = jnp.where(past, _dot_hi_nt(kmp, qs), NEG_INF)
        rank = jnp.zeros((nb_pad, cols), jnp.int32)
        for j in range(nblk):
            rj = sg[j:j + 1, :]
            rank = rank + ((rj > sg) | ((rj == sg) & (blk_row > j))).astype(jnp.int32)
        sel_ref[kv] = (past & (rank < MOBA_TOPK)).astype(F32)
        qsb = (qs * ATT_SCALE).astype(BF16)

        s = _dot_nt(k_ref[pl.ds(own_start, blk), lo:hi].astype(BF16), qsb)
        s = jnp.where(own_ok, s, NEG_INF)
        m0 = jnp.max(s, axis=0, keepdims=True)
        p = jnp.exp(s - m0)
        l0 = jnp.sum(p, axis=0, keepdims=True)
        acc0 = _dot_tn(v_ref[pl.ds(own_start, blk), lo:hi].astype(BF16), p.astype(BF16))

        def body(j, carry, kv=kv, lo=lo, hi=hi, qsb=qsb):
            m, l, acc = carry
            start = pl.multiple_of(j * blk, blk)
            chosen = sel_ref[kv, pl.ds(j, 1), :] > 0.0
            s = _dot_nt(k_ref[pl.ds(start, blk), lo:hi].astype(BF16), qsb)
            s = jnp.where(chosen, s, NEG_INF)
            m_new = jnp.maximum(m, jnp.max(s, axis=0, keepdims=True))
            alpha = jnp.exp(m - m_new)
            p = jnp.exp(s - m_new)
            l = alpha * l + jnp.sum(p, axis=0, keepdims=True)
            acc = alpha * acc + _dot_tn(v_ref[pl.ds(start, blk), lo:hi].astype(BF16), p.astype(BF16))
            return m_new, l, acc

        _, l, acc = lax.fori_loop(0, own, body, (m0, l0, acc0))
        accs.append(acc / l)
    o_t = jnp.concatenate(accs, axis=0).T
    o_ref[...] = jnp.concatenate(
        [o_t[g * qb:(g + 1) * qb, kv * HEAD_DIM:(kv + 1) * HEAD_DIM]
         for kv in range(C_KV_HEADS) for g in range(C_GROUP)], axis=-1)


def _moba_prompt(q, k, v, kmean, nseq, seqlen):
    qb = MOBA_QBLOCK
    nq = seqlen // qb
    nblk = kmean.shape[1]
    seq = lambda b, i: (b, 0)
    return pl.pallas_call(
        _moba_prompt_kernel,
        grid=(nseq, nq),
        in_specs=[pl.BlockSpec((qb, C_QW), lambda b, i: (b * nq + i, 0)),
                  pl.BlockSpec((seqlen, C_KVW), seq), pl.BlockSpec((seqlen, C_KVW), seq),
                  pl.BlockSpec((None, nblk, C_KVW), lambda b, i: (b, 0, 0))],
        out_specs=pl.BlockSpec((qb, C_QW), lambda b, i: (b * nq + i, 0)),
        out_shape=jax.ShapeDtypeStruct(q.shape, F32),
        scratch_shapes=[pltpu.VMEM((C_KV_HEADS, -(-nblk // 16) * 16, C_GROUP * qb), F32)],
        compiler_params=_cparams("parallel", "parallel"),
        name="moba_prompt",
    )(q, k, v, kmean)


def _swa_sample_kernel(sink_ref, q_ref, kn_ref, vn_ref, wk_ref, wv_ref, o_ref, nk_ref, nv_ref):
    q = q_ref[...]
    kn, vn = kn_ref[...], vn_ref[...]
    wk, wv = wk_ref[...], wv_ref[...]
    tb, w = wk.shape[0], wk.shape[1]
    nk_ref[:, 0:w - 1, :] = wk[:, 1:w, :]
    nk_ref[:, w - 1:w, :] = kn[:, None, :]
    nv_ref[:, 0:w - 1, :] = wv[:, 1:w, :]
    nv_ref[:, w - 1:w, :] = vn[:, None, :]
    lane = lax.broadcasted_iota(jnp.int32, (tb, LANES), 1)
    lane3 = lax.broadcasted_iota(jnp.int32, (tb, w, LANES), 2)
    heads = [None] * A_HEADS
    for g in range(A_GROUP):
        qg = jnp.concatenate(
            [q[:, (kv * A_GROUP + g) * HEAD_DIM:(kv * A_GROUP + g + 1) * HEAD_DIM] for kv in range(A_KV_HEADS)], axis=-1)
        prod = wk * qg[:, None, :]
        prod_n = kn * qg
        p_full = None
        pn_full = None
        for kv in range(A_KV_HEADS):
            lo, hi = kv * HEAD_DIM, (kv + 1) * HEAD_DIM
            sink = sink_ref[kv * A_GROUP + g]
            s = jnp.sum(prod[:, :, lo:hi], axis=-1, keepdims=True) * ATT_SCALE
            sn = jnp.sum(prod_n[:, lo:hi], axis=-1, keepdims=True) * ATT_SCALE
            m = jnp.maximum(jnp.maximum(jnp.max(s, axis=1), sn), sink)
            p = jnp.exp(s - m[:, None, :])
            pn = jnp.exp(sn - m)
            inv = 1.0 / (jnp.sum(p, axis=1) + pn + jnp.exp(sink - m))
            p = p * inv[:, None, :]
            pn = pn * inv
            pb = jnp.broadcast_to(p, (tb, w, LANES))
            pnb = jnp.broadcast_to(pn, (tb, LANES))
            if kv == 0:
                p_full, pn_full = pb, pnb
            else:
                p_full = jnp.where(lane3 >= lo, pb, p_full)
                pn_full = jnp.where(lane >= lo, pnb, pn_full)
        og = jnp.sum(p_full * wv, axis=1) + pn_full * vn
        for kv in range(A_KV_HEADS):
            heads[kv * A_GROUP + g] = og[:, kv * HEAD_DIM:(kv + 1) * HEAD_DIM]
    o_ref[...] = jnp.concatenate(heads, axis=-1)


def _swa_sample(q, kn, vn, win_k, win_v, sinks):
    db, w = win_k.shape[0], win_k.shape[1]
    tb = math.gcd(db, 8)
    row = lambda i: (i, 0)
    row3 = lambda i: (i, 0, 0)
    return pl.pallas_call(
        _swa_sample_kernel,
        grid=(db // tb,),
        in_specs=[pl.BlockSpec(memory_space=pltpu.SMEM),
                  pl.BlockSpec((tb, A_QW), row), pl.BlockSpec((tb, A_KVW), row), pl.BlockSpec((tb, A_KVW), row),
                  pl.BlockSpec((tb, w, A_KVW), row3), pl.BlockSpec((tb, w, A_KVW), row3)],
        out_specs=[pl.BlockSpec((tb, A_QW), row), pl.BlockSpec((tb, w, A_KVW), row3),
                   pl.BlockSpec((tb, w, A_KVW), row3)],
        out_shape=[jax.ShapeDtypeStruct((db, A_QW), F32), jax.ShapeDtypeStruct(win_k.shape, F32),
                   jax.ShapeDtypeStruct(win_v.shape, F32)],
        compiler_params=_cparams("parallel"),
        name="swa_sample",
    )(sinks, q, kn, vn, win_k, win_v)


def _ssd_sample_pre_kernel(xbc_ref, cst_ref, dt_ref, cw_ref, cb_ref, dtb_ref, alog_ref, exp_ref,
                           xs_ref, bm_ref, cm_ref, xd_ref, dec_ref, cnew_ref):
    xn = xbc_ref[...]
    halo = MB_CONV - 1
    acc = cb_ref[...] + xn * cw_ref[halo:halo + 1, :]
    for tap in range(halo):
        acc = acc + cst_ref[tap] * cw_ref[tap:tap + 1, :]
    for tap in range(1, halo):
        cnew_ref[tap - 1] = cst_ref[tap]
    cnew_ref[halo - 1] = xn
    xbc = _silu(acc)
    xs = xbc[:, :MB_INNER]
    xs_ref[...] = xs
    bm_ref[...] = xbc[:, MB_INNER:MB_INNER + MB_GN]
    cm_ref[...] = xbc[:, MB_INNER + MB_GN:]
    dt = _softplus(dt_ref[...] + dtb_ref[...])
    da = dt * (-jnp.exp(alog_ref[...]))
    expand = exp_ref[...]
    xd_ref[...] = xs * _dot_exact_lhs_rhs(dt, expand)
    dec_ref[...] = jnp.exp(_dot_exact_lhs_rhs(da, expand))


def _dot_exact_lhs_rhs(a, b_bf16):
    a0 = a.astype(BF16)
    r = a - a0.astype(F32)
    a1 = r.astype(BF16)
    a2 = (r - a1.astype(F32)).astype(BF16)
    return _dot(a0, b_bf16) + (_dot(a1, b_bf16) + _dot(a2, b_bf16))


def _ssd_sample_state_kernel(h_ref, xd_ref, dec_ref, bm_ref, cm_ref, hn_ref, y_ref):
    tb = h_ref.shape[0]
    rpg = MB_HEADS // MB_GROUPS
    for t in range(tb):
        for h in range(MB_HEADS):
            g = h // rpg
            rows = slice(h * MB_HEADDIM, (h + 1) * MB_HEADDIM)
            hn = h_ref[t, h] * dec_ref[t, rows, :] + xd_ref[t, rows, :] * bm_ref[t, g:g + 1, :]
            hn_ref[t, h] = hn
            y_ref[t, rows, :] = jnp.sum(hn * cm_ref[t, g:g + 1, :], axis=-1, keepdims=True)


def _ssd_sample_post_kernel(y_ref, xs_ref, z_ref, dskip_ref, ng_ref, o_ref):
    y = (y_ref[...] + xs_ref[...] * dskip_ref[...]) * _silu(z_ref[...])
    gw = MB_INNER // MB_GROUPS
    ng = ng_ref[...]
    o_ref[...] = jnp.concatenate(
        [_rms(y[:, g * gw:(g + 1) * gw], ng[:, g * gw:(g + 1) * gw]) for g in range(MB_GROUPS)], axis=-1)


def _ssd_sample(xbc, z, dt, conv_state, h0, conv_w, conv_b, dt_bias, a_log, d_skip, norm_g):
    db = xbc.shape[0]
    halo = MB_CONV - 1
    expand = (jnp.arange(LANES)[:, None] == (jnp.arange(MB_INNER) // MB_HEADDIM)[None, :]).astype(BF16)
    full = lambda *shape: pl.BlockSpec(shape, lambda: (0,) * len(shape))
    xs, bm, cm, xd, dec, conv_new = pl.pallas_call(
        _ssd_sample_pre_kernel,
        in_specs=[full(db, MB_CONV_DIM), full(halo, db, MB_CONV_DIM), full(db, LANES),
                  full(MB_CONV, MB_CONV_DIM), full(1, MB_CONV_DIM), full(1, LANES), full(1, LANES),
                  full(LANES, MB_INNER)],
        out_specs=[full(db, MB_INNER), full(db, MB_GN), full(db, MB_GN), full(db, MB_INNER), full(db, MB_INNER),
                   full(halo, db, MB_CONV_DIM)],
        out_shape=[jax.ShapeDtypeStruct((db, MB_INNER), F32), jax.ShapeDtypeStruct((db, MB_GN), F32),
                   jax.ShapeDtypeStruct((db, MB_GN), F32), jax.ShapeDtypeStruct((db, MB_INNER), F32),
                   jax.ShapeDtypeStruct((db, MB_INNER), F32), jax.ShapeDtypeStruct((halo, db, MB_CONV_DIM), F32)],
        compiler_params=pltpu.CompilerParams(vmem_limit_bytes=VMEM_LIMIT_BYTES),
        name="ssd_sample_pre",
    )(xbc, jnp.swapaxes(conv_state, 0, 1), dt, conv_w, conv_b.reshape(1, -1), _pad_lanes(dt_bias), _pad_lanes(a_log), expand)

    tb = math.gcd(db, 8)
    r3 = lambda i: (i, 0, 0)
    r4 = lambda i: (i, 0, 0, 0)
    h_new, y_col = pl.pallas_call(
        _ssd_sample_state_kernel,
        grid=(db // tb,),
        in_specs=[pl.BlockSpec((tb, MB_HEADS, MB_HEADDIM, MB_DSTATE), r4),
                  pl.BlockSpec((tb, MB_INNER, 1), r3), pl.BlockSpec((tb, MB_INNER, 1), r3),
                  pl.BlockSpec((tb, MB_GROUPS, MB_DSTATE), r3), pl.BlockSpec((tb, MB_GROUPS, MB_DSTATE), r3)],
        out_specs=[pl.BlockSpec((tb, MB_HEADS, MB_HEADDIM, MB_DSTATE), r4), pl.BlockSpec((tb, MB_INNER, 1), r3)],
        out_shape=[jax.ShapeDtypeStruct(h0.shape, F32), jax.ShapeDtypeStruct((db, MB_INNER, 1), F32)],
        compiler_params=_cparams("parallel"),
        name="ssd_sample_state",
    )(h0, xd.reshape(db, MB_INNER, 1), dec.reshape(db, MB_INNER, 1),
      bm.reshape(db, MB_GROUPS, MB_DSTATE), cm.reshape(db, MB_GROUPS, MB_DSTATE))

    o_b = pl.pallas_call(
        _ssd_sample_post_kernel,
        in_specs=[full(db, MB_INNER), full(db, MB_INNER), full(db, MB_INNER), full(1, MB_INNER), full(1, MB_INNER)],
        out_specs=full(db, MB_INNER),
        out_shape=jax.ShapeDtypeStruct((db, MB_INNER), F32),
        name="ssd_sample_post",
    )(y_col.reshape(db, MB_INNER), xs, z, jnp.repeat(d_skip, MB_HEADDIM).reshape(1, -1), norm_g.reshape(1, -1))
    return o_b, jnp.swapaxes(conv_new, 0, 1), h_new


def _select_column(x_t, b):
    lane = lax.broadcasted_iota(jnp.int32, x_t.shape, 1)
    return jnp.sum(jnp.where(lane == b, x_t, 0.0), axis=1, keepdims=True)


def _moba_gate_sample_kernel(pt_ref, q_ref, ck_ref, idx_ref, pbuf, sem, qt_ref, km_ref,
                             *, pages_per_block, pages_per_chunk, layer):
    b, c = pl.program_id(0), pl.program_id(1)
    nchunks = pl.num_programs(1)
    t = b * nchunks + c
    total = pl.num_programs(0) * nchunks
    slot = t % 2
    ppb, cpp = pages_per_block, pages_per_chunk
    bpc = cpp // ppb
    nblk = nchunks * bpc

    def copies(bb, cc, sl):
        return [pltpu.make_async_copy(ck_ref.at[layer, pt_ref[bb, cc * cpp + e]], pbuf.at[sl, e], sem.at[sl])
                for e in range(cpp)]

    @pl.when(t == 0)
    def _():
        for cp in copies(0, 0, 0):
            cp.start()
        qt_ref[...] = q_ref[...].T

    @pl.when(t + 1 < total)
    def _():
        for cp in copies((t + 1) // nchunks, (t + 1) % nchunks, 1 - slot):
            cp.start()

    for cp in copies(b, c, slot):
        cp.wait()

    @pl.when(c == 0)
    def _():
        km_ref[...] = jnp.zeros_like(km_ref)

    lane = lax.broadcasted_iota(jnp.int32, km_ref.shape, 1)
    for jb in range(bpc):
        pg = pbuf[slot, jb * ppb]
        for e in range(1, ppb):
            pg = pg + pbuf[slot, jb * ppb + e]
        mean = jnp.sum(pg, axis=1, keepdims=True) * (1.0 / MOBA_BLOCK)
        km_ref[...] = jnp.where(lane == c * bpc + jb, mean, km_ref[...])

    @pl.when(c == nchunks - 1)
    def _():
        qcol = _select_column(qt_ref[...], b)
        km = km_ref[...]
        lane1 = lax.broadcasted_iota(jnp.int32, (1, LANES), 1)
        out = jnp.zeros((1, LANES), jnp.int32)
        for h in range(C_HEADS):
            kv = h // C_GROUP
            s = jnp.sum(km[kv * HEAD_DIM:(kv + 1) * HEAD_DIM, :] * qcol[h * HEAD_DIM:(h + 1) * HEAD_DIM, :],
                        axis=0, keepdims=True)
            s = jnp.where(lane1 < nblk, s, -jnp.inf)
            for k in range(MOBA_TOPK):
                m = jnp.max(s, axis=1, keepdims=True)
                idx = jnp.min(jnp.where(s == m, lane1, LANES), axis=1, keepdims=True)
                out = jnp.where(lane1 == h * MOBA_TOPK + k, idx, out)
                s = jnp.where(lane1 == idx, -jnp.inf, s)
        idx_ref[0] = out


def _moba_gate_sample(page_table, q, cache_kt, layer):
    db, n_pages = page_table.shape
    ppb = MOBA_BLOCK // PAGE_SIZE
    cpp = math.gcd(n_pages, 32)
    nblk = n_pages // ppb
    assert cpp % ppb == 0 and MOBA_TOPK <= nblk <= LANES and C_HEADS * MOBA_TOPK <= LANES
    idx = pl.pallas_call(
        functools.partial(_moba_gate_sample_kernel, pages_per_block=ppb, pages_per_chunk=cpp, layer=layer),
        grid_spec=pltpu.PrefetchScalarGridSpec(
            num_scalar_prefetch=1,
            grid=(db, n_pages // cpp),
            in_specs=[pl.BlockSpec((db, C_QW), lambda b, c, pt: (0, 0)), pl.BlockSpec(memory_space=pl.ANY)],
            out_specs=pl.BlockSpec((1, 1, LANES), lambda b, c, pt: (b, 0, 0)),
            scratch_shapes=[pltpu.VMEM((2, cpp, C_KVW, PAGE_SIZE), F32), pltpu.SemaphoreType.DMA((2,)),
                            pltpu.VMEM((C_QW, db), F32), pltpu.VMEM((C_KVW, LANES), F32)]),
        out_shape=jax.ShapeDtypeStruct((db, 1, LANES), jnp.int32),
        compiler_params=_cparams("arbitrary", "arbitrary"),
        name="moba_gate_sample",
    )(page_table, q, cache_kt)
    return idx.reshape(db, LANES)


def _moba_sample_kernel(idx_ref, pt_ref, q_ref, kn_ref, vn_ref, ck_ref, cv_ref, o_ref, kbuf, vbuf, sem,
                        qt_ref, knt_ref, vnt_ref, ot_ref, *, pages_per_block, layer):
    b = pl.program_id(0)
    nb = pl.num_programs(0)
    slot = b % 2
    ppb = pages_per_block
    npg = MOBA_TOPK * ppb

    def copies(bb, sl):
        out = []
        for h in range(C_HEADS):
            rows = pl.ds((h // C_GROUP) * HEAD_DIM, HEAD_DIM)
            for k in range(MOBA_TOPK):
                blk = idx_ref[bb, h * MOBA_TOPK + k]
                for e in range(ppb):
                    page = pt_ref[bb, blk * ppb + e]
                    dst = h * npg + k * ppb + e
                    out.append(pltpu.make_async_copy(ck_ref.at[layer, page, rows, :], kbuf.at[sl, dst], sem.at[sl, 0]))
                    out.append(pltpu.make_async_copy(cv_ref.at[layer, page, rows, :], vbuf.at[sl, dst], sem.at[sl, 1]))
        return out

    @pl.when(b == 0)
    def _():
        for cp in copies(0, 0):
            cp.start()
        qt_ref[...] = q_ref[...].T
        knt_ref[...] = kn_ref[...].T
        vnt_ref[...] = vn_ref[...].T
        ot_ref[...] = jnp.zeros_like(ot_ref)

    @pl.when(b + 1 < nb)
    def _():
        for cp in copies(b + 1, 1 - slot):
            cp.start()

    for cp in copies(b, slot):
        cp.wait()

    qcol = _select_column(qt_ref[...], b) * ATT_SCALE
    kncol = _select_column(knt_ref[...], b)
    vncol = _select_column(vnt_ref[...], b)
    ocols = []
    for h in range(C_HEADS):
        kv = h // C_GROUP
        qh = qcol[h * HEAD_DIM:(h + 1) * HEAD_DIM, :]
        knh = kncol[kv * HEAD_DIM:(kv + 1) * HEAD_DIM, :]
        vnh = vncol[kv * HEAD_DIM:(kv + 1) * HEAD_DIM, :]
        s = [jnp.sum(kbuf[slot, h * npg + pg] * qh, axis=0, keepdims=True) for pg in range(npg)]
        sn = jnp.sum(qh * knh, axis=0, keepdims=True)
        m = sn
        for row in s:
            m = jnp.maximum(m, jnp.max(row, axis=1, keepdims=True))
        pn = jnp.exp(sn - m)
        denom = pn
        acc = None
        for pg, row in enumerate(s):
            p = jnp.exp(row - m)
            denom = denom + jnp.sum(p, axis=1, keepdims=True)
            term = vbuf[slot, h * npg + pg] * p
            acc = term if acc is None else acc + term
        ocols.append((jnp.sum(acc, axis=1, keepdims=True) + pn * vnh) / denom)
    lane = lax.broadcasted_iota(jnp.int32, ot_ref.shape, 1)
    ot_ref[...] = jnp.where(lane == b, jnp.concatenate(ocols, axis=0), ot_ref[...])

    @pl.when(b == nb - 1)
    def _():
        o_ref[...] = ot_ref[...].T


def _moba_sample(idx, page_table, q, kn, vn, cache_kt, cache_vt, layer):
    db = q.shape[0]
    ppb = MOBA_BLOCK // PAGE_SIZE
    nbuf = C_HEADS * MOBA_TOPK * ppb
    full = lambda *shape: pl.BlockSpec(shape, lambda b, *_: (0,) * len(shape))
    return pl.pallas_call(
        functools.partial(_moba_sample_kernel, pages_per_block=ppb, layer=layer),
        grid_spec=pltpu.PrefetchScalarGridSpec(
            num_scalar_prefetch=2,
            grid=(db,),
            in_specs=[full(db, C_QW), full(db, C_KVW), full(db, C_KVW),
                      pl.BlockSpec(memory_space=pl.ANY), pl.BlockSpec(memory_space=pl.ANY)],
            out_specs=full(db, C_QW),
            scratch_shapes=[pltpu.VMEM((2, nbuf, HEAD_DIM, PAGE_SIZE), F32),
                            pltpu.VMEM((2, nbuf, HEAD_DIM, PAGE_SIZE), F32),
                            pltpu.SemaphoreType.DMA((2, 2)),
                            pltpu.VMEM((C_QW, db), F32), pltpu.VMEM((C_KVW, db), F32),
                            pltpu.VMEM((C_KVW, db), F32), pltpu.VMEM((C_QW, db), F32)]),
        out_shape=jax.ShapeDtypeStruct((db, C_QW), F32),
        compiler_params=_cparams("arbitrary"),
        name="moba_sample",
    )(idx, page_table, q, kn, vn, cache_kt, cache_vt)


def _decoder(st, x, pos, p, state):
    is_prompt = state is None
    nseq, seqlen, d = st.nseq, st.seqlen, st.d
    depth = p['w_ada'].shape[0]
    rope_tabs = _rope_tables(pos if is_prompt else jnp.broadcast_to(pos, (st.tokens,)))
    new = {'win_k': [], 'win_v': [], 'conv': [], 'ssm': [], 'k': [], 'v': []}
    for layer in range(depth):
        g_norm = p['norm_g'][layer]
        if layer % 2 == 0:
            i = layer // 2
            splits = ((0, A_QW, True), (A_QW, A_KVW, True), (A_QW + A_KVW, A_KVW, False),
                      (A_QW + 2 * A_KVW, MB_INNER, False), (A_QW + 2 * A_KVW + MB_INNER, MB_CONV_DIM, False),
                      (A_QW + 2 * A_KVW + MB_INNER + MB_CONV_DIM, LANES, False))
            qa, ka, va, z, xbc, dt = _proj(st, layer, x, g_norm[0], p['w_in_a'][i], rope_tabs, splits)
            if is_prompt:
                o_a = _swa_prompt(qa, ka, va, p['sinks'][i], nseq, seqlen)
                wk = ka.reshape(nseq, seqlen, A_KV_HEADS, HEAD_DIM)[:, -WINDOW:]
                wv = va.reshape(nseq, seqlen, A_KV_HEADS, HEAD_DIM)[:, -WINDOW:]
                conv_prev = jnp.zeros((nseq, MB_CONV - 1, MB_CONV_DIM), F32)
                h0 = jnp.zeros((nseq, MB_HEADS, MB_HEADDIM, MB_DSTATE), F32)
                o_b, conv_new, h_new = _ssd_prompt(xbc, z, dt, conv_prev, h0, p['conv_w'][i], p['conv_b'][i],
                                                   p['dt_bias'][i], p['a_log'][i], p['d_skip'][i],
                                                   p['ssm_norm_g'][i], nseq, seqlen)
            else:
                o_a, wk, wv = _swa_sample(qa, ka, va, state['win_k'][i].reshape(nseq, WINDOW, A_KVW),
                                          state['win_v'][i].reshape(nseq, WINDOW, A_KVW), p['sinks'][i])
                wk = wk.reshape(nseq, WINDOW, A_KV_HEADS, HEAD_DIM)
                wv = wv.reshape(nseq, WINDOW, A_KV_HEADS, HEAD_DIM)
                o_b, conv_new, h_new = _ssd_sample(xbc, z, dt, state['conv'][i], state['ssm'][i], p['conv_w'][i],
                                                   p['conv_b'][i], p['dt_bias'][i], p['a_log'][i], p['d_skip'][i],
                                                   p['ssm_norm_g'][i])
            a_list = [o_a, o_b]
            w_list = [p['w_out_a'][i][:A_QW], p['w_out_a'][i][A_QW:]]
            new['win_k'].append(wk)
            new['win_v'].append(wv)
            new['conv'].append(conv_new)
            new['ssm'].append(h_new)
        else:
            j = layer // 2
            splits = ((0, C_QW, True), (C_QW, C_KVW, True), (C_QW + C_KVW, C_KVW, False))
            qc, kc, vc = _proj(st, layer, x, g_norm[0], p['w_in_c'][j], rope_tabs, splits)
            if is_prompt:
                kmean = _kmean(kc, nseq, seqlen)
                o_c = _moba_prompt(qc, kc, vc, kmean, nseq, seqlen)
            else:
                idx = _moba_gate_sample(state['page_table'], qc, state['cache_k'], j)
                o_c = _moba_sample(idx, state['page_table'], qc, kc, vc, state['cache_k'], state['cache_v'], j)
            a_list = [o_c]
            w_list = [p['w_out_c'][j]]
            new['k'].append(kc.reshape(nseq, seqlen, C_KV_HEADS, HEAD_DIM))
            new['v'].append(vc.reshape(nseq, seqlen, C_KV_HEADS, HEAD_DIM))
        sorted_moe = st.tokens >= 2 * SORT_TILE
        moe = _moe_sorted if sorted_moe else _moe
        x1, h2, aux = _mix(st, layer, x, a_list, w_list, g_norm[1], p['wr_pad'], p['rb_col'], sorted_moe)
        x = moe(st, layer, h2, aux, p['w_gate'], p['w_up'], p['w_down'], x1, p['final_norm_g'],
                final=(layer == depth - 1))
    return x.reshape(nseq, seqlen, d), {name: jnp.stack(rows) for name, rows in new.items()}


def kernel(x_prompt, x_sample, c_prompt, c_sample, state_win_k, state_win_v, state_conv, state_ssm, cache_k, cache_v, page_table, w_ada, b_ada, norm_g, w_in_a, sinks, conv_w, conv_b, dt_bias, a_log, d_skip, ssm_norm_g, w_out_a, w_in_c, w_out_c, w_router, router_bias, w_gate, w_up, w_down, final_norm_g):
    nb, seqlen, d = x_prompt.shape
    db, dec_seq, _ = x_sample.shape
    assert dec_seq == 1 and seqlen % MOBA_BLOCK == 0 and d % LANES == 0
    n_odd, n_pool, page_size, ckv, hd = cache_k.shape
    assert page_size == PAGE_SIZE and ckv == C_KV_HEADS and hd == HEAD_DIM
    n_pages = page_table.shape[1]

    n_in_a = w_in_a.shape[2]
    pad_a = (-n_in_a) % LANES
    p = {'w_ada': w_ada, 'norm_g': norm_g, 'sinks': sinks, 'conv_w': conv_w, 'conv_b': conv_b, 'dt_bias': dt_bias,
         'a_log': a_log, 'd_skip': d_skip, 'ssm_norm_g': ssm_norm_g, 'final_norm_g': final_norm_g,
         'w_in_a': jnp.pad(w_in_a, ((0, 0), (0, 0), (0, pad_a))).astype(BF16),
         'w_out_a': w_out_a.astype(BF16), 'w_in_c': w_in_c.astype(BF16), 'w_out_c': w_out_c.astype(BF16),
         'w_gate': w_gate.astype(BF16), 'w_up': w_up.astype(BF16), 'w_down': w_down.astype(BF16),
         'wr_pad': jnp.pad(w_router, ((0, 0), (0, LANES - N_EXPERTS))),
         'rb_col': jnp.pad(router_bias, (0, LANES - N_EXPERTS)).reshape(LANES, 1)}

    mod = _ada(jnp.concatenate([c_prompt, c_sample], axis=0), w_ada, b_ada)
    st_p = _Stream(nb, seqlen, d, mod[:, :nb])
    st_s = _Stream(db, 1, d, mod[:, nb:])

    pos_p = jnp.arange(seqlen, dtype=jnp.int32)
    y_prompt, pn = _decoder(st_p, x_prompt.reshape(nb * seqlen, d), pos_p, p, None)

    cache_kt = jnp.transpose(cache_k, (0, 1, 3, 4, 2)).reshape(n_odd, n_pool, C_KVW, PAGE_SIZE)
    cache_vt = jnp.transpose(cache_v, (0, 1, 3, 4, 2)).reshape(n_odd, n_pool, C_KVW, PAGE_SIZE)
    state = {'win_k': state_win_k, 'win_v': state_win_v, 'conv': state_conv, 'ssm': state_ssm,
             'cache_k': cache_kt, 'cache_v': cache_vt, 'page_table': page_table}
    pos_s = n_pages * PAGE_SIZE + jnp.arange(1, dtype=jnp.int32)
    y_sample, sn = _decoder(st_s, x_sample.reshape(db, d), pos_s, p, state)
    return (y_prompt, y_sample, pn['win_k'], pn['win_v'], pn['conv'], pn['ssm'], pn['k'], pn['v'],
            sn['win_k'], sn['win_v'], sn['conv'], sn['ssm'], sn['k'], sn['v'])
```

```python
import functools
import math

import jax
import jax.numpy as jnp
import numpy as np
from jax import lax
from jax.experimental import pallas as pl
from jax.experimental.pallas import tpu as pltpu

F32 = jnp.float32
BF16 = jnp.bfloat16

HEAD_DIM = 64
ROT_DIM = HEAD_DIM // 4
ROPE_THETA = 500000.0
A_HEADS = 8
A_KV_HEADS = 2
A_GROUP = A_HEADS // A_KV_HEADS
WINDOW = 128
MB_HEADDIM = 64
MB_HEADS = 8
MB_INNER = MB_HEADS * MB_HEADDIM
MB_GROUPS = 2
MB_DSTATE = 128
MB_CONV = 4
MB_GN = MB_GROUPS * MB_DSTATE
MB_CONV_DIM = MB_INNER + 2 * MB_GN
SSD_CHUNK = 128
C_HEADS = 16
C_KV_HEADS = 4
C_GROUP = C_HEADS // C_KV_HEADS
MOBA_BLOCK = 256
MOBA_TOPK = 3
MOBA_QBLOCK = 128
PAGE_SIZE = 128
A_QW = A_HEADS * HEAD_DIM
A_KVW = A_KV_HEADS * HEAD_DIM
C_QW = C_HEADS * HEAD_DIM
C_KVW = C_KV_HEADS * HEAD_DIM
N_EXPERTS = 16
N_EXPERT_GROUPS = 4
EXPERTS_PER_GROUP = N_EXPERTS // N_EXPERT_GROUPS
PAIRS_PER_GROUP = EXPERTS_PER_GROUP * (EXPERTS_PER_GROUP - 1) // 2
N_BUCKETS = N_EXPERT_GROUPS * PAIRS_PER_GROUP
BUCKET_ROWS = 32
ROUTE_ROWS = 8
SORT_TILE = 256
RMS_EPS = 1e-6
NEG_INF = -1e30
ATT_SCALE = HEAD_DIM ** -0.5

LANES = 128
VMEM_LIMIT_BYTES = 56 * 1024 * 1024


def _cparams(*sem):
    return pltpu.CompilerParams(dimension_semantics=sem, vmem_limit_bytes=VMEM_LIMIT_BYTES)


def _dot(a, b):
    return jnp.dot(a, b, preferred_element_type=F32)


def _dot_nt(a, b):
    return lax.dot_general(a, b, (((1,), (1,)), ((), ())), preferred_element_type=F32)


def _dot_tn(a, b):
    return lax.dot_general(a, b, (((0,), (0,)), ((), ())), preferred_element_type=F32)


def _split2(x):
    hi = x.astype(BF16)
    lo = (x - hi.astype(F32)).astype(BF16)
    return hi, lo


def _dot_hi(a, b):
    ah, al = _split2(a)
    bh, bl = _split2(b)
    return _dot(ah, bh) + (_dot(al, bh) + _dot(ah, bl))


def _dot_hi_nt(a, b):
    ah, al = _split2(a)
    bh, bl = _split2(b)
    return _dot_nt(ah, bh) + (_dot_nt(al, bh) + _dot_nt(ah, bl))


def _dot_exact_lhs(a_bf16, b):
    b0 = b.astype(BF16)
    r = b - b0.astype(F32)
    b1 = r.astype(BF16)
    b2 = (r - b1.astype(F32)).astype(BF16)
    return _dot(a_bf16, b0) + (_dot(a_bf16, b1) + _dot(a_bf16, b2))


def _sigmoid(x):
    return 1.0 / (1.0 + jnp.exp(-x))


def _silu(x):
    return x * _sigmoid(x)


def _softplus(x):
    return jnp.maximum(x, 0.0) + jnp.log(1.0 + jnp.exp(-jnp.abs(x)))


def _rms(x, g):
    return x * lax.rsqrt(jnp.mean(x * x, axis=-1, keepdims=True) + RMS_EPS) * g


def _rope_tables(pos):
    half = ROT_DIM // 2
    inv_freq = jnp.power(ROPE_THETA, -jnp.arange(half, dtype=F32) / half)
    ang = pos.astype(F32)[:, None] * inv_freq
    cos, sin = jnp.cos(ang), jnp.sin(ang)
    n = pos.shape[0]
    rest = HEAD_DIM - ROT_DIM
    ct = jnp.concatenate([cos, cos, jnp.ones((n, rest), F32)], axis=1)
    sa = jnp.concatenate([jnp.zeros((n, half), F32), sin, jnp.zeros((n, rest), F32)], axis=1)
    sb = jnp.concatenate([-sin, jnp.zeros((n, half), F32), jnp.zeros((n, rest), F32)], axis=1)
    rep = LANES // HEAD_DIM
    return jnp.tile(ct, (1, rep)), jnp.tile(sa, (1, rep)), jnp.tile(sb, (1, rep))


def _rope(x, ct, sa, sb):
    half = ROT_DIM // 2
    return x * ct + pltpu.roll(x, half, 1) * sa + pltpu.roll(x, LANES - half, 1) * sb


def _ada_kernel(c_ref, w_ref, b_ref, o_ref):
    o_ref[...] = _dot_hi(_silu(c_ref[...]), w_ref[...]) + b_ref[...]


def _ada(c_all, w_ada, b_ada):
    depth, d, n6 = w_ada.shape
    nc = c_all.shape[0]
    nk = n6 // d
    return pl.pallas_call(
        _ada_kernel,
        grid=(depth, nk),
        in_specs=[pl.BlockSpec((nc, d), lambda l, k: (0, 0)),
                  pl.BlockSpec((None, d, d), lambda l, k: (l, 0, k)),
                  pl.BlockSpec((None, 1, d), lambda l, k: (l, 0, k))],
        out_specs=pl.BlockSpec((None, nc, d), lambda l, k: (l, 0, k)),
        out_shape=jax.ShapeDtypeStruct((depth, nc, n6), F32),
        compiler_params=_cparams("parallel", "parallel"),
        name="ada",
    )(c_all, w_ada, b_ada.reshape(depth, 1, n6))


class _Stream:
    def __init__(self, nseq, seqlen, d, mod):
        self.nseq, self.seqlen, self.d = nseq, seqlen, d
        self.tokens = nseq * seqlen
        if seqlen == 1:
            self.tm = self.tokens
            self.tiles_per_seq = None
            self.mod = mod
        else:
            self.tm = math.gcd(seqlen, 512)
            self.tiles_per_seq = seqlen // self.tm
            depth = mod.shape[0]
            self.mod = mod.reshape(depth, nseq * 6, 1, d)
        self.ntiles = self.tokens // self.tm

    def mod_arg(self, layer):
        return self.mod[layer]

    def mod_spec(self, k):
        if self.tiles_per_seq is None:
            return pl.BlockSpec((self.tm, self.d), lambda i, *_: (0, k))
        tps = self.tiles_per_seq
        return pl.BlockSpec((None, 1, self.d), lambda i, *_: ((i // tps) * 6 + k, 0, 0))

    def rope_spec(self):
        if self.tiles_per_seq is None:
            return pl.BlockSpec((self.tm, LANES), lambda i, *_: (0, 0))
        tps = self.tiles_per_seq
        return pl.BlockSpec((self.tm, LANES), lambda i, *_: (i % tps, 0))


def _proj_kernel(x_ref, sh_ref, sc_ref, g_ref, w_ref, ct_ref, sa_ref, sb_ref, *out_refs, splits, extras):
    h = _rms(x_ref[...], g_ref[...]) * (1.0 + sc_ref[...]) + sh_ref[...]
    u = _dot(h.astype(BF16), w_ref[...])
    outs = list(out_refs)
    pieces = []
    for start, width, rope, emit in splits:
        if rope:
            ct, sa, sb = ct_ref[...], sa_ref[...], sb_ref[...]
            piece = jnp.concatenate([_rope(u[:, start + c0:start + c0 + LANES], ct, sa, sb)
                                     for c0 in range(0, width, LANES)], axis=-1)
        else:
            piece = u[:, start:start + width]
        pieces.append(piece)
        if emit:
            outs.pop(0)[...] = piece
    for kind, idx in extras:
        o_ref, piece = outs.pop(0), pieces[idx]
        if kind == 'bf16':
            o_ref[...] = piece.astype(BF16)
        elif kind == 'transposed':
            o_ref[...] = piece.T
        else:
            for c in range(o_ref.shape[0]):
                o_ref[c] = jnp.sum(piece[c * MOBA_BLOCK:(c + 1) * MOBA_BLOCK, :], axis=0,
                                   keepdims=True) * (1.0 / MOBA_BLOCK)


def _proj(st, layer, x, norm_g, w_bf16, rope_tabs, splits, extras=()):
    d = st.d
    n = w_bf16.shape[1]
    tm = st.tm
    row = lambda i: (i, 0)
    const = lambda i: (0, 0)
    out_specs = [pl.BlockSpec((tm, w), row) for _, w, _, emit in splits if emit]
    out_shape = [jax.ShapeDtypeStruct((st.tokens, w), F32) for _, w, _, emit in splits if emit]
    for kind, idx in extras:
        w = splits[idx][1]
        if kind == 'bf16':
            out_specs.append(pl.BlockSpec((tm, w), row))
            out_shape.append(jax.ShapeDtypeStruct((st.tokens, w), BF16))
        elif kind == 'transposed':
            tps = st.tiles_per_seq
            out_specs.append(pl.BlockSpec((None, w, tm), lambda i: (i // tps, 0, i % tps)))
            out_shape.append(jax.ShapeDtypeStruct((st.nseq, w, st.seqlen), F32))
        else:
            nb = tm // MOBA_BLOCK
            out_specs.append(pl.BlockSpec((nb, 1, w), lambda i: (i, 0, 0)))
            out_shape.append(jax.ShapeDtypeStruct((st.tokens // MOBA_BLOCK, 1, w), F32))
    return pl.pallas_call(
        functools.partial(_proj_kernel, splits=splits, extras=extras),
        grid=(st.ntiles,),
        in_specs=[pl.BlockSpec((tm, d), row), st.mod_spec(0), st.mod_spec(1),
                  pl.BlockSpec((1, d), const), pl.BlockSpec((d, n), const),
                  st.rope_spec(), st.rope_spec(), st.rope_spec()],
        out_specs=out_specs,
        out_shape=out_shape,
        compiler_params=_cparams("parallel"),
        name="proj",
    )(x, st.mod_arg(layer), st.mod_arg(layer), norm_g.reshape(1, d), w_bf16, *rope_tabs)


def _swa_prompt_kernel(sink_ref, q_ref, kp_ref, kc_ref, vp_ref, vc_ref, o_ref):
    i = pl.program_id(1)
    q = q_ref[...]
    kk = jnp.concatenate([kp_ref[...], kc_ref[...]], axis=0).astype(BF16)
    vv = jnp.concatenate([vp_ref[...], vc_ref[...]], axis=0).astype(BF16)
    qb = q_ref.shape[0]
    row = lax.broadcasted_iota(jnp.int32, (qb, 2 * qb), 0)
    col = lax.broadcasted_iota(jnp.int32, (qb, 2 * qb), 1)
    diff = row + qb - col
    ok = (diff >= 0) & (diff <= WINDOW) & ((col >= qb) | (i > 0))
    outs = []
    for h in range(A_HEADS):
        kv = h // A_GROUP
        qh = q[:, h * HEAD_DIM:(h + 1) * HEAD_DIM].astype(BF16)
        s = _dot_nt(qh, kk[:, kv * HEAD_DIM:(kv + 1) * HEAD_DIM]) * ATT_SCALE
        s = jnp.where(ok, s, NEG_INF)
        sink = sink_ref[h]
        m = jnp.maximum(jnp.max(s, axis=-1, keepdims=True), sink)
        p = jnp.exp(s - m)
        denom = jnp.sum(p, axis=-1, keepdims=True) + jnp.exp(sink - m)
        outs.append(_dot(p.astype(BF16), vv[:, kv * HEAD_DIM:(kv + 1) * HEAD_DIM]) / denom)
    o_ref[...] = jnp.concatenate(outs, axis=-1)


def _swa_prompt(q, k, v, sinks, nseq, seqlen):
    qb = WINDOW
    nb = seqlen // qb
    cur = lambda b, i: (b * nb + i, 0)
    prev = lambda b, i: (b * nb + jnp.maximum(i - 1, 0), 0)
    return pl.pallas_call(
        _swa_prompt_kernel,
        grid=(nseq, nb),
        in_specs=[pl.BlockSpec(memory_space=pltpu.SMEM),
                  pl.BlockSpec((qb, A_QW), cur),
                  pl.BlockSpec((qb, A_KVW), prev), pl.BlockSpec((qb, A_KVW), cur),
                  pl.BlockSpec((qb, A_KVW), prev), pl.BlockSpec((qb, A_KVW), cur)],
        out_specs=pl.BlockSpec((qb, A_QW), cur),
        out_shape=jax.ShapeDtypeStruct(q.shape, F32),
        compiler_params=_cparams("parallel", "parallel"),
        name="swa_prompt",
    )(sinks, q, k, k, v, v)


def _ssd_prompt_kernel(xbc_ref, z_ref, dt_ref, cprev_ref, h0_ref, cw_ref, cb_ref, dtb_ref, alog_ref, dskip_ref,
                       ng_ref, o_ref, cnew_ref, hlast_ref, xp_ref, h_ref):
    c = pl.program_id(1)
    nc = pl.num_programs(1)
    q = SSD_CHUNK
    halo = MB_CONV - 1
    base = 8 - halo

    @pl.when(c == 0)
    def _():
        xp_ref[base:8, :] = cprev_ref[...]
        h_ref[...] = h0_ref[...]

    xp_ref[8:8 + q, :] = xbc_ref[...]
    acc = cb_ref[...] + xp_ref[base:base + q, :] * cw_ref[0:1, :]
    for tap in range(1, MB_CONV):
        acc = acc + xp_ref[base + tap:base + tap + q, :] * cw_ref[tap:tap + 1, :]
    tail = xp_ref[8 + q - halo:8 + q, :]
    xp_ref[base:8, :] = tail

    @pl.when(c == nc - 1)
    def _():
        cnew_ref[...] = tail

    xbc = _silu(acc)
    xs = xbc[:, :MB_INNER]
    dt = _softplus(dt_ref[...] + dtb_ref[...])
    da = dt * (-jnp.exp(alog_ref[...]))
    r_i = lax.broadcasted_iota(jnp.int32, (q, q), 0)
    c_i = lax.broadcasted_iota(jnp.int32, (q, q), 1)
    causal = r_i >= c_i
    acum = _dot_exact_lhs(causal.astype(BF16), da)
    acum_t = acum.T
    rpg = MB_HEADS // MB_GROUPS
    ys = []
    for g in range(MB_GROUPS):
        bq = xbc[:, MB_INNER + g * MB_DSTATE:MB_INNER + (g + 1) * MB_DSTATE].astype(BF16)
        cq = xbc[:, MB_INNER + MB_GN + g * MB_DSTATE:MB_INNER + MB_GN + (g + 1) * MB_DSTATE].astype(BF16)
        cbm = _dot_nt(cq, bq)
        for r in range(rpg):
            h = g * rpg + r
            a_col = acum[:, h:h + 1]
            a_row = acum_t[h:h + 1, :]
            a_last = acum[q - 1:q, h:h + 1]
            decay = jnp.where(causal, jnp.exp(jnp.where(causal, a_col - a_row, 0.0)), 0.0)
            xd = xs[:, h * MB_HEADDIM:(h + 1) * MB_HEADDIM] * dt[:, h:h + 1]
            hprev = h_ref[h]
            y = _dot((cbm * decay).astype(BF16), xd.astype(BF16))
            y = y + jnp.exp(a_col) * _dot_nt(cq, hprev.astype(BF16))
            xw = (xd * jnp.exp(a_last - a_col)).astype(BF16)
            states = lax.dot_general(xw, bq, (((0,), (0,)), ((), ())), preferred_element_type=F32)
            h_ref[h] = hprev * jnp.exp(a_last) + states
            ys.append(y)
    y = jnp.concatenate(ys, axis=-1) + xs * dskip_ref[...]
    y = y * _silu(z_ref[...])
    gw = MB_INNER // MB_GROUPS
    ng = ng_ref[...]
    o_ref[...] = jnp.concatenate(
        [_rms(y[:, g * gw:(g + 1) * gw], ng[:, g * gw:(g + 1) * gw]) for g in range(MB_GROUPS)], axis=-1)

    @pl.when(c == nc - 1)
    def _():
        hlast_ref[...] = h_ref[...]


def _pad_lanes(v, n=LANES):
    return jnp.pad(v, (0, n - v.shape[0])).reshape(1, n)


def _ssd_prompt(xbc, z, dt, conv_prev, h0, conv_w, conv_b, dt_bias, a_log, d_skip, norm_g, nseq, seqlen):
    q = SSD_CHUNK
    nc = seqlen // q
    row = lambda b, c: (b * nc + c, 0)
    per_seq3 = lambda b, c: (b, 0, 0)
    per_seq4 = lambda b, c: (b, 0, 0, 0)
    const = lambda b, c: (0, 0)
    halo = MB_CONV - 1
    return pl.pallas_call(
        _ssd_prompt_kernel,
        grid=(nseq, nc),
        in_specs=[pl.BlockSpec((q, MB_CONV_DIM), row), pl.BlockSpec((q, MB_INNER), row),
                  pl.BlockSpec((q, LANES), row),
                  pl.BlockSpec((None, halo, MB_CONV_DIM), per_seq3),
                  pl.BlockSpec((None, MB_HEADS, MB_HEADDIM, MB_DSTATE), per_seq4),
                  pl.BlockSpec((MB_CONV, MB_CONV_DIM), const), pl.BlockSpec((1, MB_CONV_DIM), const),
                  pl.BlockSpec((1, LANES), const), pl.BlockSpec((1, LANES), const),
                  pl.BlockSpec((1, MB_INNER), const), pl.BlockSpec((1, MB_INNER), const)],
        out_specs=[pl.BlockSpec((q, MB_INNER), row),
                   pl.BlockSpec((None, halo, MB_CONV_DIM), per_seq3),
                   pl.BlockSpec((None, MB_HEADS, MB_HEADDIM, MB_DSTATE), per_seq4)],
        out_shape=[jax.ShapeDtypeStruct((nseq * seqlen, MB_INNER), F32),
                   jax.ShapeDtypeStruct((nseq, halo, MB_CONV_DIM), F32),
                   jax.ShapeDtypeStruct((nseq, MB_HEADS, MB_HEADDIM, MB_DSTATE), F32)],
        scratch_shapes=[pltpu.VMEM((8 + q, MB_CONV_DIM), F32),
                        pltpu.VMEM((MB_HEADS, MB_HEADDIM, MB_DSTATE), F32)],
        compiler_params=_cparams("parallel", "arbitrary"),
        name="ssd_prompt",
    )(xbc, z, dt, conv_prev, h0, conv_w, conv_b.reshape(1, -1), _pad_lanes(dt_bias), _pad_lanes(a_log),
      jnp.repeat(d_skip, MB_HEADDIM).reshape(1, -1), norm_g.reshape(1, -1))


def _route(logits_t, rbias_col):
    tm = logits_t.shape[1]
    scores = _sigmoid(logits_t[0:N_EXPERTS, :])
    biased = scores + rbias_col[0:N_EXPERTS, :]
    s = [scores[e:e + 1, :] for e in range(N_EXPERTS)]
    b = [biased[e:e + 1, :] for e in range(N_EXPERTS)]
    epg = EXPERTS_PER_GROUP
    gscore = []
    for g in range(N_EXPERT_GROUPS):
        v = b[g * epg:(g + 1) * epg]
        best = None
        for i in range(epg):
            for j in range(i + 1, epg):
                pair = v[i] + v[j]
                best = pair if best is None else jnp.maximum(best, pair)
        gscore.append(best)
    gsel = jnp.zeros((1, tm), jnp.int32)
    gbest = gscore[0]
    for g in range(1, N_EXPERT_GROUPS):
        better = gscore[g] > gbest
        gsel = jnp.where(better, g, gsel)
        gbest = jnp.where(better, gscore[g], gbest)
    bs, ss = [], []
    for k in range(epg):
        bk, sk = b[k], s[k]
        for g in range(1, N_EXPERT_GROUPS):
            bk = jnp.where(gsel == g, b[g * epg + k], bk)
            sk = jnp.where(gsel == g, s[g * epg + k], sk)
        bs.append(bk)
        ss.append(sk)
    i1 = jnp.zeros((1, tm), jnp.int32)
    m1 = bs[0]
    for k in range(1, epg):
        better = bs[k] > m1
        i1 = jnp.where(better, k, i1)
        m1 = jnp.where(better, bs[k], m1)
    i2 = jnp.full((1, tm), -1, jnp.int32)
    m2 = jnp.full((1, tm), -jnp.inf, F32)
    for k in range(epg):
        better = (i1 != k) & ((bs[k] > m2) | (i2 < 0))
        i2 = jnp.where(better, k, i2)
        m2 = jnp.where(better, bs[k], m2)
    s1 = jnp.zeros((1, tm), F32)
    s2 = jnp.zeros((1, tm), F32)
    for k in range(epg):
        s1 = jnp.where(i1 == k, ss[k], s1)
        s2 = jnp.where(i2 == k, ss[k], s2)
    denom = s1 + s2
    rows = lax.broadcasted_iota(jnp.int32, (N_EXPERTS, tm), 0)
    comb = jnp.zeros((N_EXPERTS, tm), F32)
    for e in range(N_EXPERTS):
        g, k = divmod(e, epg)
        chosen = (gsel == g) & ((i1 == k) | (i2 == k))
        comb = jnp.where(rows == e, jnp.where(chosen, s[e] / denom, 0.0), comb)
    lo = jnp.minimum(i1, i2)
    hi = jnp.maximum(i1, i2)
    pair = jnp.where(lo == 0, 0, jnp.where(lo == 1, epg - 1, 2 * epg - 3)) + (hi - lo - 1)
    bucket = (gsel * PAIRS_PER_GROUP + pair).astype(F32)
    first_is_lo = i1 < i2
    w_lo = jnp.where(first_is_lo, s1, s2) / denom
    w_hi = jnp.where(first_is_lo, s2, s1) / denom
    return comb, bucket, w_lo, w_hi


def _mix_kernel(*refs, n_in, sorted_moe):
    x_ref, g1_ref = refs[0], refs[1]
    a_refs = refs[2:2 + n_in]
    w_refs = refs[2 + n_in:2 + 2 * n_in]
    ng_ref, sc_ref, sh_ref, wr_ref, rb_ref, x1_ref, h2_ref, aux_ref = refs[2 + 2 * n_in:]
    mix = _dot(a_refs[0][...].astype(BF16), w_refs[0][...])
    for a_ref, w_ref in zip(a_refs[1:], w_refs[1:]):
        mix = mix + _dot(a_ref[...].astype(BF16), w_ref[...])
    x1 = x_ref[...] + g1_ref[...] * mix
    x1_ref[...] = x1
    h2 = _rms(x1, ng_ref[...]) * (1.0 + sc_ref[...]) + sh_ref[...]
    logits = _dot_hi(h2, wr_ref[...])
    comb_t, bucket, w_lo, w_hi = _route(logits.T, rb_ref[...])
    tm, d = x1.shape
    if sorted_moe:
        r = lax.broadcasted_iota(jnp.int32, (LANES, tm), 0)
        rt = jnp.where(r == 0, bucket, jnp.where(r == 1, w_lo, jnp.where(r == 2, w_hi, 0.0)))
        h2_ref[:, :d] = h2
        h2_ref[:, d:] = rt.T
        aux_ref[...] = rt[0:aux_ref.shape[0], :]
    else:
        h2_ref[...] = h2.astype(BF16)
        aux_ref[...] = jnp.concatenate([comb_t, jnp.zeros((LANES - N_EXPERTS, tm), F32)], axis=0).T


def _mix(st, layer, x, a_list, w_list, norm_g, wr_pad, rb_col, sorted_moe):
    d, tm = st.d, st.tm
    row = lambda i: (i, 0)
    const = lambda i: (0, 0)
    n_in = len(a_list)
    in_specs = [pl.BlockSpec((tm, d), row), st.mod_spec(2)]
    in_specs += [pl.BlockSpec((tm, a.shape[1]), row) for a in a_list]
    in_specs += [pl.BlockSpec(w.shape, const) for w in w_list]
    in_specs += [pl.BlockSpec((1, d), const), st.mod_spec(4), st.mod_spec(3),
                 pl.BlockSpec((d, LANES), const), pl.BlockSpec((LANES, 1), const)]
    m = st.mod_arg(layer)
    if sorted_moe:
        out_specs = [pl.BlockSpec((tm, d), row), pl.BlockSpec((tm, d + LANES), row),
                     pl.BlockSpec((ROUTE_ROWS, tm), lambda i: (0, i))]
        out_shape = [jax.ShapeDtypeStruct((st.tokens, d), F32), jax.ShapeDtypeStruct((st.tokens, d + LANES), F32),
                     jax.ShapeDtypeStruct((ROUTE_ROWS, st.tokens), F32)]
    else:
        out_specs = [pl.BlockSpec((tm, d), row), pl.BlockSpec((tm, d), row), pl.BlockSpec((tm, LANES), row)]
        out_shape = [jax.ShapeDtypeStruct((st.tokens, d), F32), jax.ShapeDtypeStruct((st.tokens, d), BF16),
                     jax.ShapeDtypeStruct((st.tokens, LANES), F32)]
    return pl.pallas_call(
        functools.partial(_mix_kernel, n_in=n_in, sorted_moe=sorted_moe),
        grid=(st.ntiles,),
        in_specs=in_specs,
        out_specs=out_specs,
        out_shape=out_shape,
        compiler_params=_cparams("parallel"),
        name="mix",
    )(x, m, *a_list, *w_list, norm_g.reshape(1, d), m, m, wr_pad, rb_col)


def _moe_kernel(h_ref, comb_ref, wg_ref, wu_ref, wd_ref, x1_ref, g2_ref, fg_ref, o_ref, acc_ref, *, final):
    e = pl.program_id(1)

    @pl.when(e == 0)
    def _():
        acc_ref[...] = jnp.zeros_like(acc_ref)

    h = h_ref[...]
    he = _silu(_dot(h, wg_ref[...])) * _dot(h, wu_ref[...])
    comb = comb_ref[...]
    lane = lax.broadcasted_iota(jnp.int32, comb.shape, 1)
    ce = jnp.sum(jnp.where(lane == e, comb, 0.0), axis=-1, keepdims=True)
    acc_ref[...] += ce * _dot(he.astype(BF16), wd_ref[...])

    @pl.when(e == pl.num_programs(1) - 1)
    def _():
        x2 = x1_ref[...] + g2_ref[...] * acc_ref[...]
        o_ref[...] = _rms(x2, fg_ref[...]) if final else x2


def _moe(st, layer, h2, comb, wg, wu, wd, x1, final_g, final):
    d, tm = st.d, st.tm
    ne, _, dff = wg.shape[1:]
    row = lambda i, e: (i, 0)
    return pl.pallas_call(
        functools.partial(_moe_kernel, final=final),
        grid=(st.ntiles, ne),
        in_specs=[pl.BlockSpec((tm, d), row), pl.BlockSpec((tm, LANES), row),
                  pl.BlockSpec((None, None, d, dff), lambda i, e: (layer, e, 0, 0)),
                  pl.BlockSpec((None, None, d, dff), lambda i, e: (layer, e, 0, 0)),
                  pl.BlockSpec((None, None, dff, d), lambda i, e: (layer, e, 0, 0)),
                  pl.BlockSpec((tm, d), row), st.mod_spec(5),
                  pl.BlockSpec((1, d), lambda i, e: (0, 0))],
        out_specs=pl.BlockSpec((tm, d), row),
        out_shape=jax.ShapeDtypeStruct((st.tokens, d), F32),
        scratch_shapes=[pltpu.VMEM((tm, d), F32)],
        compiler_params=_cparams("parallel", "arbitrary"),
        name="moe",
    )(h2, comb, wg, wu, wd, x1, st.mod_arg(layer), final_g.reshape(1, d))


def _bucket_rank_kernel(rt_ref, rank_ref, cnt_ref, carry_ref):
    i = pl.program_id(0)
    tm = rt_ref.shape[1]

    @pl.when(i == 0)
    def _():
        carry_ref[...] = jnp.zeros_like(carry_ref)

    bucket = rt_ref[0:1, :]
    rows = lax.broadcasted_iota(jnp.int32, (BUCKET_ROWS, tm), 0).astype(F32)
    onehot = rows == bucket
    s_i = lax.broadcasted_iota(jnp.int32, (tm, tm), 0)
    t_i = lax.broadcasted_iota(jnp.int32, (tm, tm), 1)
    incl = jnp.where(s_i <= t_i, 1.0, 0.0).astype(BF16)
    cum = _dot(jnp.where(onehot, 1.0, 0.0).astype(BF16), incl)
    carry = carry_ref[...]
    rank = jnp.sum(jnp.where(onehot, cum - 1.0 + carry, 0.0), axis=0, keepdims=True)
    rank_ref[...] = rank.astype(jnp.int32)
    carry = carry + cum[:, tm - 1:tm]
    carry_ref[...] = carry
    cnt_ref[...] = jnp.broadcast_to(carry, cnt_ref.shape)


def _bucket_rank(st, rt):
    tm = st.tm
    return pl.pallas_call(
        _bucket_rank_kernel,
        grid=(st.ntiles,),
        in_specs=[pl.BlockSpec((ROUTE_ROWS, tm), lambda i: (0, i))],
        out_specs=[pl.BlockSpec((1, tm), lambda i: (0, i)), pl.BlockSpec((BUCKET_ROWS, LANES), lambda i: (0, 0))],
        out_shape=[jax.ShapeDtypeStruct((1, st.tokens), jnp.int32),
                   jax.ShapeDtypeStruct((BUCKET_ROWS, LANES), F32)],
        scratch_shapes=[pltpu.VMEM((BUCKET_ROWS, 1), F32)],
        compiler_params=_cparams("arbitrary"),
        name="bucket_rank",
    )(rt)


def _sort_plan(rt, rank, cnt, tokens):
    ts = SORT_TILE
    bucket = rt[0].astype(jnp.int32)
    counts = cnt[:N_BUCKETS, 0].astype(jnp.int32)
    padded = (counts + ts - 1) // ts * ts
    ends = jnp.cumsum(padded)
    dest = (ends - padded)[bucket] + rank[0]
    n_tiles = -(-tokens // ts) + N_BUCKETS
    n_used = ends[-1] // ts
    tile = jnp.arange(n_tiles, dtype=jnp.int32)
    tile_row = jnp.minimum(tile, n_used - 1) * ts
    tile_bucket = jnp.sum((ends[None, :] <= tile_row[:, None]).astype(jnp.int32), axis=1)
    tile_bucket = jnp.minimum(tile_bucket, N_BUCKETS - 1)
    pairs = [(a, b) for a in range(EXPERTS_PER_GROUP) for b in range(a + 1, EXPERTS_PER_GROUP)]
    lo_tab = jnp.array([a for a, _ in pairs], jnp.int32)
    hi_tab = jnp.array([b for _, b in pairs], jnp.int32)
    base = tile_bucket // PAIRS_PER_GROUP * EXPERTS_PER_GROUP
    pair = tile_bucket % PAIRS_PER_GROUP
    return dest, base + lo_tab[pair], base + hi_tab[pair], n_used.reshape(1).astype(jnp.int32), n_tiles


def _dispatch_kernel(dest_ref, h_ref, init_ref, out_ref, sem):
    del init_ref
    i = pl.program_id(0)
    tm = h_ref.shape[0]

    def body(r, carry):
        pltpu.make_async_copy(h_ref.at[pl.ds(r, 1)], out_ref.at[pl.ds(dest_ref[i * tm + r], 1)], sem).start()
        return carry

    lax.fori_loop(0, tm, body, 0, unroll=8)
    pltpu.make_async_copy(h_ref, out_ref.at[pl.ds(0, tm)], sem).wait()


def _dispatch(st, dest, h2w, n_rows):
    tm = st.tm
    w = h2w.shape[1]
    return pl.pallas_call(
        _dispatch_kernel,
        grid_spec=pltpu.PrefetchScalarGridSpec(
            num_scalar_prefetch=1,
            grid=(st.ntiles,),
            in_specs=[pl.BlockSpec((tm, w), lambda i, d: (i, 0)), pl.BlockSpec(memory_space=pl.ANY)],
            out_specs=pl.BlockSpec(memory_space=pl.ANY),
            scratch_shapes=[pltpu.SemaphoreType.DMA(())]),
        out_shape=jax.ShapeDtypeStruct((n_rows, w), F32),
        input_output_aliases={2: 0},
        compiler_params=_cparams("arbitrary"),
        name="moe_dispatch",
    )(dest, h2w, jnp.zeros((n_rows, w), F32))


def _experts_kernel(elo_ref, ehi_ref, nused_ref, hs_ref, wgl_ref, wul_ref, wdl_ref, wgh_ref, wuh_ref, wdh_ref, y_ref):
    del elo_ref, ehi_ref
    i = pl.program_id(0)
    d = y_ref.shape[1]

    @pl.when(i < nused_ref[0])
    def _():
        h = hs_ref[:, :d].astype(BF16)

        def expert(wg_ref, wu_ref, wd_ref):
            he = _silu(_dot(h, wg_ref[...])) * _dot(h, wu_ref[...])
            return _dot(he.astype(BF16), wd_ref[...])

        y_ref[...] = (hs_ref[:, d + 1:d + 2] * expert(wgl_ref, wul_ref, wdl_ref)
                      + hs_ref[:, d + 2:d + 3] * expert(wgh_ref, wuh_ref, wdh_ref))

    @pl.when(i >= nused_ref[0])
    def _():
        y_ref[...] = jnp.zeros_like(y_ref)


def _experts(layer, hs, e_lo, e_hi, n_used, n_tiles, wg, wu, wd):
    ts = SORT_TILE
    _, _, d, dff = wg.shape
    lo = lambda i, elo, ehi, nu: (layer, elo[i], 0, 0)
    hi = lambda i, elo, ehi, nu: (layer, ehi[i], 0, 0)
    up = lambda idx: pl.BlockSpec((None, None, d, dff), idx)
    down = lambda idx: pl.BlockSpec((None, None, dff, d), idx)
    return pl.pallas_call(
        _experts_kernel,
        grid_spec=pltpu.PrefetchScalarGridSpec(
            num_scalar_prefetch=3,
            grid=(n_tiles,),
            in_specs=[pl.BlockSpec((ts, hs.shape[1]), lambda i, *_: (i, 0)),
                      up(lo), up(lo), down(lo), up(hi), up(hi), down(hi)],
            out_specs=pl.BlockSpec((ts, d), lambda i, *_: (i, 0))),
        out_shape=jax.ShapeDtypeStruct((n_tiles * ts, d), F32),
        compiler_params=_cparams("arbitrary"),
        name="moe_experts",
    )(e_lo, e_hi, n_used, hs, wg, wu, wd, wg, wu, wd)


def _combine_kernel(dest_ref, x1_ref, g2_ref, fg_ref, ys_ref, o_ref, buf, sem, *, final):
    i = pl.program_id(0)
    n = pl.num_programs(0)
    tm = x1_ref.shape[0]
    slot = i % 2

    def start(tile, sl):
        def body(r, carry):
            pltpu.make_async_copy(ys_ref.at[pl.ds(dest_ref[tile * tm + r], 1)], buf.at[sl, pl.ds(r, 1)],
                                  sem.at[sl]).start()
            return carry

        lax.fori_loop(0, tm, body, 0, unroll=8)

    @pl.when(i == 0)
    def _():
        start(0, 0)

    @pl.when(i + 1 < n)
    def _():
        start(i + 1, 1 - slot)

    pltpu.make_async_copy(ys_ref.at[pl.ds(0, tm)], buf.at[slot], sem.at[slot]).wait()
    x2 = x1_ref[...] + g2_ref[...] * buf[slot]
    o_ref[...] = _rms(x2, fg_ref[...]) if final else x2


def _combine(st, layer, dest, ys, x1, final_g, final):
    d, tm = st.d, st.tm
    row = lambda i, dst: (i, 0)
    return pl.pallas_call(
        functools.partial(_combine_kernel, final=final),
        grid_spec=pltpu.PrefetchScalarGridSpec(
            num_scalar_prefetch=1,
            grid=(st.ntiles,),
            in_specs=[pl.BlockSpec((tm, d), row), st.mod_spec(5), pl.BlockSpec((1, d), lambda i, dst: (0, 0)),
                      pl.BlockSpec(memory_space=pl.ANY)],
            out_specs=pl.BlockSpec((tm, d), row),
            scratch_shapes=[pltpu.VMEM((2, tm, d), F32), pltpu.SemaphoreType.DMA((2,))]),
        out_shape=jax.ShapeDtypeStruct((st.tokens, d), F32),
        compiler_params=_cparams("arbitrary"),
        name="moe_combine",
    )(dest, x1, st.mod_arg(layer), final_g.reshape(1, d), ys)


def _moe_sorted(st, layer, h2w, rt, wg, wu, wd, x1, final_g, final):
    rank, cnt = _bucket_rank(st, rt)
    dest, e_lo, e_hi, n_used, n_tiles = _sort_plan(rt, rank, cnt, st.tokens)
    hs = _dispatch(st, dest, h2w, n_tiles * SORT_TILE)
    ys = _experts(layer, hs, e_lo, e_hi, n_used, n_tiles, wg, wu, wd)
    return _combine(st, layer, dest, ys, x1, final_g, final)


def _moba_prompt_kernel(q_ref, k_ref, vt_ref, km_ref, o_ref, sel_ref, qs_ref, m_ref, l_ref, acc_ref):
    i = pl.program_id(1)
    qb = MOBA_QBLOCK
    blk = MOBA_BLOCK
    nblk = km_ref.shape[0]
    own = (i * qb) // blk
    q_off = i * qb - own * blk
    q_t = q_ref[...].T
    km = km_ref[...]
    cols = C_GROUP * qb
    nb_pad = sel_ref.shape[1]
    blk_row = lax.broadcasted_iota(jnp.int32, (nb_pad, cols), 0)
    past = blk_row < own
    key_i = lax.broadcasted_iota(jnp.int32, (blk, cols), 0)
    qry_i = lax.broadcasted_iota(jnp.int32, (blk, cols), 1)
    own_ok = key_i <= (qry_i % qb) + q_off
    own_start = pl.multiple_of(own * blk, blk)
    kv_rows = [slice(kv * HEAD_DIM, (kv + 1) * HEAD_DIM) for kv in range(C_KV_HEADS)]

    state = []
    for kv, rows in enumerate(kv_rows):
        qs_t = jnp.concatenate(
            [q_t[(kv * C_GROUP + g) * HEAD_DIM:(kv * C_GROUP + g + 1) * HEAD_DIM, :] for g in range(C_GROUP)], axis=1)
        kmp = jnp.concatenate([km[:, rows], jnp.zeros((nb_pad - nblk, HEAD_DIM), F32)], axis=0)
        sg = jnp.where(past, _dot_hi(kmp, qs_t), NEG_INF)
        rank = jnp.zeros((nb_pad, cols), jnp.int32)
        for j in range(nblk):
            rj = sg[j:j + 1, :]
            rank = rank + ((rj > sg) | ((rj == sg) & (blk_row > j))).astype(jnp.int32)
        sel = (past & (rank < MOBA_TOPK)).astype(F32)
        qsb = (qs_t * ATT_SCALE).astype(BF16)
        s = _dot(k_ref[pl.ds(own_start, blk), rows], qsb)
        s = jnp.where(own_ok, s, NEG_INF)
        m0 = jnp.max(s, axis=0, keepdims=True)
        p = jnp.exp(s - m0)
        l0 = jnp.sum(p, axis=0, keepdims=True)
        acc0 = _dot(vt_ref[rows, pl.ds(own_start, blk)].astype(BF16), p.astype(BF16))
        state.append((sel, qsb, m0, l0, acc0))
    for kv, (sel, qsb, m0, l0, acc0) in enumerate(state):
        sel_ref[kv], qs_ref[kv], m_ref[kv], l_ref[kv], acc_ref[kv] = sel, qsb, m0, l0, acc0

    def body(j, carry):
        start = pl.multiple_of(j * blk, blk)
        kvs = range(C_KV_HEADS)
        chosen = [sel_ref[kv, pl.ds(j, 1), :] > 0.0 for kv in kvs]
        m_old = [m_ref[kv] for kv in kvs]
        l_old = [l_ref[kv] for kv in kvs]
        acc_old = [acc_ref[kv] for kv in kvs]
        s = [_dot(k_ref[pl.ds(start, blk), kv_rows[kv]], qs_ref[kv]) for kv in kvs]
        s = [jnp.where(chosen[kv], s[kv], NEG_INF) for kv in kvs]
        m_new = [jnp.maximum(m_old[kv], jnp.max(s[kv], axis=0, keepdims=True)) for kv in kvs]
        alpha = [jnp.exp(m_old[kv] - m_new[kv]) for kv in kvs]
        p = [jnp.exp(s[kv] - m_new[kv]) for kv in kvs]
        l_new = [alpha[kv] * l_old[kv] + jnp.sum(p[kv], axis=0, keepdims=True) for kv in kvs]
        pv = [_dot(vt_ref[kv_rows[kv], pl.ds(start, blk)].astype(BF16), p[kv].astype(BF16)) for kv in kvs]
        for kv in kvs:
            m_ref[kv] = m_new[kv]
            l_ref[kv] = l_new[kv]
            acc_ref[kv] = alpha[kv] * acc_old[kv] + pv[kv]
        return carry

    lax.fori_loop(0, own, body, 0)
    o_t = jnp.concatenate([acc_ref[kv] / l_ref[kv] for kv in range(C_KV_HEADS)], axis=0).T
    o_ref[...] = jnp.concatenate(
        [o_t[g * qb:(g + 1) * qb, kv * HEAD_DIM:(kv + 1) * HEAD_DIM]
         for kv in range(C_KV_HEADS) for g in range(C_GROUP)], axis=-1)


def _moba_prompt(q, k_rows, v_t, kmean, nseq, seqlen):
    qb = MOBA_QBLOCK
    nq = seqlen // qb
    nblk = kmean.shape[1]
    cols = C_GROUP * qb
    per_seq = lambda b, i: (b, 0, 0)
    return pl.pallas_call(
        _moba_prompt_kernel,
        grid=(nseq, nq),
        in_specs=[pl.BlockSpec((qb, C_QW), lambda b, i: (b * nq + i, 0)),
                  pl.BlockSpec((seqlen, C_KVW), lambda b, i: (b, 0)),
                  pl.BlockSpec((None, C_KVW, seqlen), per_seq),
                  pl.BlockSpec((None, nblk, C_KVW), per_seq)],
        out_specs=pl.BlockSpec((qb, C_QW), lambda b, i: (b * nq + i, 0)),
        out_shape=jax.ShapeDtypeStruct(q.shape, F32),
        scratch_shapes=[pltpu.VMEM((C_KV_HEADS, -(-nblk // 16) * 16, cols), F32),
                        pltpu.VMEM((C_KV_HEADS, HEAD_DIM, cols), BF16),
                        pltpu.VMEM((C_KV_HEADS, 1, cols), F32), pltpu.VMEM((C_KV_HEADS, 1, cols), F32),
                        pltpu.VMEM((C_KV_HEADS, HEAD_DIM, cols), F32)],
        compiler_params=_cparams("parallel", "parallel"),
        name="moba_prompt",
    )(q, k_rows, v_t, kmean)


def _swa_sample_kernel(sink_ref, q_ref, kn_ref, vn_ref, wk_ref, wv_ref, o_ref, nk_ref, nv_ref):
    q = q_ref[...]
    kn, vn = kn_ref[...], vn_ref[...]
    wk, wv = wk_ref[...], wv_ref[...]
    tb, w = wk.shape[0], wk.shape[1]
    nk_ref[:, 0:w - 1, :] = wk[:, 1:w, :]
    nk_ref[:, w - 1:w, :] = kn[:, None, :]
    nv_ref[:, 0:w - 1, :] = wv[:, 1:w, :]
    nv_ref[:, w - 1:w, :] = vn[:, None, :]
    lane = lax.broadcasted_iota(jnp.int32, (tb, LANES), 1)
    lane3 = lax.broadcasted_iota(jnp.int32, (tb, w, LANES), 2)
    heads = [None] * A_HEADS
    for g in range(A_GROUP):
        qg = jnp.concatenate(
            [q[:, (kv * A_GROUP + g) * HEAD_DIM:(kv * A_GROUP + g + 1) * HEAD_DIM] for kv in range(A_KV_HEADS)], axis=-1)
        prod = wk * qg[:, None, :]
        prod_n = kn * qg
        p_full = None
        pn_full = None
        for kv in range(A_KV_HEADS):
            lo, hi = kv * HEAD_DIM, (kv + 1) * HEAD_DIM
            sink = sink_ref[kv * A_GROUP + g]
            s = jnp.sum(prod[:, :, lo:hi], axis=-1, keepdims=True) * ATT_SCALE
            sn = jnp.sum(prod_n[:, lo:hi], axis=-1, keepdims=True) * ATT_SCALE
            m = jnp.maximum(jnp.maximum(jnp.max(s, axis=1), sn), sink)
            p = jnp.exp(s - m[:, None, :])
            pn = jnp.exp(sn - m)
            inv = 1.0 / (jnp.sum(p, axis=1) + pn + jnp.exp(sink - m))
            p = p * inv[:, None, :]
            pn = pn * inv
            pb = jnp.broadcast_to(p, (tb, w, LANES))
            pnb = jnp.broadcast_to(pn, (tb, LANES))
            if kv == 0:
                p_full, pn_full = pb, pnb
            else:
                p_full = jnp.where(lane3 >= lo, pb, p_full)
                pn_full = jnp.where(lane >= lo, pnb, pn_full)
        og = jnp.sum(p_full * wv, axis=1) + pn_full * vn
        for kv in range(A_KV_HEADS):
            heads[kv * A_GROUP + g] = og[:, kv * HEAD_DIM:(kv + 1) * HEAD_DIM]
    o_ref[...] = jnp.concatenate(heads, axis=-1)


def _swa_sample(q, kn, vn, win_k, win_v, sinks):
    db, w = win_k.shape[0], win_k.shape[1]
    tb = math.gcd(db, 8)
    row = lambda i: (i, 0)
    row3 = lambda i: (i, 0, 0)
    return pl.pallas_call(
        _swa_sample_kernel,
        grid=(db // tb,),
        in_specs=[pl.BlockSpec(memory_space=pltpu.SMEM),
                  pl.BlockSpec((tb, A_QW), row), pl.BlockSpec((tb, A_KVW), row), pl.BlockSpec((tb, A_KVW), row),
                  pl.BlockSpec((tb, w, A_KVW), row3), pl.BlockSpec((tb, w, A_KVW), row3)],
        out_specs=[pl.BlockSpec((tb, A_QW), row), pl.BlockSpec((tb, w, A_KVW), row3),
                   pl.BlockSpec((tb, w, A_KVW), row3)],
        out_shape=[jax.ShapeDtypeStruct((db, A_QW), F32), jax.ShapeDtypeStruct(win_k.shape, F32),
                   jax.ShapeDtypeStruct(win_v.shape, F32)],
        compiler_params=_cparams("parallel"),
        name="swa_sample",
    )(sinks, q, kn, vn, win_k, win_v)


def _ssd_sample_pre_kernel(xbc_ref, cst_ref, dt_ref, cw_ref, cb_ref, dtb_ref, alog_ref, exp_ref,
                           xs_ref, bm_ref, cm_ref, xd_ref, dec_ref, cnew_ref):
    xn = xbc_ref[...]
    halo = MB_CONV - 1
    acc = cb_ref[...] + xn * cw_ref[halo:halo + 1, :]
    for tap in range(halo):
        acc = acc + cst_ref[tap] * cw_ref[tap:tap + 1, :]
    for tap in range(1, halo):
        cnew_ref[tap - 1] = cst_ref[tap]
    cnew_ref[halo - 1] = xn
    xbc = _silu(acc)
    xs = xbc[:, :MB_INNER]
    xs_ref[...] = xs
    bm_ref[...] = xbc[:, MB_INNER:MB_INNER + MB_GN]
    cm_ref[...] = xbc[:, MB_INNER + MB_GN:]
    dt = _softplus(dt_ref[...] + dtb_ref[...])
    da = dt * (-jnp.exp(alog_ref[...]))
    expand = exp_ref[...]
    xd_ref[...] = xs * _dot_exact_lhs_rhs(dt, expand)
    dec_ref[...] = jnp.exp(_dot_exact_lhs_rhs(da, expand))


def _dot_exact_lhs_rhs(a, b_bf16):
    a0 = a.astype(BF16)
    r = a - a0.astype(F32)
    a1 = r.astype(BF16)
    a2 = (r - a1.astype(F32)).astype(BF16)
    return _dot(a0, b_bf16) + (_dot(a1, b_bf16) + _dot(a2, b_bf16))


def _ssd_sample_state_kernel(h_ref, xd_ref, dec_ref, bm_ref, cm_ref, hn_ref, y_ref):
    tb = h_ref.shape[0]
    rpg = MB_HEADS // MB_GROUPS
    for t in range(tb):
        for h in range(MB_HEADS):
            g = h // rpg
            rows = slice(h * MB_HEADDIM, (h + 1) * MB_HEADDIM)
            hn = h_ref[t, h] * dec_ref[t, rows, :] + xd_ref[t, rows, :] * bm_ref[t, g:g + 1, :]
            hn_ref[t, h] = hn
            y_ref[t, rows, :] = jnp.sum(hn * cm_ref[t, g:g + 1, :], axis=-1, keepdims=True)


def _ssd_sample_post_kernel(y_ref, xs_ref, z_ref, dskip_ref, ng_ref, o_ref):
    y = (y_ref[...] + xs_ref[...] * dskip_ref[...]) * _silu(z_ref[...])
    gw = MB_INNER // MB_GROUPS
    ng = ng_ref[...]
    o_ref[...] = jnp.concatenate(
        [_rms(y[:, g * gw:(g + 1) * gw], ng[:, g * gw:(g + 1) * gw]) for g in range(MB_GROUPS)], axis=-1)


def _ssd_sample(xbc, z, dt, conv_state, h0, conv_w, conv_b, dt_bias, a_log, d_skip, norm_g):
    db = xbc.shape[0]
    halo = MB_CONV - 1
    expand = (jnp.arange(LANES)[:, None] == (jnp.arange(MB_INNER) // MB_HEADDIM)[None, :]).astype(BF16)
    full = lambda *shape: pl.BlockSpec(shape, lambda: (0,) * len(shape))
    xs, bm, cm, xd, dec, conv_new = pl.pallas_call(
        _ssd_sample_pre_kernel,
        in_specs=[full(db, MB_CONV_DIM), full(halo, db, MB_CONV_DIM), full(db, LANES),
                  full(MB_CONV, MB_CONV_DIM), full(1, MB_CONV_DIM), full(1, LANES), full(1, LANES),
                  full(LANES, MB_INNER)],
        out_specs=[full(db, MB_INNER), full(db, MB_GN), full(db, MB_GN), full(db, MB_INNER), full(db, MB_INNER),
                   full(halo, db, MB_CONV_DIM)],
        out_shape=[jax.ShapeDtypeStruct((db, MB_INNER), F32), jax.ShapeDtypeStruct((db, MB_GN), F32),
                   jax.ShapeDtypeStruct((db, MB_GN), F32), jax.ShapeDtypeStruct((db, MB_INNER), F32),
                   jax.ShapeDtypeStruct((db, MB_INNER), F32), jax.ShapeDtypeStruct((halo, db, MB_CONV_DIM), F32)],
        compiler_params=pltpu.CompilerParams(vmem_limit_bytes=VMEM_LIMIT_BYTES),
        name="ssd_sample_pre",
    )(xbc, jnp.swapaxes(conv_state, 0, 1), dt, conv_w, conv_b.reshape(1, -1), _pad_lanes(dt_bias), _pad_lanes(a_log), expand)

    tb = math.gcd(db, 8)
    r3 = lambda i: (i, 0, 0)
    r4 = lambda i: (i, 0, 0, 0)
    h_new, y_col = pl.pallas_call(
        _ssd_sample_state_kernel,
        grid=(db // tb,),
        in_specs=[pl.BlockSpec((tb, MB_HEADS, MB_HEADDIM, MB_DSTATE), r4),
                  pl.BlockSpec((tb, MB_INNER, 1), r3), pl.BlockSpec((tb, MB_INNER, 1), r3),
                  pl.BlockSpec((tb, MB_GROUPS, MB_DSTATE), r3), pl.BlockSpec((tb, MB_GROUPS, MB_DSTATE), r3)],
        out_specs=[pl.BlockSpec((tb, MB_HEADS, MB_HEADDIM, MB_DSTATE), r4), pl.BlockSpec((tb, MB_INNER, 1), r3)],
        out_shape=[jax.ShapeDtypeStruct(h0.shape, F32), jax.ShapeDtypeStruct((db, MB_INNER, 1), F32)],
        compiler_params=_cparams("parallel"),
        name="ssd_sample_state",
    )(h0, xd.reshape(db, MB_INNER, 1), dec.reshape(db, MB_INNER, 1),
      bm.reshape(db, MB_GROUPS, MB_DSTATE), cm.reshape(db, MB_GROUPS, MB_DSTATE))

    o_b = pl.pallas_call(
        _ssd_sample_post_kernel,
        in_specs=[full(db, MB_INNER), full(db, MB_INNER), full(db, MB_INNER), full(1, MB_INNER), full(1, MB_INNER)],
        out_specs=full(db, MB_INNER),
        out_shape=jax.ShapeDtypeStruct((db, MB_INNER), F32),
        name="ssd_sample_post",
    )(y_col.reshape(db, MB_INNER), xs, z, jnp.repeat(d_skip, MB_HEADDIM).reshape(1, -1), norm_g.reshape(1, -1))
    return o_b, jnp.swapaxes(conv_new, 0, 1), h_new


def _select_column(x_t, b):
    lane = lax.broadcasted_iota(jnp.int32, x_t.shape, 1)
    return jnp.sum(jnp.where(lane == b, x_t, 0.0), axis=1, keepdims=True)


def _moba_gate_sample_kernel(pt_ref, q_ref, ck_ref, idx_ref, pbuf, sem, qt_ref, km_ref,
                             *, pages_per_block, pages_per_chunk, layer):
    b, c = pl.program_id(0), pl.program_id(1)
    nchunks = pl.num_programs(1)
    t = b * nchunks + c
    total = pl.num_programs(0) * nchunks
    slot = t % 2
    ppb, cpp = pages_per_block, pages_per_chunk
    bpc = cpp // ppb
    nblk = nchunks * bpc

    def copies(bb, cc, sl):
        return [pltpu.make_async_copy(ck_ref.at[layer, pt_ref[bb, cc * cpp + e]], pbuf.at[sl, e], sem.at[sl])
                for e in range(cpp)]

    @pl.when(t == 0)
    def _():
        for cp in copies(0, 0, 0):
            cp.start()
        qt_ref[...] = q_ref[...].T

    @pl.when(t + 1 < total)
    def _():
        for cp in copies((t + 1) // nchunks, (t + 1) % nchunks, 1 - slot):
            cp.start()

    for cp in copies(b, c, slot):
        cp.wait()

    @pl.when(c == 0)
    def _():
        km_ref[...] = jnp.zeros_like(km_ref)

    lane = lax.broadcasted_iota(jnp.int32, km_ref.shape, 1)
    km = km_ref[...]
    for jb in range(bpc):
        pg = pbuf[slot, jb * ppb]
        for e in range(1, ppb):
            pg = pg + pbuf[slot, jb * ppb + e]
        mean = jnp.sum(pg, axis=1, keepdims=True) * (1.0 / MOBA_BLOCK)
        km = jnp.where(lane == c * bpc + jb, mean, km)
    km_ref[...] = km

    @pl.when(c == nchunks - 1)
    def _():
        qcol = _select_column(qt_ref[...], b)
        km = km_ref[...]
        lane1 = lax.broadcasted_iota(jnp.int32, (1, LANES), 1)
        out = jnp.zeros((1, LANES), jnp.int32)
        for h in range(C_HEADS):
            kv = h // C_GROUP
            s = jnp.sum(km[kv * HEAD_DIM:(kv + 1) * HEAD_DIM, :] * qcol[h * HEAD_DIM:(h + 1) * HEAD_DIM, :],
                        axis=0, keepdims=True)
            s = jnp.where(lane1 < nblk, s, -jnp.inf)
            for k in range(MOBA_TOPK):
                m = jnp.max(s, axis=1, keepdims=True)
                idx = jnp.min(jnp.where(s == m, lane1, LANES), axis=1, keepdims=True)
                out = jnp.where(lane1 == h * MOBA_TOPK + k, idx, out)
                s = jnp.where(lane1 == idx, -jnp.inf, s)
        idx_ref[0] = out


def _moba_gate_sample(page_table, q, cache_kt, layer):
    db, n_pages = page_table.shape
    ppb = MOBA_BLOCK // PAGE_SIZE
    cpp = math.gcd(n_pages, 32)
    nblk = n_pages // ppb
    assert cpp % ppb == 0 and MOBA_TOPK <= nblk <= LANES and C_HEADS * MOBA_TOPK <= LANES
    idx = pl.pallas_call(
        functools.partial(_moba_gate_sample_kernel, pages_per_block=ppb, pages_per_chunk=cpp, layer=layer),
        grid_spec=pltpu.PrefetchScalarGridSpec(
            num_scalar_prefetch=1,
            grid=(db, n_pages // cpp),
            in_specs=[pl.BlockSpec((db, C_QW), lambda b, c, pt: (0, 0)), pl.BlockSpec(memory_space=pl.ANY)],
            out_specs=pl.BlockSpec((1, 1, LANES), lambda b, c, pt: (b, 0, 0)),
            scratch_shapes=[pltpu.VMEM((2, cpp, C_KVW, PAGE_SIZE), F32), pltpu.SemaphoreType.DMA((2,)),
                            pltpu.VMEM((C_QW, db), F32), pltpu.VMEM((C_KVW, LANES), F32)]),
        out_shape=jax.ShapeDtypeStruct((db, 1, LANES), jnp.int32),
        compiler_params=_cparams("arbitrary", "arbitrary"),
        name="moba_gate_sample",
    )(page_table, q, cache_kt)
    return idx.reshape(db, LANES)


def _moba_sample_kernel(idx_ref, pt_ref, q_ref, kn_ref, vn_ref, ck_ref, cv_ref, o_ref, kbuf, vbuf, sem,
                        qt_ref, knt_ref, vnt_ref, ot_ref, *, pages_per_block, layer):
    b = pl.program_id(0)
    nb = pl.num_programs(0)
    slot = b % 2
    ppb = pages_per_block
    npg = MOBA_TOPK * ppb

    def copies(bb, sl):
        out = []
        for h in range(C_HEADS):
            rows = pl.ds((h // C_GROUP) * HEAD_DIM, HEAD_DIM)
            for k in range(MOBA_TOPK):
                blk = idx_ref[bb, h * MOBA_TOPK + k]
                for e in range(ppb):
                    page = pt_ref[bb, blk * ppb + e]
                    dst = h * npg + k * ppb + e
                    out.append(pltpu.make_async_copy(ck_ref.at[layer, page, rows, :], kbuf.at[sl, dst], sem.at[sl, 0]))
                    out.append(pltpu.make_async_copy(cv_ref.at[layer, page, rows, :], vbuf.at[sl, dst], sem.at[sl, 1]))
        return out

    @pl.when(b == 0)
    def _():
        for cp in copies(0, 0):
            cp.start()
        qt_ref[...] = q_ref[...].T
        knt_ref[...] = kn_ref[...].T
        vnt_ref[...] = vn_ref[...].T
        ot_ref[...] = jnp.zeros_like(ot_ref)

    @pl.when(b + 1 < nb)
    def _():
        for cp in copies(b + 1, 1 - slot):
            cp.start()

    for cp in copies(b, slot):
        cp.wait()

    qcol = _select_column(qt_ref[...], b) * ATT_SCALE
    kncol = _select_column(knt_ref[...], b)
    vncol = _select_column(vnt_ref[...], b)
    ocols = []
    for h in range(C_HEADS):
        kv = h // C_GROUP
        qh = qcol[h * HEAD_DIM:(h + 1) * HEAD_DIM, :]
        knh = kncol[kv * HEAD_DIM:(kv + 1) * HEAD_DIM, :]
        vnh = vncol[kv * HEAD_DIM:(kv + 1) * HEAD_DIM, :]
        s = [jnp.sum(kbuf[slot, h * npg + pg] * qh, axis=0, keepdims=True) for pg in range(npg)]
        sn = jnp.sum(qh * knh, axis=0, keepdims=True)
        m = sn
        for row in s:
            m = jnp.maximum(m, jnp.max(row, axis=1, keepdims=True))
        pn = jnp.exp(sn - m)
        denom = pn
        acc = None
        for pg, row in enumerate(s):
            p = jnp.exp(row - m)
            denom = denom + jnp.sum(p, axis=1, keepdims=True)
            term = vbuf[slot, h * npg + pg] * p
            acc = term if acc is None else acc + term
        ocols.append((jnp.sum(acc, axis=1, keepdims=True) + pn * vnh) / denom)
    lane = lax.broadcasted_iota(jnp.int32, ot_ref.shape, 1)
    ot_ref[...] = jnp.where(lane == b, jnp.concatenate(ocols, axis=0), ot_ref[...])

    @pl.when(b == nb - 1)
    def _():
        o_ref[...] = ot_ref[...].T


def _moba_sample(idx, page_table, q, kn, vn, cache_kt, cache_vt, layer):
    db = q.shape[0]
    ppb = MOBA_BLOCK // PAGE_SIZE
    nbuf = C_HEADS * MOBA_TOPK * ppb
    full = lambda *shape: pl.BlockSpec(shape, lambda b, *_: (0,) * len(shape))
    return pl.pallas_call(
        functools.partial(_moba_sample_kernel, pages_per_block=ppb, layer=layer),
        grid_spec=pltpu.PrefetchScalarGridSpec(
            num_scalar_prefetch=2,
            grid=(db,),
            in_specs=[full(db, C_QW), full(db, C_KVW), full(db, C_KVW),
                      pl.BlockSpec(memory_space=pl.ANY), pl.BlockSpec(memory_space=pl.ANY)],
            out_specs=full(db, C_QW),
            scratch_shapes=[pltpu.VMEM((2, nbuf, HEAD_DIM, PAGE_SIZE), F32),
                            pltpu.VMEM((2, nbuf, HEAD_DIM, PAGE_SIZE), F32),
                            pltpu.SemaphoreType.DMA((2, 2)),
                            pltpu.VMEM((C_QW, db), F32), pltpu.VMEM((C_KVW, db), F32),
                            pltpu.VMEM((C_KVW, db), F32), pltpu.VMEM((C_QW, db), F32)]),
        out_shape=jax.ShapeDtypeStruct((db, C_QW), F32),
        compiler_params=_cparams("arbitrary"),
        name="moba_sample",
    )(idx, page_table, q, kn, vn, cache_kt, cache_vt)


def _decoder(st, x, pos, p, state):
    is_prompt = state is None
    nseq, seqlen, d = st.nseq, st.seqlen, st.d
    depth = p['w_ada'].shape[0]
    rope_tabs = _rope_tables(pos if is_prompt else jnp.broadcast_to(pos, (st.tokens,)))
    new = {'win_k': [], 'win_v': [], 'conv': [], 'ssm': [], 'k': [], 'v': []}
    for layer in range(depth):
        g_norm = p['norm_g'][layer]
        if layer % 2 == 0:
            i = layer // 2
            splits = ((0, A_QW, True, True), (A_QW, A_KVW, True, True), (A_QW + A_KVW, A_KVW, False, True),
                      (A_QW + 2 * A_KVW, MB_INNER, False, True),
                      (A_QW + 2 * A_KVW + MB_INNER, MB_CONV_DIM, False, True),
                      (A_QW + 2 * A_KVW + MB_INNER + MB_CONV_DIM, LANES, False, True))
            qa, ka, va, z, xbc, dt = _proj(st, layer, x, g_norm[0], p['w_in_a'][i], rope_tabs, splits)
            if is_prompt:
                o_a = _swa_prompt(qa, ka, va, p['sinks'][i], nseq, seqlen)
                wk = ka.reshape(nseq, seqlen, A_KV_HEADS, HEAD_DIM)[:, -WINDOW:]
                wv = va.reshape(nseq, seqlen, A_KV_HEADS, HEAD_DIM)[:, -WINDOW:]
                conv_prev = jnp.zeros((nseq, MB_CONV - 1, MB_CONV_DIM), F32)
                h0 = jnp.zeros((nseq, MB_HEADS, MB_HEADDIM, MB_DSTATE), F32)
                o_b, conv_new, h_new = _ssd_prompt(xbc, z, dt, conv_prev, h0, p['conv_w'][i], p['conv_b'][i],
                                                   p['dt_bias'][i], p['a_log'][i], p['d_skip'][i],
                                                   p['ssm_norm_g'][i], nseq, seqlen)
            else:
                o_a, wk, wv = _swa_sample(qa, ka, va, state['win_k'][i].reshape(nseq, WINDOW, A_KVW),
                                          state['win_v'][i].reshape(nseq, WINDOW, A_KVW), p['sinks'][i])
                wk = wk.reshape(nseq, WINDOW, A_KV_HEADS, HEAD_DIM)
                wv = wv.reshape(nseq, WINDOW, A_KV_HEADS, HEAD_DIM)
                o_b, conv_new, h_new = _ssd_sample(xbc, z, dt, state['conv'][i], state['ssm'][i], p['conv_w'][i],
                                                   p['conv_b'][i], p['dt_bias'][i], p['a_log'][i], p['d_skip'][i],
                                                   p['ssm_norm_g'][i])
            a_list = [o_a, o_b]
            w_list = [p['w_out_a'][i][:A_QW], p['w_out_a'][i][A_QW:]]
            new['win_k'].append(wk)
            new['win_v'].append(wv)
            new['conv'].append(conv_new)
            new['ssm'].append(h_new)
        else:
            j = layer // 2
            cols = ((0, C_QW, True), (C_QW, C_KVW, True), (C_QW + C_KVW, C_KVW, False))
            if is_prompt:
                splits = tuple(c + (e,) for c, e in zip(cols, (True, False, False)))
                extras = (('bf16', 1), ('transposed', 1), ('transposed', 2), ('blockmean', 1))
                qc, k_rows, k_t, v_t, kmean = _proj(st, layer, x, g_norm[0], p['w_in_c'][j], rope_tabs, splits, extras)
                o_c = _moba_prompt(qc, k_rows, v_t, kmean.reshape(nseq, seqlen // MOBA_BLOCK, C_KVW), nseq, seqlen)
                kc, vc = (jnp.transpose(t.reshape(nseq, C_KV_HEADS, HEAD_DIM, seqlen), (0, 3, 1, 2)) for t in (k_t, v_t))
            else:
                splits = tuple(c + (True,) for c in cols)
                qc, kc, vc = _proj(st, layer, x, g_norm[0], p['w_in_c'][j], rope_tabs, splits)
                idx = _moba_gate_sample(state['page_table'], qc, state['cache_k'], j)
                o_c = _moba_sample(idx, state['page_table'], qc, kc, vc, state['cache_k'], state['cache_v'], j)
                kc, vc = (t.reshape(nseq, seqlen, C_KV_HEADS, HEAD_DIM) for t in (kc, vc))
            a_list = [o_c]
            w_list = [p['w_out_c'][j]]
            new['k'].append(kc)
            new['v'].append(vc)
        sorted_moe = st.tokens >= 2 * SORT_TILE
        moe = _moe_sorted if sorted_moe else _moe
        x1, h2, aux = _mix(st, layer, x, a_list, w_list, g_norm[1], p['wr_pad'], p['rb_col'], sorted_moe)
        x = moe(st, layer, h2, aux, p['w_gate'], p['w_up'], p['w_down'], x1, p['final_norm_g'],
                final=(layer == depth - 1))
    return x.reshape(nseq, seqlen, d), {name: jnp.stack(rows) for name, rows in new.items()}


def kernel(x_prompt, x_sample, c_prompt, c_sample, state_win_k, state_win_v, state_conv, state_ssm, cache_k, cache_v, page_table, w_ada, b_ada, norm_g, w_in_a, sinks, conv_w, conv_b, dt_bias, a_log, d_skip, ssm_norm_g, w_out_a, w_in_c, w_out_c, w_router, router_bias, w_gate, w_up, w_down, final_norm_g):
    nb, seqlen, d = x_prompt.shape
    db, dec_seq, _ = x_sample.shape
    assert dec_seq == 1 and seqlen % MOBA_BLOCK == 0 and d % LANES == 0
    n_odd, n_pool, page_size, ckv, hd = cache_k.shape
    assert page_size == PAGE_SIZE and ckv == C_KV_HEADS and hd == HEAD_DIM
    n_pages = page_table.shape[1]

    n_in_a = w_in_a.shape[2]
    pad_a = (-n_in_a) % LANES
    p = {'w_ada': w_ada, 'norm_g': norm_g, 'sinks': sinks, 'conv_w': conv_w, 'conv_b': conv_b, 'dt_bias': dt_bias,
         'a_log': a_log, 'd_skip': d_skip, 'ssm_norm_g': ssm_norm_g, 'final_norm_g': final_norm_g,
         'w_in_a': jnp.pad(w_in_a, ((0, 0), (0, 0), (0, pad_a))).astype(BF16),
         'w_out_a': w_out_a.astype(BF16), 'w_in_c': w_in_c.astype(BF16), 'w_out_c': w_out_c.astype(BF16),
         'w_gate': w_gate.astype(BF16), 'w_up': w_up.astype(BF16), 'w_down': w_down.astype(BF16),
         'wr_pad': jnp.pad(w_router, ((0, 0), (0, LANES - N_EXPERTS))),
         'rb_col': jnp.pad(router_bias, (0, LANES - N_EXPERTS)).reshape(LANES, 1)}

    mod = _ada(jnp.concatenate([c_prompt, c_sample], axis=0), w_ada, b_ada)
    st_p = _Stream(nb, seqlen, d, mod[:, :nb])
    st_s = _Stream(db, 1, d, mod[:, nb:])

    pos_p = jnp.arange(seqlen, dtype=jnp.int32)
    y_prompt, pn = _decoder(st_p, x_prompt.reshape(nb * seqlen, d), pos_p, p, None)

    cache_kt = jnp.transpose(cache_k, (0, 1, 3, 4, 2)).reshape(n_odd, n_pool, C_KVW, PAGE_SIZE)
    cache_vt = jnp.transpose(cache_v, (0, 1, 3, 4, 2)).reshape(n_odd, n_pool, C_KVW, PAGE_SIZE)
    state = {'win_k': state_win_k, 'win_v': state_win_v, 'conv': state_conv, 'ssm': state_ssm,
             'cache_k': cache_kt, 'cache_v': cache_vt, 'page_table': page_table}
    pos_s = n_pages * PAGE_SIZE + jnp.arange(1, dtype=jnp.int32)
    y_sample, sn = _decoder(st_s, x_sample.reshape(db, d), pos_s, p, state)
    return (y_prompt, y_sample, pn['win_k'], pn['win_v'], pn['conv'], pn['ssm'], pn['k'], pn['v'],
            sn['win_k'], sn['win_v'], sn['conv'], sn['ssm'], sn['k'], sn['v'])
```

```python
import functools
import math

import jax
import jax.numpy as jnp
import numpy as np
from jax import lax
from jax.experimental import pallas as pl
from jax.experimental.pallas import tpu as pltpu

F32 = jnp.float32
BF16 = jnp.bfloat16

HEAD_DIM = 64
ROT_DIM = HEAD_DIM // 4
ROPE_THETA = 500000.0
A_HEADS = 8
A_KV_HEADS = 2
A_GROUP = A_HEADS // A_KV_HEADS
WINDOW = 128
MB_HEADDIM = 64
MB_HEADS = 8
MB_INNER = MB_HEADS * MB_HEADDIM
MB_GROUPS = 2
MB_DSTATE = 128
MB_CONV = 4
MB_GN = MB_GROUPS * MB_DSTATE
MB_CONV_DIM = MB_INNER + 2 * MB_GN
SSD_CHUNK = 128
C_HEADS = 16
C_KV_HEADS = 4
C_GROUP = C_HEADS // C_KV_HEADS
MOBA_BLOCK = 256
MOBA_TOPK = 3
MOBA_QTILE = MOBA_BLOCK
PAGE_SIZE = 128
A_QW = A_HEADS * HEAD_DIM
A_KVW = A_KV_HEADS * HEAD_DIM
C_QW = C_HEADS * HEAD_DIM
C_KVW = C_KV_HEADS * HEAD_DIM
N_EXPERTS = 16
N_EXPERT_GROUPS = 4
EXPERTS_PER_GROUP = N_EXPERTS // N_EXPERT_GROUPS
PAIRS_PER_GROUP = EXPERTS_PER_GROUP * (EXPERTS_PER_GROUP - 1) // 2
N_BUCKETS = N_EXPERT_GROUPS * PAIRS_PER_GROUP
BUCKET_ROWS = 32
ROUTE_ROWS = 8
SORT_TILE = 256
RMS_EPS = 1e-6
NEG_INF = -1e30
ATT_SCALE = HEAD_DIM ** -0.5

LANES = 128
VMEM_LIMIT_BYTES = 56 * 1024 * 1024


def _cparams(*sem):
    return pltpu.CompilerParams(dimension_semantics=sem, vmem_limit_bytes=VMEM_LIMIT_BYTES)


def _dot(a, b):
    return jnp.dot(a, b, preferred_element_type=F32)


def _dot_nt(a, b):
    return lax.dot_general(a, b, (((1,), (1,)), ((), ())), preferred_element_type=F32)


def _dot_tn(a, b):
    return lax.dot_general(a, b, (((0,), (0,)), ((), ())), preferred_element_type=F32)


def _split2(x):
    hi = x.astype(BF16)
    lo = (x - hi.astype(F32)).astype(BF16)
    return hi, lo


def _dot_hi(a, b):
    ah, al = _split2(a)
    bh, bl = _split2(b)
    return _dot(ah, bh) + (_dot(al, bh) + _dot(ah, bl))


def _dot_hi_nt(a, b):
    ah, al = _split2(a)
    bh, bl = _split2(b)
    return _dot_nt(ah, bh) + (_dot_nt(al, bh) + _dot_nt(ah, bl))


def _dot_exact_lhs(a_bf16, b):
    b0 = b.astype(BF16)
    r = b - b0.astype(F32)
    b1 = r.astype(BF16)
    b2 = (r - b1.astype(F32)).astype(BF16)
    return _dot(a_bf16, b0) + (_dot(a_bf16, b1) + _dot(a_bf16, b2))


def _sigmoid(x):
    return 1.0 / (1.0 + jnp.exp(-x))


def _silu(x):
    return x * _sigmoid(x)


def _softplus(x):
    return jnp.maximum(x, 0.0) + jnp.log(1.0 + jnp.exp(-jnp.abs(x)))


def _rms(x, g):
    return x * lax.rsqrt(jnp.mean(x * x, axis=-1, keepdims=True) + RMS_EPS) * g


def _rope_tables(pos):
    half = ROT_DIM // 2
    inv_freq = jnp.power(ROPE_THETA, -jnp.arange(half, dtype=F32) / half)
    ang = pos.astype(F32)[:, None] * inv_freq
    cos, sin = jnp.cos(ang), jnp.sin(ang)
    n = pos.shape[0]
    rest = HEAD_DIM - ROT_DIM
    ct = jnp.concatenate([cos, cos, jnp.ones((n, rest), F32)], axis=1)
    sa = jnp.concatenate([jnp.zeros((n, half), F32), sin, jnp.zeros((n, rest), F32)], axis=1)
    sb = jnp.concatenate([-sin, jnp.zeros((n, half), F32), jnp.zeros((n, rest), F32)], axis=1)
    rep = LANES // HEAD_DIM
    return jnp.tile(ct, (1, rep)), jnp.tile(sa, (1, rep)), jnp.tile(sb, (1, rep))


def _rope(x, ct, sa, sb):
    half = ROT_DIM // 2
    return x * ct + pltpu.roll(x, half, 1) * sa + pltpu.roll(x, LANES - half, 1) * sb


def _ada_kernel(c_ref, w_ref, b_ref, o_ref):
    o_ref[...] = _dot_hi(_silu(c_ref[...]), w_ref[...]) + b_ref[...]


def _ada(c_all, w_ada, b_ada):
    depth, d, n6 = w_ada.shape
    nc = c_all.shape[0]
    nk = n6 // d
    return pl.pallas_call(
        _ada_kernel,
        grid=(depth, nk),
        in_specs=[pl.BlockSpec((nc, d), lambda l, k: (0, 0)),
                  pl.BlockSpec((None, d, d), lambda l, k: (l, 0, k)),
                  pl.BlockSpec((None, 1, d), lambda l, k: (l, 0, k))],
        out_specs=pl.BlockSpec((None, nc, d), lambda l, k: (l, 0, k)),
        out_shape=jax.ShapeDtypeStruct((depth, nc, n6), F32),
        compiler_params=_cparams("parallel", "parallel"),
        name="ada",
    )(c_all, w_ada, b_ada.reshape(depth, 1, n6))


class _Stream:
    def __init__(self, nseq, seqlen, d, mod):
        self.nseq, self.seqlen, self.d = nseq, seqlen, d
        self.tokens = nseq * seqlen
        if seqlen == 1:
            self.tm = self.tokens
            self.tiles_per_seq = None
            self.mod = mod
        else:
            self.tm = math.gcd(seqlen, 512)
            self.tiles_per_seq = seqlen // self.tm
            depth = mod.shape[0]
            self.mod = mod.reshape(depth, nseq * 6, 1, d)
        self.ntiles = self.tokens // self.tm

    def mod_arg(self, layer):
        return self.mod[layer]

    def mod_spec(self, k):
        if self.tiles_per_seq is None:
            return pl.BlockSpec((self.tm, self.d), lambda i, *_: (0, k))
        tps = self.tiles_per_seq
        return pl.BlockSpec((None, 1, self.d), lambda i, *_: ((i // tps) * 6 + k, 0, 0))

    def rope_spec(self):
        if self.tiles_per_seq is None:
            return pl.BlockSpec((self.tm, LANES), lambda i, *_: (0, 0))
        tps = self.tiles_per_seq
        return pl.BlockSpec((self.tm, LANES), lambda i, *_: (i % tps, 0))


def _proj_kernel(x_ref, sh_ref, sc_ref, g_ref, w_ref, ct_ref, sa_ref, sb_ref, *out_refs, splits, extras):
    h = _rms(x_ref[...], g_ref[...]) * (1.0 + sc_ref[...]) + sh_ref[...]
    u = _dot(h.astype(BF16), w_ref[...])
    outs = list(out_refs)
    pieces = []
    for start, width, rope, emit in splits:
        if rope:
            ct, sa, sb = ct_ref[...], sa_ref[...], sb_ref[...]
            piece = jnp.concatenate([_rope(u[:, start + c0:start + c0 + LANES], ct, sa, sb)
                                     for c0 in range(0, width, LANES)], axis=-1)
        else:
            piece = u[:, start:start + width]
        pieces.append(piece)
        if emit:
            outs.pop(0)[...] = piece
    for kind, idx in extras:
        o_ref, piece = outs.pop(0), pieces[idx]
        if kind == 'bf16':
            o_ref[...] = piece.astype(BF16)
        elif kind == 'transposed':
            o_ref[...] = piece.T
        else:
            for c in range(o_ref.shape[0]):
                o_ref[c] = jnp.sum(piece[c * MOBA_BLOCK:(c + 1) * MOBA_BLOCK, :], axis=0,
                                   keepdims=True) * (1.0 / MOBA_BLOCK)


def _proj(st, layer, x, norm_g, w_bf16, rope_tabs, splits, extras=()):
    d = st.d
    n = w_bf16.shape[1]
    tm = st.tm
    row = lambda i: (i, 0)
    const = lambda i: (0, 0)
    out_specs = [pl.BlockSpec((tm, w), row) for _, w, _, emit in splits if emit]
    out_shape = [jax.ShapeDtypeStruct((st.tokens, w), F32) for _, w, _, emit in splits if emit]
    for kind, idx in extras:
        w = splits[idx][1]
        if kind == 'bf16':
            out_specs.append(pl.BlockSpec((tm, w), row))
            out_shape.append(jax.ShapeDtypeStruct((st.tokens, w), BF16))
        elif kind == 'transposed':
            tps = st.tiles_per_seq
            out_specs.append(pl.BlockSpec((None, w, tm), lambda i: (i // tps, 0, i % tps)))
            out_shape.append(jax.ShapeDtypeStruct((st.nseq, w, st.seqlen), F32))
        else:
            nb = tm // MOBA_BLOCK
            out_specs.append(pl.BlockSpec((nb, 1, w), lambda i: (i, 0, 0)))
            out_shape.append(jax.ShapeDtypeStruct((st.tokens // MOBA_BLOCK, 1, w), F32))
    return pl.pallas_call(
        functools.partial(_proj_kernel, splits=splits, extras=extras),
        grid=(st.ntiles,),
        in_specs=[pl.BlockSpec((tm, d), row), st.mod_spec(0), st.mod_spec(1),
                  pl.BlockSpec((1, d), const), pl.BlockSpec((d, n), const),
                  st.rope_spec(), st.rope_spec(), st.rope_spec()],
        out_specs=out_specs,
        out_shape=out_shape,
        compiler_params=_cparams("parallel"),
        name="proj",
    )(x, st.mod_arg(layer), st.mod_arg(layer), norm_g.reshape(1, d), w_bf16, *rope_tabs)


def _swa_prompt_kernel(sink_ref, q_ref, kp_ref, kc_ref, vp_ref, vc_ref, o_ref):
    i = pl.program_id(1)
    qb = q_ref.shape[0]
    cols = A_GROUP * qb
    q_t = q_ref[...].T
    kk = jnp.concatenate([kp_ref[...], kc_ref[...]], axis=0).astype(BF16)
    vv_t = jnp.concatenate([vp_ref[...], vc_ref[...]], axis=0).T.astype(BF16)
    key_i = lax.broadcasted_iota(jnp.int32, (2 * qb, cols), 0)
    qry_i = lax.broadcasted_iota(jnp.int32, (2 * qb, cols), 1) % qb
    diff = qry_i + qb - key_i
    ok = (diff >= 0) & (diff <= WINDOW) & ((key_i >= qb) | (i > 0))
    head_of_lane = lax.broadcasted_iota(jnp.int32, (1, cols), 1) // qb
    outs = []
    for kv in range(A_KV_HEADS):
        rows = slice(kv * HEAD_DIM, (kv + 1) * HEAD_DIM)
        qs_t = jnp.concatenate(
            [q_t[(kv * A_GROUP + g) * HEAD_DIM:(kv * A_GROUP + g + 1) * HEAD_DIM, :] for g in range(A_GROUP)], axis=1)
        s = _dot(kk[:, rows], (qs_t * ATT_SCALE).astype(BF16))
        s = jnp.where(ok, s, NEG_INF)
        sink = jnp.zeros((1, cols), F32)
        for g in range(A_GROUP):
            sink = jnp.where(head_of_lane == g, sink_ref[kv * A_GROUP + g], sink)
        m = jnp.maximum(jnp.max(s, axis=0, keepdims=True), sink)
        p = jnp.exp(s - m)
        denom = jnp.sum(p, axis=0, keepdims=True) + jnp.exp(sink - m)
        outs.append(_dot(vv_t[rows, :], p.astype(BF16)) / denom)
    o_t = jnp.concatenate(outs, axis=0).T
    o_ref[...] = jnp.concatenate(
        [o_t[g * qb:(g + 1) * qb, kv * HEAD_DIM:(kv + 1) * HEAD_DIM]
         for kv in range(A_KV_HEADS) for g in range(A_GROUP)], axis=-1)


def _swa_prompt(q, k, v, sinks, nseq, seqlen):
    qb = WINDOW
    nb = seqlen // qb
    cur = lambda b, i: (b * nb + i, 0)
    prev = lambda b, i: (b * nb + jnp.maximum(i - 1, 0), 0)
    return pl.pallas_call(
        _swa_prompt_kernel,
        grid=(nseq, nb),
        in_specs=[pl.BlockSpec(memory_space=pltpu.SMEM),
                  pl.BlockSpec((qb, A_QW), cur),
                  pl.BlockSpec((qb, A_KVW), prev), pl.BlockSpec((qb, A_KVW), cur),
                  pl.BlockSpec((qb, A_KVW), prev), pl.BlockSpec((qb, A_KVW), cur)],
        out_specs=pl.BlockSpec((qb, A_QW), cur),
        out_shape=jax.ShapeDtypeStruct(q.shape, F32),
        compiler_params=_cparams("parallel", "parallel"),
        name="swa_prompt",
    )(sinks, q, k, k, v, v)


def _ssd_prompt_kernel(xbc_ref, z_ref, dt_ref, cprev_ref, h0_ref, cw_ref, cb_ref, dtb_ref, alog_ref, dskip_ref,
                       ng_ref, o_ref, cnew_ref, hlast_ref, xp_ref, h_ref):
    c = pl.program_id(1)
    nc = pl.num_programs(1)
    q = SSD_CHUNK
    halo = MB_CONV - 1
    base = 8 - halo

    @pl.when(c == 0)
    def _():
        xp_ref[base:8, :] = cprev_ref[...]
        h_ref[...] = h0_ref[...]

    xp_ref[8:8 + q, :] = xbc_ref[...]
    acc = cb_ref[...] + xp_ref[base:base + q, :] * cw_ref[0:1, :]
    for tap in range(1, MB_CONV):
        acc = acc + xp_ref[base + tap:base + tap + q, :] * cw_ref[tap:tap + 1, :]
    tail = xp_ref[8 + q - halo:8 + q, :]
    xp_ref[base:8, :] = tail

    @pl.when(c == nc - 1)
    def _():
        cnew_ref[...] = tail

    xbc = _silu(acc)
    xs = xbc[:, :MB_INNER]
    dt = _softplus(dt_ref[...] + dtb_ref[...])
    da = dt * (-jnp.exp(alog_ref[...]))
    r_i = lax.broadcasted_iota(jnp.int32, (q, q), 0)
    c_i = lax.broadcasted_iota(jnp.int32, (q, q), 1)
    causal = r_i >= c_i
    acum = _dot_exact_lhs(causal.astype(BF16), da)
    acum_t = acum.T
    rpg = MB_HEADS // MB_GROUPS
    ys = []
    h_all = [h_ref[h] for h in range(MB_HEADS)]
    h_new = []
    for g in range(MB_GROUPS):
        bq = xbc[:, MB_INNER + g * MB_DSTATE:MB_INNER + (g + 1) * MB_DSTATE].astype(BF16)
        cq = xbc[:, MB_INNER + MB_GN + g * MB_DSTATE:MB_INNER + MB_GN + (g + 1) * MB_DSTATE].astype(BF16)
        cbm = _dot_nt(cq, bq)
        for r in range(rpg):
            h = g * rpg + r
            a_col = acum[:, h:h + 1]
            a_row = acum_t[h:h + 1, :]
            a_last = acum[q - 1:q, h:h + 1]
            decay = jnp.where(causal, jnp.exp(jnp.where(causal, a_col - a_row, 0.0)), 0.0)
            xd = xs[:, h * MB_HEADDIM:(h + 1) * MB_HEADDIM] * dt[:, h:h + 1]
            hprev = h_all[h]
            y = _dot((cbm * decay).astype(BF16), xd.astype(BF16))
            y = y + jnp.exp(a_col) * _dot_nt(cq, hprev.astype(BF16))
            xw = (xd * jnp.exp(a_last - a_col)).astype(BF16)
            h_new.append(hprev * jnp.exp(a_last) + _dot_tn(xw, bq))
            ys.append(y)
    for h in range(MB_HEADS):
        h_ref[h] = h_new[h]
    y = jnp.concatenate(ys, axis=-1) + xs * dskip_ref[...]
    y = y * _silu(z_ref[...])
    gw = MB_INNER // MB_GROUPS
    ng = ng_ref[...]
    o_ref[...] = jnp.concatenate(
        [_rms(y[:, g * gw:(g + 1) * gw], ng[:, g * gw:(g + 1) * gw]) for g in range(MB_GROUPS)], axis=-1)

    @pl.when(c == nc - 1)
    def _():
        hlast_ref[...] = h_ref[...]


def _pad_lanes(v, n=LANES):
    return jnp.pad(v, (0, n - v.shape[0])).reshape(1, n)


def _ssd_prompt(xbc, z, dt, conv_prev, h0, conv_w, conv_b, dt_bias, a_log, d_skip, norm_g, nseq, seqlen):
    q = SSD_CHUNK
    nc = seqlen // q
    row = lambda b, c: (b * nc + c, 0)
    per_seq3 = lambda b, c: (b, 0, 0)
    per_seq4 = lambda b, c: (b, 0, 0, 0)
    const = lambda b, c: (0, 0)
    halo = MB_CONV - 1
    return pl.pallas_call(
        _ssd_prompt_kernel,
        grid=(nseq, nc),
        in_specs=[pl.BlockSpec((q, MB_CONV_DIM), row), pl.BlockSpec((q, MB_INNER), row),
                  pl.BlockSpec((q, LANES), row),
                  pl.BlockSpec((None, halo, MB_CONV_DIM), per_seq3),
                  pl.BlockSpec((None, MB_HEADS, MB_HEADDIM, MB_DSTATE), per_seq4),
                  pl.BlockSpec((MB_CONV, MB_CONV_DIM), const), pl.BlockSpec((1, MB_CONV_DIM), const),
                  pl.BlockSpec((1, LANES), const), pl.BlockSpec((1, LANES), const),
                  pl.BlockSpec((1, MB_INNER), const), pl.BlockSpec((1, MB_INNER), const)],
        out_specs=[pl.BlockSpec((q, MB_INNER), row),
                   pl.BlockSpec((None, halo, MB_CONV_DIM), per_seq3),
                   pl.BlockSpec((None, MB_HEADS, MB_HEADDIM, MB_DSTATE), per_seq4)],
        out_shape=[jax.ShapeDtypeStruct((nseq * seqlen, MB_INNER), F32),
                   jax.ShapeDtypeStruct((nseq, halo, MB_CONV_DIM), F32),
                   jax.ShapeDtypeStruct((nseq, MB_HEADS, MB_HEADDIM, MB_DSTATE), F32)],
        scratch_shapes=[pltpu.VMEM((8 + q, MB_CONV_DIM), F32),
                        pltpu.VMEM((MB_HEADS, MB_HEADDIM, MB_DSTATE), F32)],
        compiler_params=_cparams("parallel", "arbitrary"),
        name="ssd_prompt",
    )(xbc, z, dt, conv_prev, h0, conv_w, conv_b.reshape(1, -1), _pad_lanes(dt_bias), _pad_lanes(a_log),
      jnp.repeat(d_skip, MB_HEADDIM).reshape(1, -1), norm_g.reshape(1, -1))


def _route(logits_t, rbias_col):
    tm = logits_t.shape[1]
    scores = _sigmoid(logits_t[0:N_EXPERTS, :])
    biased = scores + rbias_col[0:N_EXPERTS, :]
    s = [scores[e:e + 1, :] for e in range(N_EXPERTS)]
    b = [biased[e:e + 1, :] for e in range(N_EXPERTS)]
    epg = EXPERTS_PER_GROUP
    gscore = []
    for g in range(N_EXPERT_GROUPS):
        v = b[g * epg:(g + 1) * epg]
        best = None
        for i in range(epg):
            for j in range(i + 1, epg):
                pair = v[i] + v[j]
                best = pair if best is None else jnp.maximum(best, pair)
        gscore.append(best)
    gsel = jnp.zeros((1, tm), jnp.int32)
    gbest = gscore[0]
    for g in range(1, N_EXPERT_GROUPS):
        better = gscore[g] > gbest
        gsel = jnp.where(better, g, gsel)
        gbest = jnp.where(better, gscore[g], gbest)
    bs, ss = [], []
    for k in range(epg):
        bk, sk = b[k], s[k]
        for g in range(1, N_EXPERT_GROUPS):
            bk = jnp.where(gsel == g, b[g * epg + k], bk)
            sk = jnp.where(gsel == g, s[g * epg + k], sk)
        bs.append(bk)
        ss.append(sk)
    i1 = jnp.zeros((1, tm), jnp.int32)
    m1 = bs[0]
    for k in range(1, epg):
        better = bs[k] > m1
        i1 = jnp.where(better, k, i1)
        m1 = jnp.where(better, bs[k], m1)
    i2 = jnp.full((1, tm), -1, jnp.int32)
    m2 = jnp.full((1, tm), -jnp.inf, F32)
    for k in range(epg):
        better = (i1 != k) & ((bs[k] > m2) | (i2 < 0))
        i2 = jnp.where(better, k, i2)
        m2 = jnp.where(better, bs[k], m2)
    s1 = jnp.zeros((1, tm), F32)
    s2 = jnp.zeros((1, tm), F32)
    for k in range(epg):
        s1 = jnp.where(i1 == k, ss[k], s1)
        s2 = jnp.where(i2 == k, ss[k], s2)
    denom = s1 + s2
    rows = lax.broadcasted_iota(jnp.int32, (N_EXPERTS, tm), 0)
    comb = jnp.zeros((N_EXPERTS, tm), F32)
    for e in range(N_EXPERTS):
        g, k = divmod(e, epg)
        chosen = (gsel == g) & ((i1 == k) | (i2 == k))
        comb = jnp.where(rows == e, jnp.where(chosen, s[e] / denom, 0.0), comb)
    lo = jnp.minimum(i1, i2)
    hi = jnp.maximum(i1, i2)
    pair = jnp.where(lo == 0, 0, jnp.where(lo == 1, epg - 1, 2 * epg - 3)) + (hi - lo - 1)
    bucket = (gsel * PAIRS_PER_GROUP + pair).astype(F32)
    first_is_lo = i1 < i2
    w_lo = jnp.where(first_is_lo, s1, s2) / denom
    w_hi = jnp.where(first_is_lo, s2, s1) / denom
    return comb, bucket, w_lo, w_hi


def _mix_kernel(*refs, n_in, sorted_moe):
    x_ref, g1_ref = refs[0], refs[1]
    a_refs = refs[2:2 + n_in]
    w_refs = refs[2 + n_in:2 + 2 * n_in]
    ng_ref, sc_ref, sh_ref, wr_ref, rb_ref, x1_ref, h2_ref, aux_ref = refs[2 + 2 * n_in:]
    mix = _dot(a_refs[0][...].astype(BF16), w_refs[0][...])
    for a_ref, w_ref in zip(a_refs[1:], w_refs[1:]):
        mix = mix + _dot(a_ref[...].astype(BF16), w_ref[...])
    x1 = x_ref[...] + g1_ref[...] * mix
    x1_ref[...] = x1
    h2 = _rms(x1, ng_ref[...]) * (1.0 + sc_ref[...]) + sh_ref[...]
    logits = _dot_hi(h2, wr_ref[...])
    comb_t, bucket, w_lo, w_hi = _route(logits.T, rb_ref[...])
    tm, d = x1.shape
    if sorted_moe:
        r = lax.broadcasted_iota(jnp.int32, (LANES, tm), 0)
        rt = jnp.where(r == 0, bucket, jnp.where(r == 1, w_lo, jnp.where(r == 2, w_hi, 0.0)))
        h2_ref[:, :d] = h2
        h2_ref[:, d:] = rt.T
        aux_ref[...] = rt[0:aux_ref.shape[0], :]
    else:
        h2_ref[...] = h2.astype(BF16)
        aux_ref[...] = jnp.concatenate([comb_t, jnp.zeros((LANES - N_EXPERTS, tm), F32)], axis=0).T


def _mix(st, layer, x, a_list, w_list, norm_g, wr_pad, rb_col, sorted_moe):
    d, tm = st.d, st.tm
    row = lambda i: (i, 0)
    const = lambda i: (0, 0)
    n_in = len(a_list)
    in_specs = [pl.BlockSpec((tm, d), row), st.mod_spec(2)]
    in_specs += [pl.BlockSpec((tm, a.shape[1]), row) for a in a_list]
    in_specs += [pl.BlockSpec(w.shape, const) for w in w_list]
    in_specs += [pl.BlockSpec((1, d), const), st.mod_spec(4), st.mod_spec(3),
                 pl.BlockSpec((d, LANES), const), pl.BlockSpec((LANES, 1), const)]
    m = st.mod_arg(layer)
    if sorted_moe:
        out_specs = [pl.BlockSpec((tm, d), row), pl.BlockSpec((tm, d + LANES), row),
                     pl.BlockSpec((ROUTE_ROWS, tm), lambda i: (0, i))]
        out_shape = [jax.ShapeDtypeStruct((st.tokens, d), F32), jax.ShapeDtypeStruct((st.tokens, d + LANES), F32),
                     jax.ShapeDtypeStruct((ROUTE_ROWS, st.tokens), F32)]
    else:
        out_specs = [pl.BlockSpec((tm, d), row), pl.BlockSpec((tm, d), row), pl.BlockSpec((tm, LANES), row)]
        out_shape = [jax.ShapeDtypeStruct((st.tokens, d), F32), jax.ShapeDtypeStruct((st.tokens, d), BF16),
                     jax.ShapeDtypeStruct((st.tokens, LANES), F32)]
    return pl.pallas_call(
        functools.partial(_mix_kernel, n_in=n_in, sorted_moe=sorted_moe),
        grid=(st.ntiles,),
        in_specs=in_specs,
        out_specs=out_specs,
        out_shape=out_shape,
        compiler_params=_cparams("parallel"),
        name="mix",
    )(x, m, *a_list, *w_list, norm_g.reshape(1, d), m, m, wr_pad, rb_col)


def _moe_kernel(h_ref, comb_ref, wg_ref, wu_ref, wd_ref, x1_ref, g2_ref, fg_ref, o_ref, acc_ref, *, final):
    e = pl.program_id(1)

    @pl.when(e == 0)
    def _():
        acc_ref[...] = jnp.zeros_like(acc_ref)

    h = h_ref[...]
    he = _silu(_dot(h, wg_ref[...])) * _dot(h, wu_ref[...])
    comb = comb_ref[...]
    lane = lax.broadcasted_iota(jnp.int32, comb.shape, 1)
    ce = jnp.sum(jnp.where(lane == e, comb, 0.0), axis=-1, keepdims=True)
    acc_ref[...] += ce * _dot(he.astype(BF16), wd_ref[...])

    @pl.when(e == pl.num_programs(1) - 1)
    def _():
        x2 = x1_ref[...] + g2_ref[...] * acc_ref[...]
        o_ref[...] = _rms(x2, fg_ref[...]) if final else x2


def _moe(st, layer, h2, comb, wg, wu, wd, x1, final_g, final):
    d, tm = st.d, st.tm
    ne, _, dff = wg.shape[1:]
    row = lambda i, e: (i, 0)
    return pl.pallas_call(
        functools.partial(_moe_kernel, final=final),
        grid=(st.ntiles, ne),
        in_specs=[pl.BlockSpec((tm, d), row), pl.BlockSpec((tm, LANES), row),
                  pl.BlockSpec((None, None, d, dff), lambda i, e: (layer, e, 0, 0)),
                  pl.BlockSpec((None, None, d, dff), lambda i, e: (layer, e, 0, 0)),
                  pl.BlockSpec((None, None, dff, d), lambda i, e: (layer, e, 0, 0)),
                  pl.BlockSpec((tm, d), row), st.mod_spec(5),
                  pl.BlockSpec((1, d), lambda i, e: (0, 0))],
        out_specs=pl.BlockSpec((tm, d), row),
        out_shape=jax.ShapeDtypeStruct((st.tokens, d), F32),
        scratch_shapes=[pltpu.VMEM((tm, d), F32)],
        compiler_params=_cparams("parallel", "arbitrary"),
        name="moe",
    )(h2, comb, wg, wu, wd, x1, st.mod_arg(layer), final_g.reshape(1, d))


def _bucket_rank_kernel(rt_ref, rank_ref, cnt_ref, carry_ref):
    i = pl.program_id(0)
    tm = rt_ref.shape[1]

    @pl.when(i == 0)
    def _():
        carry_ref[...] = jnp.zeros_like(carry_ref)

    bucket = rt_ref[0:1, :]
    rows = lax.broadcasted_iota(jnp.int32, (BUCKET_ROWS, tm), 0).astype(F32)
    onehot = rows == bucket
    s_i = lax.broadcasted_iota(jnp.int32, (tm, tm), 0)
    t_i = lax.broadcasted_iota(jnp.int32, (tm, tm), 1)
    incl = jnp.where(s_i <= t_i, 1.0, 0.0).astype(BF16)
    cum = _dot(jnp.where(onehot, 1.0, 0.0).astype(BF16), incl)
    carry = carry_ref[...]
    rank = jnp.sum(jnp.where(onehot, cum - 1.0 + carry, 0.0), axis=0, keepdims=True)
    rank_ref[...] = rank.astype(jnp.int32)
    carry = carry + cum[:, tm - 1:tm]
    carry_ref[...] = carry
    cnt_ref[...] = jnp.broadcast_to(carry, cnt_ref.shape)


def _bucket_rank(st, rt):
    tm = st.tm
    return pl.pallas_call(
        _bucket_rank_kernel,
        grid=(st.ntiles,),
        in_specs=[pl.BlockSpec((ROUTE_ROWS, tm), lambda i: (0, i))],
        out_specs=[pl.BlockSpec((1, tm), lambda i: (0, i)), pl.BlockSpec((BUCKET_ROWS, LANES), lambda i: (0, 0))],
        out_shape=[jax.ShapeDtypeStruct((1, st.tokens), jnp.int32),
                   jax.ShapeDtypeStruct((BUCKET_ROWS, LANES), F32)],
        scratch_shapes=[pltpu.VMEM((BUCKET_ROWS, 1), F32)],
        compiler_params=_cparams("arbitrary"),
        name="bucket_rank",
    )(rt)


def _sort_plan(rt, rank, cnt, tokens):
    ts = SORT_TILE
    bucket = rt[0].astype(jnp.int32)
    counts = cnt[:N_BUCKETS, 0].astype(jnp.int32)
    padded = (counts + ts - 1) // ts * ts
    ends = jnp.cumsum(padded)
    dest = (ends - padded)[bucket] + rank[0]
    n_tiles = -(-tokens // ts) + N_BUCKETS
    n_used = ends[-1] // ts
    tile = jnp.arange(n_tiles, dtype=jnp.int32)
    tile_row = jnp.minimum(tile, n_used - 1) * ts
    tile_bucket = jnp.sum((ends[None, :] <= tile_row[:, None]).astype(jnp.int32), axis=1)
    tile_bucket = jnp.minimum(tile_bucket, N_BUCKETS - 1)
    pairs = [(a, b) for a in range(EXPERTS_PER_GROUP) for b in range(a + 1, EXPERTS_PER_GROUP)]
    lo_tab = jnp.array([a for a, _ in pairs], jnp.int32)
    hi_tab = jnp.array([b for _, b in pairs], jnp.int32)
    base = tile_bucket // PAIRS_PER_GROUP * EXPERTS_PER_GROUP
    pair = tile_bucket % PAIRS_PER_GROUP
    return dest, base + lo_tab[pair], base + hi_tab[pair], n_used.reshape(1).astype(jnp.int32), n_tiles


def _dispatch_kernel(dest_ref, h_ref, init_ref, out_ref, sem):
    del init_ref
    i = pl.program_id(0)
    tm = h_ref.shape[0]

    def body(r, carry):
        pltpu.make_async_copy(h_ref.at[pl.ds(r, 1)], out_ref.at[pl.ds(dest_ref[i * tm + r], 1)], sem).start()
        return carry

    lax.fori_loop(0, tm, body, 0, unroll=8)
    pltpu.make_async_copy(h_ref, out_ref.at[pl.ds(0, tm)], sem).wait()


def _dispatch(st, dest, h2w, n_rows):
    tm = st.tm
    w = h2w.shape[1]
    return pl.pallas_call(
        _dispatch_kernel,
        grid_spec=pltpu.PrefetchScalarGridSpec(
            num_scalar_prefetch=1,
            grid=(st.ntiles,),
            in_specs=[pl.BlockSpec((tm, w), lambda i, d: (i, 0)), pl.BlockSpec(memory_space=pl.ANY)],
            out_specs=pl.BlockSpec(memory_space=pl.ANY),
            scratch_shapes=[pltpu.SemaphoreType.DMA(())]),
        out_shape=jax.ShapeDtypeStruct((n_rows, w), F32),
        input_output_aliases={2: 0},
        compiler_params=_cparams("arbitrary"),
        name="moe_dispatch",
    )(dest, h2w, jnp.zeros((n_rows, w), F32))


def _experts_kernel(elo_ref, ehi_ref, nused_ref, hs_ref, wgl_ref, wul_ref, wdl_ref, wgh_ref, wuh_ref, wdh_ref, y_ref):
    del elo_ref, ehi_ref
    i = pl.program_id(0)
    d = y_ref.shape[1]

    @pl.when(i < nused_ref[0])
    def _():
        h = hs_ref[:, :d].astype(BF16)

        def expert(wg_ref, wu_ref, wd_ref):
            he = _silu(_dot(h, wg_ref[...])) * _dot(h, wu_ref[...])
            return _dot(he.astype(BF16), wd_ref[...])

        y_ref[...] = (hs_ref[:, d + 1:d + 2] * expert(wgl_ref, wul_ref, wdl_ref)
                      + hs_ref[:, d + 2:d + 3] * expert(wgh_ref, wuh_ref, wdh_ref))

    @pl.when(i >= nused_ref[0])
    def _():
        y_ref[...] = jnp.zeros_like(y_ref)


def _experts(layer, hs, e_lo, e_hi, n_used, n_tiles, wg, wu, wd):
    ts = SORT_TILE
    _, _, d, dff = wg.shape
    lo = lambda i, elo, ehi, nu: (layer, elo[i], 0, 0)
    hi = lambda i, elo, ehi, nu: (layer, ehi[i], 0, 0)
    up = lambda idx: pl.BlockSpec((None, None, d, dff), idx)
    down = lambda idx: pl.BlockSpec((None, None, dff, d), idx)
    return pl.pallas_call(
        _experts_kernel,
        grid_spec=pltpu.PrefetchScalarGridSpec(
            num_scalar_prefetch=3,
            grid=(n_tiles,),
            in_specs=[pl.BlockSpec((ts, hs.shape[1]), lambda i, *_: (i, 0)),
                      up(lo), up(lo), down(lo), up(hi), up(hi), down(hi)],
            out_specs=pl.BlockSpec((ts, d), lambda i, *_: (i, 0))),
        out_shape=jax.ShapeDtypeStruct((n_tiles * ts, d), F32),
        compiler_params=_cparams("arbitrary"),
        name="moe_experts",
    )(e_lo, e_hi, n_used, hs, wg, wu, wd, wg, wu, wd)


def _combine_kernel(dest_ref, x1_ref, g2_ref, fg_ref, ys_ref, o_ref, buf, sem, *, final):
    i = pl.program_id(0)
    n = pl.num_programs(0)
    tm = x1_ref.shape[0]
    slot = i % 2

    def start(tile, sl):
        def body(r, carry):
            pltpu.make_async_copy(ys_ref.at[pl.ds(dest_ref[tile * tm + r], 1)], buf.at[sl, pl.ds(r, 1)],
                                  sem.at[sl]).start()
            return carry

        lax.fori_loop(0, tm, body, 0, unroll=8)

    @pl.when(i == 0)
    def _():
        start(0, 0)

    @pl.when(i + 1 < n)
    def _():
        start(i + 1, 1 - slot)

    pltpu.make_async_copy(ys_ref.at[pl.ds(0, tm)], buf.at[slot], sem.at[slot]).wait()
    x2 = x1_ref[...] + g2_ref[...] * buf[slot]
    o_ref[...] = _rms(x2, fg_ref[...]) if final else x2


def _combine(st, layer, dest, ys, x1, final_g, final):
    d, tm = st.d, st.tm
    row = lambda i, dst: (i, 0)
    return pl.pallas_call(
        functools.partial(_combine_kernel, final=final),
        grid_spec=pltpu.PrefetchScalarGridSpec(
            num_scalar_prefetch=1,
            grid=(st.ntiles,),
            in_specs=[pl.BlockSpec((tm, d), row), st.mod_spec(5), pl.BlockSpec((1, d), lambda i, dst: (0, 0)),
                      pl.BlockSpec(memory_space=pl.ANY)],
            out_specs=pl.BlockSpec((tm, d), row),
            scratch_shapes=[pltpu.VMEM((2, tm, d), F32), pltpu.SemaphoreType.DMA((2,))]),
        out_shape=jax.ShapeDtypeStruct((st.tokens, d), F32),
        compiler_params=_cparams("arbitrary"),
        name="moe_combine",
    )(dest, x1, st.mod_arg(layer), final_g.reshape(1, d), ys)


def _moe_sorted(st, layer, h2w, rt, wg, wu, wd, x1, final_g, final):
    rank, cnt = _bucket_rank(st, rt)
    dest, e_lo, e_hi, n_used, n_tiles = _sort_plan(rt, rank, cnt, st.tokens)
    hs = _dispatch(st, dest, h2w, n_tiles * SORT_TILE)
    ys = _experts(layer, hs, e_lo, e_hi, n_used, n_tiles, wg, wu, wd)
    return _combine(st, layer, dest, ys, x1, final_g, final)


def _moba_prompt_kernel(q_ref, k_ref, vt_ref, km_ref, o_ref, sel_ref, qs_ref, m_ref, l_ref, acc_ref):
    i = pl.program_id(1)
    qb = MOBA_QTILE
    blk = MOBA_BLOCK
    nblk = km_ref.shape[0]
    own = (i * qb) // blk
    q_off = i * qb - own * blk
    q_t = q_ref[...].T
    km = km_ref[...]
    cols = C_GROUP * qb
    nb_pad = sel_ref.shape[1]
    blk_row = lax.broadcasted_iota(jnp.int32, (nb_pad, cols), 0)
    past = blk_row < own
    key_i = lax.broadcasted_iota(jnp.int32, (blk, cols), 0)
    qry_i = lax.broadcasted_iota(jnp.int32, (blk, cols), 1)
    own_ok = key_i <= (qry_i % qb) + q_off
    own_start = pl.multiple_of(own * blk, blk)
    kv_rows = [slice(kv * HEAD_DIM, (kv + 1) * HEAD_DIM) for kv in range(C_KV_HEADS)]

    state = []
    for kv, rows in enumerate(kv_rows):
        qs_t = jnp.concatenate(
            [q_t[(kv * C_GROUP + g) * HEAD_DIM:(kv * C_GROUP + g + 1) * HEAD_DIM, :] for g in range(C_GROUP)], axis=1)
        kmp = jnp.concatenate([km[:, rows], jnp.zeros((nb_pad - nblk, HEAD_DIM), F32)], axis=0)
        sg = jnp.where(past, _dot_hi(kmp, qs_t), NEG_INF)
        rank = jnp.zeros((nb_pad, cols), jnp.int32)
        for j in range(nblk):
            rj = sg[j:j + 1, :]
            rank = rank + ((rj > sg) | ((rj == sg) & (blk_row > j))).astype(jnp.int32)
        sel = (past & (rank < MOBA_TOPK)).astype(F32)
        qsb = (qs_t * ATT_SCALE).astype(BF16)
        s = _dot(k_ref[pl.ds(own_start, blk), rows], qsb)
        s = jnp.where(own_ok, s, NEG_INF)
        m0 = jnp.max(s, axis=0, keepdims=True)
        p = jnp.exp(s - m0)
        l0 = jnp.sum(p, axis=0, keepdims=True)
        acc0 = _dot(vt_ref[rows, pl.ds(own_start, blk)].astype(BF16), p.astype(BF16))
        state.append((sel, qsb, m0, l0, acc0))
    for kv, (sel, qsb, m0, l0, acc0) in enumerate(state):
        sel_ref[kv], qs_ref[kv], m_ref[kv], l_ref[kv], acc_ref[kv] = sel, qsb, m0, l0, acc0

    def body(j, carry):
        start = pl.multiple_of(j * blk, blk)
        kvs = range(C_KV_HEADS)
        chosen = [sel_ref[kv, pl.ds(j, 1), :] > 0.0 for kv in kvs]
        m_old = [m_ref[kv] for kv in kvs]
        l_old = [l_ref[kv] for kv in kvs]
        acc_old = [acc_ref[kv] for kv in kvs]
        s = [_dot(k_ref[pl.ds(start, blk), kv_rows[kv]], qs_ref[kv]) for kv in kvs]
        s = [jnp.where(chosen[kv], s[kv], NEG_INF) for kv in kvs]
        m_new = [jnp.maximum(m_old[kv], jnp.max(s[kv], axis=0, keepdims=True)) for kv in kvs]
        alpha = [jnp.exp(m_old[kv] - m_new[kv]) for kv in kvs]
        p = [jnp.exp(s[kv] - m_new[kv]) for kv in kvs]
        l_new = [alpha[kv] * l_old[kv] + jnp.sum(p[kv], axis=0, keepdims=True) for kv in kvs]
        pv = [_dot(vt_ref[kv_rows[kv], pl.ds(start, blk)].astype(BF16), p[kv].astype(BF16)) for kv in kvs]
        for kv in kvs:
            m_ref[kv] = m_new[kv]
            l_ref[kv] = l_new[kv]
            acc_ref[kv] = alpha[kv] * acc_old[kv] + pv[kv]
        return carry

    lax.fori_loop(0, own, body, 0)
    o_t = jnp.concatenate([acc_ref[kv] / l_ref[kv] for kv in range(C_KV_HEADS)], axis=0).T
    o_ref[...] = jnp.concatenate(
        [o_t[g * qb:(g + 1) * qb, kv * HEAD_DIM:(kv + 1) * HEAD_DIM]
         for kv in range(C_KV_HEADS) for g in range(C_GROUP)], axis=-1)


def _moba_prompt(q, k_rows, v_t, kmean, nseq, seqlen):
    qb = MOBA_QTILE
    nq = seqlen // qb
    nblk = kmean.shape[1]
    cols = C_GROUP * qb
    per_seq = lambda b, i: (b, 0, 0)
    return pl.pallas_call(
        _moba_prompt_kernel,
        grid=(nseq, nq),
        in_specs=[pl.BlockSpec((qb, C_QW), lambda b, i: (b * nq + i, 0)),
                  pl.BlockSpec((seqlen, C_KVW), lambda b, i: (b, 0)),
                  pl.BlockSpec((None, C_KVW, seqlen), per_seq),
                  pl.BlockSpec((None, nblk, C_KVW), per_seq)],
        out_specs=pl.BlockSpec((qb, C_QW), lambda b, i: (b * nq + i, 0)),
        out_shape=jax.ShapeDtypeStruct(q.shape, F32),
        scratch_shapes=[pltpu.VMEM((C_KV_HEADS, -(-nblk // 16) * 16, cols), F32),
                        pltpu.VMEM((C_KV_HEADS, HEAD_DIM, cols), BF16),
                        pltpu.VMEM((C_KV_HEADS, 1, cols), F32), pltpu.VMEM((C_KV_HEADS, 1, cols), F32),
                        pltpu.VMEM((C_KV_HEADS, HEAD_DIM, cols), F32)],
        compiler_params=_cparams("parallel", "parallel"),
        name="moba_prompt",
    )(q, k_rows, v_t, kmean)


def _swa_sample_kernel(sink_ref, q_ref, kn_ref, vn_ref, wk_ref, wv_ref, o_ref, nk_ref, nv_ref):
    q = q_ref[...]
    kn, vn = kn_ref[...], vn_ref[...]
    wk, wv = wk_ref[...], wv_ref[...]
    tb, w = wk.shape[0], wk.shape[1]
    nk_ref[:, 0:w - 1, :] = wk[:, 1:w, :]
    nk_ref[:, w - 1:w, :] = kn[:, None, :]
    nv_ref[:, 0:w - 1, :] = wv[:, 1:w, :]
    nv_ref[:, w - 1:w, :] = vn[:, None, :]
    lane = lax.broadcasted_iota(jnp.int32, (tb, LANES), 1)
    lane3 = lax.broadcasted_iota(jnp.int32, (tb, w, LANES), 2)
    heads = [None] * A_HEADS
    for g in range(A_GROUP):
        qg = jnp.concatenate(
            [q[:, (kv * A_GROUP + g) * HEAD_DIM:(kv * A_GROUP + g + 1) * HEAD_DIM] for kv in range(A_KV_HEADS)], axis=-1)
        prod = wk * qg[:, None, :]
        prod_n = kn * qg
        p_full = None
        pn_full = None
        for kv in range(A_KV_HEADS):
            lo, hi = kv * HEAD_DIM, (kv + 1) * HEAD_DIM
            sink = sink_ref[kv * A_GROUP + g]
            s = jnp.sum(prod[:, :, lo:hi], axis=-1, keepdims=True) * ATT_SCALE
            sn = jnp.sum(prod_n[:, lo:hi], axis=-1, keepdims=True) * ATT_SCALE
            m = jnp.maximum(jnp.maximum(jnp.max(s, axis=1), sn), sink)
            p = jnp.exp(s - m[:, None, :])
            pn = jnp.exp(sn - m)
            inv = 1.0 / (jnp.sum(p, axis=1) + pn + jnp.exp(sink - m))
            p = p * inv[:, None, :]
            pn = pn * inv
            pb = jnp.broadcast_to(p, (tb, w, LANES))
            pnb = jnp.broadcast_to(pn, (tb, LANES))
            if kv == 0:
                p_full, pn_full = pb, pnb
            else:
                p_full = jnp.where(lane3 >= lo, pb, p_full)
                pn_full = jnp.where(lane >= lo, pnb, pn_full)
        og = jnp.sum(p_full * wv, axis=1) + pn_full * vn
        for kv in range(A_KV_HEADS):
            heads[kv * A_GROUP + g] = og[:, kv * HEAD_DIM:(kv + 1) * HEAD_DIM]
    o_ref[...] = jnp.concatenate(heads, axis=-1)


def _swa_sample(q, kn, vn, win_k, win_v, sinks):
    db, w = win_k.shape[0], win_k.shape[1]
    tb = math.gcd(db, 8)
    row = lambda i: (i, 0)
    row3 = lambda i: (i, 0, 0)
    return pl.pallas_call(
        _swa_sample_kernel,
        grid=(db // tb,),
        in_specs=[pl.BlockSpec(memory_space=pltpu.SMEM),
                  pl.BlockSpec((tb, A_QW), row), pl.BlockSpec((tb, A_KVW), row), pl.BlockSpec((tb, A_KVW), row),
                  pl.BlockSpec((tb, w, A_KVW), row3), pl.BlockSpec((tb, w, A_KVW), row3)],
        out_specs=[pl.BlockSpec((tb, A_QW), row), pl.BlockSpec((tb, w, A_KVW), row3),
                   pl.BlockSpec((tb, w, A_KVW), row3)],
        out_shape=[jax.ShapeDtypeStruct((db, A_QW), F32), jax.ShapeDtypeStruct(win_k.shape, F32),
                   jax.ShapeDtypeStruct(win_v.shape, F32)],
        compiler_params=_cparams("parallel"),
        name="swa_sample",
    )(sinks, q, kn, vn, win_k, win_v)


def _ssd_sample_pre_kernel(xbc_ref, cst_ref, dt_ref, cw_ref, cb_ref, dtb_ref, alog_ref, exp_ref,
                           xs_ref, bm_ref, cm_ref, xd_ref, dec_ref, cnew_ref):
    xn = xbc_ref[...]
    halo = MB_CONV - 1
    acc = cb_ref[...] + xn * cw_ref[halo:halo + 1, :]
    for tap in range(halo):
        acc = acc + cst_ref[tap] * cw_ref[tap:tap + 1, :]
    for tap in range(1, halo):
        cnew_ref[tap - 1] = cst_ref[tap]
    cnew_ref[halo - 1] = xn
    xbc = _silu(acc)
    xs = xbc[:, :MB_INNER]
    xs_ref[...] = xs
    bm_ref[...] = xbc[:, MB_INNER:MB_INNER + MB_GN]
    cm_ref[...] = xbc[:, MB_INNER + MB_GN:]
    dt = _softplus(dt_ref[...] + dtb_ref[...])
    da = dt * (-jnp.exp(alog_ref[...]))
    expand = exp_ref[...]
    xd_ref[...] = xs * _dot_exact_lhs_rhs(dt, expand)
    dec_ref[...] = jnp.exp(_dot_exact_lhs_rhs(da, expand))


def _dot_exact_lhs_rhs(a, b_bf16):
    a0 = a.astype(BF16)
    r = a - a0.astype(F32)
    a1 = r.astype(BF16)
    a2 = (r - a1.astype(F32)).astype(BF16)
    return _dot(a0, b_bf16) + (_dot(a1, b_bf16) + _dot(a2, b_bf16))


def _ssd_sample_state_kernel(h_ref, xd_ref, dec_ref, bm_ref, cm_ref, hn_ref, y_ref):
    tb = h_ref.shape[0]
    rpg = MB_HEADS // MB_GROUPS
    for t in range(tb):
        for h in range(MB_HEADS):
            g = h // rpg
            rows = slice(h * MB_HEADDIM, (h + 1) * MB_HEADDIM)
            hn = h_ref[t, h] * dec_ref[t, rows, :] + xd_ref[t, rows, :] * bm_ref[t, g:g + 1, :]
            hn_ref[t, h] = hn
            y_ref[t, rows, :] = jnp.sum(hn * cm_ref[t, g:g + 1, :], axis=-1, keepdims=True)


def _ssd_sample_post_kernel(y_ref, xs_ref, z_ref, dskip_ref, ng_ref, o_ref):
    y = (y_ref[...] + xs_ref[...] * dskip_ref[...]) * _silu(z_ref[...])
    gw = MB_INNER // MB_GROUPS
    ng = ng_ref[...]
    o_ref[...] = jnp.concatenate(
        [_rms(y[:, g * gw:(g + 1) * gw], ng[:, g * gw:(g + 1) * gw]) for g in range(MB_GROUPS)], axis=-1)


def _ssd_sample(xbc, z, dt, conv_state, h0, conv_w, conv_b, dt_bias, a_log, d_skip, norm_g):
    db = xbc.shape[0]
    halo = MB_CONV - 1
    expand = (jnp.arange(LANES)[:, None] == (jnp.arange(MB_INNER) // MB_HEADDIM)[None, :]).astype(BF16)
    full = lambda *shape: pl.BlockSpec(shape, lambda: (0,) * len(shape))
    xs, bm, cm, xd, dec, conv_new = pl.pallas_call(
        _ssd_sample_pre_kernel,
        in_specs=[full(db, MB_CONV_DIM), full(halo, db, MB_CONV_DIM), full(db, LANES),
                  full(MB_CONV, MB_CONV_DIM), full(1, MB_CONV_DIM), full(1, LANES), full(1, LANES),
                  full(LANES, MB_INNER)],
        out_specs=[full(db, MB_INNER), full(db, MB_GN), full(db, MB_GN), full(db, MB_INNER), full(db, MB_INNER),
                   full(halo, db, MB_CONV_DIM)],
        out_shape=[jax.ShapeDtypeStruct((db, MB_INNER), F32), jax.ShapeDtypeStruct((db, MB_GN), F32),
                   jax.ShapeDtypeStruct((db, MB_GN), F32), jax.ShapeDtypeStruct((db, MB_INNER), F32),
                   jax.ShapeDtypeStruct((db, MB_INNER), F32), jax.ShapeDtypeStruct((halo, db, MB_CONV_DIM), F32)],
        compiler_params=pltpu.CompilerParams(vmem_limit_bytes=VMEM_LIMIT_BYTES),
        name="ssd_sample_pre",
    )(xbc, jnp.swapaxes(conv_state, 0, 1), dt, conv_w, conv_b.reshape(1, -1), _pad_lanes(dt_bias), _pad_lanes(a_log), expand)

    tb = math.gcd(db, 8)
    r3 = lambda i: (i, 0, 0)
    r4 = lambda i: (i, 0, 0, 0)
    h_new, y_col = pl.pallas_call(
        _ssd_sample_state_kernel,
        grid=(db // tb,),
        in_specs=[pl.BlockSpec((tb, MB_HEADS, MB_HEADDIM, MB_DSTATE), r4),
                  pl.BlockSpec((tb, MB_INNER, 1), r3), pl.BlockSpec((tb, MB_INNER, 1), r3),
                  pl.BlockSpec((tb, MB_GROUPS, MB_DSTATE), r3), pl.BlockSpec((tb, MB_GROUPS, MB_DSTATE), r3)],
        out_specs=[pl.BlockSpec((tb, MB_HEADS, MB_HEADDIM, MB_DSTATE), r4), pl.BlockSpec((tb, MB_INNER, 1), r3)],
        out_shape=[jax.ShapeDtypeStruct(h0.shape, F32), jax.ShapeDtypeStruct((db, MB_INNER, 1), F32)],
        compiler_params=_cparams("parallel"),
        name="ssd_sample_state",
    )(h0, xd.reshape(db, MB_INNER, 1), dec.reshape(db, MB_INNER, 1),
      bm.reshape(db, MB_GROUPS, MB_DSTATE), cm.reshape(db, MB_GROUPS, MB_DSTATE))

    o_b = pl.pallas_call(
        _ssd_sample_post_kernel,
        in_specs=[full(db, MB_INNER), full(db, MB_INNER), full(db, MB_INNER), full(1, MB_INNER), full(1, MB_INNER)],
        out_specs=full(db, MB_INNER),
        out_shape=jax.ShapeDtypeStruct((db, MB_INNER), F32),
        name="ssd_sample_post",
    )(y_col.reshape(db, MB_INNER), xs, z, jnp.repeat(d_skip, MB_HEADDIM).reshape(1, -1), norm_g.reshape(1, -1))
    return o_b, jnp.swapaxes(conv_new, 0, 1), h_new


def _select_column(x_t, b):
    lane = lax.broadcasted_iota(jnp.int32, x_t.shape, 1)
    return jnp.sum(jnp.where(lane == b, x_t, 0.0), axis=1, keepdims=True)


def _moba_gate_sample_kernel(pt_ref, q_ref, ck_ref, idx_ref, pbuf, sem, qt_ref, km_ref,
                             *, pages_per_block, pages_per_chunk, layer):
    b, c = pl.program_id(0), pl.program_id(1)
    nchunks = pl.num_programs(1)
    t = b * nchunks + c
    total = pl.num_programs(0) * nchunks
    slot = t % 2
    ppb, cpp = pages_per_block, pages_per_chunk
    bpc = cpp // ppb
    nblk = nchunks * bpc

    def copies(bb, cc, sl):
        return [pltpu.make_async_copy(ck_ref.at[layer, pt_ref[bb, cc * cpp + e]], pbuf.at[sl, e], sem.at[sl])
                for e in range(cpp)]

    @pl.when(t == 0)
    def _():
        for cp in copies(0, 0, 0):
            cp.start()
        qt_ref[...] = q_ref[...].T

    @pl.when(t + 1 < total)
    def _():
        for cp in copies((t + 1) // nchunks, (t + 1) % nchunks, 1 - slot):
            cp.start()

    for cp in copies(b, c, slot):
        cp.wait()

    @pl.when(c == 0)
    def _():
        km_ref[...] = jnp.zeros_like(km_ref)

    lane = lax.broadcasted_iota(jnp.int32, km_ref.shape, 1)
    km = km_ref[...]
    for jb in range(bpc):
        pg = pbuf[slot, jb * ppb]
        for e in range(1, ppb):
            pg = pg + pbuf[slot, jb * ppb + e]
        mean = jnp.sum(pg, axis=1, keepdims=True) * (1.0 / MOBA_BLOCK)
        km = jnp.where(lane == c * bpc + jb, mean, km)
    km_ref[...] = km

    @pl.when(c == nchunks - 1)
    def _():
        qcol = _select_column(qt_ref[...], b)
        km = km_ref[...]
        lane1 = lax.broadcasted_iota(jnp.int32, (1, LANES), 1)
        out = jnp.zeros((1, LANES), jnp.int32)
        heads = range(C_HEADS)
        s = [jnp.where(lane1 < nblk,
                       jnp.sum(km[(h // C_GROUP) * HEAD_DIM:(h // C_GROUP + 1) * HEAD_DIM, :]
                               * qcol[h * HEAD_DIM:(h + 1) * HEAD_DIM, :], axis=0, keepdims=True),
                       -jnp.inf) for h in heads]
        for k in range(MOBA_TOPK):
            m = [jnp.max(s[h], axis=1, keepdims=True) for h in heads]
            idx = [jnp.min(jnp.where(s[h] == m[h], lane1, LANES), axis=1, keepdims=True) for h in heads]
            s = [jnp.where(lane1 == idx[h], -jnp.inf, s[h]) for h in heads]
            for h in heads:
                out = jnp.where(lane1 == h * MOBA_TOPK + k, idx[h], out)
        idx_ref[0] = out


def _moba_gate_sample(page_table, q, cache_kt, layer):
    db, n_pages = page_table.shape
    ppb = MOBA_BLOCK // PAGE_SIZE
    cpp = math.gcd(n_pages, 64)
    nblk = n_pages // ppb
    assert cpp % ppb == 0 and MOBA_TOPK <= nblk <= LANES and C_HEADS * MOBA_TOPK <= LANES
    idx = pl.pallas_call(
        functools.partial(_moba_gate_sample_kernel, pages_per_block=ppb, pages_per_chunk=cpp, layer=layer),
        grid_spec=pltpu.PrefetchScalarGridSpec(
            num_scalar_prefetch=1,
            grid=(db, n_pages // cpp),
            in_specs=[pl.BlockSpec((db, C_QW), lambda b, c, pt: (0, 0)), pl.BlockSpec(memory_space=pl.ANY)],
            out_specs=pl.BlockSpec((1, 1, LANES), lambda b, c, pt: (b, 0, 0)),
            scratch_shapes=[pltpu.VMEM((2, cpp, C_KVW, PAGE_SIZE), F32), pltpu.SemaphoreType.DMA((2,)),
                            pltpu.VMEM((C_QW, db), F32), pltpu.VMEM((C_KVW, LANES), F32)]),
        out_shape=jax.ShapeDtypeStruct((db, 1, LANES), jnp.int32),
        compiler_params=_cparams("arbitrary", "arbitrary"),
        name="moba_gate_sample",
    )(page_table, q, cache_kt)
    return idx.reshape(db, LANES)


def _moba_sample_kernel(idx_ref, pt_ref, q_ref, kn_ref, vn_ref, ck_ref, cv_ref, o_ref, kbuf, vbuf, sem,
                        qt_ref, knt_ref, vnt_ref, ot_ref, *, pages_per_block, layer):
    b = pl.program_id(0)
    nb = pl.num_programs(0)
    slot = b % 2
    ppb = pages_per_block
    npg = MOBA_TOPK * ppb

    def copies(bb, sl):
        out = []
        for h in range(C_HEADS):
            rows = pl.ds((h // C_GROUP) * HEAD_DIM, HEAD_DIM)
            for k in range(MOBA_TOPK):
                blk = idx_ref[bb, h * MOBA_TOPK + k]
                for e in range(ppb):
                    page = pt_ref[bb, blk * ppb + e]
                    dst = h * npg + k * ppb + e
                    out.append(pltpu.make_async_copy(ck_ref.at[layer, page, rows, :], kbuf.at[sl, dst], sem.at[sl, 0]))
                    out.append(pltpu.make_async_copy(cv_ref.at[layer, page, rows, :], vbuf.at[sl, dst], sem.at[sl, 1]))
        return out

    @pl.when(b == 0)
    def _():
        for cp in copies(0, 0):
            cp.start()
        qt_ref[...] = q_ref[...].T
        knt_ref[...] = kn_ref[...].T
        vnt_ref[...] = vn_ref[...].T
        ot_ref[...] = jnp.zeros_like(ot_ref)

    @pl.when(b + 1 < nb)
    def _():
        for cp in copies(b + 1, 1 - slot):
            cp.start()

    for cp in copies(b, slot):
        cp.wait()

    qcol = _select_column(qt_ref[...], b) * ATT_SCALE
    kncol = _select_column(knt_ref[...], b)
    vncol = _select_column(vnt_ref[...], b)
    heads = range(C_HEADS)
    hrows = [slice(h * HEAD_DIM, (h + 1) * HEAD_DIM) for h in heads]
    kvrows = [slice((h // C_GROUP) * HEAD_DIM, (h // C_GROUP + 1) * HEAD_DIM) for h in heads]
    s = [[jnp.sum(kbuf[slot, h * npg + pg] * qcol[hrows[h], :], axis=0, keepdims=True) for pg in range(npg)]
         for h in heads]
    sn = [jnp.sum(qcol[hrows[h], :] * kncol[kvrows[h], :], axis=0, keepdims=True) for h in heads]
    m = [jnp.maximum(sn[h], jnp.max(functools.reduce(jnp.maximum, s[h]), axis=1, keepdims=True)) for h in heads]
    p = [[jnp.exp(row - m[h]) for row in s[h]] for h in heads]
    pn = [jnp.exp(sn[h] - m[h]) for h in heads]
    denom = [pn[h] + jnp.sum(functools.reduce(jnp.add, p[h]), axis=1, keepdims=True) for h in heads]
    acc = [functools.reduce(jnp.add, [vbuf[slot, h * npg + pg] * p[h][pg] for pg in range(npg)]) for h in heads]
    ocols = [(jnp.sum(acc[h], axis=1, keepdims=True) + pn[h] * vncol[kvrows[h], :]) / denom[h] for h in heads]
    lane = lax.broadcasted_iota(jnp.int32, ot_ref.shape, 1)
    ot_ref[...] = jnp.where(lane == b, jnp.concatenate(ocols, axis=0), ot_ref[...])

    @pl.when(b == nb - 1)
    def _():
        o_ref[...] = ot_ref[...].T


def _moba_sample(idx, page_table, q, kn, vn, cache_kt, cache_vt, layer):
    db = q.shape[0]
    ppb = MOBA_BLOCK // PAGE_SIZE
    nbuf = C_HEADS * MOBA_TOPK * ppb
    full = lambda *shape: pl.BlockSpec(shape, lambda b, *_: (0,) * len(shape))
    return pl.pallas_call(
        functools.partial(_moba_sample_kernel, pages_per_block=ppb, layer=layer),
        grid_spec=pltpu.PrefetchScalarGridSpec(
            num_scalar_prefetch=2,
            grid=(db,),
            in_specs=[full(db, C_QW), full(db, C_KVW), full(db, C_KVW),
                      pl.BlockSpec(memory_space=pl.ANY), pl.BlockSpec(memory_space=pl.ANY)],
            out_specs=full(db, C_QW),
            scratch_shapes=[pltpu.VMEM((2, nbuf, HEAD_DIM, PAGE_SIZE), F32),
                            pltpu.VMEM((2, nbuf, HEAD_DIM, PAGE_SIZE), F32),
                            pltpu.SemaphoreType.DMA((2, 2)),
                            pltpu.VMEM((C_QW, db), F32), pltpu.VMEM((C_KVW, db), F32),
                            pltpu.VMEM((C_KVW, db), F32), pltpu.VMEM((C_QW, db), F32)]),
        out_shape=jax.ShapeDtypeStruct((db, C_QW), F32),
        compiler_params=_cparams("arbitrary"),
        name="moba_sample",
    )(idx, page_table, q, kn, vn, cache_kt, cache_vt)


def _decoder(st, x, pos, p, state):
    is_prompt = state is None
    nseq, seqlen, d = st.nseq, st.seqlen, st.d
    depth = p['w_ada'].shape[0]
    rope_tabs = _rope_tables(pos if is_prompt else jnp.broadcast_to(pos, (st.tokens,)))
    new = {'win_k': [], 'win_v': [], 'conv': [], 'ssm': [], 'k': [], 'v': []}
    for layer in range(depth):
        g_norm = p['norm_g'][layer]
        if layer % 2 == 0:
            i = layer // 2
            splits = ((0, A_QW, True, True), (A_QW, A_KVW, True, True), (A_QW + A_KVW, A_KVW, False, True),
                      (A_QW + 2 * A_KVW, MB_INNER, False, True),
                      (A_QW + 2 * A_KVW + MB_INNER, MB_CONV_DIM, False, True),
                      (A_QW + 2 * A_KVW + MB_INNER + MB_CONV_DIM, LANES, False, True))
            qa, ka, va, z, xbc, dt = _proj(st, layer, x, g_norm[0], p['w_in_a'][i], rope_tabs, splits)
            if is_prompt:
                o_a = _swa_prompt(qa, ka, va, p['sinks'][i], nseq, seqlen)
                wk = ka.reshape(nseq, seqlen, A_KV_HEADS, HEAD_DIM)[:, -WINDOW:]
                wv = va.reshape(nseq, seqlen, A_KV_HEADS, HEAD_DIM)[:, -WINDOW:]
                conv_prev = jnp.zeros((nseq, MB_CONV - 1, MB_CONV_DIM), F32)
                h0 = jnp.zeros((nseq, MB_HEADS, MB_HEADDIM, MB_DSTATE), F32)
                o_b, conv_new, h_new = _ssd_prompt(xbc, z, dt, conv_prev, h0, p['conv_w'][i], p['conv_b'][i],
                                                   p['dt_bias'][i], p['a_log'][i], p['d_skip'][i],
                                                   p['ssm_norm_g'][i], nseq, seqlen)
            else:
                o_a, wk, wv = _swa_sample(qa, ka, va, state['win_k'][i].reshape(nseq, WINDOW, A_KVW),
                                          state['win_v'][i].reshape(nseq, WINDOW, A_KVW), p['sinks'][i])
                wk = wk.reshape(nseq, WINDOW, A_KV_HEADS, HEAD_DIM)
                wv = wv.reshape(nseq, WINDOW, A_KV_HEADS, HEAD_DIM)
                o_b, conv_new, h_new = _ssd_sample(xbc, z, dt, state['conv'][i], state['ssm'][i], p['conv_w'][i],
                                                   p['conv_b'][i], p['dt_bias'][i], p['a_log'][i], p['d_skip'][i],
                                                   p['ssm_norm_g'][i])
            a_list = [o_a, o_b]
            w_list = [p['w_out_a'][i][:A_QW], p['w_out_a'][i][A_QW:]]
            new['win_k'].append(wk)
            new['win_v'].append(wv)
            new['conv'].append(conv_new)
            new['ssm'].append(h_new)
        else:
            j = layer // 2
            cols = ((0, C_QW, True), (C_QW, C_KVW, True), (C_QW + C_KVW, C_KVW, False))
            if is_prompt:
                splits = tuple(c + (e,) for c, e in zip(cols, (True, False, False)))
                extras = (('bf16', 1), ('transposed', 1), ('transposed', 2), ('blockmean', 1))
                qc, k_rows, k_t, v_t, kmean = _proj(st, layer, x, g_norm[0], p['w_in_c'][j], rope_tabs, splits, extras)
                o_c = _moba_prompt(qc, k_rows, v_t, kmean.reshape(nseq, seqlen // MOBA_BLOCK, C_KVW), nseq, seqlen)
                kc, vc = (jnp.transpose(t.reshape(nseq, C_KV_HEADS, HEAD_DIM, seqlen), (0, 3, 1, 2)) for t in (k_t, v_t))
            else:
                splits = tuple(c + (True,) for c in cols)
                qc, kc, vc = _proj(st, layer, x, g_norm[0], p['w_in_c'][j], rope_tabs, splits)
                idx = _moba_gate_sample(state['page_table'], qc, state['cache_k'], j)
                o_c = _moba_sample(idx, state['page_table'], qc, kc, vc, state['cache_k'], state['cache_v'], j)
                kc, vc = (t.reshape(nseq, seqlen, C_KV_HEADS, HEAD_DIM) for t in (kc, vc))
            a_list = [o_c]
            w_list = [p['w_out_c'][j]]
            new['k'].append(kc)
            new['v'].append(vc)
        sorted_moe = st.tokens >= 2 * SORT_TILE
        moe = _moe_sorted if sorted_moe else _moe
        x1, h2, aux = _mix(st, layer, x, a_list, w_list, g_norm[1], p['wr_pad'], p['rb_col'], sorted_moe)
        x = moe(st, layer, h2, aux, p['w_gate'], p['w_up'], p['w_down'], x1, p['final_norm_g'],
                final=(layer == depth - 1))
    return x.reshape(nseq, seqlen, d), {name: jnp.stack(rows) for name, rows in new.items()}


def kernel(x_prompt, x_sample, c_prompt, c_sample, state_win_k, state_win_v, state_conv, state_ssm, cache_k, cache_v, page_table, w_ada, b_ada, norm_g, w_in_a, sinks, conv_w, conv_b, dt_bias, a_log, d_skip, ssm_norm_g, w_out_a, w_in_c, w_out_c, w_router, router_bias, w_gate, w_up, w_down, final_norm_g):
    nb, seqlen, d = x_prompt.shape
    db, dec_seq, _ = x_sample.shape
    assert dec_seq == 1 and seqlen % MOBA_BLOCK == 0 and d % LANES == 0
    n_odd, n_pool, page_size, ckv, hd = cache_k.shape
    assert page_size == PAGE_SIZE and ckv == C_KV_HEADS and hd == HEAD_DIM
    n_pages = page_table.shape[1]

    n_in_a = w_in_a.shape[2]
    pad_a = (-n_in_a) % LANES
    p = {'w_ada': w_ada, 'norm_g': norm_g, 'sinks': sinks, 'conv_w': conv_w, 'conv_b': conv_b, 'dt_bias': dt_bias,
         'a_log': a_log, 'd_skip': d_skip, 'ssm_norm_g': ssm_norm_g, 'final_norm_g': final_norm_g,
         'w_in_a': jnp.pad(w_in_a, ((0, 0), (0, 0), (0, pad_a))).astype(BF16),
         'w_out_a': w_out_a.astype(BF16), 'w_in_c': w_in_c.astype(BF16), 'w_out_c': w_out_c.astype(BF16),
         'w_gate': w_gate.astype(BF16), 'w_up': w_up.astype(BF16), 'w_down': w_down.astype(BF16),
         'wr_pad': jnp.pad(w_router, ((0, 0), (0, LANES - N_EXPERTS))),
         'rb_col': jnp.pad(router_bias, (0, LANES - N_EXPERTS)).reshape(LANES, 1)}

    mod = _ada(jnp.concatenate([c_prompt, c_sample], axis=0), w_ada, b_ada)
    st_p = _Stream(nb, seqlen, d, mod[:, :nb])
    st_s = _Stream(db, 1, d, mod[:, nb:])

    pos_p = jnp.arange(seqlen, dtype=jnp.int32)
    y_prompt, pn = _decoder(st_p, x_prompt.reshape(nb * seqlen, d), pos_p, p, None)

    cache_kt = jnp.transpose(cache_k, (0, 1, 3, 4, 2)).reshape(n_odd, n_pool, C_KVW, PAGE_SIZE)
    cache_vt = jnp.transpose(cache_v, (0, 1, 3, 4, 2)).reshape(n_odd, n_pool, C_KVW, PAGE_SIZE)
    state = {'win_k': state_win_k, 'win_v': state_win_v, 'conv': state_conv, 'ssm': state_ssm,
             'cache_k': cache_kt, 'cache_v': cache_vt, 'page_table': page_table}
    pos_s = n_pages * PAGE_SIZE + jnp.arange(1, dtype=jnp.int32)
    y_sample, sn = _decoder(st_s, x_sample.reshape(db, d), pos_s, p, state)
    return (y_prompt, y_sample, pn['win_k'], pn['win_v'], pn['conv'], pn['ssm'], pn['k'], pn['v'],
            sn['win_k'], sn['win_v'], sn['conv'], sn['ssm'], sn['k'], sn['v'])
```

```python
import functools
import math

import jax
import jax.numpy as jnp
import numpy as np
from jax import lax
from jax.experimental import pallas as pl
from jax.experimental.pallas import tpu as pltpu

F32 = jnp.float32
BF16 = jnp.bfloat16

HEAD_DIM = 64
ROT_DIM = HEAD_DIM // 4
ROPE_THETA = 500000.0
A_HEADS = 8
A_KV_HEADS = 2
A_GROUP = A_HEADS // A_KV_HEADS
WINDOW = 128
MB_HEADDIM = 64
MB_HEADS = 8
MB_INNER = MB_HEADS * MB_HEADDIM
MB_GROUPS = 2
MB_DSTATE = 128
MB_CONV = 4
MB_GN = MB_GROUPS * MB_DSTATE
MB_CONV_DIM = MB_INNER + 2 * MB_GN
SSD_CHUNK = 128
C_HEADS = 16
C_KV_HEADS = 4
C_GROUP = C_HEADS // C_KV_HEADS
MOBA_BLOCK = 256
MOBA_TOPK = 3
MOBA_QTILE = MOBA_BLOCK
PAGE_SIZE = 128
A_QW = A_HEADS * HEAD_DIM
A_KVW = A_KV_HEADS * HEAD_DIM
C_QW = C_HEADS * HEAD_DIM
C_KVW = C_KV_HEADS * HEAD_DIM
N_EXPERTS = 16
N_EXPERT_GROUPS = 4
EXPERTS_PER_GROUP = N_EXPERTS // N_EXPERT_GROUPS
PAIRS_PER_GROUP = EXPERTS_PER_GROUP * (EXPERTS_PER_GROUP - 1) // 2
N_BUCKETS = N_EXPERT_GROUPS * PAIRS_PER_GROUP
BUCKET_ROWS = 32
ROUTE_ROWS = 8
SORT_TILE = 256
RMS_EPS = 1e-6
NEG_INF = -1e30
ATT_SCALE = HEAD_DIM ** -0.5

LANES = 128
VMEM_LIMIT_BYTES = 56 * 1024 * 1024


def _cparams(*sem):
    return pltpu.CompilerParams(dimension_semantics=sem, vmem_limit_bytes=VMEM_LIMIT_BYTES)


def _dot(a, b):
    return jnp.dot(a, b, preferred_element_type=F32)


def _dot_nt(a, b):
    return lax.dot_general(a, b, (((1,), (1,)), ((), ())), preferred_element_type=F32)


def _dot_tn(a, b):
    return lax.dot_general(a, b, (((0,), (0,)), ((), ())), preferred_element_type=F32)


def _split2(x):
    hi = x.astype(BF16)
    lo = (x - hi.astype(F32)).astype(BF16)
    return hi, lo


def _dot_hi(a, b):
    ah, al = _split2(a)
    bh, bl = _split2(b)
    return _dot(ah, bh) + (_dot(al, bh) + _dot(ah, bl))


def _dot_hi_nt(a, b):
    ah, al = _split2(a)
    bh, bl = _split2(b)
    return _dot_nt(ah, bh) + (_dot_nt(al, bh) + _dot_nt(ah, bl))


def _dot_exact_lhs(a_bf16, b):
    b0 = b.astype(BF16)
    r = b - b0.astype(F32)
    b1 = r.astype(BF16)
    b2 = (r - b1.astype(F32)).astype(BF16)
    return _dot(a_bf16, b0) + (_dot(a_bf16, b1) + _dot(a_bf16, b2))


def _sigmoid(x):
    return 1.0 / (1.0 + jnp.exp(-x))


def _silu(x):
    return x * _sigmoid(x)


def _softplus(x):
    return jnp.maximum(x, 0.0) + jnp.log(1.0 + jnp.exp(-jnp.abs(x)))


def _rms(x, g):
    return x * lax.rsqrt(jnp.mean(x * x, axis=-1, keepdims=True) + RMS_EPS) * g


def _rope_tables(pos):
    half = ROT_DIM // 2
    inv_freq = jnp.power(ROPE_THETA, -jnp.arange(half, dtype=F32) / half)
    ang = pos.astype(F32)[:, None] * inv_freq
    cos, sin = jnp.cos(ang), jnp.sin(ang)
    n = pos.shape[0]
    rest = HEAD_DIM - ROT_DIM
    ct = jnp.concatenate([cos, cos, jnp.ones((n, rest), F32)], axis=1)
    sa = jnp.concatenate([jnp.zeros((n, half), F32), sin, jnp.zeros((n, rest), F32)], axis=1)
    sb = jnp.concatenate([-sin, jnp.zeros((n, half), F32), jnp.zeros((n, rest), F32)], axis=1)
    rep = LANES // HEAD_DIM
    return jnp.tile(ct, (1, rep)), jnp.tile(sa, (1, rep)), jnp.tile(sb, (1, rep))


def _rope(x, ct, sa, sb):
    half = ROT_DIM // 2
    return x * ct + pltpu.roll(x, half, 1) * sa + pltpu.roll(x, LANES - half, 1) * sb


def _ada_kernel(c_ref, w_ref, b_ref, o_ref):
    o_ref[...] = _dot_hi(_silu(c_ref[...]), w_ref[...]) + b_ref[...]


def _ada(c_all, w_ada, b_ada):
    depth, d, n6 = w_ada.shape
    nc = c_all.shape[0]
    nk = n6 // d
    return pl.pallas_call(
        _ada_kernel,
        grid=(depth, nk),
        in_specs=[pl.BlockSpec((nc, d), lambda l, k: (0, 0)),
                  pl.BlockSpec((None, d, d), lambda l, k: (l, 0, k)),
                  pl.BlockSpec((None, 1, d), lambda l, k: (l, 0, k))],
        out_specs=pl.BlockSpec((None, nc, d), lambda l, k: (l, 0, k)),
        out_shape=jax.ShapeDtypeStruct((depth, nc, n6), F32),
        compiler_params=_cparams("parallel", "parallel"),
        name="ada",
    )(c_all, w_ada, b_ada.reshape(depth, 1, n6))


class _Stream:
    def __init__(self, nseq, seqlen, d, mod):
        self.nseq, self.seqlen, self.d = nseq, seqlen, d
        self.tokens = nseq * seqlen
        if seqlen == 1:
            self.tm = self.tokens
            self.tiles_per_seq = None
            self.mod = mod
        else:
            self.tm = math.gcd(seqlen, 512)
            self.tiles_per_seq = seqlen // self.tm
            depth = mod.shape[0]
            self.mod = mod.reshape(depth, nseq * 6, 1, d)
        self.ntiles = self.tokens // self.tm

    def mod_arg(self, layer):
        return self.mod[layer]

    def mod_spec(self, k):
        if self.tiles_per_seq is None:
            return pl.BlockSpec((self.tm, self.d), lambda i, *_: (0, k))
        tps = self.tiles_per_seq
        return pl.BlockSpec((None, 1, self.d), lambda i, *_: ((i // tps) * 6 + k, 0, 0))

    def rope_spec(self):
        if self.tiles_per_seq is None:
            return pl.BlockSpec((self.tm, LANES), lambda i, *_: (0, 0))
        tps = self.tiles_per_seq
        return pl.BlockSpec((self.tm, LANES), lambda i, *_: (i % tps, 0))


def _proj_kernel(x_ref, *refs, splits, extras):
    _proj_compute(x_ref[...], *refs, splits=splits, extras=extras)


def _proj_combine_kernel(dest_ref, x1_ref, g2_ref, ys_ref, *refs, splits, extras):
    *refs, x_out_ref, buf, sem = refs
    i = pl.program_id(0)
    n = pl.num_programs(0)
    tm = x1_ref.shape[0]
    slot = i % 2

    def row_copy(tile, r, sl):
        return pltpu.make_async_copy(ys_ref.at[pl.ds(dest_ref[tile * tm + r], 1)], buf.at[sl, pl.ds(r, 1)], sem.at[sl])

    def tile_wait(sl):
        pltpu.make_async_copy(ys_ref.at[pl.ds(0, tm)], buf.at[sl], sem.at[sl]).wait()

    @pl.when(i == 0)
    def _():
        def body(r, carry):
            row_copy(0, r, 0).start()
            return carry

        lax.fori_loop(0, tm, body, 0, unroll=8)

    tile_wait(slot)
    nxt = jnp.minimum(i + 1, n - 1)
    for r in range(tm):
        row_copy(nxt, r, 1 - slot).start()
    x = x1_ref[...] + g2_ref[...] * buf[slot]
    x_out_ref[...] = x
    _proj_compute(x, *refs, splits=splits, extras=extras)

    @pl.when(i == n - 1)
    def _():
        tile_wait(1 - slot)


def _proj_compute(x, sh_ref, sc_ref, g_ref, w_ref, ct_ref, sa_ref, sb_ref, *out_refs, splits, extras):
    h = _rms(x, g_ref[...]) * (1.0 + sc_ref[...]) + sh_ref[...]
    u = _dot(h.astype(BF16), w_ref[...])
    outs = list(out_refs)
    pieces = []
    for start, width, rope, emit in splits:
        if rope:
            ct, sa, sb = ct_ref[...], sa_ref[...], sb_ref[...]
            piece = jnp.concatenate([_rope(u[:, start + c0:start + c0 + LANES], ct, sa, sb)
                                     for c0 in range(0, width, LANES)], axis=-1)
        else:
            piece = u[:, start:start + width]
        pieces.append(piece)
        if emit:
            outs.pop(0)[...] = piece
    for kind, idx in extras:
        o_ref, piece = outs.pop(0), pieces[idx]
        if kind == 'bf16':
            o_ref[...] = piece.astype(BF16)
        elif kind == 'transposed':
            o_ref[...] = piece.T
        else:
            for c in range(o_ref.shape[0]):
                o_ref[c] = jnp.sum(piece[c * MOBA_BLOCK:(c + 1) * MOBA_BLOCK, :], axis=0,
                                   keepdims=True) * (1.0 / MOBA_BLOCK)


def _proj(st, layer, x, norm_g, w_bf16, rope_tabs, splits, extras=()):
    d = st.d
    n = w_bf16.shape[1]
    tm = st.tm
    row = lambda i, *_: (i, 0)
    const = lambda i, *_: (0, 0)
    out_specs = [pl.BlockSpec((tm, w), row) for _, w, _, emit in splits if emit]
    out_shape = [jax.ShapeDtypeStruct((st.tokens, w), F32) for _, w, _, emit in splits if emit]
    for kind, idx in extras:
        w = splits[idx][1]
        if kind == 'bf16':
            out_specs.append(pl.BlockSpec((tm, w), row))
            out_shape.append(jax.ShapeDtypeStruct((st.tokens, w), BF16))
        elif kind == 'transposed':
            tps = st.tiles_per_seq
            out_specs.append(pl.BlockSpec((None, w, tm), lambda i, *_: (i // tps, 0, i % tps)))
            out_shape.append(jax.ShapeDtypeStruct((st.nseq, w, st.seqlen), F32))
        else:
            nb = tm // MOBA_BLOCK
            out_specs.append(pl.BlockSpec((nb, 1, w), lambda i, *_: (i, 0, 0)))
            out_shape.append(jax.ShapeDtypeStruct((st.tokens // MOBA_BLOCK, 1, w), F32))
    in_specs = [st.mod_spec(0), st.mod_spec(1), pl.BlockSpec((1, d), const), pl.BlockSpec((d, n), const),
                st.rope_spec(), st.rope_spec(), st.rope_spec()]
    args = (st.mod_arg(layer), st.mod_arg(layer), norm_g.reshape(1, d), w_bf16, *rope_tabs)
    if not isinstance(x, tuple):
        return pl.pallas_call(
            functools.partial(_proj_kernel, splits=splits, extras=extras),
            grid=(st.ntiles,),
            in_specs=[pl.BlockSpec((tm, d), row)] + in_specs,
            out_specs=out_specs,
            out_shape=out_shape,
            compiler_params=_cparams("parallel"),
            name="proj",
        )(x, *args)
    x1, ys, dest = x
    return pl.pallas_call(
        functools.partial(_proj_combine_kernel, splits=splits, extras=extras),
        grid_spec=pltpu.PrefetchScalarGridSpec(
            num_scalar_prefetch=1,
            grid=(st.ntiles,),
            in_specs=[pl.BlockSpec((tm, d), row), st.mod_spec(5), pl.BlockSpec(memory_space=pl.ANY)] + in_specs,
            out_specs=out_specs + [pl.BlockSpec((tm, d), row)],
            scratch_shapes=[pltpu.VMEM((2, tm, d), F32), pltpu.SemaphoreType.DMA((2,))]),
        out_shape=out_shape + [jax.ShapeDtypeStruct((st.tokens, d), F32)],
        compiler_params=_cparams("arbitrary"),
        name="proj_combine",
    )(dest, x1, st.mod_arg(layer - 1), ys, *args)


def _swa_prompt_kernel(sink_ref, q_ref, kp_ref, kc_ref, vp_ref, vc_ref, o_ref):
    i = pl.program_id(1)
    qb = q_ref.shape[0]
    cols = A_GROUP * qb
    q_t = q_ref[...].T
    kk = jnp.concatenate([kp_ref[...], kc_ref[...]], axis=0).astype(BF16)
    vv_t = jnp.concatenate([vp_ref[...], vc_ref[...]], axis=0).T.astype(BF16)
    key_i = lax.broadcasted_iota(jnp.int32, (2 * qb, cols), 0)
    qry_i = lax.broadcasted_iota(jnp.int32, (2 * qb, cols), 1) % qb
    diff = qry_i + qb - key_i
    ok = (diff >= 0) & (diff <= WINDOW) & ((key_i >= qb) | (i > 0))
    head_of_lane = lax.broadcasted_iota(jnp.int32, (1, cols), 1) // qb
    outs = []
    for kv in range(A_KV_HEADS):
        rows = slice(kv * HEAD_DIM, (kv + 1) * HEAD_DIM)
        qs_t = jnp.concatenate(
            [q_t[(kv * A_GROUP + g) * HEAD_DIM:(kv * A_GROUP + g + 1) * HEAD_DIM, :] for g in range(A_GROUP)], axis=1)
        s = _dot(kk[:, rows], (qs_t * ATT_SCALE).astype(BF16))
        s = jnp.where(ok, s, NEG_INF)
        sink = jnp.zeros((1, cols), F32)
        for g in range(A_GROUP):
            sink = jnp.where(head_of_lane == g, sink_ref[kv * A_GROUP + g], sink)
        m = jnp.maximum(jnp.max(s, axis=0, keepdims=True), sink)
        p = jnp.exp(s - m)
        denom = jnp.sum(p, axis=0, keepdims=True) + jnp.exp(sink - m)
        outs.append(_dot(vv_t[rows, :], p.astype(BF16)) / denom)
    o_t = jnp.concatenate(outs, axis=0).T
    o_ref[...] = jnp.concatenate(
        [o_t[g * qb:(g + 1) * qb, kv * HEAD_DIM:(kv + 1) * HEAD_DIM]
         for kv in range(A_KV_HEADS) for g in range(A_GROUP)], axis=-1)


def _swa_prompt(q, k, v, sinks, nseq, seqlen):
    qb = WINDOW
    nb = seqlen // qb
    cur = lambda b, i: (b * nb + i, 0)
    prev = lambda b, i: (b * nb + jnp.maximum(i - 1, 0), 0)
    return pl.pallas_call(
        _swa_prompt_kernel,
        grid=(nseq, nb),
        in_specs=[pl.BlockSpec(memory_space=pltpu.SMEM),
                  pl.BlockSpec((qb, A_QW), cur),
                  pl.BlockSpec((qb, A_KVW), prev), pl.BlockSpec((qb, A_KVW), cur),
                  pl.BlockSpec((qb, A_KVW), prev), pl.BlockSpec((qb, A_KVW), cur)],
        out_specs=pl.BlockSpec((qb, A_QW), cur),
        out_shape=jax.ShapeDtypeStruct(q.shape, F32),
        compiler_params=_cparams("parallel", "parallel"),
        name="swa_prompt",
    )(sinks, q, k, k, v, v)


def _ssd_prompt_kernel(xbc_ref, z_ref, dt_ref, cprev_ref, h0_ref, cw_ref, cb_ref, dtb_ref, alog_ref, dskip_ref,
                       ng_ref, o_ref, cnew_ref, hlast_ref, xp_ref, h_ref):
    c = pl.program_id(1)
    nc = pl.num_programs(1)
    q = SSD_CHUNK
    halo = MB_CONV - 1
    base = 8 - halo

    @pl.when(c == 0)
    def _():
        xp_ref[base:8, :] = cprev_ref[...]
        h_ref[...] = h0_ref[...]

    xp_ref[8:8 + q, :] = xbc_ref[...]
    acc = cb_ref[...] + xp_ref[base:base + q, :] * cw_ref[0:1, :]
    for tap in range(1, MB_CONV):
        acc = acc + xp_ref[base + tap:base + tap + q, :] * cw_ref[tap:tap + 1, :]
    tail = xp_ref[8 + q - halo:8 + q, :]
    xp_ref[base:8, :] = tail

    @pl.when(c == nc - 1)
    def _():
        cnew_ref[...] = tail

    xbc = _silu(acc)
    xs = xbc[:, :MB_INNER]
    dt = _softplus(dt_ref[...] + dtb_ref[...])
    da = dt * (-jnp.exp(alog_ref[...]))
    r_i = lax.broadcasted_iota(jnp.int32, (q, q), 0)
    c_i = lax.broadcasted_iota(jnp.int32, (q, q), 1)
    causal = r_i >= c_i
    acum = _dot_exact_lhs(causal.astype(BF16), da)
    acum_t = acum.T
    rpg = MB_HEADS // MB_GROUPS
    ys = []
    h_all = [h_ref[h] for h in range(MB_HEADS)]
    h_new = []
    for g in range(MB_GROUPS):
        bq = xbc[:, MB_INNER + g * MB_DSTATE:MB_INNER + (g + 1) * MB_DSTATE].astype(BF16)
        cq = xbc[:, MB_INNER + MB_GN + g * MB_DSTATE:MB_INNER + MB_GN + (g + 1) * MB_DSTATE].astype(BF16)
        cbm = _dot_nt(cq, bq)
        for r in range(rpg):
            h = g * rpg + r
            a_col = acum[:, h:h + 1]
            a_row = acum_t[h:h + 1, :]
            a_last = acum[q - 1:q, h:h + 1]
            decay = jnp.where(causal, jnp.exp(jnp.where(causal, a_col - a_row, 0.0)), 0.0)
            xd = xs[:, h * MB_HEADDIM:(h + 1) * MB_HEADDIM] * dt[:, h:h + 1]
            hprev = h_all[h]
            y = _dot((cbm * decay).astype(BF16), xd.astype(BF16))
            y = y + jnp.exp(a_col) * _dot_nt(cq, hprev.astype(BF16))
            xw = (xd * jnp.exp(a_last - a_col)).astype(BF16)
            h_new.append(hprev * jnp.exp(a_last) + _dot_tn(xw, bq))
            ys.append(y)
    for h in range(MB_HEADS):
        h_ref[h] = h_new[h]
    y = jnp.concatenate(ys, axis=-1) + xs * dskip_ref[...]
    y = y * _silu(z_ref[...])
    gw = MB_INNER // MB_GROUPS
    ng = ng_ref[...]
    o_ref[...] = jnp.concatenate(
        [_rms(y[:, g * gw:(g + 1) * gw], ng[:, g * gw:(g + 1) * gw]) for g in range(MB_GROUPS)], axis=-1)

    @pl.when(c == nc - 1)
    def _():
        hlast_ref[...] = h_ref[...]


def _pad_lanes(v, n=LANES):
    return jnp.pad(v, (0, n - v.shape[0])).reshape(1, n)


def _ssd_prompt(xbc, z, dt, conv_prev, h0, conv_w, conv_b, dt_bias, a_log, d_skip, norm_g, nseq, seqlen):
    q = SSD_CHUNK
    nc = seqlen // q
    row = lambda b, c: (b * nc + c, 0)
    per_seq3 = lambda b, c: (b, 0, 0)
    per_seq4 = lambda b, c: (b, 0, 0, 0)
    const = lambda b, c: (0, 0)
    halo = MB_CONV - 1
    return pl.pallas_call(
        _ssd_prompt_kernel,
        grid=(nseq, nc),
        in_specs=[pl.BlockSpec((q, MB_CONV_DIM), row), pl.BlockSpec((q, MB_INNER), row),
                  pl.BlockSpec((q, LANES), row),
                  pl.BlockSpec((None, halo, MB_CONV_DIM), per_seq3),
                  pl.BlockSpec((None, MB_HEADS, MB_HEADDIM, MB_DSTATE), per_seq4),
                  pl.BlockSpec((MB_CONV, MB_CONV_DIM), const), pl.BlockSpec((1, MB_CONV_DIM), const),
                  pl.BlockSpec((1, LANES), const), pl.BlockSpec((1, LANES), const),
                  pl.BlockSpec((1, MB_INNER), const), pl.BlockSpec((1, MB_INNER), const)],
        out_specs=[pl.BlockSpec((q, MB_INNER), row),
                   pl.BlockSpec((None, halo, MB_CONV_DIM), per_seq3),
                   pl.BlockSpec((None, MB_HEADS, MB_HEADDIM, MB_DSTATE), per_seq4)],
        out_shape=[jax.ShapeDtypeStruct((nseq * seqlen, MB_INNER), F32),
                   jax.ShapeDtypeStruct((nseq, halo, MB_CONV_DIM), F32),
                   jax.ShapeDtypeStruct((nseq, MB_HEADS, MB_HEADDIM, MB_DSTATE), F32)],
        scratch_shapes=[pltpu.VMEM((8 + q, MB_CONV_DIM), F32),
                        pltpu.VMEM((MB_HEADS, MB_HEADDIM, MB_DSTATE), F32)],
        compiler_params=_cparams("parallel", "arbitrary"),
        name="ssd_prompt",
    )(xbc, z, dt, conv_prev, h0, conv_w, conv_b.reshape(1, -1), _pad_lanes(dt_bias), _pad_lanes(a_log),
      jnp.repeat(d_skip, MB_HEADDIM).reshape(1, -1), norm_g.reshape(1, -1))


def _route(logits_t, rbias_col):
    tm = logits_t.shape[1]
    scores = _sigmoid(logits_t[0:N_EXPERTS, :])
    biased = scores + rbias_col[0:N_EXPERTS, :]
    s = [scores[e:e + 1, :] for e in range(N_EXPERTS)]
    b = [biased[e:e + 1, :] for e in range(N_EXPERTS)]
    epg = EXPERTS_PER_GROUP
    gscore = []
    for g in range(N_EXPERT_GROUPS):
        v = b[g * epg:(g + 1) * epg]
        best = None
        for i in range(epg):
            for j in range(i + 1, epg):
                pair = v[i] + v[j]
                best = pair if best is None else jnp.maximum(best, pair)
        gscore.append(best)
    gsel = jnp.zeros((1, tm), jnp.int32)
    gbest = gscore[0]
    for g in range(1, N_EXPERT_GROUPS):
        better = gscore[g] > gbest
        gsel = jnp.where(better, g, gsel)
        gbest = jnp.where(better, gscore[g], gbest)
    bs, ss = [], []
    for k in range(epg):
        bk, sk = b[k], s[k]
        for g in range(1, N_EXPERT_GROUPS):
            bk = jnp.where(gsel == g, b[g * epg + k], bk)
            sk = jnp.where(gsel == g, s[g * epg + k], sk)
        bs.append(bk)
        ss.append(sk)
    i1 = jnp.zeros((1, tm), jnp.int32)
    m1 = bs[0]
    for k in range(1, epg):
        better = bs[k] > m1
        i1 = jnp.where(better, k, i1)
        m1 = jnp.where(better, bs[k], m1)
    i2 = jnp.full((1, tm), -1, jnp.int32)
    m2 = jnp.full((1, tm), -jnp.inf, F32)
    for k in range(epg):
        better = (i1 != k) & ((bs[k] > m2) | (i2 < 0))
        i2 = jnp.where(better, k, i2)
        m2 = jnp.where(better, bs[k], m2)
    s1 = jnp.zeros((1, tm), F32)
    s2 = jnp.zeros((1, tm), F32)
    for k in range(epg):
        s1 = jnp.where(i1 == k, ss[k], s1)
        s2 = jnp.where(i2 == k, ss[k], s2)
    denom = s1 + s2
    rows = lax.broadcasted_iota(jnp.int32, (N_EXPERTS, tm), 0)
    comb = jnp.zeros((N_EXPERTS, tm), F32)
    for e in range(N_EXPERTS):
        g, k = divmod(e, epg)
        chosen = (gsel == g) & ((i1 == k) | (i2 == k))
        comb = jnp.where(rows == e, jnp.where(chosen, s[e] / denom, 0.0), comb)
    lo = jnp.minimum(i1, i2)
    hi = jnp.maximum(i1, i2)
    pair = jnp.where(lo == 0, 0, jnp.where(lo == 1, epg - 1, 2 * epg - 3)) + (hi - lo - 1)
    bucket = (gsel * PAIRS_PER_GROUP + pair).astype(F32)
    first_is_lo = i1 < i2
    w_lo = jnp.where(first_is_lo, s1, s2) / denom
    w_hi = jnp.where(first_is_lo, s2, s1) / denom
    return comb, bucket, w_lo, w_hi


def _mix_kernel(*refs, n_in, sorted_moe):
    x_ref, g1_ref = refs[0], refs[1]
    a_refs = refs[2:2 + n_in]
    w_refs = refs[2 + n_in:2 + 2 * n_in]
    ng_ref, sc_ref, sh_ref, wr_ref, rb_ref, x1_ref, h2_ref, aux_ref = refs[2 + 2 * n_in:]
    mix = _dot(a_refs[0][...].astype(BF16), w_refs[0][...])
    for a_ref, w_ref in zip(a_refs[1:], w_refs[1:]):
        mix = mix + _dot(a_ref[...].astype(BF16), w_ref[...])
    x1 = x_ref[...] + g1_ref[...] * mix
    x1_ref[...] = x1
    h2 = _rms(x1, ng_ref[...]) * (1.0 + sc_ref[...]) + sh_ref[...]
    logits = _dot_hi(h2, wr_ref[...])
    comb_t, bucket, w_lo, w_hi = _route(logits.T, rb_ref[...])
    tm, d = x1.shape
    if sorted_moe:
        r = lax.broadcasted_iota(jnp.int32, (LANES, tm), 0)
        rt = jnp.where(r == 0, bucket, jnp.where(r == 1, w_lo, jnp.where(r == 2, w_hi, 0.0)))
        h2_ref[:, :d] = h2
        h2_ref[:, d:] = rt.T
        aux_ref[...] = rt[0:aux_ref.shape[0], :]
    else:
        h2_ref[...] = h2.astype(BF16)
        aux_ref[...] = jnp.concatenate([comb_t, jnp.zeros((LANES - N_EXPERTS, tm), F32)], axis=0).T


def _mix(st, layer, x, a_list, w_list, norm_g, wr_pad, rb_col, sorted_moe):
    d, tm = st.d, st.tm
    row = lambda i: (i, 0)
    const = lambda i: (0, 0)
    n_in = len(a_list)
    in_specs = [pl.BlockSpec((tm, d), row), st.mod_spec(2)]
    in_specs += [pl.BlockSpec((tm, a.shape[1]), row) for a in a_list]
    in_specs += [pl.BlockSpec(w.shape, const) for w in w_list]
    in_specs += [pl.BlockSpec((1, d), const), st.mod_spec(4), st.mod_spec(3),
                 pl.BlockSpec((d, LANES), const), pl.BlockSpec((LANES, 1), const)]
    m = st.mod_arg(layer)
    if sorted_moe:
        out_specs = [pl.BlockSpec((tm, d), row), pl.BlockSpec((tm, d + LANES), row),
                     pl.BlockSpec((ROUTE_ROWS, tm), lambda i: (0, i))]
        out_shape = [jax.ShapeDtypeStruct((st.tokens, d), F32), jax.ShapeDtypeStruct((st.tokens, d + LANES), F32),
                     jax.ShapeDtypeStruct((ROUTE_ROWS, st.tokens), F32)]
    else:
        out_specs = [pl.BlockSpec((tm, d), row), pl.BlockSpec((tm, d), row), pl.BlockSpec((tm, LANES), row)]
        out_shape = [jax.ShapeDtypeStruct((st.tokens, d), F32), jax.ShapeDtypeStruct((st.tokens, d), BF16),
                     jax.ShapeDtypeStruct((st.tokens, LANES), F32)]
    return pl.pallas_call(
        functools.partial(_mix_kernel, n_in=n_in, sorted_moe=sorted_moe),
        grid=(st.ntiles,),
        in_specs=in_specs,
        out_specs=out_specs,
        out_shape=out_shape,
        compiler_params=_cparams("parallel"),
        name="mix",
    )(x, m, *a_list, *w_list, norm_g.reshape(1, d), m, m, wr_pad, rb_col)


def _gated_up(h, wgu_ref):
    u = _dot(h, wgu_ref[...])
    dff = u.shape[1] // 2
    return _silu(u[:, :dff]) * u[:, dff:]


def _moe_kernel(h_ref, comb_ref, wgu_ref, wd_ref, x1_ref, g2_ref, fg_ref, o_ref, acc_ref, *, final):
    e = pl.program_id(1)

    @pl.when(e == 0)
    def _():
        acc_ref[...] = jnp.zeros_like(acc_ref)

    he = _gated_up(h_ref[...], wgu_ref)
    comb = comb_ref[...]
    lane = lax.broadcasted_iota(jnp.int32, comb.shape, 1)
    ce = jnp.sum(jnp.where(lane == e, comb, 0.0), axis=-1, keepdims=True)
    acc_ref[...] += ce * _dot(he.astype(BF16), wd_ref[...])

    @pl.when(e == pl.num_programs(1) - 1)
    def _():
        x2 = x1_ref[...] + g2_ref[...] * acc_ref[...]
        o_ref[...] = _rms(x2, fg_ref[...]) if final else x2


def _moe(st, layer, h2, comb, wgu, wd, x1, final_g, final):
    d, tm = st.d, st.tm
    ne, dff = wd.shape[1:3]
    row = lambda i, e: (i, 0)
    return pl.pallas_call(
        functools.partial(_moe_kernel, final=final),
        grid=(st.ntiles, ne),
        in_specs=[pl.BlockSpec((tm, d), row), pl.BlockSpec((tm, LANES), row),
                  pl.BlockSpec((None, None, d, 2 * dff), lambda i, e: (layer, e, 0, 0)),
                  pl.BlockSpec((None, None, dff, d), lambda i, e: (layer, e, 0, 0)),
                  pl.BlockSpec((tm, d), row), st.mod_spec(5),
                  pl.BlockSpec((1, d), lambda i, e: (0, 0))],
        out_specs=pl.BlockSpec((tm, d), row),
        out_shape=jax.ShapeDtypeStruct((st.tokens, d), F32),
        scratch_shapes=[pltpu.VMEM((tm, d), F32)],
        compiler_params=_cparams("parallel", "arbitrary"),
        name="moe",
    )(h2, comb, wgu, wd, x1, st.mod_arg(layer), final_g.reshape(1, d))


def _bucket_rank_kernel(rt_ref, rank_ref, cnt_ref, carry_ref):
    i = pl.program_id(0)
    tm = rt_ref.shape[1]

    @pl.when(i == 0)
    def _():
        carry_ref[...] = jnp.zeros_like(carry_ref)

    bucket = rt_ref[0:1, :]
    rows = lax.broadcasted_iota(jnp.int32, (BUCKET_ROWS, tm), 0).astype(F32)
    onehot = rows == bucket
    s_i = lax.broadcasted_iota(jnp.int32, (tm, tm), 0)
    t_i = lax.broadcasted_iota(jnp.int32, (tm, tm), 1)
    incl = jnp.where(s_i <= t_i, 1.0, 0.0).astype(BF16)
    cum = _dot(jnp.where(onehot, 1.0, 0.0).astype(BF16), incl)
    carry = carry_ref[...]
    rank = jnp.sum(jnp.where(onehot, cum - 1.0 + carry, 0.0), axis=0, keepdims=True)
    rank_ref[...] = rank.astype(jnp.int32)
    carry = carry + cum[:, tm - 1:tm]
    carry_ref[...] = carry
    cnt_ref[...] = jnp.broadcast_to(carry, cnt_ref.shape)


def _bucket_rank(st, rt):
    tm = st.tm
    return pl.pallas_call(
        _bucket_rank_kernel,
        grid=(st.ntiles,),
        in_specs=[pl.BlockSpec((ROUTE_ROWS, tm), lambda i: (0, i))],
        out_specs=[pl.BlockSpec((1, tm), lambda i: (0, i)), pl.BlockSpec((BUCKET_ROWS, LANES), lambda i: (0, 0))],
        out_shape=[jax.ShapeDtypeStruct((1, st.tokens), jnp.int32),
                   jax.ShapeDtypeStruct((BUCKET_ROWS, LANES), F32)],
        scratch_shapes=[pltpu.VMEM((BUCKET_ROWS, 1), F32)],
        compiler_params=_cparams("arbitrary"),
        name="bucket_rank",
    )(rt)


def _sort_plan(rt, rank, cnt, tokens):
    ts = SORT_TILE
    bucket = rt[0].astype(jnp.int32)
    counts = cnt[:N_BUCKETS, 0].astype(jnp.int32)
    padded = (counts + ts - 1) // ts * ts
    ends = jnp.cumsum(padded)
    dest = (ends - padded)[bucket] + rank[0]
    n_tiles = -(-tokens // ts) + N_BUCKETS
    n_used = ends[-1] // ts
    tile = jnp.arange(n_tiles, dtype=jnp.int32)
    tile_row = jnp.minimum(tile, n_used - 1) * ts
    tile_bucket = jnp.sum((ends[None, :] <= tile_row[:, None]).astype(jnp.int32), axis=1)
    tile_bucket = jnp.minimum(tile_bucket, N_BUCKETS - 1)
    pairs = [(a, b) for a in range(EXPERTS_PER_GROUP) for b in range(a + 1, EXPERTS_PER_GROUP)]
    lo_tab = jnp.array([a for a, _ in pairs], jnp.int32)
    hi_tab = jnp.array([b for _, b in pairs], jnp.int32)
    base = tile_bucket // PAIRS_PER_GROUP * EXPERTS_PER_GROUP
    pair = tile_bucket % PAIRS_PER_GROUP
    return dest, base + lo_tab[pair], base + hi_tab[pair], n_used.reshape(1).astype(jnp.int32), n_tiles


def _dispatch_kernel(dest_ref, h_ref, init_ref, out_ref, sem):
    del init_ref
    i = pl.program_id(0)
    tm = h_ref.shape[0]

    def body(r, carry):
        pltpu.make_async_copy(h_ref.at[pl.ds(r, 1)], out_ref.at[pl.ds(dest_ref[i * tm + r], 1)], sem).start()
        return carry

    lax.fori_loop(0, tm, body, 0, unroll=8)
    pltpu.make_async_copy(h_ref, out_ref.at[pl.ds(0, tm)], sem).wait()


def _dispatch(st, dest, h2w, n_rows):
    tm = st.tm
    w = h2w.shape[1]
    return pl.pallas_call(
        _dispatch_kernel,
        grid_spec=pltpu.PrefetchScalarGridSpec(
            num_scalar_prefetch=1,
            grid=(st.ntiles,),
            in_specs=[pl.BlockSpec((tm, w), lambda i, d: (i, 0)), pl.BlockSpec(memory_space=pl.ANY)],
            out_specs=pl.BlockSpec(memory_space=pl.ANY),
            scratch_shapes=[pltpu.SemaphoreType.DMA(())]),
        out_shape=jax.ShapeDtypeStruct((n_rows, w), F32),
        input_output_aliases={2: 0},
        compiler_params=_cparams("arbitrary"),
        name="moe_dispatch",
    )(dest, h2w, jnp.zeros((n_rows, w), F32))


def _experts_kernel(elo_ref, ehi_ref, nused_ref, hs_ref, wgul_ref, wdl_ref, wguh_ref, wdh_ref, y_ref):
    del elo_ref, ehi_ref
    i = pl.program_id(0)
    d = y_ref.shape[1]

    @pl.when(i < nused_ref[0])
    def _():
        h = hs_ref[:, :d].astype(BF16)

        def expert(wgu_ref, wd_ref):
            return _dot(_gated_up(h, wgu_ref).astype(BF16), wd_ref[...])

        y_ref[...] = (hs_ref[:, d + 1:d + 2] * expert(wgul_ref, wdl_ref)
                      + hs_ref[:, d + 2:d + 3] * expert(wguh_ref, wdh_ref))

    @pl.when(i >= nused_ref[0])
    def _():
        y_ref[...] = jnp.zeros_like(y_ref)


def _experts(layer, hs, e_lo, e_hi, n_used, n_tiles, wgu, wd):
    ts = SORT_TILE
    _, _, dff, d = wd.shape
    lo = lambda i, elo, ehi, nu: (layer, elo[i], 0, 0)
    hi = lambda i, elo, ehi, nu: (layer, ehi[i], 0, 0)
    up = lambda idx: pl.BlockSpec((None, None, d, 2 * dff), idx)
    down = lambda idx: pl.BlockSpec((None, None, dff, d), idx)
    return pl.pallas_call(
        _experts_kernel,
        grid_spec=pltpu.PrefetchScalarGridSpec(
            num_scalar_prefetch=3,
            grid=(n_tiles,),
            in_specs=[pl.BlockSpec((ts, hs.shape[1]), lambda i, *_: (i, 0)),
                      up(lo), down(lo), up(hi), down(hi)],
            out_specs=pl.BlockSpec((ts, d), lambda i, *_: (i, 0))),
        out_shape=jax.ShapeDtypeStruct((n_tiles * ts, d), F32),
        compiler_params=_cparams("arbitrary"),
        name="moe_experts",
    )(e_lo, e_hi, n_used, hs, wgu, wd, wgu, wd)


def _combine_kernel(dest_ref, x1_ref, g2_ref, fg_ref, ys_ref, o_ref, buf, sem, *, final):
    i = pl.program_id(0)
    n = pl.num_programs(0)
    tm = x1_ref.shape[0]
    slot = i % 2

    def start(tile, sl):
        def body(r, carry):
            pltpu.make_async_copy(ys_ref.at[pl.ds(dest_ref[tile * tm + r], 1)], buf.at[sl, pl.ds(r, 1)],
                                  sem.at[sl]).start()
            return carry

        lax.fori_loop(0, tm, body, 0, unroll=8)

    @pl.when(i == 0)
    def _():
        start(0, 0)

    @pl.when(i + 1 < n)
    def _():
        start(i + 1, 1 - slot)

    pltpu.make_async_copy(ys_ref.at[pl.ds(0, tm)], buf.at[slot], sem.at[slot]).wait()
    x2 = x1_ref[...] + g2_ref[...] * buf[slot]
    o_ref[...] = _rms(x2, fg_ref[...]) if final else x2


def _combine(st, layer, dest, ys, x1, final_g, final):
    d, tm = st.d, st.tm
    row = lambda i, dst: (i, 0)
    return pl.pallas_call(
        functools.partial(_combine_kernel, final=final),
        grid_spec=pltpu.PrefetchScalarGridSpec(
            num_scalar_prefetch=1,
            grid=(st.ntiles,),
            in_specs=[pl.BlockSpec((tm, d), row), st.mod_spec(5), pl.BlockSpec((1, d), lambda i, dst: (0, 0)),
                      pl.BlockSpec(memory_space=pl.ANY)],
            out_specs=pl.BlockSpec((tm, d), row),
            scratch_shapes=[pltpu.VMEM((2, tm, d), F32), pltpu.SemaphoreType.DMA((2,))]),
        out_shape=jax.ShapeDtypeStruct((st.tokens, d), F32),
        compiler_params=_cparams("arbitrary"),
        name="moe_combine",
    )(dest, x1, st.mod_arg(layer), final_g.reshape(1, d), ys)


def _moe_sorted(st, layer, h2w, rt, wgu, wd, x1, final_g, final):
    rank, cnt = _bucket_rank(st, rt)
    dest, e_lo, e_hi, n_used, n_tiles = _sort_plan(rt, rank, cnt, st.tokens)
    hs = _dispatch(st, dest, h2w, n_tiles * SORT_TILE)
    ys = _experts(layer, hs, e_lo, e_hi, n_used, n_tiles, wgu, wd)
    if not final:
        return x1, ys, dest
    return _combine(st, layer, dest, ys, x1, final_g, final)


def _moba_prompt_kernel(q_ref, k_ref, vt_ref, km_ref, o_ref, sel_ref, qs_ref, m_ref, l_ref, acc_ref):
    i = pl.program_id(1)
    qb = MOBA_QTILE
    blk = MOBA_BLOCK
    nblk = km_ref.shape[0]
    own = (i * qb) // blk
    q_off = i * qb - own * blk
    q_t = q_ref[...].T
    km = km_ref[...]
    cols = C_GROUP * qb
    nb_pad = sel_ref.shape[1]
    blk_row = lax.broadcasted_iota(jnp.int32, (nb_pad, cols), 0)
    past = blk_row < own
    key_i = lax.broadcasted_iota(jnp.int32, (blk, cols), 0)
    qry_i = lax.broadcasted_iota(jnp.int32, (blk, cols), 1)
    own_ok = key_i <= (qry_i % qb) + q_off
    own_start = pl.multiple_of(own * blk, blk)
    kv_rows = [slice(kv * HEAD_DIM, (kv + 1) * HEAD_DIM) for kv in range(C_KV_HEADS)]

    state = []
    for kv, rows in enumerate(kv_rows):
        qs_t = jnp.concatenate(
            [q_t[(kv * C_GROUP + g) * HEAD_DIM:(kv * C_GROUP + g + 1) * HEAD_DIM, :] for g in range(C_GROUP)], axis=1)
        kmp = jnp.concatenate([km[:, rows], jnp.zeros((nb_pad - nblk, HEAD_DIM), F32)], axis=0)
        sg = jnp.where(past, _dot_hi(kmp, qs_t), NEG_INF)
        rank = jnp.zeros((nb_pad, cols), jnp.int32)
        for j in range(nblk):
            rj = sg[j:j + 1, :]
            rank = rank + ((rj > sg) | ((rj == sg) & (blk_row > j))).astype(jnp.int32)
        sel = (past & (rank < MOBA_TOPK)).astype(F32)
        qsb = (qs_t * ATT_SCALE).astype(BF16)
        s = _dot(k_ref[pl.ds(own_start, blk), rows], qsb)
        s = jnp.where(own_ok, s, NEG_INF)
        m0 = jnp.max(s, axis=0, keepdims=True)
        p = jnp.exp(s - m0)
        l0 = jnp.sum(p, axis=0, keepdims=True)
        acc0 = _dot(vt_ref[rows, pl.ds(own_start, blk)].astype(BF16), p.astype(BF16))
        state.append((sel, qsb, m0, l0, acc0))
    for kv, (sel, qsb, m0, l0, acc0) in enumerate(state):
        sel_ref[kv], qs_ref[kv], m_ref[kv], l_ref[kv], acc_ref[kv] = sel, qsb, m0, l0, acc0

    def body(j, carry):
        start = pl.multiple_of(j * blk, blk)
        kvs = range(C_KV_HEADS)
        chosen = [sel_ref[kv, pl.ds(j, 1), :] > 0.0 for kv in kvs]
        m_old = [m_ref[kv] for kv in kvs]
        l_old = [l_ref[kv] for kv in kvs]
        acc_old = [acc_ref[kv] for kv in kvs]
        s = [_dot(k_ref[pl.ds(start, blk), kv_rows[kv]], qs_ref[kv]) for kv in kvs]
        s = [jnp.where(chosen[kv], s[kv], NEG_INF) for kv in kvs]
        m_new = [jnp.maximum(m_old[kv], jnp.max(s[kv], axis=0, keepdims=True)) for kv in kvs]
        alpha = [jnp.exp(m_old[kv] - m_new[kv]) for kv in kvs]
        p = [jnp.exp(s[kv] - m_new[kv]) for kv in kvs]
        l_new = [alpha[kv] * l_old[kv] + jnp.sum(p[kv], axis=0, keepdims=True) for kv in kvs]
        pv = [_dot(vt_ref[kv_rows[kv], pl.ds(start, blk)].astype(BF16), p[kv].astype(BF16)) for kv in kvs]
        for kv in kvs:
            m_ref[kv] = m_new[kv]
            l_ref[kv] = l_new[kv]
            acc_ref[kv] = alpha[kv] * acc_old[kv] + pv[kv]
        return carry

    lax.fori_loop(0, own, body, 0)
    o_t = jnp.concatenate([acc_ref[kv] / l_ref[kv] for kv in range(C_KV_HEADS)], axis=0).T
    o_ref[...] = jnp.concatenate(
        [o_t[g * qb:(g + 1) * qb, kv * HEAD_DIM:(kv + 1) * HEAD_DIM]
         for kv in range(C_KV_HEADS) for g in range(C_GROUP)], axis=-1)


def _moba_prompt(q, k_rows, v_t, kmean, nseq, seqlen):
    qb = MOBA_QTILE
    nq = seqlen // qb
    nblk = kmean.shape[1]
    cols = C_GROUP * qb
    per_seq = lambda b, i: (b, 0, 0)
    return pl.pallas_call(
        _moba_prompt_kernel,
        grid=(nseq, nq),
        in_specs=[pl.BlockSpec((qb, C_QW), lambda b, i: (b * nq + i, 0)),
                  pl.BlockSpec((seqlen, C_KVW), lambda b, i: (b, 0)),
                  pl.BlockSpec((None, C_KVW, seqlen), per_seq),
                  pl.BlockSpec((None, nblk, C_KVW), per_seq)],
        out_specs=pl.BlockSpec((qb, C_QW), lambda b, i: (b * nq + i, 0)),
        out_shape=jax.ShapeDtypeStruct(q.shape, F32),
        scratch_shapes=[pltpu.VMEM((C_KV_HEADS, -(-nblk // 16) * 16, cols), F32),
                        pltpu.VMEM((C_KV_HEADS, HEAD_DIM, cols), BF16),
                        pltpu.VMEM((C_KV_HEADS, 1, cols), F32), pltpu.VMEM((C_KV_HEADS, 1, cols), F32),
                        pltpu.VMEM((C_KV_HEADS, HEAD_DIM, cols), F32)],
        compiler_params=_cparams("parallel", "parallel"),
        name="moba_prompt",
    )(q, k_rows, v_t, kmean)


def _swa_sample_kernel(sink_ref, q_ref, kn_ref, vn_ref, wk_ref, wv_ref, o_ref, nk_ref, nv_ref):
    q = q_ref[...]
    kn, vn = kn_ref[...], vn_ref[...]
    wk, wv = wk_ref[...], wv_ref[...]
    tb, w = wk.shape[0], wk.shape[1]
    nk_ref[:, 0:w - 1, :] = wk[:, 1:w, :]
    nk_ref[:, w - 1:w, :] = kn[:, None, :]
    nv_ref[:, 0:w - 1, :] = wv[:, 1:w, :]
    nv_ref[:, w - 1:w, :] = vn[:, None, :]
    lane = lax.broadcasted_iota(jnp.int32, (tb, LANES), 1)
    lane3 = lax.broadcasted_iota(jnp.int32, (tb, w, LANES), 2)
    heads = [None] * A_HEADS
    for g in range(A_GROUP):
        qg = jnp.concatenate(
            [q[:, (kv * A_GROUP + g) * HEAD_DIM:(kv * A_GROUP + g + 1) * HEAD_DIM] for kv in range(A_KV_HEADS)], axis=-1)
        prod = wk * qg[:, None, :]
        prod_n = kn * qg
        p_full = None
        pn_full = None
        for kv in range(A_KV_HEADS):
            lo, hi = kv * HEAD_DIM, (kv + 1) * HEAD_DIM
            sink = sink_ref[kv * A_GROUP + g]
            s = jnp.sum(prod[:, :, lo:hi], axis=-1, keepdims=True) * ATT_SCALE
            sn = jnp.sum(prod_n[:, lo:hi], axis=-1, keepdims=True) * ATT_SCALE
            m = jnp.maximum(jnp.maximum(jnp.max(s, axis=1), sn), sink)
            p = jnp.exp(s - m[:, None, :])
            pn = jnp.exp(sn - m)
            inv = 1.0 / (jnp.sum(p, axis=1) + pn + jnp.exp(sink - m))
            p = p * inv[:, None, :]
            pn = pn * inv
            pb = jnp.broadcast_to(p, (tb, w, LANES))
            pnb = jnp.broadcast_to(pn, (tb, LANES))
            if kv == 0:
                p_full, pn_full = pb, pnb
            else:
                p_full = jnp.where(lane3 >= lo, pb, p_full)
                pn_full = jnp.where(lane >= lo, pnb, pn_full)
        og = jnp.sum(p_full * wv, axis=1) + pn_full * vn
        for kv in range(A_KV_HEADS):
            heads[kv * A_GROUP + g] = og[:, kv * HEAD_DIM:(kv + 1) * HEAD_DIM]
    o_ref[...] = jnp.concatenate(heads, axis=-1)


def _swa_sample(q, kn, vn, win_k, win_v, sinks):
    db, w = win_k.shape[0], win_k.shape[1]
    tb = math.gcd(db, 8)
    row = lambda i: (i, 0)
    row3 = lambda i: (i, 0, 0)
    return pl.pallas_call(
        _swa_sample_kernel,
        grid=(db // tb,),
        in_specs=[pl.BlockSpec(memory_space=pltpu.SMEM),
                  pl.BlockSpec((tb, A_QW), row), pl.BlockSpec((tb, A_KVW), row), pl.BlockSpec((tb, A_KVW), row),
                  pl.BlockSpec((tb, w, A_KVW), row3), pl.BlockSpec((tb, w, A_KVW), row3)],
        out_specs=[pl.BlockSpec((tb, A_QW), row), pl.BlockSpec((tb, w, A_KVW), row3),
                   pl.BlockSpec((tb, w, A_KVW), row3)],
        out_shape=[jax.ShapeDtypeStruct((db, A_QW), F32), jax.ShapeDtypeStruct(win_k.shape, F32),
                   jax.ShapeDtypeStruct(win_v.shape, F32)],
        compiler_params=_cparams("parallel"),
        name="swa_sample",
    )(sinks, q, kn, vn, win_k, win_v)


def _ssd_sample_pre_kernel(xbc_ref, cst_ref, dt_ref, cw_ref, cb_ref, dtb_ref, alog_ref, exp_ref,
                           xs_ref, bm_ref, cm_ref, xd_ref, dec_ref, cnew_ref):
    xn = xbc_ref[...]
    halo = MB_CONV - 1
    acc = cb_ref[...] + xn * cw_ref[halo:halo + 1, :]
    for tap in range(halo):
        acc = acc + cst_ref[tap] * cw_ref[tap:tap + 1, :]
    for tap in range(1, halo):
        cnew_ref[tap - 1] = cst_ref[tap]
    cnew_ref[halo - 1] = xn
    xbc = _silu(acc)
    xs = xbc[:, :MB_INNER]
    xs_ref[...] = xs
    bm_ref[...] = xbc[:, MB_INNER:MB_INNER + MB_GN]
    cm_ref[...] = xbc[:, MB_INNER + MB_GN:]
    dt = _softplus(dt_ref[...] + dtb_ref[...])
    da = dt * (-jnp.exp(alog_ref[...]))
    expand = exp_ref[...]
    xd_ref[...] = xs * _dot_exact_lhs_rhs(dt, expand)
    dec_ref[...] = jnp.exp(_dot_exact_lhs_rhs(da, expand))


def _dot_exact_lhs_rhs(a, b_bf16):
    a0 = a.astype(BF16)
    r = a - a0.astype(F32)
    a1 = r.astype(BF16)
    a2 = (r - a1.astype(F32)).astype(BF16)
    return _dot(a0, b_bf16) + (_dot(a1, b_bf16) + _dot(a2, b_bf16))


def _ssd_sample_state_kernel(h_ref, xd_ref, dec_ref, bm_ref, cm_ref, hn_ref, y_ref, xdt_ref, dect_ref, yt_ref):
    i = pl.program_id(0)
    tb = h_ref.shape[0]
    rpg = MB_HEADS // MB_GROUPS

    @pl.when(i == 0)
    def _():
        xdt_ref[...] = xd_ref[...].T
        dect_ref[...] = dec_ref[...].T
        yt_ref[...] = jnp.zeros_like(yt_ref)

    lane = lax.broadcasted_iota(jnp.int32, yt_ref.shape, 1)
    yt = yt_ref[...]
    for t in range(tb):
        xcol = _select_column(xdt_ref[...], i * tb + t)
        dcol = _select_column(dect_ref[...], i * tb + t)
        ycols = []
        for h in range(MB_HEADS):
            g = h // rpg
            rows = slice(h * MB_HEADDIM, (h + 1) * MB_HEADDIM)
            hn = h_ref[t, h] * dcol[rows, :] + xcol[rows, :] * bm_ref[t, g:g + 1, :]
            hn_ref[t, h] = hn
            ycols.append(jnp.sum(hn * cm_ref[t, g:g + 1, :], axis=-1, keepdims=True))
        yt = jnp.where(lane == i * tb + t, jnp.concatenate(ycols, axis=0), yt)
    yt_ref[...] = yt

    @pl.when(i == pl.num_programs(0) - 1)
    def _():
        y_ref[...] = yt_ref[...].T


def _ssd_sample_post_kernel(y_ref, xs_ref, z_ref, dskip_ref, ng_ref, o_ref):
    y = (y_ref[...] + xs_ref[...] * dskip_ref[...]) * _silu(z_ref[...])
    gw = MB_INNER // MB_GROUPS
    ng = ng_ref[...]
    o_ref[...] = jnp.concatenate(
        [_rms(y[:, g * gw:(g + 1) * gw], ng[:, g * gw:(g + 1) * gw]) for g in range(MB_GROUPS)], axis=-1)


def _ssd_sample(xbc, z, dt, conv_state, h0, conv_w, conv_b, dt_bias, a_log, d_skip, norm_g):
    db = xbc.shape[0]
    halo = MB_CONV - 1
    expand = (jnp.arange(LANES)[:, None] == (jnp.arange(MB_INNER) // MB_HEADDIM)[None, :]).astype(BF16)
    full = lambda *shape: pl.BlockSpec(shape, lambda: (0,) * len(shape))
    xs, bm, cm, xd, dec, conv_new = pl.pallas_call(
        _ssd_sample_pre_kernel,
        in_specs=[full(db, MB_CONV_DIM), full(halo, db, MB_CONV_DIM), full(db, LANES),
                  full(MB_CONV, MB_CONV_DIM), full(1, MB_CONV_DIM), full(1, LANES), full(1, LANES),
                  full(LANES, MB_INNER)],
        out_specs=[full(db, MB_INNER), full(db, MB_GN), full(db, MB_GN), full(db, MB_INNER), full(db, MB_INNER),
                   full(halo, db, MB_CONV_DIM)],
        out_shape=[jax.ShapeDtypeStruct((db, MB_INNER), F32), jax.ShapeDtypeStruct((db, MB_GN), F32),
                   jax.ShapeDtypeStruct((db, MB_GN), F32), jax.ShapeDtypeStruct((db, MB_INNER), F32),
                   jax.ShapeDtypeStruct((db, MB_INNER), F32), jax.ShapeDtypeStruct((halo, db, MB_CONV_DIM), F32)],
        compiler_params=pltpu.CompilerParams(vmem_limit_bytes=VMEM_LIMIT_BYTES),
        name="ssd_sample_pre",
    )(xbc, jnp.swapaxes(conv_state, 0, 1), dt, conv_w, conv_b.reshape(1, -1), _pad_lanes(dt_bias), _pad_lanes(a_log), expand)

    tb = math.gcd(db, 8)
    r3 = lambda i: (i, 0, 0)
    r4 = lambda i: (i, 0, 0, 0)
    whole = pl.BlockSpec((db, MB_INNER), lambda i: (0, 0))
    h_new, y = pl.pallas_call(
        _ssd_sample_state_kernel,
        grid=(db // tb,),
        in_specs=[pl.BlockSpec((tb, MB_HEADS, MB_HEADDIM, MB_DSTATE), r4), whole, whole,
                  pl.BlockSpec((tb, MB_GROUPS, MB_DSTATE), r3), pl.BlockSpec((tb, MB_GROUPS, MB_DSTATE), r3)],
        out_specs=[pl.BlockSpec((tb, MB_HEADS, MB_HEADDIM, MB_DSTATE), r4), whole],
        out_shape=[jax.ShapeDtypeStruct(h0.shape, F32), jax.ShapeDtypeStruct((db, MB_INNER), F32)],
        scratch_shapes=[pltpu.VMEM((MB_INNER, db), F32)] * 3,
        compiler_params=_cparams("arbitrary"),
        name="ssd_sample_state",
    )(h0, xd, dec, bm.reshape(db, MB_GROUPS, MB_DSTATE), cm.reshape(db, MB_GROUPS, MB_DSTATE))

    o_b = pl.pallas_call(
        _ssd_sample_post_kernel,
        in_specs=[full(db, MB_INNER), full(db, MB_INNER), full(db, MB_INNER), full(1, MB_INNER), full(1, MB_INNER)],
        out_specs=full(db, MB_INNER),
        out_shape=jax.ShapeDtypeStruct((db, MB_INNER), F32),
        name="ssd_sample_post",
    )(y, xs, z, jnp.repeat(d_skip, MB_HEADDIM).reshape(1, -1), norm_g.reshape(1, -1))
    return o_b, jnp.swapaxes(conv_new, 0, 1), h_new


def _select_column(x_t, b):
    lane = lax.broadcasted_iota(jnp.int32, x_t.shape, 1)
    return jnp.sum(jnp.where(lane == b, x_t, 0.0), axis=1, keepdims=True)


def _moba_gate_sample_kernel(pt_ref, q_ref, ck_ref, idx_ref, pbuf, sem, qt_ref, km_ref,
                             *, pages_per_block, pages_per_chunk, layer):
    b, c = pl.program_id(0), pl.program_id(1)
    nchunks = pl.num_programs(1)
    t = b * nchunks + c
    total = pl.num_programs(0) * nchunks
    slot = t % 2
    ppb, cpp = pages_per_block, pages_per_chunk
    bpc = cpp // ppb
    nblk = nchunks * bpc

    def copies(bb, cc, sl):
        return [pltpu.make_async_copy(ck_ref.at[layer, pt_ref[bb, cc * cpp + e]], pbuf.at[sl, e], sem.at[sl])
                for e in range(cpp)]

    @pl.when(t == 0)
    def _():
        for cp in copies(0, 0, 0):
            cp.start()
        qt_ref[...] = q_ref[...].T

    @pl.when(t + 1 < total)
    def _():
        for cp in copies((t + 1) // nchunks, (t + 1) % nchunks, 1 - slot):
            cp.start()

    for cp in copies(b, c, slot):
        cp.wait()

    @pl.when(c == 0)
    def _():
        km_ref[...] = jnp.zeros_like(km_ref)

    lane = lax.broadcasted_iota(jnp.int32, km_ref.shape, 1)
    km = km_ref[...]
    for jb in range(bpc):
        pg = pbuf[slot, jb * ppb]
        for e in range(1, ppb):
            pg = pg + pbuf[slot, jb * ppb + e]
        mean = jnp.sum(pg, axis=1, keepdims=True) * (1.0 / MOBA_BLOCK)
        km = jnp.where(lane == c * bpc + jb, mean, km)
    km_ref[...] = km

    @pl.when(c == nchunks - 1)
    def _():
        qcol = _select_column(qt_ref[...], b)
        km = km_ref[...]
        lane1 = lax.broadcasted_iota(jnp.int32, (1, LANES), 1)
        out = jnp.zeros((1, LANES), jnp.int32)
        heads = range(C_HEADS)
        s = [jnp.where(lane1 < nblk,
                       jnp.sum(km[(h // C_GROUP) * HEAD_DIM:(h // C_GROUP + 1) * HEAD_DIM, :]
                               * qcol[h * HEAD_DIM:(h + 1) * HEAD_DIM, :], axis=0, keepdims=True),
                       -jnp.inf) for h in heads]
        for k in range(MOBA_TOPK):
            m = [jnp.max(s[h], axis=1, keepdims=True) for h in heads]
            idx = [jnp.min(jnp.where(s[h] == m[h], lane1, LANES), axis=1, keepdims=True) for h in heads]
            s = [jnp.where(lane1 == idx[h], -jnp.inf, s[h]) for h in heads]
            for h in heads:
                out = jnp.where(lane1 == h * MOBA_TOPK + k, idx[h], out)
        idx_ref[0] = out


def _moba_gate_sample(page_table, q, cache_kt, layer):
    db, n_pages = page_table.shape
    ppb = MOBA_BLOCK // PAGE_SIZE
    cpp = math.gcd(n_pages, 64)
    nblk = n_pages // ppb
    assert cpp % ppb == 0 and MOBA_TOPK <= nblk <= LANES and C_HEADS * MOBA_TOPK <= LANES
    idx = pl.pallas_call(
        functools.partial(_moba_gate_sample_kernel, pages_per_block=ppb, pages_per_chunk=cpp, layer=layer),
        grid_spec=pltpu.PrefetchScalarGridSpec(
            num_scalar_prefetch=1,
            grid=(db, n_pages // cpp),
            in_specs=[pl.BlockSpec((db, C_QW), lambda b, c, pt: (0, 0)), pl.BlockSpec(memory_space=pl.ANY)],
            out_specs=pl.BlockSpec((1, 1, LANES), lambda b, c, pt: (b, 0, 0)),
            scratch_shapes=[pltpu.VMEM((2, cpp, C_KVW, PAGE_SIZE), F32), pltpu.SemaphoreType.DMA((2,)),
                            pltpu.VMEM((C_QW, db), F32), pltpu.VMEM((C_KVW, LANES), F32)]),
        out_shape=jax.ShapeDtypeStruct((db, 1, LANES), jnp.int32),
        compiler_params=_cparams("arbitrary", "arbitrary"),
        name="moba_gate_sample",
    )(page_table, q, cache_kt)
    return idx.reshape(db, LANES)


def _moba_sample_kernel(idx_ref, pt_ref, q_ref, kn_ref, vn_ref, ck_ref, cv_ref, o_ref, kbuf, vbuf, sem,
                        qt_ref, knt_ref, vnt_ref, ot_ref, *, pages_per_block, layer):
    b = pl.program_id(0)
    nb = pl.num_programs(0)
    slot = b % 2
    ppb = pages_per_block
    npg = MOBA_TOPK * ppb

    def copies(bb, sl):
        out = []
        for h in range(C_HEADS):
            rows = pl.ds((h // C_GROUP) * HEAD_DIM, HEAD_DIM)
            for k in range(MOBA_TOPK):
                blk = idx_ref[bb, h * MOBA_TOPK + k]
                for e in range(ppb):
                    page = pt_ref[bb, blk * ppb + e]
                    dst = h * npg + k * ppb + e
                    out.append(pltpu.make_async_copy(ck_ref.at[layer, page, rows, :], kbuf.at[sl, dst], sem.at[sl, 0]))
                    out.append(pltpu.make_async_copy(cv_ref.at[layer, page, rows, :], vbuf.at[sl, dst], sem.at[sl, 1]))
        return out

    @pl.when(b == 0)
    def _():
        for cp in copies(0, 0):
            cp.start()
        qt_ref[...] = q_ref[...].T
        knt_ref[...] = kn_ref[...].T
        vnt_ref[...] = vn_ref[...].T
        ot_ref[...] = jnp.zeros_like(ot_ref)

    @pl.when(b + 1 < nb)
    def _():
        for cp in copies(b + 1, 1 - slot):
            cp.start()

    for cp in copies(b, slot):
        cp.wait()

    qcol = _select_column(qt_ref[...], b) * ATT_SCALE
    kncol = _select_column(knt_ref[...], b)
    vncol = _select_column(vnt_ref[...], b)
    heads = range(C_HEADS)
    hrows = [slice(h * HEAD_DIM, (h + 1) * HEAD_DIM) for h in heads]
    kvrows = [slice((h // C_GROUP) * HEAD_DIM, (h // C_GROUP + 1) * HEAD_DIM) for h in heads]
    s = [[jnp.sum(kbuf[slot, h * npg + pg] * qcol[hrows[h], :], axis=0, keepdims=True) for pg in range(npg)]
         for h in heads]
    sn = [jnp.sum(qcol[hrows[h], :] * kncol[kvrows[h], :], axis=0, keepdims=True) for h in heads]
    m = [jnp.maximum(sn[h], jnp.max(functools.reduce(jnp.maximum, s[h]), axis=1, keepdims=True)) for h in heads]
    p = [[jnp.exp(row - m[h]) for row in s[h]] for h in heads]
    pn = [jnp.exp(sn[h] - m[h]) for h in heads]
    denom = [pn[h] + jnp.sum(functools.reduce(jnp.add, p[h]), axis=1, keepdims=True) for h in heads]
    acc = [functools.reduce(jnp.add, [vbuf[slot, h * npg + pg] * p[h][pg] for pg in range(npg)]) for h in heads]
    ocols = [(jnp.sum(acc[h], axis=1, keepdims=True) + pn[h] * vncol[kvrows[h], :]) / denom[h] for h in heads]
    lane = lax.broadcasted_iota(jnp.int32, ot_ref.shape, 1)
    ot_ref[...] = jnp.where(lane == b, jnp.concatenate(ocols, axis=0), ot_ref[...])

    @pl.when(b == nb - 1)
    def _():
        o_ref[...] = ot_ref[...].T


def _moba_sample(idx, page_table, q, kn, vn, cache_kt, cache_vt, layer):
    db = q.shape[0]
    ppb = MOBA_BLOCK // PAGE_SIZE
    nbuf = C_HEADS * MOBA_TOPK * ppb
    full = lambda *shape: pl.BlockSpec(shape, lambda b, *_: (0,) * len(shape))
    return pl.pallas_call(
        functools.partial(_moba_sample_kernel, pages_per_block=ppb, layer=layer),
        grid_spec=pltpu.PrefetchScalarGridSpec(
            num_scalar_prefetch=2,
            grid=(db,),
            in_specs=[full(db, C_QW), full(db, C_KVW), full(db, C_KVW),
                      pl.BlockSpec(memory_space=pl.ANY), pl.BlockSpec(memory_space=pl.ANY)],
            out_specs=full(db, C_QW),
            scratch_shapes=[pltpu.VMEM((2, nbuf, HEAD_DIM, PAGE_SIZE), F32),
                            pltpu.VMEM((2, nbuf, HEAD_DIM, PAGE_SIZE), F32),
                            pltpu.SemaphoreType.DMA((2, 2)),
                            pltpu.VMEM((C_QW, db), F32), pltpu.VMEM((C_KVW, db), F32),
                            pltpu.VMEM((C_KVW, db), F32), pltpu.VMEM((C_QW, db), F32)]),
        out_shape=jax.ShapeDtypeStruct((db, C_QW), F32),
        compiler_params=_cparams("arbitrary"),
        name="moba_sample",
    )(idx, page_table, q, kn, vn, cache_kt, cache_vt)


def _decoder(st, x, pos, p, state):
    is_prompt = state is None
    nseq, seqlen, d = st.nseq, st.seqlen, st.d
    depth = p['w_ada'].shape[0]
    rope_tabs = _rope_tables(pos if is_prompt else jnp.broadcast_to(pos, (st.tokens,)))
    new = {'win_k': [], 'win_v': [], 'conv': [], 'ssm': [], 'k': [], 'v': []}

    def proj(layer, x, *args):
        outs = _proj(st, layer, x, *args)
        return (outs[:-1], outs[-1]) if isinstance(x, tuple) else (outs, x)

    for layer in range(depth):
        g_norm = p['norm_g'][layer]
        if layer % 2 == 0:
            i = layer // 2
            splits = ((0, A_QW, True, True), (A_QW, A_KVW, True, True), (A_QW + A_KVW, A_KVW, False, True),
                      (A_QW + 2 * A_KVW, MB_INNER, False, True),
                      (A_QW + 2 * A_KVW + MB_INNER, MB_CONV_DIM, False, True),
                      (A_QW + 2 * A_KVW + MB_INNER + MB_CONV_DIM, LANES, False, True))
            (qa, ka, va, z, xbc, dt), x = proj(layer, x, g_norm[0], p['w_in_a'][i], rope_tabs, splits)
            if is_prompt:
                o_a = _swa_prompt(qa, ka, va, p['sinks'][i], nseq, seqlen)
                wk = ka.reshape(nseq, seqlen, A_KV_HEADS, HEAD_DIM)[:, -WINDOW:]
                wv = va.reshape(nseq, seqlen, A_KV_HEADS, HEAD_DIM)[:, -WINDOW:]
                conv_prev = jnp.zeros((nseq, MB_CONV - 1, MB_CONV_DIM), F32)
                h0 = jnp.zeros((nseq, MB_HEADS, MB_HEADDIM, MB_DSTATE), F32)
                o_b, conv_new, h_new = _ssd_prompt(xbc, z, dt, conv_prev, h0, p['conv_w'][i], p['conv_b'][i],
                                                   p['dt_bias'][i], p['a_log'][i], p['d_skip'][i],
                                                   p['ssm_norm_g'][i], nseq, seqlen)
            else:
                o_a, wk, wv = _swa_sample(qa, ka, va, state['win_k'][i].reshape(nseq, WINDOW, A_KVW),
                                          state['win_v'][i].reshape(nseq, WINDOW, A_KVW), p['sinks'][i])
                wk = wk.reshape(nseq, WINDOW, A_KV_HEADS, HEAD_DIM)
                wv = wv.reshape(nseq, WINDOW, A_KV_HEADS, HEAD_DIM)
                o_b, conv_new, h_new = _ssd_sample(xbc, z, dt, state['conv'][i], state['ssm'][i], p['conv_w'][i],
                                                   p['conv_b'][i], p['dt_bias'][i], p['a_log'][i], p['d_skip'][i],
                                                   p['ssm_norm_g'][i])
            a_list = [o_a, o_b]
            w_list = [p['w_out_a'][i][:A_QW], p['w_out_a'][i][A_QW:]]
            new['win_k'].append(wk)
            new['win_v'].append(wv)
            new['conv'].append(conv_new)
            new['ssm'].append(h_new)
        else:
            j = layer // 2
            cols = ((0, C_QW, True), (C_QW, C_KVW, True), (C_QW + C_KVW, C_KVW, False))
            if is_prompt:
                splits = tuple(c + (e,) for c, e in zip(cols, (True, False, False)))
                extras = (('bf16', 1), ('transposed', 1), ('transposed', 2), ('blockmean', 1))
                (qc, k_rows, k_t, v_t, kmean), x = proj(layer, x, g_norm[0], p['w_in_c'][j], rope_tabs, splits, extras)
                o_c = _moba_prompt(qc, k_rows, v_t, kmean.reshape(nseq, seqlen // MOBA_BLOCK, C_KVW), nseq, seqlen)
                kc, vc = (jnp.transpose(t.reshape(nseq, C_KV_HEADS, HEAD_DIM, seqlen), (0, 3, 1, 2)) for t in (k_t, v_t))
            else:
                splits = tuple(c + (True,) for c in cols)
                (qc, kc, vc), x = proj(layer, x, g_norm[0], p['w_in_c'][j], rope_tabs, splits)
                idx = _moba_gate_sample(state['page_table'], qc, state['cache_k'], j)
                o_c = _moba_sample(idx, state['page_table'], qc, kc, vc, state['cache_k'], state['cache_v'], j)
                kc, vc = (t.reshape(nseq, seqlen, C_KV_HEADS, HEAD_DIM) for t in (kc, vc))
            a_list = [o_c]
            w_list = [p['w_out_c'][j]]
            new['k'].append(kc)
            new['v'].append(vc)
        sorted_moe = st.tokens >= 2 * SORT_TILE
        moe = _moe_sorted if sorted_moe else _moe
        x1, h2, aux = _mix(st, layer, x, a_list, w_list, g_norm[1], p['wr_pad'], p['rb_col'], sorted_moe)
        x = moe(st, layer, h2, aux, p['w_gate_up'], p['w_down'], x1, p['final_norm_g'],
                final=(layer == depth - 1))
    return x.reshape(nseq, seqlen, d), {name: jnp.stack(rows) for name, rows in new.items()}


def kernel(x_prompt, x_sample, c_prompt, c_sample, state_win_k, state_win_v, state_conv, state_ssm, cache_k, cache_v, page_table, w_ada, b_ada, norm_g, w_in_a, sinks, conv_w, conv_b, dt_bias, a_log, d_skip, ssm_norm_g, w_out_a, w_in_c, w_out_c, w_router, router_bias, w_gate, w_up, w_down, final_norm_g):
    nb, seqlen, d = x_prompt.shape
    db, dec_seq, _ = x_sample.shape
    assert dec_seq == 1 and seqlen % MOBA_BLOCK == 0 and d % LANES == 0
    n_odd, n_pool, page_size, ckv, hd = cache_k.shape
    assert page_size == PAGE_SIZE and ckv == C_KV_HEADS and hd == HEAD_DIM
    n_pages = page_table.shape[1]

    n_in_a = w_in_a.shape[2]
    pad_a = (-n_in_a) % LANES
    p = {'w_ada': w_ada, 'norm_g': norm_g, 'sinks': sinks, 'conv_w': conv_w, 'conv_b': conv_b, 'dt_bias': dt_bias,
         'a_log': a_log, 'd_skip': d_skip, 'ssm_norm_g': ssm_norm_g, 'final_norm_g': final_norm_g,
         'w_in_a': jnp.pad(w_in_a, ((0, 0), (0, 0), (0, pad_a))).astype(BF16),
         'w_out_a': w_out_a.astype(BF16), 'w_in_c': w_in_c.astype(BF16), 'w_out_c': w_out_c.astype(BF16),
         'w_gate_up': jnp.concatenate([w_gate, w_up], axis=-1).astype(BF16), 'w_down': w_down.astype(BF16),
         'wr_pad': jnp.pad(w_router, ((0, 0), (0, LANES - N_EXPERTS))),
         'rb_col': jnp.pad(router_bias, (0, LANES - N_EXPERTS)).reshape(LANES, 1)}

    mod = _ada(jnp.concatenate([c_prompt, c_sample], axis=0), w_ada, b_ada)
    st_p = _Stream(nb, seqlen, d, mod[:, :nb])
    st_s = _Stream(db, 1, d, mod[:, nb:])

    pos_p = jnp.arange(seqlen, dtype=jnp.int32)
    y_prompt, pn = _decoder(st_p, x_prompt.reshape(nb * seqlen, d), pos_p, p, None)

    cache_kt = jnp.transpose(cache_k, (0, 1, 3, 4, 2)).reshape(n_odd, n_pool, C_KVW, PAGE_SIZE)
    cache_vt = jnp.transpose(cache_v, (0, 1, 3, 4, 2)).reshape(n_odd, n_pool, C_KVW, PAGE_SIZE)
    state = {'win_k': state_win_k, 'win_v': state_win_v, 'conv': state_conv, 'ssm': state_ssm,
             'cache_k': cache_kt, 'cache_v': cache_vt, 'page_table': page_table}
    pos_s = n_pages * PAGE_SIZE + jnp.arange(1, dtype=jnp.int32)
    y_sample, sn = _decoder(st_s, x_sample.reshape(db, d), pos_s, p, state)
    return (y_prompt, y_sample, pn['win_k'], pn['win_v'], pn['conv'], pn['ssm'], pn['k'], pn['v'],
            sn['win_k'], sn['win_v'], sn['conv'], sn['ssm'], sn['k'], sn['v'])
```

```python
import functools
import math

import jax
import jax.numpy as jnp
import numpy as np
from jax import lax
from jax.experimental import pallas as pl
from jax.experimental.pallas import tpu as pltpu

F32 = jnp.float32
BF16 = jnp.bfloat16

HEAD_DIM = 64
ROT_DIM = HEAD_DIM // 4
ROPE_THETA = 500000.0
A_HEADS = 8
A_KV_HEADS = 2
A_GROUP = A_HEADS // A_KV_HEADS
WINDOW = 128
MB_HEADDIM = 64
MB_HEADS = 8
MB_INNER = MB_HEADS * MB_HEADDIM
MB_GROUPS = 2
MB_DSTATE = 128
MB_CONV = 4
MB_GN = MB_GROUPS * MB_DSTATE
MB_CONV_DIM = MB_INNER + 2 * MB_GN
SSD_CHUNK = 128
C_HEADS = 16
C_KV_HEADS = 4
C_GROUP = C_HEADS // C_KV_HEADS
MOBA_BLOCK = 256
MOBA_TOPK = 3
MOBA_QTILE = MOBA_BLOCK
PAGE_SIZE = 128
A_QW = A_HEADS * HEAD_DIM
A_KVW = A_KV_HEADS * HEAD_DIM
C_QW = C_HEADS * HEAD_DIM
C_KVW = C_KV_HEADS * HEAD_DIM
N_EXPERTS = 16
N_EXPERT_GROUPS = 4
EXPERTS_PER_GROUP = N_EXPERTS // N_EXPERT_GROUPS
PAIRS_PER_GROUP = EXPERTS_PER_GROUP * (EXPERTS_PER_GROUP - 1) // 2
N_BUCKETS = N_EXPERT_GROUPS * PAIRS_PER_GROUP
BUCKET_ROWS = 32
ROUTE_ROWS = 8
SORT_TILE = 256
RMS_EPS = 1e-6
NEG_INF = -1e30
ATT_SCALE = HEAD_DIM ** -0.5

LANES = 128
VMEM_LIMIT_BYTES = 56 * 1024 * 1024


def _cparams(*sem):
    return pltpu.CompilerParams(dimension_semantics=sem, vmem_limit_bytes=VMEM_LIMIT_BYTES)


def _dot(a, b):
    return jnp.dot(a, b, preferred_element_type=F32)


def _dot_nt(a, b):
    return lax.dot_general(a, b, (((1,), (1,)), ((), ())), preferred_element_type=F32)


def _dot_tn(a, b):
    return lax.dot_general(a, b, (((0,), (0,)), ((), ())), preferred_element_type=F32)


def _split2(x):
    hi = x.astype(BF16)
    lo = (x - hi.astype(F32)).astype(BF16)
    return hi, lo


def _dot_hi(a, b):
    ah, al = _split2(a)
    bh, bl = _split2(b)
    return _dot(ah, bh) + (_dot(al, bh) + _dot(ah, bl))


def _dot_hi_nt(a, b):
    ah, al = _split2(a)
    bh, bl = _split2(b)
    return _dot_nt(ah, bh) + (_dot_nt(al, bh) + _dot_nt(ah, bl))


def _dot_exact_lhs(a_bf16, b):
    b0 = b.astype(BF16)
    r = b - b0.astype(F32)
    b1 = r.astype(BF16)
    b2 = (r - b1.astype(F32)).astype(BF16)
    return _dot(a_bf16, b0) + (_dot(a_bf16, b1) + _dot(a_bf16, b2))


def _sigmoid(x):
    return 1.0 / (1.0 + jnp.exp(-x))


def _silu(x):
    return x * _sigmoid(x)


def _softplus(x):
    return jnp.maximum(x, 0.0) + jnp.log(1.0 + jnp.exp(-jnp.abs(x)))


def _rms(x, g):
    return x * lax.rsqrt(jnp.mean(x * x, axis=-1, keepdims=True) + RMS_EPS) * g


def _rope_tables(pos):
    half = ROT_DIM // 2
    inv_freq = jnp.power(ROPE_THETA, -jnp.arange(half, dtype=F32) / half)
    ang = pos.astype(F32)[:, None] * inv_freq
    cos, sin = jnp.cos(ang), jnp.sin(ang)
    n = pos.shape[0]
    rest = HEAD_DIM - ROT_DIM
    ct = jnp.concatenate([cos, cos, jnp.ones((n, rest), F32)], axis=1)
    sa = jnp.concatenate([jnp.zeros((n, half), F32), sin, jnp.zeros((n, rest), F32)], axis=1)
    sb = jnp.concatenate([-sin, jnp.zeros((n, half), F32), jnp.zeros((n, rest), F32)], axis=1)
    rep = LANES // HEAD_DIM
    return jnp.tile(ct, (1, rep)), jnp.tile(sa, (1, rep)), jnp.tile(sb, (1, rep))


def _rope(x, ct, sa, sb):
    half = ROT_DIM // 2
    return x * ct + pltpu.roll(x, half, 1) * sa + pltpu.roll(x, LANES - half, 1) * sb


def _ada_kernel(c_ref, w_ref, b_ref, o_ref):
    o_ref[...] = _dot_hi(_silu(c_ref[...]), w_ref[...]) + b_ref[...]


def _ada(c_all, w_ada, b_ada):
    depth, d, n6 = w_ada.shape
    nc = c_all.shape[0]
    nk = n6 // d
    return pl.pallas_call(
        _ada_kernel,
        grid=(depth, nk),
        in_specs=[pl.BlockSpec((nc, d), lambda l, k: (0, 0)),
                  pl.BlockSpec((None, d, d), lambda l, k: (l, 0, k)),
                  pl.BlockSpec((None, 1, d), lambda l, k: (l, 0, k))],
        out_specs=pl.BlockSpec((None, nc, d), lambda l, k: (l, 0, k)),
        out_shape=jax.ShapeDtypeStruct((depth, nc, n6), F32),
        compiler_params=_cparams("parallel", "parallel"),
        name="ada",
    )(c_all, w_ada, b_ada.reshape(depth, 1, n6))


class _Stream:
    def __init__(self, nseq, seqlen, d, mod):
        self.nseq, self.seqlen, self.d = nseq, seqlen, d
        self.tokens = nseq * seqlen
        if seqlen == 1:
            self.tm = self.tokens
            self.tiles_per_seq = None
            self.mod = mod
        else:
            self.tm = math.gcd(seqlen, 512)
            self.tiles_per_seq = seqlen // self.tm
            depth = mod.shape[0]
            self.mod = mod.reshape(depth, nseq * 6, 1, d)
        self.ntiles = self.tokens // self.tm

    def mod_arg(self, layer):
        return self.mod[layer]

    def mod_spec(self, k):
        if self.tiles_per_seq is None:
            return pl.BlockSpec((self.tm, self.d), lambda i, *_: (0, k))
        tps = self.tiles_per_seq
        return pl.BlockSpec((None, 1, self.d), lambda i, *_: ((i // tps) * 6 + k, 0, 0))

    def rope_spec(self):
        if self.tiles_per_seq is None:
            return pl.BlockSpec((self.tm, LANES), lambda i, *_: (0, 0))
        tps = self.tiles_per_seq
        return pl.BlockSpec((self.tm, LANES), lambda i, *_: (i % tps, 0))


def _proj_kernel(x_ref, *refs, splits, extras):
    _proj_compute(x_ref[...], *refs, splits=splits, extras=extras)


def _proj_combine_kernel(dest_ref, x1_ref, g2_ref, ys_ref, *refs, splits, extras):
    *refs, x_out_ref, buf, sem = refs
    i = pl.program_id(0)
    n = pl.num_programs(0)
    tm = x1_ref.shape[0]
    slot = i % 2

    def row_copy(tile, r, sl):
        return pltpu.make_async_copy(ys_ref.at[pl.ds(dest_ref[tile * tm + r], 1)], buf.at[sl, pl.ds(r, 1)], sem.at[sl])

    def tile_wait(sl):
        pltpu.make_async_copy(ys_ref.at[pl.ds(0, tm)], buf.at[sl], sem.at[sl]).wait()

    @pl.when(i == 0)
    def _():
        def body(r, carry):
            row_copy(0, r, 0).start()
            return carry

        lax.fori_loop(0, tm, body, 0, unroll=8)

    tile_wait(slot)
    nxt = jnp.minimum(i + 1, n - 1)
    for r in range(tm):
        row_copy(nxt, r, 1 - slot).start()
    x = x1_ref[...] + g2_ref[...] * buf[slot]
    x_out_ref[...] = x
    _proj_compute(x, *refs, splits=splits, extras=extras)

    @pl.when(i == n - 1)
    def _():
        tile_wait(1 - slot)


def _proj_compute(x, sh_ref, sc_ref, g_ref, w_ref, ct_ref, sa_ref, sb_ref, *out_refs, splits, extras):
    h = _rms(x, g_ref[...]) * (1.0 + sc_ref[...]) + sh_ref[...]
    u = _dot(h.astype(BF16), w_ref[...])
    outs = list(out_refs)
    pieces = []
    for start, width, rope, emit in splits:
        if rope:
            ct, sa, sb = ct_ref[...], sa_ref[...], sb_ref[...]
            piece = jnp.concatenate([_rope(u[:, start + c0:start + c0 + LANES], ct, sa, sb)
                                     for c0 in range(0, width, LANES)], axis=-1)
        else:
            piece = u[:, start:start + width]
        pieces.append(piece)
        if emit:
            outs.pop(0)[...] = piece
    for kind, idx in extras:
        o_ref, piece = outs.pop(0), pieces[idx]
        if kind == 'bf16':
            o_ref[...] = piece.astype(BF16)
        elif kind == 'transposed':
            o_ref[...] = piece.T
        else:
            for c in range(o_ref.shape[0]):
                o_ref[c] = jnp.sum(piece[c * MOBA_BLOCK:(c + 1) * MOBA_BLOCK, :], axis=0,
                                   keepdims=True) * (1.0 / MOBA_BLOCK)


def _proj(st, layer, x, norm_g, w_bf16, rope_tabs, splits, extras=()):
    d = st.d
    n = w_bf16.shape[1]
    tm = st.tm
    row = lambda i, *_: (i, 0)
    const = lambda i, *_: (0, 0)
    out_specs = [pl.BlockSpec((tm, w), row) for _, w, _, emit in splits if emit]
    out_shape = [jax.ShapeDtypeStruct((st.tokens, w), F32) for _, w, _, emit in splits if emit]
    for kind, idx in extras:
        w = splits[idx][1]
        if kind == 'bf16':
            out_specs.append(pl.BlockSpec((tm, w), row))
            out_shape.append(jax.ShapeDtypeStruct((st.tokens, w), BF16))
        elif kind == 'transposed':
            tps = st.tiles_per_seq
            out_specs.append(pl.BlockSpec((None, w, tm), lambda i, *_: (i // tps, 0, i % tps)))
            out_shape.append(jax.ShapeDtypeStruct((st.nseq, w, st.seqlen), F32))
        else:
            nb = tm // MOBA_BLOCK
            out_specs.append(pl.BlockSpec((nb, 1, w), lambda i, *_: (i, 0, 0)))
            out_shape.append(jax.ShapeDtypeStruct((st.tokens // MOBA_BLOCK, 1, w), F32))
    in_specs = [st.mod_spec(0), st.mod_spec(1), pl.BlockSpec((1, d), const), pl.BlockSpec((d, n), const),
                st.rope_spec(), st.rope_spec(), st.rope_spec()]
    args = (st.mod_arg(layer), st.mod_arg(layer), norm_g.reshape(1, d), w_bf16, *rope_tabs)
    if not isinstance(x, tuple):
        return pl.pallas_call(
            functools.partial(_proj_kernel, splits=splits, extras=extras),
            grid=(st.ntiles,),
            in_specs=[pl.BlockSpec((tm, d), row)] + in_specs,
            out_specs=out_specs,
            out_shape=out_shape,
            compiler_params=_cparams("parallel"),
            name="proj",
        )(x, *args)
    x1, ys, dest = x
    return pl.pallas_call(
        functools.partial(_proj_combine_kernel, splits=splits, extras=extras),
        grid_spec=pltpu.PrefetchScalarGridSpec(
            num_scalar_prefetch=1,
            grid=(st.ntiles,),
            in_specs=[pl.BlockSpec((tm, d), row), st.mod_spec(5), pl.BlockSpec(memory_space=pl.ANY)] + in_specs,
            out_specs=out_specs + [pl.BlockSpec((tm, d), row)],
            scratch_shapes=[pltpu.VMEM((2, tm, d), F32), pltpu.SemaphoreType.DMA((2,))]),
        out_shape=out_shape + [jax.ShapeDtypeStruct((st.tokens, d), F32)],
        compiler_params=_cparams("arbitrary"),
        name="proj_combine",
    )(dest, x1, st.mod_arg(layer - 1), ys, *args)


def _swa_prompt_kernel(sink_ref, q_ref, kp_ref, kc_ref, vp_ref, vc_ref, o_ref):
    i = pl.program_id(1)
    qb = q_ref.shape[0]
    cols = A_GROUP * qb
    q_t = q_ref[...].T
    kk = jnp.concatenate([kp_ref[...], kc_ref[...]], axis=0).astype(BF16)
    vv_t = jnp.concatenate([vp_ref[...], vc_ref[...]], axis=0).T.astype(BF16)
    key_i = lax.broadcasted_iota(jnp.int32, (2 * qb, cols), 0)
    qry_i = lax.broadcasted_iota(jnp.int32, (2 * qb, cols), 1) % qb
    diff = qry_i + qb - key_i
    ok = (diff >= 0) & (diff <= WINDOW) & ((key_i >= qb) | (i > 0))
    head_of_lane = lax.broadcasted_iota(jnp.int32, (1, cols), 1) // qb
    outs = []
    for kv in range(A_KV_HEADS):
        rows = slice(kv * HEAD_DIM, (kv + 1) * HEAD_DIM)
        qs_t = jnp.concatenate(
            [q_t[(kv * A_GROUP + g) * HEAD_DIM:(kv * A_GROUP + g + 1) * HEAD_DIM, :] for g in range(A_GROUP)], axis=1)
        s = _dot(kk[:, rows], (qs_t * ATT_SCALE).astype(BF16))
        s = jnp.where(ok, s, NEG_INF)
        sink = jnp.zeros((1, cols), F32)
        for g in range(A_GROUP):
            sink = jnp.where(head_of_lane == g, sink_ref[kv * A_GROUP + g], sink)
        m = jnp.maximum(jnp.max(s, axis=0, keepdims=True), sink)
        p = jnp.exp(s - m)
        denom = jnp.sum(p, axis=0, keepdims=True) + jnp.exp(sink - m)
        outs.append(_dot(vv_t[rows, :], p.astype(BF16)) / denom)
    o_t = jnp.concatenate(outs, axis=0).T
    o_ref[...] = jnp.concatenate(
        [o_t[g * qb:(g + 1) * qb, kv * HEAD_DIM:(kv + 1) * HEAD_DIM]
         for kv in range(A_KV_HEADS) for g in range(A_GROUP)], axis=-1)


def _swa_prompt(q, k, v, sinks, nseq, seqlen):
    qb = WINDOW
    nb = seqlen // qb
    cur = lambda b, i: (b * nb + i, 0)
    prev = lambda b, i: (b * nb + jnp.maximum(i - 1, 0), 0)
    return pl.pallas_call(
        _swa_prompt_kernel,
        grid=(nseq, nb),
        in_specs=[pl.BlockSpec(memory_space=pltpu.SMEM),
                  pl.BlockSpec((qb, A_QW), cur),
                  pl.BlockSpec((qb, A_KVW), prev), pl.BlockSpec((qb, A_KVW), cur),
                  pl.BlockSpec((qb, A_KVW), prev), pl.BlockSpec((qb, A_KVW), cur)],
        out_specs=pl.BlockSpec((qb, A_QW), cur),
        out_shape=jax.ShapeDtypeStruct(q.shape, F32),
        compiler_params=_cparams("parallel", "parallel"),
        name="swa_prompt",
    )(sinks, q, k, k, v, v)


def _ssd_prompt_kernel(xbc_ref, z_ref, dt_ref, cprev_ref, h0_ref, cw_ref, cb_ref, dtb_ref, alog_ref, dskip_ref,
                       ng_ref, o_ref, cnew_ref, hlast_ref, xp_ref, h_ref):
    c = pl.program_id(1)
    nc = pl.num_programs(1)
    q = SSD_CHUNK
    halo = MB_CONV - 1
    base = 8 - halo

    @pl.when(c == 0)
    def _():
        xp_ref[base:8, :] = cprev_ref[...]
        h_ref[...] = h0_ref[...]

    xp_ref[8:8 + q, :] = xbc_ref[...]
    acc = cb_ref[...] + xp_ref[base:base + q, :] * cw_ref[0:1, :]
    for tap in range(1, MB_CONV):
        acc = acc + xp_ref[base + tap:base + tap + q, :] * cw_ref[tap:tap + 1, :]
    tail = xp_ref[8 + q - halo:8 + q, :]
    xp_ref[base:8, :] = tail

    @pl.when(c == nc - 1)
    def _():
        cnew_ref[...] = tail

    xbc = _silu(acc)
    xs = xbc[:, :MB_INNER]
    dt = _softplus(dt_ref[...] + dtb_ref[...])
    da = dt * (-jnp.exp(alog_ref[...]))
    r_i = lax.broadcasted_iota(jnp.int32, (q, q), 0)
    c_i = lax.broadcasted_iota(jnp.int32, (q, q), 1)
    causal = r_i >= c_i
    acum = _dot_exact_lhs(causal.astype(BF16), da)
    acum_t = acum.T
    rpg = MB_HEADS // MB_GROUPS
    ys = []
    h_all = [h_ref[h] for h in range(MB_HEADS)]
    h_new = []
    for g in range(MB_GROUPS):
        bq = xbc[:, MB_INNER + g * MB_DSTATE:MB_INNER + (g + 1) * MB_DSTATE].astype(BF16)
        cq = xbc[:, MB_INNER + MB_GN + g * MB_DSTATE:MB_INNER + MB_GN + (g + 1) * MB_DSTATE].astype(BF16)
        cbm = _dot_nt(cq, bq)
        for r in range(rpg):
            h = g * rpg + r
            a_col = acum[:, h:h + 1]
            a_row = acum_t[h:h + 1, :]
            a_last = acum[q - 1:q, h:h + 1]
            decay = jnp.where(causal, jnp.exp(jnp.where(causal, a_col - a_row, 0.0)), 0.0)
            xd = xs[:, h * MB_HEADDIM:(h + 1) * MB_HEADDIM] * dt[:, h:h + 1]
            hprev = h_all[h]
            y = _dot((cbm * decay).astype(BF16), xd.astype(BF16))
            y = y + jnp.exp(a_col) * _dot_nt(cq, hprev.astype(BF16))
            xw = (xd * jnp.exp(a_last - a_col)).astype(BF16)
            h_new.append(hprev * jnp.exp(a_last) + _dot_tn(xw, bq))
            ys.append(y)
    for h in range(MB_HEADS):
        h_ref[h] = h_new[h]
    y = jnp.concatenate(ys, axis=-1) + xs * dskip_ref[...]
    y = y * _silu(z_ref[...])
    gw = MB_INNER // MB_GROUPS
    ng = ng_ref[...]
    o_ref[...] = jnp.concatenate(
        [_rms(y[:, g * gw:(g + 1) * gw], ng[:, g * gw:(g + 1) * gw]) for g in range(MB_GROUPS)], axis=-1)

    @pl.when(c == nc - 1)
    def _():
        hlast_ref[...] = h_ref[...]


def _pad_lanes(v, n=LANES):
    return jnp.pad(v, (0, n - v.shape[0])).reshape(1, n)


def _ssd_prompt(xbc, z, dt, conv_prev, h0, conv_w, conv_b, dt_bias, a_log, d_skip, norm_g, nseq, seqlen):
    q = SSD_CHUNK
    nc = seqlen // q
    row = lambda b, c: (b * nc + c, 0)
    per_seq3 = lambda b, c: (b, 0, 0)
    per_seq4 = lambda b, c: (b, 0, 0, 0)
    const = lambda b, c: (0, 0)
    halo = MB_CONV - 1
    return pl.pallas_call(
        _ssd_prompt_kernel,
        grid=(nseq, nc),
        in_specs=[pl.BlockSpec((q, MB_CONV_DIM), row), pl.BlockSpec((q, MB_INNER), row),
                  pl.BlockSpec((q, LANES), row),
                  pl.BlockSpec((None, halo, MB_CONV_DIM), per_seq3),
                  pl.BlockSpec((None, MB_HEADS, MB_HEADDIM, MB_DSTATE), per_seq4),
                  pl.BlockSpec((MB_CONV, MB_CONV_DIM), const), pl.BlockSpec((1, MB_CONV_DIM), const),
                  pl.BlockSpec((1, LANES), const), pl.BlockSpec((1, LANES), const),
                  pl.BlockSpec((1, MB_INNER), const), pl.BlockSpec((1, MB_INNER), const)],
        out_specs=[pl.BlockSpec((q, MB_INNER), row),
                   pl.BlockSpec((None, halo, MB_CONV_DIM), per_seq3),
                   pl.BlockSpec((None, MB_HEADS, MB_HEADDIM, MB_DSTATE), per_seq4)],
        out_shape=[jax.ShapeDtypeStruct((nseq * seqlen, MB_INNER), F32),
                   jax.ShapeDtypeStruct((nseq, halo, MB_CONV_DIM), F32),
                   jax.ShapeDtypeStruct((nseq, MB_HEADS, MB_HEADDIM, MB_DSTATE), F32)],
        scratch_shapes=[pltpu.VMEM((8 + q, MB_CONV_DIM), F32),
                        pltpu.VMEM((MB_HEADS, MB_HEADDIM, MB_DSTATE), F32)],
        compiler_params=_cparams("parallel", "arbitrary"),
        name="ssd_prompt",
    )(xbc, z, dt, conv_prev, h0, conv_w, conv_b.reshape(1, -1), _pad_lanes(dt_bias), _pad_lanes(a_log),
      jnp.repeat(d_skip, MB_HEADDIM).reshape(1, -1), norm_g.reshape(1, -1))


def _route(logits_t, rbias_col):
    tm = logits_t.shape[1]
    scores = _sigmoid(logits_t[0:N_EXPERTS, :])
    biased = scores + rbias_col[0:N_EXPERTS, :]
    s = [scores[e:e + 1, :] for e in range(N_EXPERTS)]
    b = [biased[e:e + 1, :] for e in range(N_EXPERTS)]
    epg = EXPERTS_PER_GROUP
    gscore = []
    for g in range(N_EXPERT_GROUPS):
        v = b[g * epg:(g + 1) * epg]
        best = None
        for i in range(epg):
            for j in range(i + 1, epg):
                pair = v[i] + v[j]
                best = pair if best is None else jnp.maximum(best, pair)
        gscore.append(best)
    gsel = jnp.zeros((1, tm), jnp.int32)
    gbest = gscore[0]
    for g in range(1, N_EXPERT_GROUPS):
        better = gscore[g] > gbest
        gsel = jnp.where(better, g, gsel)
        gbest = jnp.where(better, gscore[g], gbest)
    bs, ss = [], []
    for k in range(epg):
        bk, sk = b[k], s[k]
        for g in range(1, N_EXPERT_GROUPS):
            bk = jnp.where(gsel == g, b[g * epg + k], bk)
            sk = jnp.where(gsel == g, s[g * epg + k], sk)
        bs.append(bk)
        ss.append(sk)
    i1 = jnp.zeros((1, tm), jnp.int32)
    m1 = bs[0]
    for k in range(1, epg):
        better = bs[k] > m1
        i1 = jnp.where(better, k, i1)
        m1 = jnp.where(better, bs[k], m1)
    i2 = jnp.full((1, tm), -1, jnp.int32)
    m2 = jnp.full((1, tm), -jnp.inf, F32)
    for k in range(epg):
        better = (i1 != k) & ((bs[k] > m2) | (i2 < 0))
        i2 = jnp.where(better, k, i2)
        m2 = jnp.where(better, bs[k], m2)
    s1 = jnp.zeros((1, tm), F32)
    s2 = jnp.zeros((1, tm), F32)
    for k in range(epg):
        s1 = jnp.where(i1 == k, ss[k], s1)
        s2 = jnp.where(i2 == k, ss[k], s2)
    denom = s1 + s2
    rows = lax.broadcasted_iota(jnp.int32, (N_EXPERTS, tm), 0)
    comb = jnp.zeros((N_EXPERTS, tm), F32)
    for e in range(N_EXPERTS):
        g, k = divmod(e, epg)
        chosen = (gsel == g) & ((i1 == k) | (i2 == k))
        comb = jnp.where(rows == e, jnp.where(chosen, s[e] / denom, 0.0), comb)
    lo = jnp.minimum(i1, i2)
    hi = jnp.maximum(i1, i2)
    pair = jnp.where(lo == 0, 0, jnp.where(lo == 1, epg - 1, 2 * epg - 3)) + (hi - lo - 1)
    bucket = (gsel * PAIRS_PER_GROUP + pair).astype(F32)
    first_is_lo = i1 < i2
    w_lo = jnp.where(first_is_lo, s1, s2) / denom
    w_hi = jnp.where(first_is_lo, s2, s1) / denom
    return comb, bucket, w_lo, w_hi


def _mix_kernel(*refs, n_in, sorted_moe):
    x_ref, g1_ref = refs[0], refs[1]
    a_refs = refs[2:2 + n_in]
    w_refs = refs[2 + n_in:2 + 2 * n_in]
    ng_ref, sc_ref, sh_ref, wr_ref, rb_ref, x1_ref, h2_ref, aux_ref = refs[2 + 2 * n_in:]
    mix = _dot(a_refs[0][...].astype(BF16), w_refs[0][...])
    for a_ref, w_ref in zip(a_refs[1:], w_refs[1:]):
        mix = mix + _dot(a_ref[...].astype(BF16), w_ref[...])
    x1 = x_ref[...] + g1_ref[...] * mix
    x1_ref[...] = x1
    h2 = _rms(x1, ng_ref[...]) * (1.0 + sc_ref[...]) + sh_ref[...]
    h2_hi, h2_lo = _split2(h2)
    w_hi, w_lo = _split2(wr_ref[...])
    both = _dot(h2_hi, jnp.concatenate([w_hi, w_lo], axis=1))
    logits = both[:, :LANES] + (_dot(h2_lo, w_hi) + both[:, LANES:])
    comb_t, bucket, w_lo, w_hi = _route(logits.T, rb_ref[...])
    tm, d = x1.shape
    if sorted_moe:
        r = lax.broadcasted_iota(jnp.int32, (LANES, tm), 0)
        rt = jnp.where(r == 0, bucket, jnp.where(r == 1, w_lo, jnp.where(r == 2, w_hi, 0.0)))
        h2_ref[:, :d] = h2
        h2_ref[:, d:] = rt.T
        aux_ref[...] = rt[0:aux_ref.shape[0], :]
    else:
        h2_ref[...] = h2.astype(BF16)
        aux_ref[...] = jnp.concatenate([comb_t, jnp.zeros((LANES - N_EXPERTS, tm), F32)], axis=0).T


def _mix(st, layer, x, a_list, w_list, norm_g, wr_pad, rb_col, sorted_moe):
    d, tm = st.d, st.tm
    row = lambda i: (i, 0)
    const = lambda i: (0, 0)
    n_in = len(a_list)
    in_specs = [pl.BlockSpec((tm, d), row), st.mod_spec(2)]
    in_specs += [pl.BlockSpec((tm, a.shape[1]), row) for a in a_list]
    in_specs += [pl.BlockSpec(w.shape, const) for w in w_list]
    in_specs += [pl.BlockSpec((1, d), const), st.mod_spec(4), st.mod_spec(3),
                 pl.BlockSpec((d, LANES), const), pl.BlockSpec((LANES, 1), const)]
    m = st.mod_arg(layer)
    if sorted_moe:
        out_specs = [pl.BlockSpec((tm, d), row), pl.BlockSpec((tm, d + LANES), row),
                     pl.BlockSpec((ROUTE_ROWS, tm), lambda i: (0, i))]
        out_shape = [jax.ShapeDtypeStruct((st.tokens, d), F32), jax.ShapeDtypeStruct((st.tokens, d + LANES), F32),
                     jax.ShapeDtypeStruct((ROUTE_ROWS, st.tokens), F32)]
    else:
        out_specs = [pl.BlockSpec((tm, d), row), pl.BlockSpec((tm, d), row), pl.BlockSpec((tm, LANES), row)]
        out_shape = [jax.ShapeDtypeStruct((st.tokens, d), F32), jax.ShapeDtypeStruct((st.tokens, d), BF16),
                     jax.ShapeDtypeStruct((st.tokens, LANES), F32)]
    return pl.pallas_call(
        functools.partial(_mix_kernel, n_in=n_in, sorted_moe=sorted_moe),
        grid=(st.ntiles,),
        in_specs=in_specs,
        out_specs=out_specs,
        out_shape=out_shape,
        compiler_params=_cparams("parallel"),
        name="mix",
    )(x, m, *a_list, *w_list, norm_g.reshape(1, d), m, m, wr_pad, rb_col)


def _gated_up(h, wgu_ref):
    u = _dot(h, wgu_ref[...])
    dff = u.shape[1] // 2
    return _silu(u[:, :dff]) * u[:, dff:]


def _moe_kernel(h_ref, comb_ref, wgu_ref, wd_ref, x1_ref, g2_ref, fg_ref, o_ref, acc_ref, *, final):
    e = pl.program_id(1)

    @pl.when(e == 0)
    def _():
        acc_ref[...] = jnp.zeros_like(acc_ref)

    he = _gated_up(h_ref[...], wgu_ref)
    comb = comb_ref[...]
    lane = lax.broadcasted_iota(jnp.int32, comb.shape, 1)
    ce = jnp.sum(jnp.where(lane == e, comb, 0.0), axis=-1, keepdims=True)
    acc_ref[...] += ce * _dot(he.astype(BF16), wd_ref[...])

    @pl.when(e == pl.num_programs(1) - 1)
    def _():
        x2 = x1_ref[...] + g2_ref[...] * acc_ref[...]
        o_ref[...] = _rms(x2, fg_ref[...]) if final else x2


def _moe(st, layer, h2, comb, wgu, wd, x1, final_g, final):
    d, tm = st.d, st.tm
    ne, dff = wd.shape[1:3]
    row = lambda i, e: (i, 0)
    return pl.pallas_call(
        functools.partial(_moe_kernel, final=final),
        grid=(st.ntiles, ne),
        in_specs=[pl.BlockSpec((tm, d), row), pl.BlockSpec((tm, LANES), row),
                  pl.BlockSpec((None, None, d, 2 * dff), lambda i, e: (layer, e, 0, 0)),
                  pl.BlockSpec((None, None, dff, d), lambda i, e: (layer, e, 0, 0)),
                  pl.BlockSpec((tm, d), row), st.mod_spec(5),
                  pl.BlockSpec((1, d), lambda i, e: (0, 0))],
        out_specs=pl.BlockSpec((tm, d), row),
        out_shape=jax.ShapeDtypeStruct((st.tokens, d), F32),
        scratch_shapes=[pltpu.VMEM((tm, d), F32)],
        compiler_params=_cparams("parallel", "arbitrary"),
        name="moe",
    )(h2, comb, wgu, wd, x1, st.mod_arg(layer), final_g.reshape(1, d))


def _bucket_rank_kernel(rt_ref, rank_ref, cnt_ref, carry_ref):
    i = pl.program_id(0)
    tm = rt_ref.shape[1]

    @pl.when(i == 0)
    def _():
        carry_ref[...] = jnp.zeros_like(carry_ref)

    bucket = rt_ref[0:1, :]
    rows = lax.broadcasted_iota(jnp.int32, (BUCKET_ROWS, tm), 0).astype(F32)
    onehot = rows == bucket
    s_i = lax.broadcasted_iota(jnp.int32, (tm, tm), 0)
    t_i = lax.broadcasted_iota(jnp.int32, (tm, tm), 1)
    incl = jnp.where(s_i <= t_i, 1.0, 0.0).astype(BF16)
    cum = _dot(jnp.where(onehot, 1.0, 0.0).astype(BF16), incl)
    carry = carry_ref[...]
    rank = jnp.sum(jnp.where(onehot, cum - 1.0 + carry, 0.0), axis=0, keepdims=True)
    rank_ref[...] = rank.astype(jnp.int32)
    carry = carry + cum[:, tm - 1:tm]
    carry_ref[...] = carry
    cnt_ref[...] = jnp.broadcast_to(carry, cnt_ref.shape)


def _bucket_rank(st, rt):
    tm = st.tm
    return pl.pallas_call(
        _bucket_rank_kernel,
        grid=(st.ntiles,),
        in_specs=[pl.BlockSpec((ROUTE_ROWS, tm), lambda i: (0, i))],
        out_specs=[pl.BlockSpec((1, tm), lambda i: (0, i)), pl.BlockSpec((BUCKET_ROWS, LANES), lambda i: (0, 0))],
        out_shape=[jax.ShapeDtypeStruct((1, st.tokens), jnp.int32),
                   jax.ShapeDtypeStruct((BUCKET_ROWS, LANES), F32)],
        scratch_shapes=[pltpu.VMEM((BUCKET_ROWS, 1), F32)],
        compiler_params=_cparams("arbitrary"),
        name="bucket_rank",
    )(rt)


def _sort_plan(rt, rank, cnt, tokens):
    ts = SORT_TILE
    bucket = rt[0].astype(jnp.int32)
    counts = cnt[:N_BUCKETS, 0].astype(jnp.int32)
    padded = (counts + ts - 1) // ts * ts
    ends = jnp.cumsum(padded)
    dest = (ends - padded)[bucket] + rank[0]
    n_tiles = -(-tokens // ts) + N_BUCKETS
    n_used = ends[-1] // ts
    tile = jnp.arange(n_tiles, dtype=jnp.int32)
    tile_row = jnp.minimum(tile, n_used - 1) * ts
    tile_bucket = jnp.sum((ends[None, :] <= tile_row[:, None]).astype(jnp.int32), axis=1)
    tile_bucket = jnp.minimum(tile_bucket, N_BUCKETS - 1)
    pairs = [(a, b) for a in range(EXPERTS_PER_GROUP) for b in range(a + 1, EXPERTS_PER_GROUP)]
    lo_tab = jnp.array([a for a, _ in pairs], jnp.int32)
    hi_tab = jnp.array([b for _, b in pairs], jnp.int32)
    base = tile_bucket // PAIRS_PER_GROUP * EXPERTS_PER_GROUP
    pair = tile_bucket % PAIRS_PER_GROUP
    return dest, base + lo_tab[pair], base + hi_tab[pair], n_used.reshape(1).astype(jnp.int32), n_tiles


def _dispatch_kernel(dest_ref, h_ref, init_ref, out_ref, sem):
    del init_ref
    i = pl.program_id(0)
    tm = h_ref.shape[0]

    def body(r, carry):
        pltpu.make_async_copy(h_ref.at[pl.ds(r, 1)], out_ref.at[pl.ds(dest_ref[i * tm + r], 1)], sem).start()
        return carry

    lax.fori_loop(0, tm, body, 0, unroll=8)
    pltpu.make_async_copy(h_ref, out_ref.at[pl.ds(0, tm)], sem).wait()


def _dispatch(st, dest, h2w, n_rows):
    tm = st.tm
    w = h2w.shape[1]
    return pl.pallas_call(
        _dispatch_kernel,
        grid_spec=pltpu.PrefetchScalarGridSpec(
            num_scalar_prefetch=1,
            grid=(st.ntiles,),
            in_specs=[pl.BlockSpec((tm, w), lambda i, d: (i, 0)), pl.BlockSpec(memory_space=pl.ANY)],
            out_specs=pl.BlockSpec(memory_space=pl.ANY),
            scratch_shapes=[pltpu.SemaphoreType.DMA(())]),
        out_shape=jax.ShapeDtypeStruct((n_rows, w), F32),
        input_output_aliases={2: 0},
        compiler_params=_cparams("arbitrary"),
        name="moe_dispatch",
    )(dest, h2w, jnp.zeros((n_rows, w), F32))


def _experts_kernel(elo_ref, ehi_ref, nused_ref, hs_ref, wgul_ref, wdl_ref, wguh_ref, wdh_ref, y_ref):
    del elo_ref, ehi_ref
    i = pl.program_id(0)
    d = y_ref.shape[1]

    @pl.when(i < nused_ref[0])
    def _():
        h = hs_ref[:, :d].astype(BF16)

        def expert(wgu_ref, wd_ref):
            return _dot(_gated_up(h, wgu_ref).astype(BF16), wd_ref[...])

        y_ref[...] = (hs_ref[:, d + 1:d + 2] * expert(wgul_ref, wdl_ref)
                      + hs_ref[:, d + 2:d + 3] * expert(wguh_ref, wdh_ref))

    @pl.when(i >= nused_ref[0])
    def _():
        y_ref[...] = jnp.zeros_like(y_ref)


def _experts(layer, hs, e_lo, e_hi, n_used, n_tiles, wgu, wd):
    ts = SORT_TILE
    _, _, dff, d = wd.shape
    lo = lambda i, elo, ehi, nu: (layer, elo[i], 0, 0)
    hi = lambda i, elo, ehi, nu: (layer, ehi[i], 0, 0)
    up = lambda idx: pl.BlockSpec((None, None, d, 2 * dff), idx)
    down = lambda idx: pl.BlockSpec((None, None, dff, d), idx)
    return pl.pallas_call(
        _experts_kernel,
        grid_spec=pltpu.PrefetchScalarGridSpec(
            num_scalar_prefetch=3,
            grid=(n_tiles,),
            in_specs=[pl.BlockSpec((ts, hs.shape[1]), lambda i, *_: (i, 0)),
                      up(lo), down(lo), up(hi), down(hi)],
            out_specs=pl.BlockSpec((ts, d), lambda i, *_: (i, 0))),
        out_shape=jax.ShapeDtypeStruct((n_tiles * ts, d), F32),
        compiler_params=_cparams("arbitrary"),
        name="moe_experts",
    )(e_lo, e_hi, n_used, hs, wgu, wd, wgu, wd)


def _combine_kernel(dest_ref, x1_ref, g2_ref, fg_ref, ys_ref, o_ref, buf, sem, *, final):
    i = pl.program_id(0)
    n = pl.num_programs(0)
    tm = x1_ref.shape[0]
    slot = i % 2

    def start(tile, sl):
        def body(r, carry):
            pltpu.make_async_copy(ys_ref.at[pl.ds(dest_ref[tile * tm + r], 1)], buf.at[sl, pl.ds(r, 1)],
                                  sem.at[sl]).start()
            return carry

        lax.fori_loop(0, tm, body, 0, unroll=8)

    @pl.when(i == 0)
    def _():
        start(0, 0)

    @pl.when(i + 1 < n)
    def _():
        start(i + 1, 1 - slot)

    pltpu.make_async_copy(ys_ref.at[pl.ds(0, tm)], buf.at[slot], sem.at[slot]).wait()
    x2 = x1_ref[...] + g2_ref[...] * buf[slot]
    o_ref[...] = _rms(x2, fg_ref[...]) if final else x2


def _combine(st, layer, dest, ys, x1, final_g, final):
    d, tm = st.d, st.tm
    row = lambda i, dst: (i, 0)
    return pl.pallas_call(
        functools.partial(_combine_kernel, final=final),
        grid_spec=pltpu.PrefetchScalarGridSpec(
            num_scalar_prefetch=1,
            grid=(st.ntiles,),
            in_specs=[pl.BlockSpec((tm, d), row), st.mod_spec(5), pl.BlockSpec((1, d), lambda i, dst: (0, 0)),
                      pl.BlockSpec(memory_space=pl.ANY)],
            out_specs=pl.BlockSpec((tm, d), row),
            scratch_shapes=[pltpu.VMEM((2, tm, d), F32), pltpu.SemaphoreType.DMA((2,))]),
        out_shape=jax.ShapeDtypeStruct((st.tokens, d), F32),
        compiler_params=_cparams("arbitrary"),
        name="moe_combine",
    )(dest, x1, st.mod_arg(layer), final_g.reshape(1, d), ys)


def _moe_sorted(st, layer, h2w, rt, wgu, wd, x1, final_g, final):
    rank, cnt = _bucket_rank(st, rt)
    dest, e_lo, e_hi, n_used, n_tiles = _sort_plan(rt, rank, cnt, st.tokens)
    hs = _dispatch(st, dest, h2w, n_tiles * SORT_TILE)
    ys = _experts(layer, hs, e_lo, e_hi, n_used, n_tiles, wgu, wd)
    if not final:
        return x1, ys, dest
    return _combine(st, layer, dest, ys, x1, final_g, final)


def _moba_prompt_kernel(q_ref, k_ref, vt_ref, km_ref, o_ref, sel_ref, qs_ref, m_ref, l_ref, acc_ref):
    i = pl.program_id(1)
    qb = MOBA_QTILE
    blk = MOBA_BLOCK
    nblk = km_ref.shape[0]
    own = (i * qb) // blk
    q_off = i * qb - own * blk
    q_t = q_ref[...].T
    km = km_ref[...]
    cols = C_GROUP * qb
    nb_pad = sel_ref.shape[1]
    blk_row = lax.broadcasted_iota(jnp.int32, (nb_pad, cols), 0)
    past = blk_row < own
    key_i = lax.broadcasted_iota(jnp.int32, (blk, cols), 0)
    qry_i = lax.broadcasted_iota(jnp.int32, (blk, cols), 1)
    own_ok = key_i <= (qry_i % qb) + q_off
    own_start = pl.multiple_of(own * blk, blk)
    kv_rows = [slice(kv * HEAD_DIM, (kv + 1) * HEAD_DIM) for kv in range(C_KV_HEADS)]

    state = []
    for kv, rows in enumerate(kv_rows):
        qs_t = jnp.concatenate(
            [q_t[(kv * C_GROUP + g) * HEAD_DIM:(kv * C_GROUP + g + 1) * HEAD_DIM, :] for g in range(C_GROUP)], axis=1)
        kmp = jnp.concatenate([km[:, rows], jnp.zeros((nb_pad - nblk, HEAD_DIM), F32)], axis=0)
        sg = jnp.where(past, _dot_hi(kmp, qs_t), NEG_INF)
        rank = jnp.zeros((nb_pad, cols), jnp.int32)
        for j in range(nblk):
            rj = sg[j:j + 1, :]
            rank = rank + ((rj > sg) | ((rj == sg) & (blk_row > j))).astype(jnp.int32)
        sel = (past & (rank < MOBA_TOPK)).astype(F32)
        qsb = (qs_t * ATT_SCALE).astype(BF16)
        s = _dot(k_ref[pl.ds(own_start, blk), rows], qsb)
        s = jnp.where(own_ok, s, NEG_INF)
        m0 = jnp.max(s, axis=0, keepdims=True)
        p = jnp.exp(s - m0)
        l0 = jnp.sum(p, axis=0, keepdims=True)
        acc0 = _dot(vt_ref[rows, pl.ds(own_start, blk)].astype(BF16), p.astype(BF16))
        state.append((sel, qsb, m0, l0, acc0))
    for kv, (sel, qsb, m0, l0, acc0) in enumerate(state):
        sel_ref[kv], qs_ref[kv], m_ref[kv], l_ref[kv], acc_ref[kv] = sel, qsb, m0, l0, acc0

    def body(j, carry):
        start = pl.multiple_of(j * blk, blk)
        kvs = range(C_KV_HEADS)
        chosen = [sel_ref[kv, pl.ds(j, 1), :] > 0.0 for kv in kvs]
        m_old = [m_ref[kv] for kv in kvs]
        l_old = [l_ref[kv] for kv in kvs]
        acc_old = [acc_ref[kv] for kv in kvs]
        s = [_dot(k_ref[pl.ds(start, blk), kv_rows[kv]], qs_ref[kv]) for kv in kvs]
        s = [jnp.where(chosen[kv], s[kv], NEG_INF) for kv in kvs]
        m_new = [jnp.maximum(m_old[kv], jnp.max(s[kv], axis=0, keepdims=True)) for kv in kvs]
        alpha = [jnp.exp(m_old[kv] - m_new[kv]) for kv in kvs]
        p = [jnp.exp(s[kv] - m_new[kv]) for kv in kvs]
        l_new = [alpha[kv] * l_old[kv] + jnp.sum(p[kv], axis=0, keepdims=True) for kv in kvs]
        pv = [_dot(vt_ref[kv_rows[kv], pl.ds(start, blk)].astype(BF16), p[kv].astype(BF16)) for kv in kvs]
        for kv in kvs:
            m_ref[kv] = m_new[kv]
            l_ref[kv] = l_new[kv]
            acc_ref[kv] = alpha[kv] * acc_old[kv] + pv[kv]
        return carry

    lax.fori_loop(0, own, body, 0)
    o_t = jnp.concatenate([acc_ref[kv] / l_ref[kv] for kv in range(C_KV_HEADS)], axis=0).T
    o_ref[...] = jnp.concatenate(
        [o_t[g * qb:(g + 1) * qb, kv * HEAD_DIM:(kv + 1) * HEAD_DIM]
         for kv in range(C_KV_HEADS) for g in range(C_GROUP)], axis=-1)


def _moba_prompt(q, k_rows, v_t, kmean, nseq, seqlen):
    qb = MOBA_QTILE
    nq = seqlen // qb
    nblk = kmean.shape[1]
    cols = C_GROUP * qb
    per_seq = lambda b, i: (b, 0, 0)
    return pl.pallas_call(
        _moba_prompt_kernel,
        grid=(nseq, nq),
        in_specs=[pl.BlockSpec((qb, C_QW), lambda b, i: (b * nq + i, 0)),
                  pl.BlockSpec((seqlen, C_KVW), lambda b, i: (b, 0)),
                  pl.BlockSpec((None, C_KVW, seqlen), per_seq),
                  pl.BlockSpec((None, nblk, C_KVW), per_seq)],
        out_specs=pl.BlockSpec((qb, C_QW), lambda b, i: (b * nq + i, 0)),
        out_shape=jax.ShapeDtypeStruct(q.shape, F32),
        scratch_shapes=[pltpu.VMEM((C_KV_HEADS, -(-nblk // 16) * 16, cols), F32),
                        pltpu.VMEM((C_KV_HEADS, HEAD_DIM, cols), BF16),
                        pltpu.VMEM((C_KV_HEADS, 1, cols), F32), pltpu.VMEM((C_KV_HEADS, 1, cols), F32),
                        pltpu.VMEM((C_KV_HEADS, HEAD_DIM, cols), F32)],
        compiler_params=_cparams("parallel", "parallel"),
        name="moba_prompt",
    )(q, k_rows, v_t, kmean)


def _swa_sample_kernel(sink_ref, q_ref, kn_ref, vn_ref, wk_ref, wv_ref, o_ref, nk_ref, nv_ref):
    q = q_ref[...]
    kn, vn = kn_ref[...], vn_ref[...]
    wk, wv = wk_ref[...], wv_ref[...]
    tb, w = wk.shape[0], wk.shape[1]
    nk_ref[:, 0:w - 1, :] = wk[:, 1:w, :]
    nk_ref[:, w - 1:w, :] = kn[:, None, :]
    nv_ref[:, 0:w - 1, :] = wv[:, 1:w, :]
    nv_ref[:, w - 1:w, :] = vn[:, None, :]
    lane = lax.broadcasted_iota(jnp.int32, (tb, LANES), 1)
    lane3 = lax.broadcasted_iota(jnp.int32, (tb, w, LANES), 2)
    heads = [None] * A_HEADS
    for g in range(A_GROUP):
        qg = jnp.concatenate(
            [q[:, (kv * A_GROUP + g) * HEAD_DIM:(kv * A_GROUP + g + 1) * HEAD_DIM] for kv in range(A_KV_HEADS)], axis=-1)
        prod = wk * qg[:, None, :]
        prod_n = kn * qg
        p_full = None
        pn_full = None
        for kv in range(A_KV_HEADS):
            lo, hi = kv * HEAD_DIM, (kv + 1) * HEAD_DIM
            sink = sink_ref[kv * A_GROUP + g]
            s = jnp.sum(prod[:, :, lo:hi], axis=-1, keepdims=True) * ATT_SCALE
            sn = jnp.sum(prod_n[:, lo:hi], axis=-1, keepdims=True) * ATT_SCALE
            m = jnp.maximum(jnp.maximum(jnp.max(s, axis=1), sn), sink)
            p = jnp.exp(s - m[:, None, :])
            pn = jnp.exp(sn - m)
            inv = 1.0 / (jnp.sum(p, axis=1) + pn + jnp.exp(sink - m))
            p = p * inv[:, None, :]
            pn = pn * inv
            pb = jnp.broadcast_to(p, (tb, w, LANES))
            pnb = jnp.broadcast_to(pn, (tb, LANES))
            if kv == 0:
                p_full, pn_full = pb, pnb
            else:
                p_full = jnp.where(lane3 >= lo, pb, p_full)
                pn_full = jnp.where(lane >= lo, pnb, pn_full)
        og = jnp.sum(p_full * wv, axis=1) + pn_full * vn
        for kv in range(A_KV_HEADS):
            heads[kv * A_GROUP + g] = og[:, kv * HEAD_DIM:(kv + 1) * HEAD_DIM]
    o_ref[...] = jnp.concatenate(heads, axis=-1)


def _swa_sample(q, kn, vn, win_k, win_v, sinks):
    db, w = win_k.shape[0], win_k.shape[1]
    tb = math.gcd(db, 8)
    row = lambda i: (i, 0)
    row3 = lambda i: (i, 0, 0)
    return pl.pallas_call(
        _swa_sample_kernel,
        grid=(db // tb,),
        in_specs=[pl.BlockSpec(memory_space=pltpu.SMEM),
                  pl.BlockSpec((tb, A_QW), row), pl.BlockSpec((tb, A_KVW), row), pl.BlockSpec((tb, A_KVW), row),
                  pl.BlockSpec((tb, w, A_KVW), row3), pl.BlockSpec((tb, w, A_KVW), row3)],
        out_specs=[pl.BlockSpec((tb, A_QW), row), pl.BlockSpec((tb, w, A_KVW), row3),
                   pl.BlockSpec((tb, w, A_KVW), row3)],
        out_shape=[jax.ShapeDtypeStruct((db, A_QW), F32), jax.ShapeDtypeStruct(win_k.shape, F32),
                   jax.ShapeDtypeStruct(win_v.shape, F32)],
        compiler_params=_cparams("parallel"),
        name="swa_sample",
    )(sinks, q, kn, vn, win_k, win_v)


def _ssd_sample_pre_kernel(xbc_ref, cst_ref, dt_ref, cw_ref, cb_ref, dtb_ref, alog_ref, exp_ref,
                           xs_ref, bm_ref, cm_ref, xd_ref, dec_ref, cnew_ref):
    xn = xbc_ref[...]
    halo = MB_CONV - 1
    acc = cb_ref[...] + xn * cw_ref[halo:halo + 1, :]
    for tap in range(halo):
        acc = acc + cst_ref[tap] * cw_ref[tap:tap + 1, :]
    for tap in range(1, halo):
        cnew_ref[tap - 1] = cst_ref[tap]
    cnew_ref[halo - 1] = xn
    xbc = _silu(acc)
    xs = xbc[:, :MB_INNER]
    xs_ref[...] = xs
    bm_ref[...] = xbc[:, MB_INNER:MB_INNER + MB_GN]
    cm_ref[...] = xbc[:, MB_INNER + MB_GN:]
    dt = _softplus(dt_ref[...] + dtb_ref[...])
    da = dt * (-jnp.exp(alog_ref[...]))
    expand = exp_ref[...]
    xd_ref[...] = xs * _dot_exact_lhs_rhs(dt, expand)
    dec_ref[...] = jnp.exp(_dot_exact_lhs_rhs(da, expand))


def _dot_exact_lhs_rhs(a, b_bf16):
    a0 = a.astype(BF16)
    r = a - a0.astype(F32)
    a1 = r.astype(BF16)
    a2 = (r - a1.astype(F32)).astype(BF16)
    return _dot(a0, b_bf16) + (_dot(a1, b_bf16) + _dot(a2, b_bf16))


def _ssd_sample_state_kernel(h_ref, xd_ref, dec_ref, bm_ref, cm_ref, hn_ref, y_ref, xdt_ref, dect_ref, yt_ref):
    i = pl.program_id(0)
    tb = h_ref.shape[0]
    rpg = MB_HEADS // MB_GROUPS

    @pl.when(i == 0)
    def _():
        xdt_ref[...] = xd_ref[...].T
        dect_ref[...] = dec_ref[...].T
        yt_ref[...] = jnp.zeros_like(yt_ref)

    lane = lax.broadcasted_iota(jnp.int32, yt_ref.shape, 1)
    yt = yt_ref[...]
    for t in range(tb):
        xcol = _select_column(xdt_ref[...], i * tb + t)
        dcol = _select_column(dect_ref[...], i * tb + t)
        ycols = []
        for h in range(MB_HEADS):
            g = h // rpg
            rows = slice(h * MB_HEADDIM, (h + 1) * MB_HEADDIM)
            hn = h_ref[t, h] * dcol[rows, :] + xcol[rows, :] * bm_ref[t, g:g + 1, :]
            hn_ref[t, h] = hn
            ycols.append(jnp.sum(hn * cm_ref[t, g:g + 1, :], axis=-1, keepdims=True))
        yt = jnp.where(lane == i * tb + t, jnp.concatenate(ycols, axis=0), yt)
    yt_ref[...] = yt

    @pl.when(i == pl.num_programs(0) - 1)
    def _():
        y_ref[...] = yt_ref[...].T


def _ssd_sample_post_kernel(y_ref, xs_ref, z_ref, dskip_ref, ng_ref, o_ref):
    y = (y_ref[...] + xs_ref[...] * dskip_ref[...]) * _silu(z_ref[...])
    gw = MB_INNER // MB_GROUPS
    ng = ng_ref[...]
    o_ref[...] = jnp.concatenate(
        [_rms(y[:, g * gw:(g + 1) * gw], ng[:, g * gw:(g + 1) * gw]) for g in range(MB_GROUPS)], axis=-1)


def _ssd_sample(xbc, z, dt, conv_state, h0, conv_w, conv_b, dt_bias, a_log, d_skip, norm_g):
    db = xbc.shape[0]
    halo = MB_CONV - 1
    expand = (jnp.arange(LANES)[:, None] == (jnp.arange(MB_INNER) // MB_HEADDIM)[None, :]).astype(BF16)
    full = lambda *shape: pl.BlockSpec(shape, lambda: (0,) * len(shape))
    xs, bm, cm, xd, dec, conv_new = pl.pallas_call(
        _ssd_sample_pre_kernel,
        in_specs=[full(db, MB_CONV_DIM), full(halo, db, MB_CONV_DIM), full(db, LANES),
                  full(MB_CONV, MB_CONV_DIM), full(1, MB_CONV_DIM), full(1, LANES), full(1, LANES),
                  full(LANES, MB_INNER)],
        out_specs=[full(db, MB_INNER), full(db, MB_GN), full(db, MB_GN), full(db, MB_INNER), full(db, MB_INNER),
                   full(halo, db, MB_CONV_DIM)],
        out_shape=[jax.ShapeDtypeStruct((db, MB_INNER), F32), jax.ShapeDtypeStruct((db, MB_GN), F32),
                   jax.ShapeDtypeStruct((db, MB_GN), F32), jax.ShapeDtypeStruct((db, MB_INNER), F32),
                   jax.ShapeDtypeStruct((db, MB_INNER), F32), jax.ShapeDtypeStruct((halo, db, MB_CONV_DIM), F32)],
        compiler_params=pltpu.CompilerParams(vmem_limit_bytes=VMEM_LIMIT_BYTES),
        name="ssd_sample_pre",
    )(xbc, jnp.swapaxes(conv_state, 0, 1), dt, conv_w, conv_b.reshape(1, -1), _pad_lanes(dt_bias), _pad_lanes(a_log), expand)

    tb = math.gcd(db, 8)
    r3 = lambda i: (i, 0, 0)
    r4 = lambda i: (i, 0, 0, 0)
    whole = pl.BlockSpec((db, MB_INNER), lambda i: (0, 0))
    h_new, y = pl.pallas_call(
        _ssd_sample_state_kernel,
        grid=(db // tb,),
        in_specs=[pl.BlockSpec((tb, MB_HEADS, MB_HEADDIM, MB_DSTATE), r4), whole, whole,
                  pl.BlockSpec((tb, MB_GROUPS, MB_DSTATE), r3), pl.BlockSpec((tb, MB_GROUPS, MB_DSTATE), r3)],
        out_specs=[pl.BlockSpec((tb, MB_HEADS, MB_HEADDIM, MB_DSTATE), r4), whole],
        out_shape=[jax.ShapeDtypeStruct(h0.shape, F32), jax.ShapeDtypeStruct((db, MB_INNER), F32)],
        scratch_shapes=[pltpu.VMEM((MB_INNER, db), F32)] * 3,
        compiler_params=_cparams("arbitrary"),
        name="ssd_sample_state",
    )(h0, xd, dec, bm.reshape(db, MB_GROUPS, MB_DSTATE), cm.reshape(db, MB_GROUPS, MB_DSTATE))

    o_b = pl.pallas_call(
        _ssd_sample_post_kernel,
        in_specs=[full(db, MB_INNER), full(db, MB_INNER), full(db, MB_INNER), full(1, MB_INNER), full(1, MB_INNER)],
        out_specs=full(db, MB_INNER),
        out_shape=jax.ShapeDtypeStruct((db, MB_INNER), F32),
        name="ssd_sample_post",
    )(y, xs, z, jnp.repeat(d_skip, MB_HEADDIM).reshape(1, -1), norm_g.reshape(1, -1))
    return o_b, jnp.swapaxes(conv_new, 0, 1), h_new


def _select_column(x_t, b):
    lane = lax.broadcasted_iota(jnp.int32, x_t.shape, 1)
    return jnp.sum(jnp.where(lane == b, x_t, 0.0), axis=1, keepdims=True)


def _moba_gate_sample_kernel(pt_ref, q_ref, ck_ref, idx_ref, pbuf, sem, qt_ref, km_ref,
                             *, pages_per_block, pages_per_chunk, layer):
    b, c = pl.program_id(0), pl.program_id(1)
    nchunks = pl.num_programs(1)
    t = b * nchunks + c
    total = pl.num_programs(0) * nchunks
    nbuf = pbuf.shape[0]
    slot = t % nbuf
    ppb, cpp = pages_per_block, pages_per_chunk
    bpc = cpp // ppb
    nblk = nchunks * bpc

    def copies(tt):
        bb, cc, sl = tt // nchunks, tt % nchunks, tt % nbuf
        return [pltpu.make_async_copy(ck_ref.at[layer, pt_ref[bb, cc * cpp + e]], pbuf.at[sl, e], sem.at[sl])
                for e in range(cpp)]

    @pl.when(t == 0)
    def _():
        qt_ref[...] = q_ref[...].T

    for ahead in range(nbuf - 1):
        @pl.when((t == 0) & (ahead < total))
        def _(ahead=ahead):
            for cp in copies(ahead):
                cp.start()

    @pl.when(t + nbuf - 1 < total)
    def _():
        for cp in copies(t + nbuf - 1):
            cp.start()

    for cp in copies(t):
        cp.wait()

    @pl.when(c == 0)
    def _():
        km_ref[...] = jnp.zeros_like(km_ref)

    lane = lax.broadcasted_iota(jnp.int32, km_ref.shape, 1)
    km = km_ref[...]
    for jb in range(bpc):
        pg = pbuf[slot, jb * ppb]
        for e in range(1, ppb):
            pg = pg + pbuf[slot, jb * ppb + e]
        mean = jnp.sum(pg, axis=1, keepdims=True) * (1.0 / MOBA_BLOCK)
        km = jnp.where(lane == c * bpc + jb, mean, km)
    km_ref[...] = km

    @pl.when(c == nchunks - 1)
    def _():
        qcol = _select_column(qt_ref[...], b)
        km = km_ref[...]
        lane1 = lax.broadcasted_iota(jnp.int32, (1, LANES), 1)
        out = jnp.zeros((1, LANES), jnp.int32)
        heads = range(C_HEADS)
        s = [jnp.where(lane1 < nblk,
                       jnp.sum(km[(h // C_GROUP) * HEAD_DIM:(h // C_GROUP + 1) * HEAD_DIM, :]
                               * qcol[h * HEAD_DIM:(h + 1) * HEAD_DIM, :], axis=0, keepdims=True),
                       -jnp.inf) for h in heads]
        for k in range(MOBA_TOPK):
            m = [jnp.max(s[h], axis=1, keepdims=True) for h in heads]
            idx = [jnp.min(jnp.where(s[h] == m[h], lane1, LANES), axis=1, keepdims=True) for h in heads]
            s = [jnp.where(lane1 == idx[h], -jnp.inf, s[h]) for h in heads]
            for h in heads:
                out = jnp.where(lane1 == h * MOBA_TOPK + k, idx[h], out)
        idx_ref[0] = out


def _moba_gate_sample(page_table, q, cache_kt, layer):
    db, n_pages = page_table.shape
    ppb = MOBA_BLOCK // PAGE_SIZE
    cpp = math.gcd(n_pages, 64)
    nblk = n_pages // ppb
    assert cpp % ppb == 0 and MOBA_TOPK <= nblk <= LANES and C_HEADS * MOBA_TOPK <= LANES
    idx = pl.pallas_call(
        functools.partial(_moba_gate_sample_kernel, pages_per_block=ppb, pages_per_chunk=cpp, layer=layer),
        grid_spec=pltpu.PrefetchScalarGridSpec(
            num_scalar_prefetch=1,
            grid=(db, n_pages // cpp),
            in_specs=[pl.BlockSpec((db, C_QW), lambda b, c, pt: (0, 0)), pl.BlockSpec(memory_space=pl.ANY)],
            out_specs=pl.BlockSpec((1, 1, LANES), lambda b, c, pt: (b, 0, 0)),
            scratch_shapes=[pltpu.VMEM((3, cpp, C_KVW, PAGE_SIZE), F32), pltpu.SemaphoreType.DMA((3,)),
                            pltpu.VMEM((C_QW, db), F32), pltpu.VMEM((C_KVW, LANES), F32)]),
        out_shape=jax.ShapeDtypeStruct((db, 1, LANES), jnp.int32),
        compiler_params=_cparams("arbitrary", "arbitrary"),
        name="moba_gate_sample",
    )(page_table, q, cache_kt)
    return idx.reshape(db, LANES)


def _moba_sample_kernel(idx_ref, pt_ref, q_ref, kn_ref, vn_ref, ck_ref, cv_ref, o_ref, kbuf, vbuf, sem,
                        qt_ref, knt_ref, vnt_ref, ot_ref, *, pages_per_block, layer):
    b = pl.program_id(0)
    nb = pl.num_programs(0)
    slot = b % 2
    ppb = pages_per_block
    npg = MOBA_TOPK * ppb

    def copies(bb, sl):
        out = []
        for h in range(C_HEADS):
            rows = pl.ds((h // C_GROUP) * HEAD_DIM, HEAD_DIM)
            for k in range(MOBA_TOPK):
                blk = idx_ref[bb, h * MOBA_TOPK + k]
                for e in range(ppb):
                    page = pt_ref[bb, blk * ppb + e]
                    dst = h * npg + k * ppb + e
                    out.append(pltpu.make_async_copy(ck_ref.at[layer, page, rows, :], kbuf.at[sl, dst], sem.at[sl, 0]))
                    out.append(pltpu.make_async_copy(cv_ref.at[layer, page, rows, :], vbuf.at[sl, dst], sem.at[sl, 1]))
        return out

    @pl.when(b == 0)
    def _():
        for cp in copies(0, 0):
            cp.start()
        qt_ref[...] = q_ref[...].T
        knt_ref[...] = kn_ref[...].T
        vnt_ref[...] = vn_ref[...].T
        ot_ref[...] = jnp.zeros_like(ot_ref)

    @pl.when(b + 1 < nb)
    def _():
        for cp in copies(b + 1, 1 - slot):
            cp.start()

    for cp in copies(b, slot):
        cp.wait()

    qcol = _select_column(qt_ref[...], b) * ATT_SCALE
    kncol = _select_column(knt_ref[...], b)
    vncol = _select_column(vnt_ref[...], b)
    heads = range(C_HEADS)
    hrows = [slice(h * HEAD_DIM, (h + 1) * HEAD_DIM) for h in heads]
    kvrows = [slice((h // C_GROUP) * HEAD_DIM, (h // C_GROUP + 1) * HEAD_DIM) for h in heads]
    s = [[jnp.sum(kbuf[slot, h * npg + pg] * qcol[hrows[h], :], axis=0, keepdims=True) for pg in range(npg)]
         for h in heads]
    sn = [jnp.sum(qcol[hrows[h], :] * kncol[kvrows[h], :], axis=0, keepdims=True) for h in heads]
    m = [jnp.maximum(sn[h], jnp.max(functools.reduce(jnp.maximum, s[h]), axis=1, keepdims=True)) for h in heads]
    p = [[jnp.exp(row - m[h]) for row in s[h]] for h in heads]
    pn = [jnp.exp(sn[h] - m[h]) for h in heads]
    denom = [pn[h] + jnp.sum(functools.reduce(jnp.add, p[h]), axis=1, keepdims=True) for h in heads]
    acc = [functools.reduce(jnp.add, [vbuf[slot, h * npg + pg] * p[h][pg] for pg in range(npg)]) for h in heads]
    ocols = [(jnp.sum(acc[h], axis=1, keepdims=True) + pn[h] * vncol[kvrows[h], :]) / denom[h] for h in heads]
    lane = lax.broadcasted_iota(jnp.int32, ot_ref.shape, 1)
    ot_ref[...] = jnp.where(lane == b, jnp.concatenate(ocols, axis=0), ot_ref[...])

    @pl.when(b == nb - 1)
    def _():
        o_ref[...] = ot_ref[...].T


def _moba_sample(idx, page_table, q, kn, vn, cache_kt, cache_vt, layer):
    db = q.shape[0]
    ppb = MOBA_BLOCK // PAGE_SIZE
    nbuf = C_HEADS * MOBA_TOPK * ppb
    full = lambda *shape: pl.BlockSpec(shape, lambda b, *_: (0,) * len(shape))
    return pl.pallas_call(
        functools.partial(_moba_sample_kernel, pages_per_block=ppb, layer=layer),
        grid_spec=pltpu.PrefetchScalarGridSpec(
            num_scalar_prefetch=2,
            grid=(db,),
            in_specs=[full(db, C_QW), full(db, C_KVW), full(db, C_KVW),
                      pl.BlockSpec(memory_space=pl.ANY), pl.BlockSpec(memory_space=pl.ANY)],
            out_specs=full(db, C_QW),
            scratch_shapes=[pltpu.VMEM((2, nbuf, HEAD_DIM, PAGE_SIZE), F32),
                            pltpu.VMEM((2, nbuf, HEAD_DIM, PAGE_SIZE), F32),
                            pltpu.SemaphoreType.DMA((2, 2)),
                            pltpu.VMEM((C_QW, db), F32), pltpu.VMEM((C_KVW, db), F32),
                            pltpu.VMEM((C_KVW, db), F32), pltpu.VMEM((C_QW, db), F32)]),
        out_shape=jax.ShapeDtypeStruct((db, C_QW), F32),
        compiler_params=_cparams("arbitrary"),
        name="moba_sample",
    )(idx, page_table, q, kn, vn, cache_kt, cache_vt)


def _decoder(st, x, pos, p, state):
    is_prompt = state is None
    nseq, seqlen, d = st.nseq, st.seqlen, st.d
    depth = p['w_ada'].shape[0]
    rope_tabs = _rope_tables(pos if is_prompt else jnp.broadcast_to(pos, (st.tokens,)))
    new = {'win_k': [], 'win_v': [], 'conv': [], 'ssm': [], 'k': [], 'v': []}

    def proj(layer, x, *args):
        outs = _proj(st, layer, x, *args)
        return (outs[:-1], outs[-1]) if isinstance(x, tuple) else (outs, x)

    for layer in range(depth):
        g_norm = p['norm_g'][layer]
        if layer % 2 == 0:
            i = layer // 2
            splits = ((0, A_QW, True, True), (A_QW, A_KVW, True, True), (A_QW + A_KVW, A_KVW, False, True),
                      (A_QW + 2 * A_KVW, MB_INNER, False, True),
                      (A_QW + 2 * A_KVW + MB_INNER, MB_CONV_DIM, False, True),
                      (A_QW + 2 * A_KVW + MB_INNER + MB_CONV_DIM, LANES, False, True))
            (qa, ka, va, z, xbc, dt), x = proj(layer, x, g_norm[0], p['w_in_a'][i], rope_tabs, splits)
            if is_prompt:
                o_a = _swa_prompt(qa, ka, va, p['sinks'][i], nseq, seqlen)
                wk = ka.reshape(nseq, seqlen, A_KV_HEADS, HEAD_DIM)[:, -WINDOW:]
                wv = va.reshape(nseq, seqlen, A_KV_HEADS, HEAD_DIM)[:, -WINDOW:]
                conv_prev = jnp.zeros((nseq, MB_CONV - 1, MB_CONV_DIM), F32)
                h0 = jnp.zeros((nseq, MB_HEADS, MB_HEADDIM, MB_DSTATE), F32)
                o_b, conv_new, h_new = _ssd_prompt(xbc, z, dt, conv_prev, h0, p['conv_w'][i], p['conv_b'][i],
                                                   p['dt_bias'][i], p['a_log'][i], p['d_skip'][i],
                                                   p['ssm_norm_g'][i], nseq, seqlen)
            else:
                o_a, wk, wv = _swa_sample(qa, ka, va, state['win_k'][i].reshape(nseq, WINDOW, A_KVW),
                                          state['win_v'][i].reshape(nseq, WINDOW, A_KVW), p['sinks'][i])
                wk = wk.reshape(nseq, WINDOW, A_KV_HEADS, HEAD_DIM)
                wv = wv.reshape(nseq, WINDOW, A_KV_HEADS, HEAD_DIM)
                o_b, conv_new, h_new = _ssd_sample(xbc, z, dt, state['conv'][i], state['ssm'][i], p['conv_w'][i],
                                                   p['conv_b'][i], p['dt_bias'][i], p['a_log'][i], p['d_skip'][i],
                                                   p['ssm_norm_g'][i])
            a_list = [o_a, o_b]
            w_list = [p['w_out_a'][i][:A_QW], p['w_out_a'][i][A_QW:]]
            new['win_k'].append(wk)
            new['win_v'].append(wv)
            new['conv'].append(conv_new)
            new['ssm'].append(h_new)
        else:
            j = layer // 2
            cols = ((0, C_QW, True), (C_QW, C_KVW, True), (C_QW + C_KVW, C_KVW, False))
            if is_prompt:
                splits = tuple(c + (e,) for c, e in zip(cols, (True, False, False)))
                extras = (('bf16', 1), ('transposed', 1), ('transposed', 2), ('blockmean', 1))
                (qc, k_rows, k_t, v_t, kmean), x = proj(layer, x, g_norm[0], p['w_in_c'][j], rope_tabs, splits, extras)
                o_c = _moba_prompt(qc, k_rows, v_t, kmean.reshape(nseq, seqlen // MOBA_BLOCK, C_KVW), nseq, seqlen)
                kc, vc = (jnp.transpose(t.reshape(nseq, C_KV_HEADS, HEAD_DIM, seqlen), (0, 3, 1, 2)) for t in (k_t, v_t))
            else:
                splits = tuple(c + (True,) for c in cols)
                (qc, kc, vc), x = proj(layer, x, g_norm[0], p['w_in_c'][j], rope_tabs, splits)
                idx = _moba_gate_sample(state['page_table'], qc, state['cache_k'], j)
                o_c = _moba_sample(idx, state['page_table'], qc, kc, vc, state['cache_k'], state['cache_v'], j)
                kc, vc = (t.reshape(nseq, seqlen, C_KV_HEADS, HEAD_DIM) for t in (kc, vc))
            a_list = [o_c]
            w_list = [p['w_out_c'][j]]
            new['k'].append(kc)
            new['v'].append(vc)
        sorted_moe = st.tokens >= 2 * SORT_TILE
        moe = _moe_sorted if sorted_moe else _moe
        x1, h2, aux = _mix(st, layer, x, a_list, w_list, g_norm[1], p['wr_pad'], p['rb_col'], sorted_moe)
        x = moe(st, layer, h2, aux, p['w_gate_up'], p['w_down'], x1, p['final_norm_g'],
                final=(layer == depth - 1))
    return x.reshape(nseq, seqlen, d), {name: jnp.stack(rows) for name, rows in new.items()}


def kernel(x_prompt, x_sample, c_prompt, c_sample, state_win_k, state_win_v, state_conv, state_ssm, cache_k, cache_v, page_table, w_ada, b_ada, norm_g, w_in_a, sinks, conv_w, conv_b, dt_bias, a_log, d_skip, ssm_norm_g, w_out_a, w_in_c, w_out_c, w_router, router_bias, w_gate, w_up, w_down, final_norm_g):
    nb, seqlen, d = x_prompt.shape
    db, dec_seq, _ = x_sample.shape
    assert dec_seq == 1 and seqlen % MOBA_BLOCK == 0 and d % LANES == 0
    n_odd, n_pool, page_size, ckv, hd = cache_k.shape
    assert page_size == PAGE_SIZE and ckv == C_KV_HEADS and hd == HEAD_DIM
    n_pages = page_table.shape[1]

    n_in_a = w_in_a.shape[2]
    pad_a = (-n_in_a) % LANES
    p = {'w_ada': w_ada, 'norm_g': norm_g, 'sinks': sinks, 'conv_w': conv_w, 'conv_b': conv_b, 'dt_bias': dt_bias,
         'a_log': a_log, 'd_skip': d_skip, 'ssm_norm_g': ssm_norm_g, 'final_norm_g': final_norm_g,
         'w_in_a': jnp.pad(w_in_a, ((0, 0), (0, 0), (0, pad_a))).astype(BF16),
         'w_out_a': w_out_a.astype(BF16), 'w_in_c': w_in_c.astype(BF16), 'w_out_c': w_out_c.astype(BF16),
         'w_gate_up': jnp.concatenate([w_gate, w_up], axis=-1).astype(BF16), 'w_down': w_down.astype(BF16),
         'wr_pad': jnp.pad(w_router, ((0, 0), (0, LANES - N_EXPERTS))),
         'rb_col': jnp.pad(router_bias, (0, LANES - N_EXPERTS)).reshape(LANES, 1)}

    mod = _ada(jnp.concatenate([c_prompt, c_sample], axis=0), w_ada, b_ada)
    st_p = _Stream(nb, seqlen, d, mod[:, :nb])
    st_s = _Stream(db, 1, d, mod[:, nb:])

    pos_p = jnp.arange(seqlen, dtype=jnp.int32)
    y_prompt, pn = _decoder(st_p, x_prompt.reshape(nb * seqlen, d), pos_p, p, None)

    cache_kt = jnp.transpose(cache_k, (0, 1, 3, 4, 2)).reshape(n_odd, n_pool, C_KVW, PAGE_SIZE)
    cache_vt = jnp.transpose(cache_v, (0, 1, 3, 4, 2)).reshape(n_odd, n_pool, C_KVW, PAGE_SIZE)
    state = {'win_k': state_win_k, 'win_v': state_win_v, 'conv': state_conv, 'ssm': state_ssm,
             'cache_k': cache_kt, 'cache_v': cache_vt, 'page_table': page_table}
    pos_s = n_pages * PAGE_SIZE + jnp.arange(1, dtype=jnp.int32)
    y_sample, sn = _decoder(st_s, x_sample.reshape(db, d), pos_s, p, state)
    return (y_prompt, y_sample, pn['win_k'], pn['win_v'], pn['conv'], pn['ssm'], pn['k'], pn['v'],
            sn['win_k'], sn['win_v'], sn['conv'], sn['ssm'], sn['k'], sn['v'])
```

```python
import functools
import math

import jax
import jax.numpy as jnp
import numpy as np
from jax import lax
from jax.experimental import pallas as pl
from jax.experimental.pallas import tpu as pltpu

F32 = jnp.float32
BF16 = jnp.bfloat16

HEAD_DIM = 64
ROT_DIM = HEAD_DIM // 4
ROPE_THETA = 500000.0
A_HEADS = 8
A_KV_HEADS = 2
A_GROUP = A_HEADS // A_KV_HEADS
WINDOW = 128
MB_HEADDIM = 64
MB_HEADS = 8
MB_INNER = MB_HEADS * MB_HEADDIM
MB_GROUPS = 2
MB_DSTATE = 128
MB_CONV = 4
MB_GN = MB_GROUPS * MB_DSTATE
MB_CONV_DIM = MB_INNER + 2 * MB_GN
SSD_CHUNK = 128
C_HEADS = 16
C_KV_HEADS = 4
C_GROUP = C_HEADS // C_KV_HEADS
MOBA_BLOCK = 256
MOBA_TOPK = 3
MOBA_QTILE = MOBA_BLOCK
PAGE_SIZE = 128
A_QW = A_HEADS * HEAD_DIM
A_KVW = A_KV_HEADS * HEAD_DIM
C_QW = C_HEADS * HEAD_DIM
C_KVW = C_KV_HEADS * HEAD_DIM
N_EXPERTS = 16
N_EXPERT_GROUPS = 4
EXPERTS_PER_GROUP = N_EXPERTS // N_EXPERT_GROUPS
PAIRS_PER_GROUP = EXPERTS_PER_GROUP * (EXPERTS_PER_GROUP - 1) // 2
N_BUCKETS = N_EXPERT_GROUPS * PAIRS_PER_GROUP
BUCKET_ROWS = 32
ROUTE_ROWS = 8
SORT_TILE = 256
RMS_EPS = 1e-6
NEG_INF = -1e30
ATT_SCALE = HEAD_DIM ** -0.5

LANES = 128
VMEM_LIMIT_BYTES = 56 * 1024 * 1024


def _cparams(*sem):
    return pltpu.CompilerParams(dimension_semantics=sem, vmem_limit_bytes=VMEM_LIMIT_BYTES)


def _dot(a, b):
    return jnp.dot(a, b, preferred_element_type=F32)


def _dot_nt(a, b):
    return lax.dot_general(a, b, (((1,), (1,)), ((), ())), preferred_element_type=F32)


def _dot_tn(a, b):
    return lax.dot_general(a, b, (((0,), (0,)), ((), ())), preferred_element_type=F32)


def _split2(x):
    hi = x.astype(BF16)
    lo = (x - hi.astype(F32)).astype(BF16)
    return hi, lo


def _dot_hi(a, b):
    ah, al = _split2(a)
    bh, bl = _split2(b)
    return _dot(ah, bh) + (_dot(al, bh) + _dot(ah, bl))


def _dot_hi_nt(a, b):
    ah, al = _split2(a)
    bh, bl = _split2(b)
    return _dot_nt(ah, bh) + (_dot_nt(al, bh) + _dot_nt(ah, bl))


def _dot_exact_lhs(a_bf16, b):
    b0 = b.astype(BF16)
    r = b - b0.astype(F32)
    b1 = r.astype(BF16)
    b2 = (r - b1.astype(F32)).astype(BF16)
    return _dot(a_bf16, b0) + (_dot(a_bf16, b1) + _dot(a_bf16, b2))


def _sigmoid(x):
    return 1.0 / (1.0 + jnp.exp(-x))


def _silu(x):
    return x * _sigmoid(x)


def _softplus(x):
    return jnp.maximum(x, 0.0) + jnp.log(1.0 + jnp.exp(-jnp.abs(x)))


def _rms(x, g):
    return x * lax.rsqrt(jnp.mean(x * x, axis=-1, keepdims=True) + RMS_EPS) * g


def _rope_tables(pos):
    half = ROT_DIM // 2
    inv_freq = jnp.power(ROPE_THETA, -jnp.arange(half, dtype=F32) / half)
    ang = pos.astype(F32)[:, None] * inv_freq
    cos, sin = jnp.cos(ang), jnp.sin(ang)
    n = pos.shape[0]
    rest = HEAD_DIM - ROT_DIM
    ct = jnp.concatenate([cos, cos, jnp.ones((n, rest), F32)], axis=1)
    sa = jnp.concatenate([jnp.zeros((n, half), F32), sin, jnp.zeros((n, rest), F32)], axis=1)
    sb = jnp.concatenate([-sin, jnp.zeros((n, half), F32), jnp.zeros((n, rest), F32)], axis=1)
    rep = LANES // HEAD_DIM
    return jnp.tile(ct, (1, rep)), jnp.tile(sa, (1, rep)), jnp.tile(sb, (1, rep))


def _rope(x, ct, sa, sb):
    half = ROT_DIM // 2
    return x * ct + pltpu.roll(x, half, 1) * sa + pltpu.roll(x, LANES - half, 1) * sb


def _ada_kernel(c_ref, w_ref, b_ref, o_ref):
    o_ref[...] = _dot_hi(_silu(c_ref[...]), w_ref[...]) + b_ref[...]


def _ada(c_all, w_ada, b_ada):
    depth, d, n6 = w_ada.shape
    nc = c_all.shape[0]
    nk = n6 // d
    return pl.pallas_call(
        _ada_kernel,
        grid=(depth, nk),
        in_specs=[pl.BlockSpec((nc, d), lambda l, k: (0, 0)),
                  pl.BlockSpec((None, d, d), lambda l, k: (l, 0, k)),
                  pl.BlockSpec((None, 1, d), lambda l, k: (l, 0, k))],
        out_specs=pl.BlockSpec((None, nc, d), lambda l, k: (l, 0, k)),
        out_shape=jax.ShapeDtypeStruct((depth, nc, n6), F32),
        compiler_params=_cparams("parallel", "parallel"),
        name="ada",
    )(c_all, w_ada, b_ada.reshape(depth, 1, n6))


class _Stream:
    def __init__(self, nseq, seqlen, d, mod):
        self.nseq, self.seqlen, self.d = nseq, seqlen, d
        self.tokens = nseq * seqlen
        if seqlen == 1:
            self.tm = self.tokens
            self.tiles_per_seq = None
            self.mod = mod
        else:
            self.tm = math.gcd(seqlen, 512)
            self.tiles_per_seq = seqlen // self.tm
            depth = mod.shape[0]
            self.mod = mod.reshape(depth, nseq * 6, 1, d)
        self.ntiles = self.tokens // self.tm

    def mod_arg(self, layer):
        return self.mod[layer]

    def mod_spec(self, k):
        if self.tiles_per_seq is None:
            return pl.BlockSpec((self.tm, self.d), lambda i, *_: (0, k))
        tps = self.tiles_per_seq
        return pl.BlockSpec((None, 1, self.d), lambda i, *_: ((i // tps) * 6 + k, 0, 0))

    def rope_spec(self):
        if self.tiles_per_seq is None:
            return pl.BlockSpec((self.tm, LANES), lambda i, *_: (0, 0))
        tps = self.tiles_per_seq
        return pl.BlockSpec((self.tm, LANES), lambda i, *_: (i % tps, 0))


def _proj_kernel(x_ref, *refs, splits, extras):
    _proj_compute(x_ref[...], *refs, splits=splits, extras=extras)


def _proj_combine_kernel(dest_ref, x1_ref, g2_ref, ys_ref, *refs, splits, extras):
    *refs, x_out_ref, buf, sem = refs
    i = pl.program_id(0)
    n = pl.num_programs(0)
    tm = x1_ref.shape[0]
    slot = i % 2

    def row_copy(tile, r, sl):
        return pltpu.make_async_copy(ys_ref.at[pl.ds(dest_ref[tile * tm + r], 1)], buf.at[sl, pl.ds(r, 1)], sem.at[sl])

    def tile_wait(sl):
        pltpu.make_async_copy(ys_ref.at[pl.ds(0, tm)], buf.at[sl], sem.at[sl]).wait()

    @pl.when(i == 0)
    def _():
        def body(r, carry):
            row_copy(0, r, 0).start()
            return carry

        lax.fori_loop(0, tm, body, 0, unroll=8)

    tile_wait(slot)
    nxt = jnp.minimum(i + 1, n - 1)
    for r in range(tm):
        row_copy(nxt, r, 1 - slot).start()
    x = x1_ref[...] + g2_ref[...] * buf[slot]
    x_out_ref[...] = x
    _proj_compute(x, *refs, splits=splits, extras=extras)

    @pl.when(i == n - 1)
    def _():
        tile_wait(1 - slot)


def _proj_compute(x, sh_ref, sc_ref, g_ref, w_ref, ct_ref, sa_ref, sb_ref, *out_refs, splits, extras):
    h = _rms(x, g_ref[...]) * (1.0 + sc_ref[...]) + sh_ref[...]
    u = _dot(h.astype(BF16), w_ref[...])
    outs = list(out_refs)
    pieces = []
    for start, width, rope, emit in splits:
        if rope:
            ct, sa, sb = ct_ref[...], sa_ref[...], sb_ref[...]
            piece = jnp.concatenate([_rope(u[:, start + c0:start + c0 + LANES], ct, sa, sb)
                                     for c0 in range(0, width, LANES)], axis=-1)
        else:
            piece = u[:, start:start + width]
        pieces.append(piece)
        if emit:
            outs.pop(0)[...] = piece
    for kind, idx in extras:
        o_ref, piece = outs.pop(0), pieces[idx]
        if kind == 'bf16':
            o_ref[...] = piece.astype(BF16)
        elif kind == 'transposed':
            o_ref[...] = piece.T
        else:
            for c in range(o_ref.shape[0]):
                o_ref[c] = jnp.sum(piece[c * MOBA_BLOCK:(c + 1) * MOBA_BLOCK, :], axis=0,
                                   keepdims=True) * (1.0 / MOBA_BLOCK)


def _proj(st, layer, x, norm_g, w_bf16, rope_tabs, splits, extras=()):
    d = st.d
    n = w_bf16.shape[1]
    tm = st.tm
    row = lambda i, *_: (i, 0)
    const = lambda i, *_: (0, 0)
    out_specs = [pl.BlockSpec((tm, w), row) for _, w, _, emit in splits if emit]
    out_shape = [jax.ShapeDtypeStruct((st.tokens, w), F32) for _, w, _, emit in splits if emit]
    for kind, idx in extras:
        w = splits[idx][1]
        if kind == 'bf16':
            out_specs.append(pl.BlockSpec((tm, w), row))
            out_shape.append(jax.ShapeDtypeStruct((st.tokens, w), BF16))
        elif kind == 'transposed':
            tps = st.tiles_per_seq
            out_specs.append(pl.BlockSpec((None, w, tm), lambda i, *_: (i // tps, 0, i % tps)))
            out_shape.append(jax.ShapeDtypeStruct((st.nseq, w, st.seqlen), F32))
        else:
            nb = tm // MOBA_BLOCK
            out_specs.append(pl.BlockSpec((nb, 1, w), lambda i, *_: (i, 0, 0)))
            out_shape.append(jax.ShapeDtypeStruct((st.tokens // MOBA_BLOCK, 1, w), F32))
    in_specs = [st.mod_spec(0), st.mod_spec(1), pl.BlockSpec((1, d), const), pl.BlockSpec((d, n), const),
                st.rope_spec(), st.rope_spec(), st.rope_spec()]
    args = (st.mod_arg(layer), st.mod_arg(layer), norm_g.reshape(1, d), w_bf16, *rope_tabs)
    if not isinstance(x, tuple):
        return pl.pallas_call(
            functools.partial(_proj_kernel, splits=splits, extras=extras),
            grid=(st.ntiles,),
            in_specs=[pl.BlockSpec((tm, d), row)] + in_specs,
            out_specs=out_specs,
            out_shape=out_shape,
            compiler_params=_cparams("parallel"),
            name="proj",
        )(x, *args)
    x1, ys, dest = x
    return pl.pallas_call(
        functools.partial(_proj_combine_kernel, splits=splits, extras=extras),
        grid_spec=pltpu.PrefetchScalarGridSpec(
            num_scalar_prefetch=1,
            grid=(st.ntiles,),
            in_specs=[pl.BlockSpec((tm, d), row), st.mod_spec(5), pl.BlockSpec(memory_space=pl.ANY)] + in_specs,
            out_specs=out_specs + [pl.BlockSpec((tm, d), row)],
            scratch_shapes=[pltpu.VMEM((2, tm, d), F32), pltpu.SemaphoreType.DMA((2,))]),
        out_shape=out_shape + [jax.ShapeDtypeStruct((st.tokens, d), F32)],
        compiler_params=_cparams("arbitrary"),
        name="proj_combine",
    )(dest, x1, st.mod_arg(layer - 1), ys, *args)


def _swa_prompt_kernel(sink_ref, q_ref, kp_ref, kc_ref, vp_ref, vc_ref, o_ref):
    i = pl.program_id(1)
    qb = q_ref.shape[0]
    cols = A_GROUP * qb
    q_t = q_ref[...].T
    kk = jnp.concatenate([kp_ref[...], kc_ref[...]], axis=0).astype(BF16)
    vv_t = jnp.concatenate([vp_ref[...], vc_ref[...]], axis=0).T.astype(BF16)
    key_i = lax.broadcasted_iota(jnp.int32, (2 * qb, cols), 0)
    qry_i = lax.broadcasted_iota(jnp.int32, (2 * qb, cols), 1) % qb
    diff = qry_i + qb - key_i
    ok = (diff >= 0) & (diff <= WINDOW) & ((key_i >= qb) | (i > 0))
    head_of_lane = lax.broadcasted_iota(jnp.int32, (1, cols), 1) // qb
    outs = []
    for kv in range(A_KV_HEADS):
        rows = slice(kv * HEAD_DIM, (kv + 1) * HEAD_DIM)
        qs_t = jnp.concatenate(
            [q_t[(kv * A_GROUP + g) * HEAD_DIM:(kv * A_GROUP + g + 1) * HEAD_DIM, :] for g in range(A_GROUP)], axis=1)
        s = _dot(kk[:, rows], (qs_t * ATT_SCALE).astype(BF16))
        s = jnp.where(ok, s, NEG_INF)
        sink = jnp.zeros((1, cols), F32)
        for g in range(A_GROUP):
            sink = jnp.where(head_of_lane == g, sink_ref[kv * A_GROUP + g], sink)
        m = jnp.maximum(jnp.max(s, axis=0, keepdims=True), sink)
        p = jnp.exp(s - m)
        denom = jnp.sum(p, axis=0, keepdims=True) + jnp.exp(sink - m)
        outs.append(_dot(vv_t[rows, :], p.astype(BF16)) / denom)
    o_t = jnp.concatenate(outs, axis=0).T
    o_ref[...] = jnp.concatenate(
        [o_t[g * qb:(g + 1) * qb, kv * HEAD_DIM:(kv + 1) * HEAD_DIM]
         for kv in range(A_KV_HEADS) for g in range(A_GROUP)], axis=-1)


def _swa_prompt(q, k, v, sinks, nseq, seqlen):
    qb = WINDOW
    nb = seqlen // qb
    cur = lambda b, i: (b * nb + i, 0)
    prev = lambda b, i: (b * nb + jnp.maximum(i - 1, 0), 0)
    return pl.pallas_call(
        _swa_prompt_kernel,
        grid=(nseq, nb),
        in_specs=[pl.BlockSpec(memory_space=pltpu.SMEM),
                  pl.BlockSpec((qb, A_QW), cur),
                  pl.BlockSpec((qb, A_KVW), prev), pl.BlockSpec((qb, A_KVW), cur),
                  pl.BlockSpec((qb, A_KVW), prev), pl.BlockSpec((qb, A_KVW), cur)],
        out_specs=pl.BlockSpec((qb, A_QW), cur),
        out_shape=jax.ShapeDtypeStruct(q.shape, F32),
        compiler_params=_cparams("parallel", "parallel"),
        name="swa_prompt",
    )(sinks, q, k, k, v, v)


def _ssd_prompt_kernel(xbc_ref, z_ref, dt_ref, cprev_ref, h0_ref, cw_ref, cb_ref, dtb_ref, alog_ref, dskip_ref,
                       ng_ref, o_ref, cnew_ref, hlast_ref, xp_ref, h_ref):
    c = pl.program_id(1)
    nc = pl.num_programs(1)
    q = SSD_CHUNK
    halo = MB_CONV - 1
    base = 8 - halo

    @pl.when(c == 0)
    def _():
        xp_ref[base:8, :] = cprev_ref[...]
        h_ref[...] = h0_ref[...]

    xp_ref[8:8 + q, :] = xbc_ref[...]
    acc = cb_ref[...] + xp_ref[base:base + q, :] * cw_ref[0:1, :]
    for tap in range(1, MB_CONV):
        acc = acc + xp_ref[base + tap:base + tap + q, :] * cw_ref[tap:tap + 1, :]
    tail = xp_ref[8 + q - halo:8 + q, :]
    xp_ref[base:8, :] = tail

    @pl.when(c == nc - 1)
    def _():
        cnew_ref[...] = tail

    xbc = _silu(acc)
    xs = xbc[:, :MB_INNER]
    dt = _softplus(dt_ref[...] + dtb_ref[...])
    da = dt * (-jnp.exp(alog_ref[...]))
    r_i = lax.broadcasted_iota(jnp.int32, (q, q), 0)
    c_i = lax.broadcasted_iota(jnp.int32, (q, q), 1)
    causal = r_i >= c_i
    acum = _dot_exact_lhs(causal.astype(BF16), da)
    acum_t = acum.T
    rpg = MB_HEADS // MB_GROUPS
    ys = []
    h_all = [h_ref[h] for h in range(MB_HEADS)]
    h_new = []
    for g in range(MB_GROUPS):
        bq = xbc[:, MB_INNER + g * MB_DSTATE:MB_INNER + (g + 1) * MB_DSTATE].astype(BF16)
        cq = xbc[:, MB_INNER + MB_GN + g * MB_DSTATE:MB_INNER + MB_GN + (g + 1) * MB_DSTATE].astype(BF16)
        cbm = _dot_nt(cq, bq)
        for r in range(rpg):
            h = g * rpg + r
            a_col = acum[:, h:h + 1]
            a_row = acum_t[h:h + 1, :]
            a_last = acum[q - 1:q, h:h + 1]
            decay = jnp.where(causal, jnp.exp(jnp.where(causal, a_col - a_row, 0.0)), 0.0)
            xd = xs[:, h * MB_HEADDIM:(h + 1) * MB_HEADDIM] * dt[:, h:h + 1]
            hprev = h_all[h]
            y = _dot((cbm * decay).astype(BF16), xd.astype(BF16))
            y = y + jnp.exp(a_col) * _dot_nt(cq, hprev.astype(BF16))
            xw = (xd * jnp.exp(a_last - a_col)).astype(BF16)
            h_new.append(hprev * jnp.exp(a_last) + _dot_tn(xw, bq))
            ys.append(y)
    for h in range(MB_HEADS):
        h_ref[h] = h_new[h]
    y = jnp.concatenate(ys, axis=-1) + xs * dskip_ref[...]
    y = y * _silu(z_ref[...])
    gw = MB_INNER // MB_GROUPS
    ng = ng_ref[...]
    o_ref[...] = jnp.concatenate(
        [_rms(y[:, g * gw:(g + 1) * gw], ng[:, g * gw:(g + 1) * gw]) for g in range(MB_GROUPS)], axis=-1)

    @pl.when(c == nc - 1)
    def _():
        hlast_ref[...] = h_ref[...]


def _pad_lanes(v, n=LANES):
    return jnp.pad(v, (0, n - v.shape[0])).reshape(1, n)


def _ssd_prompt(xbc, z, dt, conv_prev, h0, conv_w, conv_b, dt_bias, a_log, d_skip, norm_g, nseq, seqlen):
    q = SSD_CHUNK
    nc = seqlen // q
    row = lambda b, c: (b * nc + c, 0)
    per_seq3 = lambda b, c: (b, 0, 0)
    per_seq4 = lambda b, c: (b, 0, 0, 0)
    const = lambda b, c: (0, 0)
    halo = MB_CONV - 1
    return pl.pallas_call(
        _ssd_prompt_kernel,
        grid=(nseq, nc),
        in_specs=[pl.BlockSpec((q, MB_CONV_DIM), row), pl.BlockSpec((q, MB_INNER), row),
                  pl.BlockSpec((q, LANES), row),
                  pl.BlockSpec((None, halo, MB_CONV_DIM), per_seq3),
                  pl.BlockSpec((None, MB_HEADS, MB_HEADDIM, MB_DSTATE), per_seq4),
                  pl.BlockSpec((MB_CONV, MB_CONV_DIM), const), pl.BlockSpec((1, MB_CONV_DIM), const),
                  pl.BlockSpec((1, LANES), const), pl.BlockSpec((1, LANES), const),
                  pl.BlockSpec((1, MB_INNER), const), pl.BlockSpec((1, MB_INNER), const)],
        out_specs=[pl.BlockSpec((q, MB_INNER), row),
                   pl.BlockSpec((None, halo, MB_CONV_DIM), per_seq3),
                   pl.BlockSpec((None, MB_HEADS, MB_HEADDIM, MB_DSTATE), per_seq4)],
        out_shape=[jax.ShapeDtypeStruct((nseq * seqlen, MB_INNER), F32),
                   jax.ShapeDtypeStruct((nseq, halo, MB_CONV_DIM), F32),
                   jax.ShapeDtypeStruct((nseq, MB_HEADS, MB_HEADDIM, MB_DSTATE), F32)],
        scratch_shapes=[pltpu.VMEM((8 + q, MB_CONV_DIM), F32),
                        pltpu.VMEM((MB_HEADS, MB_HEADDIM, MB_DSTATE), F32)],
        compiler_params=_cparams("parallel", "arbitrary"),
        name="ssd_prompt",
    )(xbc, z, dt, conv_prev, h0, conv_w, conv_b.reshape(1, -1), _pad_lanes(dt_bias), _pad_lanes(a_log),
      jnp.repeat(d_skip, MB_HEADDIM).reshape(1, -1), norm_g.reshape(1, -1))


def _route(logits_t, rbias_col):
    tm = logits_t.shape[1]
    scores = _sigmoid(logits_t[0:N_EXPERTS, :])
    biased = scores + rbias_col[0:N_EXPERTS, :]
    s = [scores[e:e + 1, :] for e in range(N_EXPERTS)]
    b = [biased[e:e + 1, :] for e in range(N_EXPERTS)]
    epg = EXPERTS_PER_GROUP
    gscore = []
    for g in range(N_EXPERT_GROUPS):
        v = b[g * epg:(g + 1) * epg]
        best = None
        for i in range(epg):
            for j in range(i + 1, epg):
                pair = v[i] + v[j]
                best = pair if best is None else jnp.maximum(best, pair)
        gscore.append(best)
    gsel = jnp.zeros((1, tm), jnp.int32)
    gbest = gscore[0]
    for g in range(1, N_EXPERT_GROUPS):
        better = gscore[g] > gbest
        gsel = jnp.where(better, g, gsel)
        gbest = jnp.where(better, gscore[g], gbest)
    bs, ss = [], []
    for k in range(epg):
        bk, sk = b[k], s[k]
        for g in range(1, N_EXPERT_GROUPS):
            bk = jnp.where(gsel == g, b[g * epg + k], bk)
            sk = jnp.where(gsel == g, s[g * epg + k], sk)
        bs.append(bk)
        ss.append(sk)
    i1 = jnp.zeros((1, tm), jnp.int32)
    m1 = bs[0]
    for k in range(1, epg):
        better = bs[k] > m1
        i1 = jnp.where(better, k, i1)
        m1 = jnp.where(better, bs[k], m1)
    i2 = jnp.full((1, tm), -1, jnp.int32)
    m2 = jnp.full((1, tm), -jnp.inf, F32)
    for k in range(epg):
        better = (i1 != k) & ((bs[k] > m2) | (i2 < 0))
        i2 = jnp.where(better, k, i2)
        m2 = jnp.where(better, bs[k], m2)
    s1 = jnp.zeros((1, tm), F32)
    s2 = jnp.zeros((1, tm), F32)
    for k in range(epg):
        s1 = jnp.where(i1 == k, ss[k], s1)
        s2 = jnp.where(i2 == k, ss[k], s2)
    denom = s1 + s2
    rows = lax.broadcasted_iota(jnp.int32, (N_EXPERTS, tm), 0)
    comb = jnp.zeros((N_EXPERTS, tm), F32)
    for e in range(N_EXPERTS):
        g, k = divmod(e, epg)
        chosen = (gsel == g) & ((i1 == k) | (i2 == k))
        comb = jnp.where(rows == e, jnp.where(chosen, s[e] / denom, 0.0), comb)
    lo = jnp.minimum(i1, i2)
    hi = jnp.maximum(i1, i2)
    pair = jnp.where(lo == 0, 0, jnp.where(lo == 1, epg - 1, 2 * epg - 3)) + (hi - lo - 1)
    bucket = (gsel * PAIRS_PER_GROUP + pair).astype(F32)
    first_is_lo = i1 < i2
    w_lo = jnp.where(first_is_lo, s1, s2) / denom
    w_hi = jnp.where(first_is_lo, s2, s1) / denom
    return comb, bucket, w_lo, w_hi


def _mix_kernel(*refs, n_in, sorted_moe):
    x_ref, g1_ref = refs[0], refs[1]
    a_refs = refs[2:2 + n_in]
    w_refs = refs[2 + n_in:2 + 2 * n_in]
    ng_ref, sc_ref, sh_ref, wr_ref, rb_ref, x1_ref, h2_ref, aux_ref = refs[2 + 2 * n_in:]
    mix = _dot(a_refs[0][...].astype(BF16), w_refs[0][...])
    for a_ref, w_ref in zip(a_refs[1:], w_refs[1:]):
        mix = mix + _dot(a_ref[...].astype(BF16), w_ref[...])
    x1 = x_ref[...] + g1_ref[...] * mix
    x1_ref[...] = x1
    h2 = _rms(x1, ng_ref[...]) * (1.0 + sc_ref[...]) + sh_ref[...]
    h2_hi, h2_lo = _split2(h2)
    w_hi, w_lo = _split2(wr_ref[...])
    both = _dot(h2_hi, jnp.concatenate([w_hi, w_lo], axis=1))
    logits = both[:, :LANES] + (_dot(h2_lo, w_hi) + both[:, LANES:])
    comb_t, bucket, w_lo, w_hi = _route(logits.T, rb_ref[...])
    tm, d = x1.shape
    if sorted_moe:
        r = lax.broadcasted_iota(jnp.int32, (LANES, tm), 0)
        rt = jnp.where(r == 0, bucket, jnp.where(r == 1, w_lo, jnp.where(r == 2, w_hi, 0.0)))
        h2_ref[:, :d] = h2
        h2_ref[:, d:] = rt.T
        aux_ref[...] = rt[0:aux_ref.shape[0], :]
    else:
        h2_ref[...] = h2.astype(BF16)
        aux_ref[...] = jnp.concatenate([comb_t, jnp.zeros((LANES - N_EXPERTS, tm), F32)], axis=0).T


def _mix(st, layer, x, a_list, w_list, norm_g, wr_pad, rb_col, sorted_moe):
    d, tm = st.d, st.tm
    row = lambda i: (i, 0)
    const = lambda i: (0, 0)
    n_in = len(a_list)
    in_specs = [pl.BlockSpec((tm, d), row), st.mod_spec(2)]
    in_specs += [pl.BlockSpec((tm, a.shape[1]), row) for a in a_list]
    in_specs += [pl.BlockSpec(w.shape, const) for w in w_list]
    in_specs += [pl.BlockSpec((1, d), const), st.mod_spec(4), st.mod_spec(3),
                 pl.BlockSpec((d, LANES), const), pl.BlockSpec((LANES, 1), const)]
    m = st.mod_arg(layer)
    if sorted_moe:
        out_specs = [pl.BlockSpec((tm, d), row), pl.BlockSpec((tm, d + LANES), row),
                     pl.BlockSpec((ROUTE_ROWS, tm), lambda i: (0, i))]
        out_shape = [jax.ShapeDtypeStruct((st.tokens, d), F32), jax.ShapeDtypeStruct((st.tokens, d + LANES), F32),
                     jax.ShapeDtypeStruct((ROUTE_ROWS, st.tokens), F32)]
    else:
        out_specs = [pl.BlockSpec((tm, d), row), pl.BlockSpec((tm, d), row), pl.BlockSpec((tm, LANES), row)]
        out_shape = [jax.ShapeDtypeStruct((st.tokens, d), F32), jax.ShapeDtypeStruct((st.tokens, d), BF16),
                     jax.ShapeDtypeStruct((st.tokens, LANES), F32)]
    return pl.pallas_call(
        functools.partial(_mix_kernel, n_in=n_in, sorted_moe=sorted_moe),
        grid=(st.ntiles,),
        in_specs=in_specs,
        out_specs=out_specs,
        out_shape=out_shape,
        compiler_params=_cparams("parallel"),
        name="mix",
    )(x, m, *a_list, *w_list, norm_g.reshape(1, d), m, m, wr_pad, rb_col)


def _gated_up(h, wgu_ref):
    u = _dot(h, wgu_ref[...])
    dff = u.shape[1] // 2
    return _silu(u[:, :dff]) * u[:, dff:]


def _moe_kernel(h_ref, comb_ref, wgu_ref, wd_ref, x1_ref, g2_ref, fg_ref, o_ref, acc_ref, *, final):
    e = pl.program_id(1)

    @pl.when(e == 0)
    def _():
        acc_ref[...] = jnp.zeros_like(acc_ref)

    he = _gated_up(h_ref[...], wgu_ref)
    comb = comb_ref[...]
    lane = lax.broadcasted_iota(jnp.int32, comb.shape, 1)
    ce = jnp.sum(jnp.where(lane == e, comb, 0.0), axis=-1, keepdims=True)
    acc_ref[...] += ce * _dot(he.astype(BF16), wd_ref[...])

    @pl.when(e == pl.num_programs(1) - 1)
    def _():
        x2 = x1_ref[...] + g2_ref[...] * acc_ref[...]
        o_ref[...] = _rms(x2, fg_ref[...]) if final else x2


def _moe(st, layer, h2, comb, wgu, wd, x1, final_g, final):
    d, tm = st.d, st.tm
    ne, dff = wd.shape[1:3]
    row = lambda i, e: (i, 0)
    return pl.pallas_call(
        functools.partial(_moe_kernel, final=final),
        grid=(st.ntiles, ne),
        in_specs=[pl.BlockSpec((tm, d), row), pl.BlockSpec((tm, LANES), row),
                  pl.BlockSpec((None, None, d, 2 * dff), lambda i, e: (layer, e, 0, 0)),
                  pl.BlockSpec((None, None, dff, d), lambda i, e: (layer, e, 0, 0)),
                  pl.BlockSpec((tm, d), row), st.mod_spec(5),
                  pl.BlockSpec((1, d), lambda i, e: (0, 0))],
        out_specs=pl.BlockSpec((tm, d), row),
        out_shape=jax.ShapeDtypeStruct((st.tokens, d), F32),
        scratch_shapes=[pltpu.VMEM((tm, d), F32)],
        compiler_params=_cparams("parallel", "arbitrary"),
        name="moe",
    )(h2, comb, wgu, wd, x1, st.mod_arg(layer), final_g.reshape(1, d))


def _bucket_rank_kernel(rt_ref, rank_ref, cnt_ref, carry_ref):
    i = pl.program_id(0)
    tm = rt_ref.shape[1]

    @pl.when(i == 0)
    def _():
        carry_ref[...] = jnp.zeros_like(carry_ref)

    bucket = rt_ref[0:1, :]
    rows = lax.broadcasted_iota(jnp.int32, (BUCKET_ROWS, tm), 0).astype(F32)
    onehot = rows == bucket
    s_i = lax.broadcasted_iota(jnp.int32, (tm, tm), 0)
    t_i = lax.broadcasted_iota(jnp.int32, (tm, tm), 1)
    incl = jnp.where(s_i <= t_i, 1.0, 0.0).astype(BF16)
    cum = _dot(jnp.where(onehot, 1.0, 0.0).astype(BF16), incl)
    carry = carry_ref[...]
    rank = jnp.sum(jnp.where(onehot, cum - 1.0 + carry, 0.0), axis=0, keepdims=True)
    rank_ref[...] = rank.astype(jnp.int32)
    carry = carry + cum[:, tm - 1:tm]
    carry_ref[...] = carry
    cnt_ref[...] = jnp.broadcast_to(carry, cnt_ref.shape)


def _bucket_rank(st, rt):
    tm = st.tm
    return pl.pallas_call(
        _bucket_rank_kernel,
        grid=(st.ntiles,),
        in_specs=[pl.BlockSpec((ROUTE_ROWS, tm), lambda i: (0, i))],
        out_specs=[pl.BlockSpec((1, tm), lambda i: (0, i)), pl.BlockSpec((BUCKET_ROWS, LANES), lambda i: (0, 0))],
        out_shape=[jax.ShapeDtypeStruct((1, st.tokens), jnp.int32),
                   jax.ShapeDtypeStruct((BUCKET_ROWS, LANES), F32)],
        scratch_shapes=[pltpu.VMEM((BUCKET_ROWS, 1), F32)],
        compiler_params=_cparams("arbitrary"),
        name="bucket_rank",
    )(rt)


def _sort_plan(rt, rank, cnt, tokens):
    ts = SORT_TILE
    bucket = rt[0].astype(jnp.int32)
    counts = cnt[:N_BUCKETS, 0].astype(jnp.int32)
    padded = (counts + ts - 1) // ts * ts
    ends = jnp.cumsum(padded)
    dest = (ends - padded)[bucket] + rank[0]
    n_tiles = -(-tokens // ts) + N_BUCKETS
    n_used = ends[-1] // ts
    tile = jnp.arange(n_tiles, dtype=jnp.int32)
    tile_row = jnp.minimum(tile, n_used - 1) * ts
    tile_bucket = jnp.sum((ends[None, :] <= tile_row[:, None]).astype(jnp.int32), axis=1)
    tile_bucket = jnp.minimum(tile_bucket, N_BUCKETS - 1)
    pairs = [(a, b) for a in range(EXPERTS_PER_GROUP) for b in range(a + 1, EXPERTS_PER_GROUP)]
    lo_tab = jnp.array([a for a, _ in pairs], jnp.int32)
    hi_tab = jnp.array([b for _, b in pairs], jnp.int32)
    base = tile_bucket // PAIRS_PER_GROUP * EXPERTS_PER_GROUP
    pair = tile_bucket % PAIRS_PER_GROUP
    return dest, base + lo_tab[pair], base + hi_tab[pair], n_used.reshape(1).astype(jnp.int32), n_tiles


def _dispatch_kernel(dest_ref, h_ref, init_ref, out_ref, sem):
    del init_ref
    i = pl.program_id(0)
    tm = h_ref.shape[0]

    def body(r, carry):
        pltpu.make_async_copy(h_ref.at[pl.ds(r, 1)], out_ref.at[pl.ds(dest_ref[i * tm + r], 1)], sem).start()
        return carry

    lax.fori_loop(0, tm, body, 0, unroll=8)
    pltpu.make_async_copy(h_ref, out_ref.at[pl.ds(0, tm)], sem).wait()


def _dispatch(st, dest, h2w, n_rows, init):
    tm = st.tm
    w = h2w.shape[1]
    return pl.pallas_call(
        _dispatch_kernel,
        grid_spec=pltpu.PrefetchScalarGridSpec(
            num_scalar_prefetch=1,
            grid=(st.ntiles,),
            in_specs=[pl.BlockSpec((tm, w), lambda i, d: (i, 0)), pl.BlockSpec(memory_space=pl.ANY)],
            out_specs=pl.BlockSpec(memory_space=pl.ANY),
            scratch_shapes=[pltpu.SemaphoreType.DMA(())]),
        out_shape=jax.ShapeDtypeStruct((n_rows, w), F32),
        input_output_aliases={2: 0},
        compiler_params=_cparams("arbitrary"),
        name="moe_dispatch",
    )(dest, h2w, jnp.zeros((n_rows, w), F32) if init is None else init)


def _experts_kernel(elo_ref, ehi_ref, nused_ref, hs_ref, wgul_ref, wdl_ref, wguh_ref, wdh_ref, y_ref):
    del elo_ref, ehi_ref
    i = pl.program_id(0)
    d = y_ref.shape[1]

    @pl.when(i < nused_ref[0])
    def _():
        h = hs_ref[:, :d].astype(BF16)

        def expert(wgu_ref, wd_ref):
            return _dot(_gated_up(h, wgu_ref).astype(BF16), wd_ref[...])

        y_ref[...] = (hs_ref[:, d + 1:d + 2] * expert(wgul_ref, wdl_ref)
                      + hs_ref[:, d + 2:d + 3] * expert(wguh_ref, wdh_ref))

    @pl.when(i >= nused_ref[0])
    def _():
        y_ref[...] = jnp.zeros_like(y_ref)


def _experts(layer, hs, e_lo, e_hi, n_used, n_tiles, wgu, wd):
    ts = SORT_TILE
    _, _, dff, d = wd.shape
    lo = lambda i, elo, ehi, nu: (layer, elo[i], 0, 0)
    hi = lambda i, elo, ehi, nu: (layer, ehi[i], 0, 0)
    up = lambda idx: pl.BlockSpec((None, None, d, 2 * dff), idx)
    down = lambda idx: pl.BlockSpec((None, None, dff, d), idx)
    return pl.pallas_call(
        _experts_kernel,
        grid_spec=pltpu.PrefetchScalarGridSpec(
            num_scalar_prefetch=3,
            grid=(n_tiles,),
            in_specs=[pl.BlockSpec((ts, hs.shape[1]), lambda i, *_: (i, 0)),
                      up(lo), down(lo), up(hi), down(hi)],
            out_specs=pl.BlockSpec((ts, d), lambda i, *_: (i, 0))),
        out_shape=jax.ShapeDtypeStruct((n_tiles * ts, d), F32),
        compiler_params=_cparams("arbitrary"),
        name="moe_experts",
    )(e_lo, e_hi, n_used, hs, wgu, wd, wgu, wd)


def _combine_kernel(dest_ref, x1_ref, g2_ref, fg_ref, ys_ref, o_ref, buf, sem, *, final):
    i = pl.program_id(0)
    n = pl.num_programs(0)
    tm = x1_ref.shape[0]
    slot = i % 2

    def start(tile, sl):
        def body(r, carry):
            pltpu.make_async_copy(ys_ref.at[pl.ds(dest_ref[tile * tm + r], 1)], buf.at[sl, pl.ds(r, 1)],
                                  sem.at[sl]).start()
            return carry

        lax.fori_loop(0, tm, body, 0, unroll=8)

    @pl.when(i == 0)
    def _():
        start(0, 0)

    @pl.when(i + 1 < n)
    def _():
        start(i + 1, 1 - slot)

    pltpu.make_async_copy(ys_ref.at[pl.ds(0, tm)], buf.at[slot], sem.at[slot]).wait()
    x2 = x1_ref[...] + g2_ref[...] * buf[slot]
    o_ref[...] = _rms(x2, fg_ref[...]) if final else x2


def _combine(st, layer, dest, ys, x1, final_g, final):
    d, tm = st.d, st.tm
    row = lambda i, dst: (i, 0)
    return pl.pallas_call(
        functools.partial(_combine_kernel, final=final),
        grid_spec=pltpu.PrefetchScalarGridSpec(
            num_scalar_prefetch=1,
            grid=(st.ntiles,),
            in_specs=[pl.BlockSpec((tm, d), row), st.mod_spec(5), pl.BlockSpec((1, d), lambda i, dst: (0, 0)),
                      pl.BlockSpec(memory_space=pl.ANY)],
            out_specs=pl.BlockSpec((tm, d), row),
            scratch_shapes=[pltpu.VMEM((2, tm, d), F32), pltpu.SemaphoreType.DMA((2,))]),
        out_shape=jax.ShapeDtypeStruct((st.tokens, d), F32),
        compiler_params=_cparams("arbitrary"),
        name="moe_combine",
    )(dest, x1, st.mod_arg(layer), final_g.reshape(1, d), ys)


def _moe_sorted(st, layer, h2w, rt, wgu, wd, x1, final_g, final, spare=None):
    rank, cnt = _bucket_rank(st, rt)
    dest, e_lo, e_hi, n_used, n_tiles = _sort_plan(rt, rank, cnt, st.tokens)
    hs = _dispatch(st, dest, h2w, n_tiles * SORT_TILE, spare)
    ys = _experts(layer, hs, e_lo, e_hi, n_used, n_tiles, wgu, wd)
    if not final:
        return (x1, ys, dest), hs
    return _combine(st, layer, dest, ys, x1, final_g, final), hs


def _moba_prompt_kernel(q_ref, k_ref, vt_ref, km_ref, o_ref, sel_ref, qs_ref, m_ref, l_ref, acc_ref):
    i = pl.program_id(1)
    qb = MOBA_QTILE
    blk = MOBA_BLOCK
    nblk = km_ref.shape[0]
    own = (i * qb) // blk
    q_off = i * qb - own * blk
    q_t = q_ref[...].T
    km = km_ref[...]
    cols = C_GROUP * qb
    nb_pad = sel_ref.shape[1]
    blk_row = lax.broadcasted_iota(jnp.int32, (nb_pad, cols), 0)
    past = blk_row < own
    key_i = lax.broadcasted_iota(jnp.int32, (blk, cols), 0)
    qry_i = lax.broadcasted_iota(jnp.int32, (blk, cols), 1)
    own_ok = key_i <= (qry_i % qb) + q_off
    own_start = pl.multiple_of(own * blk, blk)
    kv_rows = [slice(kv * HEAD_DIM, (kv + 1) * HEAD_DIM) for kv in range(C_KV_HEADS)]

    state = []
    for kv, rows in enumerate(kv_rows):
        qs_t = jnp.concatenate(
            [q_t[(kv * C_GROUP + g) * HEAD_DIM:(kv * C_GROUP + g + 1) * HEAD_DIM, :] for g in range(C_GROUP)], axis=1)
        kmp = jnp.concatenate([km[:, rows], jnp.zeros((nb_pad - nblk, HEAD_DIM), F32)], axis=0)
        sg = jnp.where(past, _dot_hi(kmp, qs_t), NEG_INF)
        rank = jnp.zeros((nb_pad, cols), jnp.int32)
        for j in range(nblk):
            rj = sg[j:j + 1, :]
            rank = rank + ((rj > sg) | ((rj == sg) & (blk_row > j))).astype(jnp.int32)
        sel = (past & (rank < MOBA_TOPK)).astype(F32)
        qsb = (qs_t * ATT_SCALE).astype(BF16)
        s = _dot(k_ref[pl.ds(own_start, blk), rows], qsb)
        s = jnp.where(own_ok, s, NEG_INF)
        m0 = jnp.max(s, axis=0, keepdims=True)
        p = jnp.exp(s - m0)
        l0 = jnp.sum(p, axis=0, keepdims=True)
        acc0 = _dot(vt_ref[rows, pl.ds(own_start, blk)].astype(BF16), p.astype(BF16))
        state.append((sel, qsb, m0, l0, acc0))
    for kv, (sel, qsb, m0, l0, acc0) in enumerate(state):
        sel_ref[kv], qs_ref[kv], m_ref[kv], l_ref[kv], acc_ref[kv] = sel, qsb, m0, l0, acc0

    def body(j, carry):
        start = pl.multiple_of(j * blk, blk)
        kvs = range(C_KV_HEADS)
        chosen = [sel_ref[kv, pl.ds(j, 1), :] > 0.0 for kv in kvs]
        m_old = [m_ref[kv] for kv in kvs]
        l_old = [l_ref[kv] for kv in kvs]
        acc_old = [acc_ref[kv] for kv in kvs]
        s = [_dot(k_ref[pl.ds(start, blk), kv_rows[kv]], qs_ref[kv]) for kv in kvs]
        s = [jnp.where(chosen[kv], s[kv], NEG_INF) for kv in kvs]
        m_new = [jnp.maximum(m_old[kv], jnp.max(s[kv], axis=0, keepdims=True)) for kv in kvs]
        alpha = [jnp.exp(m_old[kv] - m_new[kv]) for kv in kvs]
        p = [jnp.exp(s[kv] - m_new[kv]) for kv in kvs]
        l_new = [alpha[kv] * l_old[kv] + jnp.sum(p[kv], axis=0, keepdims=True) for kv in kvs]
        pv = [_dot(vt_ref[kv_rows[kv], pl.ds(start, blk)].astype(BF16), p[kv].astype(BF16)) for kv in kvs]
        for kv in kvs:
            m_ref[kv] = m_new[kv]
            l_ref[kv] = l_new[kv]
            acc_ref[kv] = alpha[kv] * acc_old[kv] + pv[kv]
        return carry

    lax.fori_loop(0, own, body, 0)
    o_t = jnp.concatenate([acc_ref[kv] / l_ref[kv] for kv in range(C_KV_HEADS)], axis=0).T
    o_ref[...] = jnp.concatenate(
        [o_t[g * qb:(g + 1) * qb, kv * HEAD_DIM:(kv + 1) * HEAD_DIM]
         for kv in range(C_KV_HEADS) for g in range(C_GROUP)], axis=-1)


def _moba_prompt(q, k_rows, v_t, kmean, nseq, seqlen):
    qb = MOBA_QTILE
    nq = seqlen // qb
    nblk = kmean.shape[1]
    cols = C_GROUP * qb
    per_seq = lambda b, i: (b, 0, 0)
    return pl.pallas_call(
        _moba_prompt_kernel,
        grid=(nseq, nq),
        in_specs=[pl.BlockSpec((qb, C_QW), lambda b, i: (b * nq + i, 0)),
                  pl.BlockSpec((seqlen, C_KVW), lambda b, i: (b, 0)),
                  pl.BlockSpec((None, C_KVW, seqlen), per_seq),
                  pl.BlockSpec((None, nblk, C_KVW), per_seq)],
        out_specs=pl.BlockSpec((qb, C_QW), lambda b, i: (b * nq + i, 0)),
        out_shape=jax.ShapeDtypeStruct(q.shape, F32),
        scratch_shapes=[pltpu.VMEM((C_KV_HEADS, -(-nblk // 16) * 16, cols), F32),
                        pltpu.VMEM((C_KV_HEADS, HEAD_DIM, cols), BF16),
                        pltpu.VMEM((C_KV_HEADS, 1, cols), F32), pltpu.VMEM((C_KV_HEADS, 1, cols), F32),
                        pltpu.VMEM((C_KV_HEADS, HEAD_DIM, cols), F32)],
        compiler_params=_cparams("parallel", "parallel"),
        name="moba_prompt",
    )(q, k_rows, v_t, kmean)


def _swa_sample_kernel(sink_ref, q_ref, kn_ref, vn_ref, wk_ref, wv_ref, o_ref, nk_ref, nv_ref):
    q = q_ref[...]
    kn, vn = kn_ref[...], vn_ref[...]
    wk, wv = wk_ref[...], wv_ref[...]
    tb, w = wk.shape[0], wk.shape[1]
    nk_ref[:, 0:w - 1, :] = wk[:, 1:w, :]
    nk_ref[:, w - 1:w, :] = kn[:, None, :]
    nv_ref[:, 0:w - 1, :] = wv[:, 1:w, :]
    nv_ref[:, w - 1:w, :] = vn[:, None, :]
    lane = lax.broadcasted_iota(jnp.int32, (tb, LANES), 1)
    lane3 = lax.broadcasted_iota(jnp.int32, (tb, w, LANES), 2)
    heads = [None] * A_HEADS
    for g in range(A_GROUP):
        qg = jnp.concatenate(
            [q[:, (kv * A_GROUP + g) * HEAD_DIM:(kv * A_GROUP + g + 1) * HEAD_DIM] for kv in range(A_KV_HEADS)], axis=-1)
        prod = wk * qg[:, None, :]
        prod_n = kn * qg
        p_full = None
        pn_full = None
        for kv in range(A_KV_HEADS):
            lo, hi = kv * HEAD_DIM, (kv + 1) * HEAD_DIM
            sink = sink_ref[kv * A_GROUP + g]
            s = jnp.sum(prod[:, :, lo:hi], axis=-1, keepdims=True) * ATT_SCALE
            sn = jnp.sum(prod_n[:, lo:hi], axis=-1, keepdims=True) * ATT_SCALE
            m = jnp.maximum(jnp.maximum(jnp.max(s, axis=1), sn), sink)
            p = jnp.exp(s - m[:, None, :])
            pn = jnp.exp(sn - m)
            inv = 1.0 / (jnp.sum(p, axis=1) + pn + jnp.exp(sink - m))
            p = p * inv[:, None, :]
            pn = pn * inv
            pb = jnp.broadcast_to(p, (tb, w, LANES))
            pnb = jnp.broadcast_to(pn, (tb, LANES))
            if kv == 0:
                p_full, pn_full = pb, pnb
            else:
                p_full = jnp.where(lane3 >= lo, pb, p_full)
                pn_full = jnp.where(lane >= lo, pnb, pn_full)
        og = jnp.sum(p_full * wv, axis=1) + pn_full * vn
        for kv in range(A_KV_HEADS):
            heads[kv * A_GROUP + g] = og[:, kv * HEAD_DIM:(kv + 1) * HEAD_DIM]
    o_ref[...] = jnp.concatenate(heads, axis=-1)


def _swa_sample(q, kn, vn, win_k, win_v, sinks):
    db, w = win_k.shape[0], win_k.shape[1]
    tb = math.gcd(db, 8)
    row = lambda i: (i, 0)
    row3 = lambda i: (i, 0, 0)
    return pl.pallas_call(
        _swa_sample_kernel,
        grid=(db // tb,),
        in_specs=[pl.BlockSpec(memory_space=pltpu.SMEM),
                  pl.BlockSpec((tb, A_QW), row), pl.BlockSpec((tb, A_KVW), row), pl.BlockSpec((tb, A_KVW), row),
                  pl.BlockSpec((tb, w, A_KVW), row3), pl.BlockSpec((tb, w, A_KVW), row3)],
        out_specs=[pl.BlockSpec((tb, A_QW), row), pl.BlockSpec((tb, w, A_KVW), row3),
                   pl.BlockSpec((tb, w, A_KVW), row3)],
        out_shape=[jax.ShapeDtypeStruct((db, A_QW), F32), jax.ShapeDtypeStruct(win_k.shape, F32),
                   jax.ShapeDtypeStruct(win_v.shape, F32)],
        compiler_params=_cparams("parallel"),
        name="swa_sample",
    )(sinks, q, kn, vn, win_k, win_v)


def _ssd_sample_pre_kernel(xbc_ref, cst_ref, dt_ref, cw_ref, cb_ref, dtb_ref, alog_ref, exp_ref,
                           xs_ref, bm_ref, cm_ref, xd_ref, dec_ref, cnew_ref):
    xn = xbc_ref[...]
    halo = MB_CONV - 1
    acc = cb_ref[...] + xn * cw_ref[halo:halo + 1, :]
    for tap in range(halo):
        acc = acc + cst_ref[tap] * cw_ref[tap:tap + 1, :]
    for tap in range(1, halo):
        cnew_ref[tap - 1] = cst_ref[tap]
    cnew_ref[halo - 1] = xn
    xbc = _silu(acc)
    xs = xbc[:, :MB_INNER]
    xs_ref[...] = xs
    bm_ref[...] = xbc[:, MB_INNER:MB_INNER + MB_GN]
    cm_ref[...] = xbc[:, MB_INNER + MB_GN:]
    dt = _softplus(dt_ref[...] + dtb_ref[...])
    da = dt * (-jnp.exp(alog_ref[...]))
    expand = exp_ref[...]
    xd_ref[...] = xs * _dot_exact_lhs_rhs(dt, expand)
    dec_ref[...] = jnp.exp(_dot_exact_lhs_rhs(da, expand))


def _dot_exact_lhs_rhs(a, b_bf16):
    a0 = a.astype(BF16)
    r = a - a0.astype(F32)
    a1 = r.astype(BF16)
    a2 = (r - a1.astype(F32)).astype(BF16)
    return _dot(a0, b_bf16) + (_dot(a1, b_bf16) + _dot(a2, b_bf16))


def _ssd_sample_state_kernel(h_ref, xd_ref, dec_ref, bm_ref, cm_ref, hn_ref, y_ref, xdt_ref, dect_ref, yt_ref):
    i = pl.program_id(0)
    tb = h_ref.shape[0]
    rpg = MB_HEADS // MB_GROUPS

    @pl.when(i == 0)
    def _():
        xdt_ref[...] = xd_ref[...].T
        dect_ref[...] = dec_ref[...].T
        yt_ref[...] = jnp.zeros_like(yt_ref)

    lane = lax.broadcasted_iota(jnp.int32, yt_ref.shape, 1)
    yt = yt_ref[...]
    for t in range(tb):
        xcol = _select_column(xdt_ref[...], i * tb + t)
        dcol = _select_column(dect_ref[...], i * tb + t)
        ycols = []
        for h in range(MB_HEADS):
            g = h // rpg
            rows = slice(h * MB_HEADDIM, (h + 1) * MB_HEADDIM)
            hn = h_ref[t, h] * dcol[rows, :] + xcol[rows, :] * bm_ref[t, g:g + 1, :]
            hn_ref[t, h] = hn
            ycols.append(jnp.sum(hn * cm_ref[t, g:g + 1, :], axis=-1, keepdims=True))
        yt = jnp.where(lane == i * tb + t, jnp.concatenate(ycols, axis=0), yt)
    yt_ref[...] = yt

    @pl.when(i == pl.num_programs(0) - 1)
    def _():
        y_ref[...] = yt_ref[...].T


def _ssd_sample_post_kernel(y_ref, xs_ref, z_ref, dskip_ref, ng_ref, o_ref):
    y = (y_ref[...] + xs_ref[...] * dskip_ref[...]) * _silu(z_ref[...])
    gw = MB_INNER // MB_GROUPS
    ng = ng_ref[...]
    o_ref[...] = jnp.concatenate(
        [_rms(y[:, g * gw:(g + 1) * gw], ng[:, g * gw:(g + 1) * gw]) for g in range(MB_GROUPS)], axis=-1)


def _ssd_sample(xbc, z, dt, conv_state, h0, conv_w, conv_b, dt_bias, a_log, d_skip, norm_g):
    db = xbc.shape[0]
    halo = MB_CONV - 1
    expand = (jnp.arange(LANES)[:, None] == (jnp.arange(MB_INNER) // MB_HEADDIM)[None, :]).astype(BF16)
    full = lambda *shape: pl.BlockSpec(shape, lambda: (0,) * len(shape))
    xs, bm, cm, xd, dec, conv_new = pl.pallas_call(
        _ssd_sample_pre_kernel,
        in_specs=[full(db, MB_CONV_DIM), full(halo, db, MB_CONV_DIM), full(db, LANES),
                  full(MB_CONV, MB_CONV_DIM), full(1, MB_CONV_DIM), full(1, LANES), full(1, LANES),
                  full(LANES, MB_INNER)],
        out_specs=[full(db, MB_INNER), full(db, MB_GN), full(db, MB_GN), full(db, MB_INNER), full(db, MB_INNER),
                   full(halo, db, MB_CONV_DIM)],
        out_shape=[jax.ShapeDtypeStruct((db, MB_INNER), F32), jax.ShapeDtypeStruct((db, MB_GN), F32),
                   jax.ShapeDtypeStruct((db, MB_GN), F32), jax.ShapeDtypeStruct((db, MB_INNER), F32),
                   jax.ShapeDtypeStruct((db, MB_INNER), F32), jax.ShapeDtypeStruct((halo, db, MB_CONV_DIM), F32)],
        compiler_params=pltpu.CompilerParams(vmem_limit_bytes=VMEM_LIMIT_BYTES),
        name="ssd_sample_pre",
    )(xbc, jnp.swapaxes(conv_state, 0, 1), dt, conv_w, conv_b.reshape(1, -1), _pad_lanes(dt_bias), _pad_lanes(a_log), expand)

    tb = math.gcd(db, 8)
    r3 = lambda i: (i, 0, 0)
    r4 = lambda i: (i, 0, 0, 0)
    whole = pl.BlockSpec((db, MB_INNER), lambda i: (0, 0))
    h_new, y = pl.pallas_call(
        _ssd_sample_state_kernel,
        grid=(db // tb,),
        in_specs=[pl.BlockSpec((tb, MB_HEADS, MB_HEADDIM, MB_DSTATE), r4), whole, whole,
                  pl.BlockSpec((tb, MB_GROUPS, MB_DSTATE), r3), pl.BlockSpec((tb, MB_GROUPS, MB_DSTATE), r3)],
        out_specs=[pl.BlockSpec((tb, MB_HEADS, MB_HEADDIM, MB_DSTATE), r4), whole],
        out_shape=[jax.ShapeDtypeStruct(h0.shape, F32), jax.ShapeDtypeStruct((db, MB_INNER), F32)],
        scratch_shapes=[pltpu.VMEM((MB_INNER, db), F32)] * 3,
        compiler_params=_cparams("arbitrary"),
        name="ssd_sample_state",
    )(h0, xd, dec, bm.reshape(db, MB_GROUPS, MB_DSTATE), cm.reshape(db, MB_GROUPS, MB_DSTATE))

    o_b = pl.pallas_call(
        _ssd_sample_post_kernel,
        in_specs=[full(db, MB_INNER), full(db, MB_INNER), full(db, MB_INNER), full(1, MB_INNER), full(1, MB_INNER)],
        out_specs=full(db, MB_INNER),
        out_shape=jax.ShapeDtypeStruct((db, MB_INNER), F32),
        name="ssd_sample_post",
    )(y, xs, z, jnp.repeat(d_skip, MB_HEADDIM).reshape(1, -1), norm_g.reshape(1, -1))
    return o_b, jnp.swapaxes(conv_new, 0, 1), h_new


def _select_column(x_t, b):
    lane = lax.broadcasted_iota(jnp.int32, x_t.shape, 1)
    return jnp.sum(jnp.where(lane == b, x_t, 0.0), axis=1, keepdims=True)


def _moba_gate_sample_kernel(pt_ref, q_ref, ck_ref, idx_ref, pbuf, sem, qt_ref, km_ref,
                             *, pages_per_block, pages_per_chunk, layer):
    b, c = pl.program_id(0), pl.program_id(1)
    nchunks = pl.num_programs(1)
    t = b * nchunks + c
    total = pl.num_programs(0) * nchunks
    nbuf = pbuf.shape[0]
    slot = t % nbuf
    ppb, cpp = pages_per_block, pages_per_chunk
    bpc = cpp // ppb
    nblk = nchunks * bpc

    def copies(tt):
        bb, cc, sl = tt // nchunks, tt % nchunks, tt % nbuf
        return [pltpu.make_async_copy(ck_ref.at[layer, pt_ref[bb, cc * cpp + e]], pbuf.at[sl, e], sem.at[sl])
                for e in range(cpp)]

    @pl.when(t == 0)
    def _():
        qt_ref[...] = q_ref[...].T

    for ahead in range(nbuf - 1):
        @pl.when((t == 0) & (ahead < total))
        def _(ahead=ahead):
            for cp in copies(ahead):
                cp.start()

    @pl.when(t + nbuf - 1 < total)
    def _():
        for cp in copies(t + nbuf - 1):
            cp.start()

    for cp in copies(t):
        cp.wait()

    @pl.when(c == 0)
    def _():
        km_ref[...] = jnp.zeros_like(km_ref)

    lane = lax.broadcasted_iota(jnp.int32, km_ref.shape, 1)
    km = km_ref[...]
    for jb in range(bpc):
        pg = pbuf[slot, jb * ppb]
        for e in range(1, ppb):
            pg = pg + pbuf[slot, jb * ppb + e]
        mean = jnp.sum(pg, axis=1, keepdims=True) * (1.0 / MOBA_BLOCK)
        km = jnp.where(lane == c * bpc + jb, mean, km)
    km_ref[...] = km

    @pl.when(c == nchunks - 1)
    def _():
        qcol = _select_column(qt_ref[...], b)
        km = km_ref[...]
        lane1 = lax.broadcasted_iota(jnp.int32, (1, LANES), 1)
        out = jnp.zeros((1, LANES), jnp.int32)
        heads = range(C_HEADS)
        s = [jnp.where(lane1 < nblk,
                       jnp.sum(km[(h // C_GROUP) * HEAD_DIM:(h // C_GROUP + 1) * HEAD_DIM, :]
                               * qcol[h * HEAD_DIM:(h + 1) * HEAD_DIM, :], axis=0, keepdims=True),
                       -jnp.inf) for h in heads]
        for k in range(MOBA_TOPK):
            m = [jnp.max(s[h], axis=1, keepdims=True) for h in heads]
            idx = [jnp.min(jnp.where(s[h] == m[h], lane1, LANES), axis=1, keepdims=True) for h in heads]
            s = [jnp.where(lane1 == idx[h], -jnp.inf, s[h]) for h in heads]
            for h in heads:
                out = jnp.where(lane1 == h * MOBA_TOPK + k, idx[h], out)
        idx_ref[0] = out


def _moba_gate_sample(page_table, q, cache_kt, layer):
    db, n_pages = page_table.shape
    ppb = MOBA_BLOCK // PAGE_SIZE
    cpp = math.gcd(n_pages, 64)
    nblk = n_pages // ppb
    assert cpp % ppb == 0 and MOBA_TOPK <= nblk <= LANES and C_HEADS * MOBA_TOPK <= LANES
    idx = pl.pallas_call(
        functools.partial(_moba_gate_sample_kernel, pages_per_block=ppb, pages_per_chunk=cpp, layer=layer),
        grid_spec=pltpu.PrefetchScalarGridSpec(
            num_scalar_prefetch=1,
            grid=(db, n_pages // cpp),
            in_specs=[pl.BlockSpec((db, C_QW), lambda b, c, pt: (0, 0)), pl.BlockSpec(memory_space=pl.ANY)],
            out_specs=pl.BlockSpec((1, 1, LANES), lambda b, c, pt: (b, 0, 0)),
            scratch_shapes=[pltpu.VMEM((3, cpp, C_KVW, PAGE_SIZE), F32), pltpu.SemaphoreType.DMA((3,)),
                            pltpu.VMEM((C_QW, db), F32), pltpu.VMEM((C_KVW, LANES), F32)]),
        out_shape=jax.ShapeDtypeStruct((db, 1, LANES), jnp.int32),
        compiler_params=_cparams("arbitrary", "arbitrary"),
        name="moba_gate_sample",
    )(page_table, q, cache_kt)
    return idx.reshape(db, LANES)


def _moba_sample_kernel(idx_ref, pt_ref, q_ref, kn_ref, vn_ref, ck_ref, cv_ref, o_ref, kbuf, vbuf, sem,
                        qt_ref, knt_ref, vnt_ref, ot_ref, *, pages_per_block, layer):
    b = pl.program_id(0)
    nb = pl.num_programs(0)
    slot = b % 2
    ppb = pages_per_block
    npg = MOBA_TOPK * ppb

    def copies(bb, sl):
        out = []
        for h in range(C_HEADS):
            rows = pl.ds((h // C_GROUP) * HEAD_DIM, HEAD_DIM)
            for k in range(MOBA_TOPK):
                blk = idx_ref[bb, h * MOBA_TOPK + k]
                for e in range(ppb):
                    page = pt_ref[bb, blk * ppb + e]
                    dst = h * npg + k * ppb + e
                    out.append(pltpu.make_async_copy(ck_ref.at[layer, page, rows, :], kbuf.at[sl, dst], sem.at[sl, 0]))
                    out.append(pltpu.make_async_copy(cv_ref.at[layer, page, rows, :], vbuf.at[sl, dst], sem.at[sl, 1]))
        return out

    @pl.when(b == 0)
    def _():
        for cp in copies(0, 0):
            cp.start()
        qt_ref[...] = q_ref[...].T
        knt_ref[...] = kn_ref[...].T
        vnt_ref[...] = vn_ref[...].T
        ot_ref[...] = jnp.zeros_like(ot_ref)

    @pl.when(b + 1 < nb)
    def _():
        for cp in copies(b + 1, 1 - slot):
            cp.start()

    for cp in copies(b, slot):
        cp.wait()

    qcol = _select_column(qt_ref[...], b) * ATT_SCALE
    kncol = _select_column(knt_ref[...], b)
    vncol = _select_column(vnt_ref[...], b)
    heads = range(C_HEADS)
    hrows = [slice(h * HEAD_DIM, (h + 1) * HEAD_DIM) for h in heads]
    kvrows = [slice((h // C_GROUP) * HEAD_DIM, (h // C_GROUP + 1) * HEAD_DIM) for h in heads]
    s = [[jnp.sum(kbuf[slot, h * npg + pg] * qcol[hrows[h], :], axis=0, keepdims=True) for pg in range(npg)]
         for h in heads]
    sn = [jnp.sum(qcol[hrows[h], :] * kncol[kvrows[h], :], axis=0, keepdims=True) for h in heads]
    m = [jnp.maximum(sn[h], jnp.max(functools.reduce(jnp.maximum, s[h]), axis=1, keepdims=True)) for h in heads]
    p = [[jnp.exp(row - m[h]) for row in s[h]] for h in heads]
    pn = [jnp.exp(sn[h] - m[h]) for h in heads]
    denom = [pn[h] + jnp.sum(functools.reduce(jnp.add, p[h]), axis=1, keepdims=True) for h in heads]
    acc = [functools.reduce(jnp.add, [vbuf[slot, h * npg + pg] * p[h][pg] for pg in range(npg)]) for h in heads]
    ocols = [(jnp.sum(acc[h], axis=1, keepdims=True) + pn[h] * vncol[kvrows[h], :]) / denom[h] for h in heads]
    lane = lax.broadcasted_iota(jnp.int32, ot_ref.shape, 1)
    ot_ref[...] = jnp.where(lane == b, jnp.concatenate(ocols, axis=0), ot_ref[...])

    @pl.when(b == nb - 1)
    def _():
        o_ref[...] = ot_ref[...].T


def _moba_sample(idx, page_table, q, kn, vn, cache_kt, cache_vt, layer):
    db = q.shape[0]
    ppb = MOBA_BLOCK // PAGE_SIZE
    nbuf = C_HEADS * MOBA_TOPK * ppb
    full = lambda *shape: pl.BlockSpec(shape, lambda b, *_: (0,) * len(shape))
    return pl.pallas_call(
        functools.partial(_moba_sample_kernel, pages_per_block=ppb, layer=layer),
        grid_spec=pltpu.PrefetchScalarGridSpec(
            num_scalar_prefetch=2,
            grid=(db,),
            in_specs=[full(db, C_QW), full(db, C_KVW), full(db, C_KVW),
                      pl.BlockSpec(memory_space=pl.ANY), pl.BlockSpec(memory_space=pl.ANY)],
            out_specs=full(db, C_QW),
            scratch_shapes=[pltpu.VMEM((2, nbuf, HEAD_DIM, PAGE_SIZE), F32),
                            pltpu.VMEM((2, nbuf, HEAD_DIM, PAGE_SIZE), F32),
                            pltpu.SemaphoreType.DMA((2, 2)),
                            pltpu.VMEM((C_QW, db), F32), pltpu.VMEM((C_KVW, db), F32),
                            pltpu.VMEM((C_KVW, db), F32), pltpu.VMEM((C_QW, db), F32)]),
        out_shape=jax.ShapeDtypeStruct((db, C_QW), F32),
        compiler_params=_cparams("arbitrary"),
        name="moba_sample",
    )(idx, page_table, q, kn, vn, cache_kt, cache_vt)


def _decoder(st, x, pos, p, state):
    is_prompt = state is None
    nseq, seqlen, d = st.nseq, st.seqlen, st.d
    depth = p['w_ada'].shape[0]
    rope_tabs = _rope_tables(pos if is_prompt else jnp.broadcast_to(pos, (st.tokens,)))
    new = {'win_k': [], 'win_v': [], 'conv': [], 'ssm': [], 'k': [], 'v': []}
    spare = None

    def proj(layer, x, *args):
        outs = _proj(st, layer, x, *args)
        return (outs[:-1], outs[-1]) if isinstance(x, tuple) else (outs, x)

    for layer in range(depth):
        g_norm = p['norm_g'][layer]
        if layer % 2 == 0:
            i = layer // 2
            splits = ((0, A_QW, True, True), (A_QW, A_KVW, True, True), (A_QW + A_KVW, A_KVW, False, True),
                      (A_QW + 2 * A_KVW, MB_INNER, False, True),
                      (A_QW + 2 * A_KVW + MB_INNER, MB_CONV_DIM, False, True),
                      (A_QW + 2 * A_KVW + MB_INNER + MB_CONV_DIM, LANES, False, True))
            (qa, ka, va, z, xbc, dt), x = proj(layer, x, g_norm[0], p['w_in_a'][i], rope_tabs, splits)
            if is_prompt:
                o_a = _swa_prompt(qa, ka, va, p['sinks'][i], nseq, seqlen)
                wk = ka.reshape(nseq, seqlen, A_KV_HEADS, HEAD_DIM)[:, -WINDOW:]
                wv = va.reshape(nseq, seqlen, A_KV_HEADS, HEAD_DIM)[:, -WINDOW:]
                conv_prev = jnp.zeros((nseq, MB_CONV - 1, MB_CONV_DIM), F32)
                h0 = jnp.zeros((nseq, MB_HEADS, MB_HEADDIM, MB_DSTATE), F32)
                o_b, conv_new, h_new = _ssd_prompt(xbc, z, dt, conv_prev, h0, p['conv_w'][i], p['conv_b'][i],
                                                   p['dt_bias'][i], p['a_log'][i], p['d_skip'][i],
                                                   p['ssm_norm_g'][i], nseq, seqlen)
            else:
                o_a, wk, wv = _swa_sample(qa, ka, va, state['win_k'][i].reshape(nseq, WINDOW, A_KVW),
                                          state['win_v'][i].reshape(nseq, WINDOW, A_KVW), p['sinks'][i])
                wk = wk.reshape(nseq, WINDOW, A_KV_HEADS, HEAD_DIM)
                wv = wv.reshape(nseq, WINDOW, A_KV_HEADS, HEAD_DIM)
                o_b, conv_new, h_new = _ssd_sample(xbc, z, dt, state['conv'][i], state['ssm'][i], p['conv_w'][i],
                                                   p['conv_b'][i], p['dt_bias'][i], p['a_log'][i], p['d_skip'][i],
                                                   p['ssm_norm_g'][i])
            a_list = [o_a, o_b]
            w_list = [p['w_out_a'][i][:A_QW], p['w_out_a'][i][A_QW:]]
            new['win_k'].append(wk)
            new['win_v'].append(wv)
            new['conv'].append(conv_new)
            new['ssm'].append(h_new)
        else:
            j = layer // 2
            cols = ((0, C_QW, True), (C_QW, C_KVW, True), (C_QW + C_KVW, C_KVW, False))
            if is_prompt:
                splits = tuple(c + (e,) for c, e in zip(cols, (True, False, False)))
                extras = (('bf16', 1), ('transposed', 1), ('transposed', 2), ('blockmean', 1))
                (qc, k_rows, k_t, v_t, kmean), x = proj(layer, x, g_norm[0], p['w_in_c'][j], rope_tabs, splits, extras)
                o_c = _moba_prompt(qc, k_rows, v_t, kmean.reshape(nseq, seqlen // MOBA_BLOCK, C_KVW), nseq, seqlen)
                kc, vc = (jnp.transpose(t.reshape(nseq, C_KV_HEADS, HEAD_DIM, seqlen), (0, 3, 1, 2)) for t in (k_t, v_t))
            else:
                splits = tuple(c + (True,) for c in cols)
                (qc, kc, vc), x = proj(layer, x, g_norm[0], p['w_in_c'][j], rope_tabs, splits)
                idx = _moba_gate_sample(state['page_table'], qc, state['cache_k'], j)
                o_c = _moba_sample(idx, state['page_table'], qc, kc, vc, state['cache_k'], state['cache_v'], j)
                kc, vc = (t.reshape(nseq, seqlen, C_KV_HEADS, HEAD_DIM) for t in (kc, vc))
            a_list = [o_c]
            w_list = [p['w_out_c'][j]]
            new['k'].append(kc)
            new['v'].append(vc)
        sorted_moe = st.tokens >= 2 * SORT_TILE
        x1, h2, aux = _mix(st, layer, x, a_list, w_list, g_norm[1], p['wr_pad'], p['rb_col'], sorted_moe)
        moe_args = (st, layer, h2, aux, p['w_gate_up'], p['w_down'], x1, p['final_norm_g'], layer == depth - 1)
        if sorted_moe:
            x, spare = _moe_sorted(*moe_args, spare=spare)
        else:
            x = _moe(*moe_args)
    return x.reshape(nseq, seqlen, d), {name: jnp.stack(rows) for name, rows in new.items()}


def kernel(x_prompt, x_sample, c_prompt, c_sample, state_win_k, state_win_v, state_conv, state_ssm, cache_k, cache_v, page_table, w_ada, b_ada, norm_g, w_in_a, sinks, conv_w, conv_b, dt_bias, a_log, d_skip, ssm_norm_g, w_out_a, w_in_c, w_out_c, w_router, router_bias, w_gate, w_up, w_down, final_norm_g):
    nb, seqlen, d = x_prompt.shape
    db, dec_seq, _ = x_sample.shape
    assert dec_seq == 1 and seqlen % MOBA_BLOCK == 0 and d % LANES == 0
    n_odd, n_pool, page_size, ckv, hd = cache_k.shape
    assert page_size == PAGE_SIZE and ckv == C_KV_HEADS and hd == HEAD_DIM
    n_pages = page_table.shape[1]

    n_in_a = w_in_a.shape[2]
    pad_a = (-n_in_a) % LANES
    p = {'w_ada': w_ada, 'norm_g': norm_g, 'sinks': sinks, 'conv_w': conv_w, 'conv_b': conv_b, 'dt_bias': dt_bias,
         'a_log': a_log, 'd_skip': d_skip, 'ssm_norm_g': ssm_norm_g, 'final_norm_g': final_norm_g,
         'w_in_a': jnp.pad(w_in_a, ((0, 0), (0, 0), (0, pad_a))).astype(BF16),
         'w_out_a': w_out_a.astype(BF16), 'w_in_c': w_in_c.astype(BF16), 'w_out_c': w_out_c.astype(BF16),
         'w_gate_up': jnp.concatenate([w_gate, w_up], axis=-1).astype(BF16), 'w_down': w_down.astype(BF16),
         'wr_pad': jnp.pad(w_router, ((0, 0), (0, LANES - N_EXPERTS))),
         'rb_col': jnp.pad(router_bias, (0, LANES - N_EXPERTS)).reshape(LANES, 1)}

    mod = _ada(jnp.concatenate([c_prompt, c_sample], axis=0), w_ada, b_ada)
    st_p = _Stream(nb, seqlen, d, mod[:, :nb])
    st_s = _Stream(db, 1, d, mod[:, nb:])

    pos_p = jnp.arange(seqlen, dtype=jnp.int32)
    y_prompt, pn = _decoder(st_p, x_prompt.reshape(nb * seqlen, d), pos_p, p, None)

    cache_kt = jnp.transpose(cache_k, (0, 1, 3, 4, 2)).reshape(n_odd, n_pool, C_KVW, PAGE_SIZE)
    cache_vt = jnp.transpose(cache_v, (0, 1, 3, 4, 2)).reshape(n_odd, n_pool, C_KVW, PAGE_SIZE)
    state = {'win_k': state_win_k, 'win_v': state_win_v, 'conv': state_conv, 'ssm': state_ssm,
             'cache_k': cache_kt, 'cache_v': cache_vt, 'page_table': page_table}
    pos_s = n_pages * PAGE_SIZE + jnp.arange(1, dtype=jnp.int32)
    y_sample, sn = _decoder(st_s, x_sample.reshape(db, d), pos_s, p, state)
    return (y_prompt, y_sample, pn['win_k'], pn['win_v'], pn['conv'], pn['ssm'], pn['k'], pn['v'],
            sn['win_k'], sn['win_v'], sn['conv'], sn['ssm'], sn['k'], sn['v'])
```

```python
import functools
import math

import jax
import jax.numpy as jnp
import numpy as np
from jax import lax
from jax.experimental import pallas as pl
from jax.experimental.pallas import tpu as pltpu

F32 = jnp.float32
BF16 = jnp.bfloat16

HEAD_DIM = 64
ROT_DIM = HEAD_DIM // 4
ROPE_THETA = 500000.0
A_HEADS = 8
A_KV_HEADS = 2
A_GROUP = A_HEADS // A_KV_HEADS
WINDOW = 128
MB_HEADDIM = 64
MB_HEADS = 8
MB_INNER = MB_HEADS * MB_HEADDIM
MB_GROUPS = 2
MB_DSTATE = 128
MB_CONV = 4
MB_GN = MB_GROUPS * MB_DSTATE
MB_CONV_DIM = MB_INNER + 2 * MB_GN
SSD_CHUNK = 128
C_HEADS = 16
C_KV_HEADS = 4
C_GROUP = C_HEADS // C_KV_HEADS
MOBA_BLOCK = 256
MOBA_TOPK = 3
MOBA_QTILE = MOBA_BLOCK
PAGE_SIZE = 128
A_QW = A_HEADS * HEAD_DIM
A_KVW = A_KV_HEADS * HEAD_DIM
C_QW = C_HEADS * HEAD_DIM
C_KVW = C_KV_HEADS * HEAD_DIM
N_EXPERTS = 16
N_EXPERT_GROUPS = 4
EXPERTS_PER_GROUP = N_EXPERTS // N_EXPERT_GROUPS
PAIRS_PER_GROUP = EXPERTS_PER_GROUP * (EXPERTS_PER_GROUP - 1) // 2
N_BUCKETS = N_EXPERT_GROUPS * PAIRS_PER_GROUP
BUCKET_ROWS = 32
ROUTE_ROWS = 8
SORT_TILE = 256
ROW_DMA_UNROLL = 8
RMS_EPS = 1e-6
NEG_INF = -1e30
ATT_SCALE = HEAD_DIM ** -0.5

LANES = 128
VMEM_LIMIT_BYTES = 56 * 1024 * 1024


def _cparams(*sem):
    return pltpu.CompilerParams(dimension_semantics=sem, vmem_limit_bytes=VMEM_LIMIT_BYTES)


def _dot(a, b):
    return jnp.dot(a, b, preferred_element_type=F32)


def _dot_nt(a, b):
    return lax.dot_general(a, b, (((1,), (1,)), ((), ())), preferred_element_type=F32)


def _dot_tn(a, b):
    return lax.dot_general(a, b, (((0,), (0,)), ((), ())), preferred_element_type=F32)


def _split2(x):
    hi = x.astype(BF16)
    lo = (x - hi.astype(F32)).astype(BF16)
    return hi, lo


def _dot_hi(a, b):
    ah, al = _split2(a)
    bh, bl = _split2(b)
    return _dot(ah, bh) + (_dot(al, bh) + _dot(ah, bl))


def _dot_hi_nt(a, b):
    ah, al = _split2(a)
    bh, bl = _split2(b)
    return _dot_nt(ah, bh) + (_dot_nt(al, bh) + _dot_nt(ah, bl))


def _dot_exact_lhs(a_bf16, b):
    b0 = b.astype(BF16)
    r = b - b0.astype(F32)
    b1 = r.astype(BF16)
    b2 = (r - b1.astype(F32)).astype(BF16)
    return _dot(a_bf16, b0) + (_dot(a_bf16, b1) + _dot(a_bf16, b2))


def _sigmoid(x):
    return 1.0 / (1.0 + jnp.exp(-x))


def _silu(x):
    return x * _sigmoid(x)


def _softplus(x):
    return jnp.maximum(x, 0.0) + jnp.log(1.0 + jnp.exp(-jnp.abs(x)))


def _rms(x, g):
    return x * lax.rsqrt(jnp.mean(x * x, axis=-1, keepdims=True) + RMS_EPS) * g


def _rope_tables(pos):
    half = ROT_DIM // 2
    inv_freq = jnp.power(ROPE_THETA, -jnp.arange(half, dtype=F32) / half)
    ang = pos.astype(F32)[:, None] * inv_freq
    cos, sin = jnp.cos(ang), jnp.sin(ang)
    n = pos.shape[0]
    rest = HEAD_DIM - ROT_DIM
    ct = jnp.concatenate([cos, cos, jnp.ones((n, rest), F32)], axis=1)
    sa = jnp.concatenate([jnp.zeros((n, half), F32), sin, jnp.zeros((n, rest), F32)], axis=1)
    sb = jnp.concatenate([-sin, jnp.zeros((n, half), F32), jnp.zeros((n, rest), F32)], axis=1)
    rep = LANES // HEAD_DIM
    return jnp.tile(ct, (1, rep)), jnp.tile(sa, (1, rep)), jnp.tile(sb, (1, rep))


def _rope(x, ct, sa, sb):
    half = ROT_DIM // 2
    return x * ct + pltpu.roll(x, half, 1) * sa + pltpu.roll(x, LANES - half, 1) * sb


def _ada_kernel(c_ref, w_ref, b_ref, o_ref):
    o_ref[...] = _dot_hi(_silu(c_ref[...]), w_ref[...]) + b_ref[...]


def _ada(c_all, w_ada, b_ada):
    depth, d, n6 = w_ada.shape
    nc = c_all.shape[0]
    nk = n6 // d
    return pl.pallas_call(
        _ada_kernel,
        grid=(depth, nk),
        in_specs=[pl.BlockSpec((nc, d), lambda l, k: (0, 0)),
                  pl.BlockSpec((None, d, d), lambda l, k: (l, 0, k)),
                  pl.BlockSpec((None, 1, d), lambda l, k: (l, 0, k))],
        out_specs=pl.BlockSpec((None, nc, d), lambda l, k: (l, 0, k)),
        out_shape=jax.ShapeDtypeStruct((depth, nc, n6), F32),
        compiler_params=_cparams("parallel", "parallel"),
        name="ada",
    )(c_all, w_ada, b_ada.reshape(depth, 1, n6))


class _Stream:
    def __init__(self, nseq, seqlen, d, mod):
        self.nseq, self.seqlen, self.d = nseq, seqlen, d
        self.tokens = nseq * seqlen
        if seqlen == 1:
            self.tm = self.tokens
            self.tiles_per_seq = None
            self.mod = mod
        else:
            self.tm = math.gcd(seqlen, 512)
            self.tiles_per_seq = seqlen // self.tm
            depth = mod.shape[0]
            self.mod = mod.reshape(depth, nseq * 6, 1, d)
        self.ntiles = self.tokens // self.tm

    def mod_arg(self, layer):
        return self.mod[layer]

    def mod_spec(self, k):
        if self.tiles_per_seq is None:
            return pl.BlockSpec((self.tm, self.d), lambda i, *_: (0, k))
        tps = self.tiles_per_seq
        return pl.BlockSpec((None, 1, self.d), lambda i, *_: ((i // tps) * 6 + k, 0, 0))

    def rope_spec(self):
        if self.tiles_per_seq is None:
            return pl.BlockSpec((self.tm, LANES), lambda i, *_: (0, 0))
        tps = self.tiles_per_seq
        return pl.BlockSpec((self.tm, LANES), lambda i, *_: (i % tps, 0))


def _proj_kernel(x_ref, *refs, splits, extras):
    _proj_compute(x_ref[...], *refs, splits=splits, extras=extras)


def _proj_combine_kernel(dest_ref, x1_ref, g2_ref, ys_ref, *refs, splits, extras):
    *refs, x_out_ref, buf, sem = refs
    i = pl.program_id(0)
    n = pl.num_programs(0)
    tm = x1_ref.shape[0]
    slot = i % 2

    def row_copy(tile, r, sl):
        return pltpu.make_async_copy(ys_ref.at[pl.ds(dest_ref[tile * tm + r], 1)], buf.at[sl, pl.ds(r, 1)], sem.at[sl])

    def tile_wait(sl):
        pltpu.make_async_copy(ys_ref.at[pl.ds(0, tm)], buf.at[sl], sem.at[sl]).wait()

    @pl.when(i == 0)
    def _():
        def body(g, carry):
            for j in range(ROW_DMA_UNROLL):
                row_copy(0, g * ROW_DMA_UNROLL + j, 0).start(priority=j % 2)
            return carry

        lax.fori_loop(0, tm // ROW_DMA_UNROLL, body, 0)

    tile_wait(slot)
    nxt = jnp.minimum(i + 1, n - 1)
    for r in range(tm):
        row_copy(nxt, r, 1 - slot).start(priority=r % 2)
    x = x1_ref[...] + g2_ref[...] * buf[slot]
    x_out_ref[...] = x
    _proj_compute(x, *refs, splits=splits, extras=extras)

    @pl.when(i == n - 1)
    def _():
        tile_wait(1 - slot)


def _proj_compute(x, sh_ref, sc_ref, g_ref, w_ref, ct_ref, sa_ref, sb_ref, *out_refs, splits, extras):
    h = _rms(x, g_ref[...]) * (1.0 + sc_ref[...]) + sh_ref[...]
    u = _dot(h.astype(BF16), w_ref[...])
    outs = list(out_refs)
    pieces = []
    for start, width, rope, emit in splits:
        if rope:
            ct, sa, sb = ct_ref[...], sa_ref[...], sb_ref[...]
            piece = jnp.concatenate([_rope(u[:, start + c0:start + c0 + LANES], ct, sa, sb)
                                     for c0 in range(0, width, LANES)], axis=-1)
        else:
            piece = u[:, start:start + width]
        pieces.append(piece)
        if emit:
            outs.pop(0)[...] = piece
    for kind, idx in extras:
        o_ref, piece = outs.pop(0), pieces[idx]
        if kind == 'bf16':
            o_ref[...] = piece.astype(BF16)
        elif kind == 'transposed':
            o_ref[...] = piece.T
        else:
            for c in range(o_ref.shape[0]):
                o_ref[c] = jnp.sum(piece[c * MOBA_BLOCK:(c + 1) * MOBA_BLOCK, :], axis=0,
                                   keepdims=True) * (1.0 / MOBA_BLOCK)


def _proj(st, layer, x, norm_g, w_bf16, rope_tabs, splits, extras=()):
    d = st.d
    n = w_bf16.shape[1]
    tm = st.tm
    row = lambda i, *_: (i, 0)
    const = lambda i, *_: (0, 0)
    out_specs = [pl.BlockSpec((tm, w), row) for _, w, _, emit in splits if emit]
    out_shape = [jax.ShapeDtypeStruct((st.tokens, w), F32) for _, w, _, emit in splits if emit]
    for kind, idx in extras:
        w = splits[idx][1]
        if kind == 'bf16':
            out_specs.append(pl.BlockSpec((tm, w), row))
            out_shape.append(jax.ShapeDtypeStruct((st.tokens, w), BF16))
        elif kind == 'transposed':
            tps = st.tiles_per_seq
            out_specs.append(pl.BlockSpec((None, w, tm), lambda i, *_: (i // tps, 0, i % tps)))
            out_shape.append(jax.ShapeDtypeStruct((st.nseq, w, st.seqlen), F32))
        else:
            nb = tm // MOBA_BLOCK
            out_specs.append(pl.BlockSpec((nb, 1, w), lambda i, *_: (i, 0, 0)))
            out_shape.append(jax.ShapeDtypeStruct((st.tokens // MOBA_BLOCK, 1, w), F32))
    in_specs = [st.mod_spec(0), st.mod_spec(1), pl.BlockSpec((1, d), const), pl.BlockSpec((d, n), const),
                st.rope_spec(), st.rope_spec(), st.rope_spec()]
    args = (st.mod_arg(layer), st.mod_arg(layer), norm_g.reshape(1, d), w_bf16, *rope_tabs)
    if not isinstance(x, tuple):
        return pl.pallas_call(
            functools.partial(_proj_kernel, splits=splits, extras=extras),
            grid=(st.ntiles,),
            in_specs=[pl.BlockSpec((tm, d), row)] + in_specs,
            out_specs=out_specs,
            out_shape=out_shape,
            compiler_params=_cparams("parallel"),
            name="proj",
        )(x, *args)
    x1, ys, dest = x
    return pl.pallas_call(
        functools.partial(_proj_combine_kernel, splits=splits, extras=extras),
        grid_spec=pltpu.PrefetchScalarGridSpec(
            num_scalar_prefetch=1,
            grid=(st.ntiles,),
            in_specs=[pl.BlockSpec((tm, d), row), st.mod_spec(5), pl.BlockSpec(memory_space=pl.ANY)] + in_specs,
            out_specs=out_specs + [pl.BlockSpec((tm, d), row)],
            scratch_shapes=[pltpu.VMEM((2, tm, d), F32), pltpu.SemaphoreType.DMA((2,))]),
        out_shape=out_shape + [jax.ShapeDtypeStruct((st.tokens, d), F32)],
        compiler_params=_cparams("arbitrary"),
        name="proj_combine",
    )(dest, x1, st.mod_arg(layer - 1), ys, *args)


def _swa_prompt_kernel(sink_ref, q_ref, kp_ref, kc_ref, vp_ref, vc_ref, o_ref):
    i = pl.program_id(1)
    qb = q_ref.shape[0]
    cols = A_GROUP * qb
    q_t = q_ref[...].T
    kk = jnp.concatenate([kp_ref[...], kc_ref[...]], axis=0).astype(BF16)
    vv_t = jnp.concatenate([vp_ref[...], vc_ref[...]], axis=0).T.astype(BF16)
    key_i = lax.broadcasted_iota(jnp.int32, (2 * qb, cols), 0)
    qry_i = lax.broadcasted_iota(jnp.int32, (2 * qb, cols), 1) % qb
    diff = qry_i + qb - key_i
    ok = (diff >= 0) & (diff <= WINDOW) & ((key_i >= qb) | (i > 0))
    head_of_lane = lax.broadcasted_iota(jnp.int32, (1, cols), 1) // qb
    outs = []
    for kv in range(A_KV_HEADS):
        rows = slice(kv * HEAD_DIM, (kv + 1) * HEAD_DIM)
        qs_t = jnp.concatenate(
            [q_t[(kv * A_GROUP + g) * HEAD_DIM:(kv * A_GROUP + g + 1) * HEAD_DIM, :] for g in range(A_GROUP)], axis=1)
        s = _dot(kk[:, rows], (qs_t * ATT_SCALE).astype(BF16))
        s = jnp.where(ok, s, NEG_INF)
        sink = jnp.zeros((1, cols), F32)
        for g in range(A_GROUP):
            sink = jnp.where(head_of_lane == g, sink_ref[kv * A_GROUP + g], sink)
        m = jnp.maximum(jnp.max(s, axis=0, keepdims=True), sink)
        p = jnp.exp(s - m)
        denom = jnp.sum(p, axis=0, keepdims=True) + jnp.exp(sink - m)
        outs.append(_dot(vv_t[rows, :], p.astype(BF16)) / denom)
    o_t = jnp.concatenate(outs, axis=0).T
    o_ref[...] = jnp.concatenate(
        [o_t[g * qb:(g + 1) * qb, kv * HEAD_DIM:(kv + 1) * HEAD_DIM]
         for kv in range(A_KV_HEADS) for g in range(A_GROUP)], axis=-1)


def _swa_prompt(q, k, v, sinks, nseq, seqlen):
    qb = WINDOW
    nb = seqlen // qb
    cur = lambda b, i: (b * nb + i, 0)
    prev = lambda b, i: (b * nb + jnp.maximum(i - 1, 0), 0)
    return pl.pallas_call(
        _swa_prompt_kernel,
        grid=(nseq, nb),
        in_specs=[pl.BlockSpec(memory_space=pltpu.SMEM),
                  pl.BlockSpec((qb, A_QW), cur),
                  pl.BlockSpec((qb, A_KVW), prev), pl.BlockSpec((qb, A_KVW), cur),
                  pl.BlockSpec((qb, A_KVW), prev), pl.BlockSpec((qb, A_KVW), cur)],
        out_specs=pl.BlockSpec((qb, A_QW), cur),
        out_shape=jax.ShapeDtypeStruct(q.shape, F32),
        compiler_params=_cparams("parallel", "parallel"),
        name="swa_prompt",
    )(sinks, q, k, k, v, v)


def _ssd_prompt_kernel(xbc_ref, z_ref, dt_ref, cprev_ref, h0_ref, cw_ref, cb_ref, dtb_ref, alog_ref, dskip_ref,
                       ng_ref, o_ref, cnew_ref, hlast_ref, xp_ref, h_ref):
    c = pl.program_id(1)
    nc = pl.num_programs(1)
    q = SSD_CHUNK
    halo = MB_CONV - 1
    base = 8 - halo

    @pl.when(c == 0)
    def _():
        xp_ref[base:8, :] = cprev_ref[...]
        h_ref[...] = h0_ref[...]

    xp_ref[8:8 + q, :] = xbc_ref[...]
    acc = cb_ref[...] + xp_ref[base:base + q, :] * cw_ref[0:1, :]
    for tap in range(1, MB_CONV):
        acc = acc + xp_ref[base + tap:base + tap + q, :] * cw_ref[tap:tap + 1, :]
    tail = xp_ref[8 + q - halo:8 + q, :]
    xp_ref[base:8, :] = tail

    @pl.when(c == nc - 1)
    def _():
        cnew_ref[...] = tail

    xbc = _silu(acc)
    xs = xbc[:, :MB_INNER]
    dt = _softplus(dt_ref[...] + dtb_ref[...])
    da = dt * (-jnp.exp(alog_ref[...]))
    r_i = lax.broadcasted_iota(jnp.int32, (q, q), 0)
    c_i = lax.broadcasted_iota(jnp.int32, (q, q), 1)
    causal = r_i >= c_i
    acum = _dot_exact_lhs(causal.astype(BF16), da)
    acum_t = acum.T
    rpg = MB_HEADS // MB_GROUPS
    ys = []
    h_all = [h_ref[h] for h in range(MB_HEADS)]
    h_new = []
    for g in range(MB_GROUPS):
        bq = xbc[:, MB_INNER + g * MB_DSTATE:MB_INNER + (g + 1) * MB_DSTATE].astype(BF16)
        cq = xbc[:, MB_INNER + MB_GN + g * MB_DSTATE:MB_INNER + MB_GN + (g + 1) * MB_DSTATE].astype(BF16)
        cbm = _dot_nt(cq, bq)
        for r in range(rpg):
            h = g * rpg + r
            a_col = acum[:, h:h + 1]
            a_row = acum_t[h:h + 1, :]
            a_last = acum[q - 1:q, h:h + 1]
            decay = jnp.where(causal, jnp.exp(jnp.where(causal, a_col - a_row, 0.0)), 0.0)
            xd = xs[:, h * MB_HEADDIM:(h + 1) * MB_HEADDIM] * dt[:, h:h + 1]
            hprev = h_all[h]
            y = _dot((cbm * decay).astype(BF16), xd.astype(BF16))
            y = y + jnp.exp(a_col) * _dot_nt(cq, hprev.astype(BF16))
            xw = (xd * jnp.exp(a_last - a_col)).astype(BF16)
            h_new.append(hprev * jnp.exp(a_last) + _dot_tn(xw, bq))
            ys.append(y)
    for h in range(MB_HEADS):
        h_ref[h] = h_new[h]
    y = jnp.concatenate(ys, axis=-1) + xs * dskip_ref[...]
    y = y * _silu(z_ref[...])
    gw = MB_INNER // MB_GROUPS
    ng = ng_ref[...]
    o_ref[...] = jnp.concatenate(
        [_rms(y[:, g * gw:(g + 1) * gw], ng[:, g * gw:(g + 1) * gw]) for g in range(MB_GROUPS)], axis=-1)

    @pl.when(c == nc - 1)
    def _():
        hlast_ref[...] = h_ref[...]


def _pad_lanes(v, n=LANES):
    return jnp.pad(v, (0, n - v.shape[0])).reshape(1, n)


def _ssd_prompt(xbc, z, dt, conv_prev, h0, conv_w, conv_b, dt_bias, a_log, d_skip, norm_g, nseq, seqlen):
    q = SSD_CHUNK
    nc = seqlen // q
    row = lambda b, c: (b * nc + c, 0)
    per_seq3 = lambda b, c: (b, 0, 0)
    per_seq4 = lambda b, c: (b, 0, 0, 0)
    const = lambda b, c: (0, 0)
    halo = MB_CONV - 1
    return pl.pallas_call(
        _ssd_prompt_kernel,
        grid=(nseq, nc),
        in_specs=[pl.BlockSpec((q, MB_CONV_DIM), row), pl.BlockSpec((q, MB_INNER), row),
                  pl.BlockSpec((q, LANES), row),
                  pl.BlockSpec((None, halo, MB_CONV_DIM), per_seq3),
                  pl.BlockSpec((None, MB_HEADS, MB_HEADDIM, MB_DSTATE), per_seq4),
                  pl.BlockSpec((MB_CONV, MB_CONV_DIM), const), pl.BlockSpec((1, MB_CONV_DIM), const),
                  pl.BlockSpec((1, LANES), const), pl.BlockSpec((1, LANES), const),
                  pl.BlockSpec((1, MB_INNER), const), pl.BlockSpec((1, MB_INNER), const)],
        out_specs=[pl.BlockSpec((q, MB_INNER), row),
                   pl.BlockSpec((None, halo, MB_CONV_DIM), per_seq3),
                   pl.BlockSpec((None, MB_HEADS, MB_HEADDIM, MB_DSTATE), per_seq4)],
        out_shape=[jax.ShapeDtypeStruct((nseq * seqlen, MB_INNER), F32),
                   jax.ShapeDtypeStruct((nseq, halo, MB_CONV_DIM), F32),
                   jax.ShapeDtypeStruct((nseq, MB_HEADS, MB_HEADDIM, MB_DSTATE), F32)],
        scratch_shapes=[pltpu.VMEM((8 + q, MB_CONV_DIM), F32),
                        pltpu.VMEM((MB_HEADS, MB_HEADDIM, MB_DSTATE), F32)],
        compiler_params=_cparams("parallel", "arbitrary"),
        name="ssd_prompt",
    )(xbc, z, dt, conv_prev, h0, conv_w, conv_b.reshape(1, -1), _pad_lanes(dt_bias), _pad_lanes(a_log),
      jnp.repeat(d_skip, MB_HEADDIM).reshape(1, -1), norm_g.reshape(1, -1))


def _route(logits_t, rbias_col):
    tm = logits_t.shape[1]
    scores = _sigmoid(logits_t[0:N_EXPERTS, :])
    biased = scores + rbias_col[0:N_EXPERTS, :]
    s = [scores[e:e + 1, :] for e in range(N_EXPERTS)]
    b = [biased[e:e + 1, :] for e in range(N_EXPERTS)]
    epg = EXPERTS_PER_GROUP
    gscore = []
    for g in range(N_EXPERT_GROUPS):
        v = b[g * epg:(g + 1) * epg]
        best = None
        for i in range(epg):
            for j in range(i + 1, epg):
                pair = v[i] + v[j]
                best = pair if best is None else jnp.maximum(best, pair)
        gscore.append(best)
    gsel = jnp.zeros((1, tm), jnp.int32)
    gbest = gscore[0]
    for g in range(1, N_EXPERT_GROUPS):
        better = gscore[g] > gbest
        gsel = jnp.where(better, g, gsel)
        gbest = jnp.where(better, gscore[g], gbest)
    bs, ss = [], []
    for k in range(epg):
        bk, sk = b[k], s[k]
        for g in range(1, N_EXPERT_GROUPS):
            bk = jnp.where(gsel == g, b[g * epg + k], bk)
            sk = jnp.where(gsel == g, s[g * epg + k], sk)
        bs.append(bk)
        ss.append(sk)
    i1 = jnp.zeros((1, tm), jnp.int32)
    m1 = bs[0]
    for k in range(1, epg):
        better = bs[k] > m1
        i1 = jnp.where(better, k, i1)
        m1 = jnp.where(better, bs[k], m1)
    i2 = jnp.full((1, tm), -1, jnp.int32)
    m2 = jnp.full((1, tm), -jnp.inf, F32)
    for k in range(epg):
        better = (i1 != k) & ((bs[k] > m2) | (i2 < 0))
        i2 = jnp.where(better, k, i2)
        m2 = jnp.where(better, bs[k], m2)
    s1 = jnp.zeros((1, tm), F32)
    s2 = jnp.zeros((1, tm), F32)
    for k in range(epg):
        s1 = jnp.where(i1 == k, ss[k], s1)
        s2 = jnp.where(i2 == k, ss[k], s2)
    denom = s1 + s2
    rows = lax.broadcasted_iota(jnp.int32, (N_EXPERTS, tm), 0)
    comb = jnp.zeros((N_EXPERTS, tm), F32)
    for e in range(N_EXPERTS):
        g, k = divmod(e, epg)
        chosen = (gsel == g) & ((i1 == k) | (i2 == k))
        comb = jnp.where(rows == e, jnp.where(chosen, s[e] / denom, 0.0), comb)
    lo = jnp.minimum(i1, i2)
    hi = jnp.maximum(i1, i2)
    pair = jnp.where(lo == 0, 0, jnp.where(lo == 1, epg - 1, 2 * epg - 3)) + (hi - lo - 1)
    bucket = (gsel * PAIRS_PER_GROUP + pair).astype(F32)
    first_is_lo = i1 < i2
    w_lo = jnp.where(first_is_lo, s1, s2) / denom
    w_hi = jnp.where(first_is_lo, s2, s1) / denom
    return comb, bucket, w_lo, w_hi


def _mix_kernel(*refs, n_in, sorted_moe):
    x_ref, g1_ref = refs[0], refs[1]
    a_refs = refs[2:2 + n_in]
    w_refs = refs[2 + n_in:2 + 2 * n_in]
    ng_ref, sc_ref, sh_ref, wr_ref, rb_ref, x1_ref, h2_ref, aux_ref = refs[2 + 2 * n_in:]
    mix = _dot(a_refs[0][...].astype(BF16), w_refs[0][...])
    for a_ref, w_ref in zip(a_refs[1:], w_refs[1:]):
        mix = mix + _dot(a_ref[...].astype(BF16), w_ref[...])
    x1 = x_ref[...] + g1_ref[...] * mix
    x1_ref[...] = x1
    h2 = _rms(x1, ng_ref[...]) * (1.0 + sc_ref[...]) + sh_ref[...]
    h2_hi, h2_lo = _split2(h2)
    w_hi, w_lo = _split2(wr_ref[...])
    both = _dot(h2_hi, jnp.concatenate([w_hi, w_lo], axis=1))
    logits = both[:, :LANES] + (_dot(h2_lo, w_hi) + both[:, LANES:])
    comb_t, bucket, w_lo, w_hi = _route(logits.T, rb_ref[...])
    tm, d = x1.shape
    if sorted_moe:
        r = lax.broadcasted_iota(jnp.int32, (LANES, tm), 0)
        rt = jnp.where(r == 0, bucket, jnp.where(r == 1, w_lo, jnp.where(r == 2, w_hi, 0.0)))
        h2_ref[:, :d] = h2
        h2_ref[:, d:] = rt.T
        aux_ref[...] = rt[0:aux_ref.shape[0], :]
    else:
        h2_ref[...] = h2.astype(BF16)
        aux_ref[...] = jnp.concatenate([comb_t, jnp.zeros((LANES - N_EXPERTS, tm), F32)], axis=0).T


def _mix(st, layer, x, a_list, w_list, norm_g, wr_pad, rb_col, sorted_moe):
    d, tm = st.d, st.tm
    row = lambda i: (i, 0)
    const = lambda i: (0, 0)
    n_in = len(a_list)
    in_specs = [pl.BlockSpec((tm, d), row), st.mod_spec(2)]
    in_specs += [pl.BlockSpec((tm, a.shape[1]), row) for a in a_list]
    in_specs += [pl.BlockSpec(w.shape, const) for w in w_list]
    in_specs += [pl.BlockSpec((1, d), const), st.mod_spec(4), st.mod_spec(3),
                 pl.BlockSpec((d, LANES), const), pl.BlockSpec((LANES, 1), const)]
    m = st.mod_arg(layer)
    if sorted_moe:
        out_specs = [pl.BlockSpec((tm, d), row), pl.BlockSpec((tm, d + LANES), row),
                     pl.BlockSpec((ROUTE_ROWS, tm), lambda i: (0, i))]
        out_shape = [jax.ShapeDtypeStruct((st.tokens, d), F32), jax.ShapeDtypeStruct((st.tokens, d + LANES), F32),
                     jax.ShapeDtypeStruct((ROUTE_ROWS, st.tokens), F32)]
    else:
        out_specs = [pl.BlockSpec((tm, d), row), pl.BlockSpec((tm, d), row), pl.BlockSpec((tm, LANES), row)]
        out_shape = [jax.ShapeDtypeStruct((st.tokens, d), F32), jax.ShapeDtypeStruct((st.tokens, d), BF16),
                     jax.ShapeDtypeStruct((st.tokens, LANES), F32)]
    return pl.pallas_call(
        functools.partial(_mix_kernel, n_in=n_in, sorted_moe=sorted_moe),
        grid=(st.ntiles,),
        in_specs=in_specs,
        out_specs=out_specs,
        out_shape=out_shape,
        compiler_params=_cparams("parallel"),
        name="mix",
    )(x, m, *a_list, *w_list, norm_g.reshape(1, d), m, m, wr_pad, rb_col)


def _gated_up(h, wgu_ref):
    u = _dot(h, wgu_ref[...])
    dff = u.shape[1] // 2
    return _silu(u[:, :dff]) * u[:, dff:]


def _moe_kernel(h_ref, comb_ref, wgu_ref, wd_ref, x1_ref, g2_ref, fg_ref, o_ref, acc_ref, *, final):
    e = pl.program_id(1)

    @pl.when(e == 0)
    def _():
        acc_ref[...] = jnp.zeros_like(acc_ref)

    he = _gated_up(h_ref[...], wgu_ref)
    comb = comb_ref[...]
    lane = lax.broadcasted_iota(jnp.int32, comb.shape, 1)
    ce = jnp.sum(jnp.where(lane == e, comb, 0.0), axis=-1, keepdims=True)
    acc_ref[...] += ce * _dot(he.astype(BF16), wd_ref[...])

    @pl.when(e == pl.num_programs(1) - 1)
    def _():
        x2 = x1_ref[...] + g2_ref[...] * acc_ref[...]
        o_ref[...] = _rms(x2, fg_ref[...]) if final else x2


def _moe(st, layer, h2, comb, wgu, wd, x1, final_g, final):
    d, tm = st.d, st.tm
    ne, dff = wd.shape[1:3]
    row = lambda i, e: (i, 0)
    return pl.pallas_call(
        functools.partial(_moe_kernel, final=final),
        grid=(st.ntiles, ne),
        in_specs=[pl.BlockSpec((tm, d), row), pl.BlockSpec((tm, LANES), row),
                  pl.BlockSpec((None, None, d, 2 * dff), lambda i, e: (layer, e, 0, 0)),
                  pl.BlockSpec((None, None, dff, d), lambda i, e: (layer, e, 0, 0)),
                  pl.BlockSpec((tm, d), row), st.mod_spec(5),
                  pl.BlockSpec((1, d), lambda i, e: (0, 0))],
        out_specs=pl.BlockSpec((tm, d), row),
        out_shape=jax.ShapeDtypeStruct((st.tokens, d), F32),
        scratch_shapes=[pltpu.VMEM((tm, d), F32)],
        compiler_params=_cparams("parallel", "arbitrary"),
        name="moe",
    )(h2, comb, wgu, wd, x1, st.mod_arg(layer), final_g.reshape(1, d))


def _bucket_rank_kernel(rt_ref, rank_ref, cnt_ref, carry_ref):
    i = pl.program_id(0)
    tm = rt_ref.shape[1]

    @pl.when(i == 0)
    def _():
        carry_ref[...] = jnp.zeros_like(carry_ref)

    bucket = rt_ref[0:1, :]
    rows = lax.broadcasted_iota(jnp.int32, (BUCKET_ROWS, tm), 0).astype(F32)
    onehot = rows == bucket
    s_i = lax.broadcasted_iota(jnp.int32, (tm, tm), 0)
    t_i = lax.broadcasted_iota(jnp.int32, (tm, tm), 1)
    incl = jnp.where(s_i <= t_i, 1.0, 0.0).astype(BF16)
    cum = _dot(jnp.where(onehot, 1.0, 0.0).astype(BF16), incl)
    carry = carry_ref[...]
    rank = jnp.sum(jnp.where(onehot, cum - 1.0 + carry, 0.0), axis=0, keepdims=True)
    rank_ref[...] = rank.astype(jnp.int32)
    carry = carry + cum[:, tm - 1:tm]
    carry_ref[...] = carry
    cnt_ref[...] = jnp.broadcast_to(carry, cnt_ref.shape)


def _bucket_rank(st, rt):
    tm = st.tm
    return pl.pallas_call(
        _bucket_rank_kernel,
        grid=(st.ntiles,),
        in_specs=[pl.BlockSpec((ROUTE_ROWS, tm), lambda i: (0, i))],
        out_specs=[pl.BlockSpec((1, tm), lambda i: (0, i)), pl.BlockSpec((BUCKET_ROWS, LANES), lambda i: (0, 0))],
        out_shape=[jax.ShapeDtypeStruct((1, st.tokens), jnp.int32),
                   jax.ShapeDtypeStruct((BUCKET_ROWS, LANES), F32)],
        scratch_shapes=[pltpu.VMEM((BUCKET_ROWS, 1), F32)],
        compiler_params=_cparams("arbitrary"),
        name="bucket_rank",
    )(rt)


def _sort_plan(rt, rank, cnt, tokens):
    ts = SORT_TILE
    bucket = rt[0].astype(jnp.int32)
    counts = cnt[:N_BUCKETS, 0].astype(jnp.int32)
    padded = (counts + ts - 1) // ts * ts
    ends = jnp.cumsum(padded)
    dest = (ends - padded)[bucket] + rank[0]
    n_tiles = -(-tokens // ts) + N_BUCKETS
    n_used = ends[-1] // ts
    tile = jnp.arange(n_tiles, dtype=jnp.int32)
    tile_row = jnp.minimum(tile, n_used - 1) * ts
    tile_bucket = jnp.sum((ends[None, :] <= tile_row[:, None]).astype(jnp.int32), axis=1)
    tile_bucket = jnp.minimum(tile_bucket, N_BUCKETS - 1)
    pairs = [(a, b) for a in range(EXPERTS_PER_GROUP) for b in range(a + 1, EXPERTS_PER_GROUP)]
    lo_tab = jnp.array([a for a, _ in pairs], jnp.int32)
    hi_tab = jnp.array([b for _, b in pairs], jnp.int32)
    base = tile_bucket // PAIRS_PER_GROUP * EXPERTS_PER_GROUP
    pair = tile_bucket % PAIRS_PER_GROUP
    return dest, base + lo_tab[pair], base + hi_tab[pair], n_used.reshape(1).astype(jnp.int32), n_tiles


def _dispatch_kernel(dest_ref, h_ref, init_ref, out_ref, sem):
    del init_ref
    i = pl.program_id(0)
    tm = h_ref.shape[0]

    def body(g, carry):
        for j in range(ROW_DMA_UNROLL):
            r = g * ROW_DMA_UNROLL + j
            pltpu.make_async_copy(h_ref.at[pl.ds(r, 1)], out_ref.at[pl.ds(dest_ref[i * tm + r], 1)],
                                  sem).start(priority=j % 2)
        return carry

    lax.fori_loop(0, tm // ROW_DMA_UNROLL, body, 0)
    pltpu.make_async_copy(h_ref, out_ref.at[pl.ds(0, tm)], sem).wait()


def _dispatch(st, dest, h2w, n_rows, init):
    tm = st.tm
    w = h2w.shape[1]
    return pl.pallas_call(
        _dispatch_kernel,
        grid_spec=pltpu.PrefetchScalarGridSpec(
            num_scalar_prefetch=1,
            grid=(st.ntiles,),
            in_specs=[pl.BlockSpec((tm, w), lambda i, d: (i, 0)), pl.BlockSpec(memory_space=pl.ANY)],
            out_specs=pl.BlockSpec(memory_space=pl.ANY),
            scratch_shapes=[pltpu.SemaphoreType.DMA(())]),
        out_shape=jax.ShapeDtypeStruct((n_rows, w), F32),
        input_output_aliases={2: 0},
        compiler_params=_cparams("arbitrary"),
        name="moe_dispatch",
    )(dest, h2w, jnp.zeros((n_rows, w), F32) if init is None else init)


def _experts_kernel(elo_ref, ehi_ref, nused_ref, hs_ref, wgul_ref, wdl_ref, wguh_ref, wdh_ref, y_ref):
    del elo_ref, ehi_ref
    i = pl.program_id(0)
    d = y_ref.shape[1]

    @pl.when(i < nused_ref[0])
    def _():
        h = hs_ref[:, :d].astype(BF16)

        def expert(wgu_ref, wd_ref):
            return _dot(_gated_up(h, wgu_ref).astype(BF16), wd_ref[...])

        y_ref[...] = (hs_ref[:, d + 1:d + 2] * expert(wgul_ref, wdl_ref)
                      + hs_ref[:, d + 2:d + 3] * expert(wguh_ref, wdh_ref))

    @pl.when(i >= nused_ref[0])
    def _():
        y_ref[...] = jnp.zeros_like(y_ref)


def _experts(layer, hs, e_lo, e_hi, n_used, n_tiles, wgu, wd):
    ts = SORT_TILE
    _, _, dff, d = wd.shape
    lo = lambda i, elo, ehi, nu: (layer, elo[i], 0, 0)
    hi = lambda i, elo, ehi, nu: (layer, ehi[i], 0, 0)
    up = lambda idx: pl.BlockSpec((None, None, d, 2 * dff), idx)
    down = lambda idx: pl.BlockSpec((None, None, dff, d), idx)
    return pl.pallas_call(
        _experts_kernel,
        grid_spec=pltpu.PrefetchScalarGridSpec(
            num_scalar_prefetch=3,
            grid=(n_tiles,),
            in_specs=[pl.BlockSpec((ts, hs.shape[1]), lambda i, *_: (i, 0)),
                      up(lo), down(lo), up(hi), down(hi)],
            out_specs=pl.BlockSpec((ts, d), lambda i, *_: (i, 0))),
        out_shape=jax.ShapeDtypeStruct((n_tiles * ts, d), F32),
        compiler_params=_cparams("arbitrary"),
        name="moe_experts",
    )(e_lo, e_hi, n_used, hs, wgu, wd, wgu, wd)


def _combine_kernel(dest_ref, x1_ref, g2_ref, fg_ref, ys_ref, o_ref, buf, sem, *, final):
    i = pl.program_id(0)
    n = pl.num_programs(0)
    tm = x1_ref.shape[0]
    slot = i % 2

    def start(tile, sl):
        def body(g, carry):
            for j in range(ROW_DMA_UNROLL):
                r = g * ROW_DMA_UNROLL + j
                pltpu.make_async_copy(ys_ref.at[pl.ds(dest_ref[tile * tm + r], 1)], buf.at[sl, pl.ds(r, 1)],
                                      sem.at[sl]).start(priority=j % 2)
            return carry

        lax.fori_loop(0, tm // ROW_DMA_UNROLL, body, 0)

    @pl.when(i == 0)
    def _():
        start(0, 0)

    @pl.when(i + 1 < n)
    def _():
        start(i + 1, 1 - slot)

    pltpu.make_async_copy(ys_ref.at[pl.ds(0, tm)], buf.at[slot], sem.at[slot]).wait()
    x2 = x1_ref[...] + g2_ref[...] * buf[slot]
    o_ref[...] = _rms(x2, fg_ref[...]) if final else x2


def _combine(st, layer, dest, ys, x1, final_g, final):
    d, tm = st.d, st.tm
    row = lambda i, dst: (i, 0)
    return pl.pallas_call(
        functools.partial(_combine_kernel, final=final),
        grid_spec=pltpu.PrefetchScalarGridSpec(
            num_scalar_prefetch=1,
            grid=(st.ntiles,),
            in_specs=[pl.BlockSpec((tm, d), row), st.mod_spec(5), pl.BlockSpec((1, d), lambda i, dst: (0, 0)),
                      pl.BlockSpec(memory_space=pl.ANY)],
            out_specs=pl.BlockSpec((tm, d), row),
            scratch_shapes=[pltpu.VMEM((2, tm, d), F32), pltpu.SemaphoreType.DMA((2,))]),
        out_shape=jax.ShapeDtypeStruct((st.tokens, d), F32),
        compiler_params=_cparams("arbitrary"),
        name="moe_combine",
    )(dest, x1, st.mod_arg(layer), final_g.reshape(1, d), ys)


def _moe_sorted(st, layer, h2w, rt, wgu, wd, x1, final_g, final, spare=None):
    rank, cnt = _bucket_rank(st, rt)
    dest, e_lo, e_hi, n_used, n_tiles = _sort_plan(rt, rank, cnt, st.tokens)
    hs = _dispatch(st, dest, h2w, n_tiles * SORT_TILE, spare)
    ys = _experts(layer, hs, e_lo, e_hi, n_used, n_tiles, wgu, wd)
    if not final:
        return (x1, ys, dest), hs
    return _combine(st, layer, dest, ys, x1, final_g, final), hs


def _moba_prompt_kernel(q_ref, k_ref, vt_ref, km_ref, o_ref, sel_ref, qs_ref, m_ref, l_ref, acc_ref):
    i = pl.program_id(1)
    qb = MOBA_QTILE
    blk = MOBA_BLOCK
    nblk = km_ref.shape[0]
    own = (i * qb) // blk
    q_off = i * qb - own * blk
    q_t = q_ref[...].T
    km = km_ref[...]
    cols = C_GROUP * qb
    nb_pad = sel_ref.shape[1]
    blk_row = lax.broadcasted_iota(jnp.int32, (nb_pad, cols), 0)
    past = blk_row < own
    key_i = lax.broadcasted_iota(jnp.int32, (blk, cols), 0)
    qry_i = lax.broadcasted_iota(jnp.int32, (blk, cols), 1)
    own_ok = key_i <= (qry_i % qb) + q_off
    own_start = pl.multiple_of(own * blk, blk)
    kv_rows = [slice(kv * HEAD_DIM, (kv + 1) * HEAD_DIM) for kv in range(C_KV_HEADS)]

    state = []
    for kv, rows in enumerate(kv_rows):
        qs_t = jnp.concatenate(
            [q_t[(kv * C_GROUP + g) * HEAD_DIM:(kv * C_GROUP + g + 1) * HEAD_DIM, :] for g in range(C_GROUP)], axis=1)
        kmp = jnp.concatenate([km[:, rows], jnp.zeros((nb_pad - nblk, HEAD_DIM), F32)], axis=0)
        sg = jnp.where(past, _dot_hi(kmp, qs_t), NEG_INF)
        rank = jnp.zeros((nb_pad, cols), jnp.int32)
        for j in range(nblk):
            rj = sg[j:j + 1, :]
            rank = rank + ((rj > sg) | ((rj == sg) & (blk_row > j))).astype(jnp.int32)
        sel = (past & (rank < MOBA_TOPK)).astype(F32)
        qsb = (qs_t * ATT_SCALE).astype(BF16)
        s = _dot(k_ref[pl.ds(own_start, blk), rows], qsb)
        s = jnp.where(own_ok, s, NEG_INF)
        m0 = jnp.max(s, axis=0, keepdims=True)
        p = jnp.exp(s - m0)
        l0 = jnp.sum(p, axis=0, keepdims=True)
        acc0 = _dot(vt_ref[rows, pl.ds(own_start, blk)].astype(BF16), p.astype(BF16))
        state.append((sel, qsb, m0, l0, acc0))
    for kv, (sel, qsb, m0, l0, acc0) in enumerate(state):
        sel_ref[kv], qs_ref[kv], m_ref[kv], l_ref[kv], acc_ref[kv] = sel, qsb, m0, l0, acc0

    def body(j, carry):
        start = pl.multiple_of(j * blk, blk)
        kvs = range(C_KV_HEADS)
        chosen = [sel_ref[kv, pl.ds(j, 1), :] > 0.0 for kv in kvs]
        m_old = [m_ref[kv] for kv in kvs]
        l_old = [l_ref[kv] for kv in kvs]
        acc_old = [acc_ref[kv] for kv in kvs]
        s = [_dot(k_ref[pl.ds(start, blk), kv_rows[kv]], qs_ref[kv]) for kv in kvs]
        s = [jnp.where(chosen[kv], s[kv], NEG_INF) for kv in kvs]
        m_new = [jnp.maximum(m_old[kv], jnp.max(s[kv], axis=0, keepdims=True)) for kv in kvs]
        alpha = [jnp.exp(m_old[kv] - m_new[kv]) for kv in kvs]
        p = [jnp.exp(s[kv] - m_new[kv]) for kv in kvs]
        l_new = [alpha[kv] * l_old[kv] + jnp.sum(p[kv], axis=0, keepdims=True) for kv in kvs]
        pv = [_dot(vt_ref[kv_rows[kv], pl.ds(start, blk)].astype(BF16), p[kv].astype(BF16)) for kv in kvs]
        for kv in kvs:
            m_ref[kv] = m_new[kv]
            l_ref[kv] = l_new[kv]
            acc_ref[kv] = alpha[kv] * acc_old[kv] + pv[kv]
        return carry

    lax.fori_loop(0, own, body, 0)
    o_t = jnp.concatenate([acc_ref[kv] / l_ref[kv] for kv in range(C_KV_HEADS)], axis=0).T
    o_ref[...] = jnp.concatenate(
        [o_t[g * qb:(g + 1) * qb, kv * HEAD_DIM:(kv + 1) * HEAD_DIM]
         for kv in range(C_KV_HEADS) for g in range(C_GROUP)], axis=-1)


def _moba_prompt(q, k_rows, v_t, kmean, nseq, seqlen):
    qb = MOBA_QTILE
    nq = seqlen // qb
    nblk = kmean.shape[1]
    cols = C_GROUP * qb
    per_seq = lambda b, i: (b, 0, 0)
    return pl.pallas_call(
        _moba_prompt_kernel,
        grid=(nseq, nq),
        in_specs=[pl.BlockSpec((qb, C_QW), lambda b, i: (b * nq + i, 0)),
                  pl.BlockSpec((seqlen, C_KVW), lambda b, i: (b, 0)),
                  pl.BlockSpec((None, C_KVW, seqlen), per_seq),
                  pl.BlockSpec((None, nblk, C_KVW), per_seq)],
        out_specs=pl.BlockSpec((qb, C_QW), lambda b, i: (b * nq + i, 0)),
        out_shape=jax.ShapeDtypeStruct(q.shape, F32),
        scratch_shapes=[pltpu.VMEM((C_KV_HEADS, -(-nblk // 16) * 16, cols), F32),
                        pltpu.VMEM((C_KV_HEADS, HEAD_DIM, cols), BF16),
                        pltpu.VMEM((C_KV_HEADS, 1, cols), F32), pltpu.VMEM((C_KV_HEADS, 1, cols), F32),
                        pltpu.VMEM((C_KV_HEADS, HEAD_DIM, cols), F32)],
        compiler_params=_cparams("parallel", "parallel"),
        name="moba_prompt",
    )(q, k_rows, v_t, kmean)


def _swa_sample_kernel(sink_ref, q_ref, kn_ref, vn_ref, wk_ref, wv_ref, o_ref, nk_ref, nv_ref):
    q = q_ref[...]
    kn, vn = kn_ref[...], vn_ref[...]
    wk, wv = wk_ref[...], wv_ref[...]
    tb, w = wk.shape[0], wk.shape[1]
    nk_ref[:, 0:w - 1, :] = wk[:, 1:w, :]
    nk_ref[:, w - 1:w, :] = kn[:, None, :]
    nv_ref[:, 0:w - 1, :] = wv[:, 1:w, :]
    nv_ref[:, w - 1:w, :] = vn[:, None, :]
    lane = lax.broadcasted_iota(jnp.int32, (tb, LANES), 1)
    lane3 = lax.broadcasted_iota(jnp.int32, (tb, w, LANES), 2)
    heads = [None] * A_HEADS
    for g in range(A_GROUP):
        qg = jnp.concatenate(
            [q[:, (kv * A_GROUP + g) * HEAD_DIM:(kv * A_GROUP + g + 1) * HEAD_DIM] for kv in range(A_KV_HEADS)], axis=-1)
        prod = wk * qg[:, None, :]
        prod_n = kn * qg
        p_full = None
        pn_full = None
        for kv in range(A_KV_HEADS):
            lo, hi = kv * HEAD_DIM, (kv + 1) * HEAD_DIM
            sink = sink_ref[kv * A_GROUP + g]
            s = jnp.sum(prod[:, :, lo:hi], axis=-1, keepdims=True) * ATT_SCALE
            sn = jnp.sum(prod_n[:, lo:hi], axis=-1, keepdims=True) * ATT_SCALE
            m = jnp.maximum(jnp.maximum(jnp.max(s, axis=1), sn), sink)
            p = jnp.exp(s - m[:, None, :])
            pn = jnp.exp(sn - m)
            inv = 1.0 / (jnp.sum(p, axis=1) + pn + jnp.exp(sink - m))
            p = p * inv[:, None, :]
            pn = pn * inv
            pb = jnp.broadcast_to(p, (tb, w, LANES))
            pnb = jnp.broadcast_to(pn, (tb, LANES))
            if kv == 0:
                p_full, pn_full = pb, pnb
            else:
                p_full = jnp.where(lane3 >= lo, pb, p_full)
                pn_full = jnp.where(lane >= lo, pnb, pn_full)
        og = jnp.sum(p_full * wv, axis=1) + pn_full * vn
        for kv in range(A_KV_HEADS):
            heads[kv * A_GROUP + g] = og[:, kv * HEAD_DIM:(kv + 1) * HEAD_DIM]
    o_ref[...] = jnp.concatenate(heads, axis=-1)


def _swa_sample(q, kn, vn, win_k, win_v, sinks):
    db, w = win_k.shape[0], win_k.shape[1]
    tb = math.gcd(db, 8)
    row = lambda i: (i, 0)
    row3 = lambda i: (i, 0, 0)
    return pl.pallas_call(
        _swa_sample_kernel,
        grid=(db // tb,),
        in_specs=[pl.BlockSpec(memory_space=pltpu.SMEM),
                  pl.BlockSpec((tb, A_QW), row), pl.BlockSpec((tb, A_KVW), row), pl.BlockSpec((tb, A_KVW), row),
                  pl.BlockSpec((tb, w, A_KVW), row3), pl.BlockSpec((tb, w, A_KVW), row3)],
        out_specs=[pl.BlockSpec((tb, A_QW), row), pl.BlockSpec((tb, w, A_KVW), row3),
                   pl.BlockSpec((tb, w, A_KVW), row3)],
        out_shape=[jax.ShapeDtypeStruct((db, A_QW), F32), jax.ShapeDtypeStruct(win_k.shape, F32),
                   jax.ShapeDtypeStruct(win_v.shape, F32)],
        compiler_params=_cparams("parallel"),
        name="swa_sample",
    )(sinks, q, kn, vn, win_k, win_v)


def _ssd_sample_pre_kernel(xbc_ref, cst_ref, dt_ref, cw_ref, cb_ref, dtb_ref, alog_ref, exp_ref,
                           xs_ref, bm_ref, cm_ref, xd_ref, dec_ref, cnew_ref):
    xn = xbc_ref[...]
    halo = MB_CONV - 1
    acc = cb_ref[...] + xn * cw_ref[halo:halo + 1, :]
    for tap in range(halo):
        acc = acc + cst_ref[tap] * cw_ref[tap:tap + 1, :]
    for tap in range(1, halo):
        cnew_ref[tap - 1] = cst_ref[tap]
    cnew_ref[halo - 1] = xn
    xbc = _silu(acc)
    xs = xbc[:, :MB_INNER]
    xs_ref[...] = xs
    bm_ref[...] = xbc[:, MB_INNER:MB_INNER + MB_GN]
    cm_ref[...] = xbc[:, MB_INNER + MB_GN:]
    dt = _softplus(dt_ref[...] + dtb_ref[...])
    da = dt * (-jnp.exp(alog_ref[...]))
    expand = exp_ref[...]
    xd_ref[...] = xs * _dot_exact_lhs_rhs(dt, expand)
    dec_ref[...] = jnp.exp(_dot_exact_lhs_rhs(da, expand))


def _dot_exact_lhs_rhs(a, b_bf16):
    a0 = a.astype(BF16)
    r = a - a0.astype(F32)
    a1 = r.astype(BF16)
    a2 = (r - a1.astype(F32)).astype(BF16)
    return _dot(a0, b_bf16) + (_dot(a1, b_bf16) + _dot(a2, b_bf16))


def _ssd_sample_state_kernel(h_ref, xd_ref, dec_ref, bm_ref, cm_ref, hn_ref, y_ref, xdt_ref, dect_ref, yt_ref):
    i = pl.program_id(0)
    tb = h_ref.shape[0]
    rpg = MB_HEADS // MB_GROUPS

    @pl.when(i == 0)
    def _():
        xdt_ref[...] = xd_ref[...].T
        dect_ref[...] = dec_ref[...].T
        yt_ref[...] = jnp.zeros_like(yt_ref)

    lane = lax.broadcasted_iota(jnp.int32, yt_ref.shape, 1)
    yt = yt_ref[...]
    for t in range(tb):
        xcol = _select_column(xdt_ref[...], i * tb + t)
        dcol = _select_column(dect_ref[...], i * tb + t)
        ycols = []
        for h in range(MB_HEADS):
            g = h // rpg
            rows = slice(h * MB_HEADDIM, (h + 1) * MB_HEADDIM)
            hn = h_ref[t, h] * dcol[rows, :] + xcol[rows, :] * bm_ref[t, g:g + 1, :]
            hn_ref[t, h] = hn
            ycols.append(jnp.sum(hn * cm_ref[t, g:g + 1, :], axis=-1, keepdims=True))
        yt = jnp.where(lane == i * tb + t, jnp.concatenate(ycols, axis=0), yt)
    yt_ref[...] = yt

    @pl.when(i == pl.num_programs(0) - 1)
    def _():
        y_ref[...] = yt_ref[...].T


def _ssd_sample_post_kernel(y_ref, xs_ref, z_ref, dskip_ref, ng_ref, o_ref):
    y = (y_ref[...] + xs_ref[...] * dskip_ref[...]) * _silu(z_ref[...])
    gw = MB_INNER // MB_GROUPS
    ng = ng_ref[...]
    o_ref[...] = jnp.concatenate(
        [_rms(y[:, g * gw:(g + 1) * gw], ng[:, g * gw:(g + 1) * gw]) for g in range(MB_GROUPS)], axis=-1)


def _ssd_sample(xbc, z, dt, conv_state, h0, conv_w, conv_b, dt_bias, a_log, d_skip, norm_g):
    db = xbc.shape[0]
    halo = MB_CONV - 1
    expand = (jnp.arange(LANES)[:, None] == (jnp.arange(MB_INNER) // MB_HEADDIM)[None, :]).astype(BF16)
    full = lambda *shape: pl.BlockSpec(shape, lambda: (0,) * len(shape))
    xs, bm, cm, xd, dec, conv_new = pl.pallas_call(
        _ssd_sample_pre_kernel,
        in_specs=[full(db, MB_CONV_DIM), full(halo, db, MB_CONV_DIM), full(db, LANES),
                  full(MB_CONV, MB_CONV_DIM), full(1, MB_CONV_DIM), full(1, LANES), full(1, LANES),
                  full(LANES, MB_INNER)],
        out_specs=[full(db, MB_INNER), full(db, MB_GN), full(db, MB_GN), full(db, MB_INNER), full(db, MB_INNER),
                   full(halo, db, MB_CONV_DIM)],
        out_shape=[jax.ShapeDtypeStruct((db, MB_INNER), F32), jax.ShapeDtypeStruct((db, MB_GN), F32),
                   jax.ShapeDtypeStruct((db, MB_GN), F32), jax.ShapeDtypeStruct((db, MB_INNER), F32),
                   jax.ShapeDtypeStruct((db, MB_INNER), F32), jax.ShapeDtypeStruct((halo, db, MB_CONV_DIM), F32)],
        compiler_params=pltpu.CompilerParams(vmem_limit_bytes=VMEM_LIMIT_BYTES),
        name="ssd_sample_pre",
    )(xbc, jnp.swapaxes(conv_state, 0, 1), dt, conv_w, conv_b.reshape(1, -1), _pad_lanes(dt_bias), _pad_lanes(a_log), expand)

    tb = math.gcd(db, 8)
    r3 = lambda i: (i, 0, 0)
    r4 = lambda i: (i, 0, 0, 0)
    whole = pl.BlockSpec((db, MB_INNER), lambda i: (0, 0))
    h_new, y = pl.pallas_call(
        _ssd_sample_state_kernel,
        grid=(db // tb,),
        in_specs=[pl.BlockSpec((tb, MB_HEADS, MB_HEADDIM, MB_DSTATE), r4), whole, whole,
                  pl.BlockSpec((tb, MB_GROUPS, MB_DSTATE), r3), pl.BlockSpec((tb, MB_GROUPS, MB_DSTATE), r3)],
        out_specs=[pl.BlockSpec((tb, MB_HEADS, MB_HEADDIM, MB_DSTATE), r4), whole],
        out_shape=[jax.ShapeDtypeStruct(h0.shape, F32), jax.ShapeDtypeStruct((db, MB_INNER), F32)],
        scratch_shapes=[pltpu.VMEM((MB_INNER, db), F32)] * 3,
        compiler_params=_cparams("arbitrary"),
        name="ssd_sample_state",
    )(h0, xd, dec, bm.reshape(db, MB_GROUPS, MB_DSTATE), cm.reshape(db, MB_GROUPS, MB_DSTATE))

    o_b = pl.pallas_call(
        _ssd_sample_post_kernel,
        in_specs=[full(db, MB_INNER), full(db, MB_INNER), full(db, MB_INNER), full(1, MB_INNER), full(1, MB_INNER)],
        out_specs=full(db, MB_INNER),
        out_shape=jax.ShapeDtypeStruct((db, MB_INNER), F32),
        name="ssd_sample_post",
    )(y, xs, z, jnp.repeat(d_skip, MB_HEADDIM).reshape(1, -1), norm_g.reshape(1, -1))
    return o_b, jnp.swapaxes(conv_new, 0, 1), h_new


def _select_column(x_t, b):
    lane = lax.broadcasted_iota(jnp.int32, x_t.shape, 1)
    return jnp.sum(jnp.where(lane == b, x_t, 0.0), axis=1, keepdims=True)


def _moba_gate_sample_kernel(pt_ref, q_ref, ck_ref, idx_ref, pbuf, sem, qt_ref, km_ref,
                             *, pages_per_block, pages_per_chunk, layer):
    b, c = pl.program_id(0), pl.program_id(1)
    nchunks = pl.num_programs(1)
    t = b * nchunks + c
    total = pl.num_programs(0) * nchunks
    nbuf = pbuf.shape[0]
    slot = t % nbuf
    ppb, cpp = pages_per_block, pages_per_chunk
    bpc = cpp // ppb
    nblk = nchunks * bpc

    def copies(tt):
        bb, cc, sl = tt // nchunks, tt % nchunks, tt % nbuf
        return [pltpu.make_async_copy(ck_ref.at[layer, pt_ref[bb, cc * cpp + e]], pbuf.at[sl, e], sem.at[sl])
                for e in range(cpp)]

    @pl.when(t == 0)
    def _():
        qt_ref[...] = q_ref[...].T

    for ahead in range(nbuf - 1):
        @pl.when((t == 0) & (ahead < total))
        def _(ahead=ahead):
            for cp in copies(ahead):
                cp.start()

    @pl.when(t + nbuf - 1 < total)
    def _():
        for cp in copies(t + nbuf - 1):
            cp.start()

    for cp in copies(t):
        cp.wait()

    @pl.when(c == 0)
    def _():
        km_ref[...] = jnp.zeros_like(km_ref)

    lane = lax.broadcasted_iota(jnp.int32, km_ref.shape, 1)
    km = km_ref[...]
    for jb in range(bpc):
        pg = pbuf[slot, jb * ppb]
        for e in range(1, ppb):
            pg = pg + pbuf[slot, jb * ppb + e]
        mean = jnp.sum(pg, axis=1, keepdims=True) * (1.0 / MOBA_BLOCK)
        km = jnp.where(lane == c * bpc + jb, mean, km)
    km_ref[...] = km

    @pl.when(c == nchunks - 1)
    def _():
        qcol = _select_column(qt_ref[...], b)
        km = km_ref[...]
        lane1 = lax.broadcasted_iota(jnp.int32, (1, LANES), 1)
        out = jnp.zeros((1, LANES), jnp.int32)
        heads = range(C_HEADS)
        s = [jnp.where(lane1 < nblk,
                       jnp.sum(km[(h // C_GROUP) * HEAD_DIM:(h // C_GROUP + 1) * HEAD_DIM, :]
                               * qcol[h * HEAD_DIM:(h + 1) * HEAD_DIM, :], axis=0, keepdims=True),
                       -jnp.inf) for h in heads]
        for k in range(MOBA_TOPK):
            m = [jnp.max(s[h], axis=1, keepdims=True) for h in heads]
            idx = [jnp.min(jnp.where(s[h] == m[h], lane1, LANES), axis=1, keepdims=True) for h in heads]
            s = [jnp.where(lane1 == idx[h], -jnp.inf, s[h]) for h in heads]
            for h in heads:
                out = jnp.where(lane1 == h * MOBA_TOPK + k, idx[h], out)
        idx_ref[0] = out


def _moba_gate_sample(page_table, q, cache_kt, layer):
    db, n_pages = page_table.shape
    ppb = MOBA_BLOCK // PAGE_SIZE
    cpp = math.gcd(n_pages, 64)
    nblk = n_pages // ppb
    assert cpp % ppb == 0 and MOBA_TOPK <= nblk <= LANES and C_HEADS * MOBA_TOPK <= LANES
    idx = pl.pallas_call(
        functools.partial(_moba_gate_sample_kernel, pages_per_block=ppb, pages_per_chunk=cpp, layer=layer),
        grid_spec=pltpu.PrefetchScalarGridSpec(
            num_scalar_prefetch=1,
            grid=(db, n_pages // cpp),
            in_specs=[pl.BlockSpec((db, C_QW), lambda b, c, pt: (0, 0)), pl.BlockSpec(memory_space=pl.ANY)],
            out_specs=pl.BlockSpec((1, 1, LANES), lambda b, c, pt: (b, 0, 0)),
            scratch_shapes=[pltpu.VMEM((3, cpp, C_KVW, PAGE_SIZE), F32), pltpu.SemaphoreType.DMA((3,)),
                            pltpu.VMEM((C_QW, db), F32), pltpu.VMEM((C_KVW, LANES), F32)]),
        out_shape=jax.ShapeDtypeStruct((db, 1, LANES), jnp.int32),
        compiler_params=_cparams("arbitrary", "arbitrary"),
        name="moba_gate_sample",
    )(page_table, q, cache_kt)
    return idx.reshape(db, LANES)


def _moba_sample_kernel(idx_ref, pt_ref, q_ref, kn_ref, vn_ref, ck_ref, cv_ref, o_ref, kbuf, vbuf, sem,
                        qt_ref, knt_ref, vnt_ref, ot_ref, *, pages_per_block, layer):
    b = pl.program_id(0)
    nb = pl.num_programs(0)
    slot = b % 2
    ppb = pages_per_block
    npg = MOBA_TOPK * ppb

    def copies(bb, sl):
        out = []
        for h in range(C_HEADS):
            rows = pl.ds((h // C_GROUP) * HEAD_DIM, HEAD_DIM)
            for k in range(MOBA_TOPK):
                blk = idx_ref[bb, h * MOBA_TOPK + k]
                for e in range(ppb):
                    page = pt_ref[bb, blk * ppb + e]
                    dst = h * npg + k * ppb + e
                    out.append(pltpu.make_async_copy(ck_ref.at[layer, page, rows, :], kbuf.at[sl, dst], sem.at[sl, 0]))
                    out.append(pltpu.make_async_copy(cv_ref.at[layer, page, rows, :], vbuf.at[sl, dst], sem.at[sl, 1]))
        return out

    @pl.when(b == 0)
    def _():
        for cp in copies(0, 0):
            cp.start()
        qt_ref[...] = q_ref[...].T
        knt_ref[...] = kn_ref[...].T
        vnt_ref[...] = vn_ref[...].T
        ot_ref[...] = jnp.zeros_like(ot_ref)

    @pl.when(b + 1 < nb)
    def _():
        for cp in copies(b + 1, 1 - slot):
            cp.start()

    for cp in copies(b, slot):
        cp.wait()

    qcol = _select_column(qt_ref[...], b) * ATT_SCALE
    kncol = _select_column(knt_ref[...], b)
    vncol = _select_column(vnt_ref[...], b)
    heads = range(C_HEADS)
    hrows = [slice(h * HEAD_DIM, (h + 1) * HEAD_DIM) for h in heads]
    kvrows = [slice((h // C_GROUP) * HEAD_DIM, (h // C_GROUP + 1) * HEAD_DIM) for h in heads]
    s = [[jnp.sum(kbuf[slot, h * npg + pg] * qcol[hrows[h], :], axis=0, keepdims=True) for pg in range(npg)]
         for h in heads]
    sn = [jnp.sum(qcol[hrows[h], :] * kncol[kvrows[h], :], axis=0, keepdims=True) for h in heads]
    m = [jnp.maximum(sn[h], jnp.max(functools.reduce(jnp.maximum, s[h]), axis=1, keepdims=True)) for h in heads]
    p = [[jnp.exp(row - m[h]) for row in s[h]] for h in heads]
    pn = [jnp.exp(sn[h] - m[h]) for h in heads]
    denom = [pn[h] + jnp.sum(functools.reduce(jnp.add, p[h]), axis=1, keepdims=True) for h in heads]
    acc = [functools.reduce(jnp.add, [vbuf[slot, h * npg + pg] * p[h][pg] for pg in range(npg)]) for h in heads]
    ocols = [(jnp.sum(acc[h], axis=1, keepdims=True) + pn[h] * vncol[kvrows[h], :]) / denom[h] for h in heads]
    lane = lax.broadcasted_iota(jnp.int32, ot_ref.shape, 1)
    ot_ref[...] = jnp.where(lane == b, jnp.concatenate(ocols, axis=0), ot_ref[...])

    @pl.when(b == nb - 1)
    def _():
        o_ref[...] = ot_ref[...].T


def _moba_sample(idx, page_table, q, kn, vn, cache_kt, cache_vt, layer):
    db = q.shape[0]
    ppb = MOBA_BLOCK // PAGE_SIZE
    nbuf = C_HEADS * MOBA_TOPK * ppb
    full = lambda *shape: pl.BlockSpec(shape, lambda b, *_: (0,) * len(shape))
    return pl.pallas_call(
        functools.partial(_moba_sample_kernel, pages_per_block=ppb, layer=layer),
        grid_spec=pltpu.PrefetchScalarGridSpec(
            num_scalar_prefetch=2,
            grid=(db,),
            in_specs=[full(db, C_QW), full(db, C_KVW), full(db, C_KVW),
                      pl.BlockSpec(memory_space=pl.ANY), pl.BlockSpec(memory_space=pl.ANY)],
            out_specs=full(db, C_QW),
            scratch_shapes=[pltpu.VMEM((2, nbuf, HEAD_DIM, PAGE_SIZE), F32),
                            pltpu.VMEM((2, nbuf, HEAD_DIM, PAGE_SIZE), F32),
                            pltpu.SemaphoreType.DMA((2, 2)),
                            pltpu.VMEM((C_QW, db), F32), pltpu.VMEM((C_KVW, db), F32),
                            pltpu.VMEM((C_KVW, db), F32), pltpu.VMEM((C_QW, db), F32)]),
        out_shape=jax.ShapeDtypeStruct((db, C_QW), F32),
        compiler_params=_cparams("arbitrary"),
        name="moba_sample",
    )(idx, page_table, q, kn, vn, cache_kt, cache_vt)


def _decoder(st, x, pos, p, state):
    is_prompt = state is None
    nseq, seqlen, d = st.nseq, st.seqlen, st.d
    depth = p['w_ada'].shape[0]
    rope_tabs = _rope_tables(pos if is_prompt else jnp.broadcast_to(pos, (st.tokens,)))
    new = {'win_k': [], 'win_v': [], 'conv': [], 'ssm': [], 'k': [], 'v': []}
    spare = None

    def proj(layer, x, *args):
        outs = _proj(st, layer, x, *args)
        return (outs[:-1], outs[-1]) if isinstance(x, tuple) else (outs, x)

    for layer in range(depth):
        g_norm = p['norm_g'][layer]
        if layer % 2 == 0:
            i = layer // 2
            splits = ((0, A_QW, True, True), (A_QW, A_KVW, True, True), (A_QW + A_KVW, A_KVW, False, True),
                      (A_QW + 2 * A_KVW, MB_INNER, False, True),
                      (A_QW + 2 * A_KVW + MB_INNER, MB_CONV_DIM, False, True),
                      (A_QW + 2 * A_KVW + MB_INNER + MB_CONV_DIM, LANES, False, True))
            (qa, ka, va, z, xbc, dt), x = proj(layer, x, g_norm[0], p['w_in_a'][i], rope_tabs, splits)
            if is_prompt:
                o_a = _swa_prompt(qa, ka, va, p['sinks'][i], nseq, seqlen)
                wk = ka.reshape(nseq, seqlen, A_KV_HEADS, HEAD_DIM)[:, -WINDOW:]
                wv = va.reshape(nseq, seqlen, A_KV_HEADS, HEAD_DIM)[:, -WINDOW:]
                conv_prev = jnp.zeros((nseq, MB_CONV - 1, MB_CONV_DIM), F32)
                h0 = jnp.zeros((nseq, MB_HEADS, MB_HEADDIM, MB_DSTATE), F32)
                o_b, conv_new, h_new = _ssd_prompt(xbc, z, dt, conv_prev, h0, p['conv_w'][i], p['conv_b'][i],
                                                   p['dt_bias'][i], p['a_log'][i], p['d_skip'][i],
                                                   p['ssm_norm_g'][i], nseq, seqlen)
            else:
                o_a, wk, wv = _swa_sample(qa, ka, va, state['win_k'][i].reshape(nseq, WINDOW, A_KVW),
                                          state['win_v'][i].reshape(nseq, WINDOW, A_KVW), p['sinks'][i])
                wk = wk.reshape(nseq, WINDOW, A_KV_HEADS, HEAD_DIM)
                wv = wv.reshape(nseq, WINDOW, A_KV_HEADS, HEAD_DIM)
                o_b, conv_new, h_new = _ssd_sample(xbc, z, dt, state['conv'][i], state['ssm'][i], p['conv_w'][i],
                                                   p['conv_b'][i], p['dt_bias'][i], p['a_log'][i], p['d_skip'][i],
                                                   p['ssm_norm_g'][i])
            a_list = [o_a, o_b]
            w_list = [p['w_out_a'][i][:A_QW], p['w_out_a'][i][A_QW:]]
            new['win_k'].append(wk)
            new['win_v'].append(wv)
            new['conv'].append(conv_new)
            new['ssm'].append(h_new)
        else:
            j = layer // 2
            cols = ((0, C_QW, True), (C_QW, C_KVW, True), (C_QW + C_KVW, C_KVW, False))
            if is_prompt:
                splits = tuple(c + (e,) for c, e in zip(cols, (True, False, False)))
                extras = (('bf16', 1), ('transposed', 1), ('transposed', 2), ('blockmean', 1))
                (qc, k_rows, k_t, v_t, kmean), x = proj(layer, x, g_norm[0], p['w_in_c'][j], rope_tabs, splits, extras)
                o_c = _moba_prompt(qc, k_rows, v_t, kmean.reshape(nseq, seqlen // MOBA_BLOCK, C_KVW), nseq, seqlen)
                kc, vc = (jnp.transpose(t.reshape(nseq, C_KV_HEADS, HEAD_DIM, seqlen), (0, 3, 1, 2)) for t in (k_t, v_t))
            else:
                splits = tuple(c + (True,) for c in cols)
                (qc, kc, vc), x = proj(layer, x, g_norm[0], p['w_in_c'][j], rope_tabs, splits)
                idx = _moba_gate_sample(state['page_table'], qc, state['cache_k'], j)
                o_c = _moba_sample(idx, state['page_table'], qc, kc, vc, state['cache_k'], state['cache_v'], j)
                kc, vc = (t.reshape(nseq, seqlen, C_KV_HEADS, HEAD_DIM) for t in (kc, vc))
            a_list = [o_c]
            w_list = [p['w_out_c'][j]]
            new['k'].append(kc)
            new['v'].append(vc)
        sorted_moe = st.tokens >= 2 * SORT_TILE
        x1, h2, aux = _mix(st, layer, x, a_list, w_list, g_norm[1], p['wr_pad'], p['rb_col'], sorted_moe)
        moe_args = (st, layer, h2, aux, p['w_gate_up'], p['w_down'], x1, p['final_norm_g'], layer == depth - 1)
        if sorted_moe:
            x, spare = _moe_sorted(*moe_args, spare=spare)
        else:
            x = _moe(*moe_args)
    return x.reshape(nseq, seqlen, d), {name: jnp.stack(rows) for name, rows in new.items()}


def kernel(x_prompt, x_sample, c_prompt, c_sample, state_win_k, state_win_v, state_conv, state_ssm, cache_k, cache_v, page_table, w_ada, b_ada, norm_g, w_in_a, sinks, conv_w, conv_b, dt_bias, a_log, d_skip, ssm_norm_g, w_out_a, w_in_c, w_out_c, w_router, router_bias, w_gate, w_up, w_down, final_norm_g):
    nb, seqlen, d = x_prompt.shape
    db, dec_seq, _ = x_sample.shape
    assert dec_seq == 1 and seqlen % MOBA_BLOCK == 0 and d % LANES == 0
    n_odd, n_pool, page_size, ckv, hd = cache_k.shape
    assert page_size == PAGE_SIZE and ckv == C_KV_HEADS and hd == HEAD_DIM
    n_pages = page_table.shape[1]

    n_in_a = w_in_a.shape[2]
    pad_a = (-n_in_a) % LANES
    p = {'w_ada': w_ada, 'norm_g': norm_g, 'sinks': sinks, 'conv_w': conv_w, 'conv_b': conv_b, 'dt_bias': dt_bias,
         'a_log': a_log, 'd_skip': d_skip, 'ssm_norm_g': ssm_norm_g, 'final_norm_g': final_norm_g,
         'w_in_a': jnp.pad(w_in_a, ((0, 0), (0, 0), (0, pad_a))).astype(BF16),
         'w_out_a': w_out_a.astype(BF16), 'w_in_c': w_in_c.astype(BF16), 'w_out_c': w_out_c.astype(BF16),
         'w_gate_up': jnp.concatenate([w_gate, w_up], axis=-1).astype(BF16), 'w_down': w_down.astype(BF16),
         'wr_pad': jnp.pad(w_router, ((0, 0), (0, LANES - N_EXPERTS))),
         'rb_col': jnp.pad(router_bias, (0, LANES - N_EXPERTS)).reshape(LANES, 1)}

    mod = _ada(jnp.concatenate([c_prompt, c_sample], axis=0), w_ada, b_ada)
    st_p = _Stream(nb, seqlen, d, mod[:, :nb])
    st_s = _Stream(db, 1, d, mod[:, nb:])

    pos_p = jnp.arange(seqlen, dtype=jnp.int32)
    y_prompt, pn = _decoder(st_p, x_prompt.reshape(nb * seqlen, d), pos_p, p, None)

    cache_kt = jnp.transpose(cache_k, (0, 1, 3, 4, 2)).reshape(n_odd, n_pool, C_KVW, PAGE_SIZE)
    cache_vt = jnp.transpose(cache_v, (0, 1, 3, 4, 2)).reshape(n_odd, n_pool, C_KVW, PAGE_SIZE)
    state = {'win_k': state_win_k, 'win_v': state_win_v, 'conv': state_conv, 'ssm': state_ssm,
             'cache_k': cache_kt, 'cache_v': cache_vt, 'page_table': page_table}
    pos_s = n_pages * PAGE_SIZE + jnp.arange(1, dtype=jnp.int32)
    y_sample, sn = _decoder(st_s, x_sample.reshape(db, d), pos_s, p, state)
    return (y_prompt, y_sample, pn['win_k'], pn['win_v'], pn['conv'], pn['ssm'], pn['k'], pn['v'],
            sn['win_k'], sn['win_v'], sn['conv'], sn['ssm'], sn['k'], sn['v'])
```
